```python
import math
import jax, jax.numpy as jnp
from jax import lax
import numpy as np

D_MODEL = 4096
BATCH = 4
SEQ = 4096
DEPTH = 1

D_CONV = D_MODEL // 2
CONV_WIDTH = 31
N_HEADS = 16
HEAD_DIM = 128
N_KV_HEADS = 4
D_ATTN = N_HEADS * HEAD_DIM
D_KV = N_KV_HEADS * HEAD_DIM
IDX_HEADS = 16
IDX_DIM = 64
TOPK_MAX = 256
Q_BLOCK = 128
N_GROUPS = 4
EXPERTS_PER_GROUP = 8
N_EXPERTS = N_GROUPS * EXPERTS_PER_GROUP
TOP_K_IN_GROUP = 2
D_EXPERT = 1024
MOE_BLOCK = 256
ALPHA = (2.0 * DEPTH) ** 0.25
BETA = (8.0 * DEPTH) ** -0.25
LN_EPS = 1e-5
SPLITS = (2 * D_CONV, D_ATTN, D_KV, D_KV, IDX_HEADS * IDX_DIM, IDX_DIM, IDX_HEADS)
D_IN = sum(SPLITS)

kernel_name = "hymba_conformer_dsa_hiermoe_deepnorm"


def layer_norm(x, g, b):
    xf = x.astype(jnp.float32)
    mu = jnp.mean(xf, axis=-1, keepdims=True)
    var = jnp.mean(jnp.square(xf - mu), axis=-1, keepdims=True)
    y = (xf - mu) * lax.rsqrt(var + LN_EPS) * g.astype(jnp.float32) + b.astype(jnp.float32)
    return y.astype(x.dtype)


def conformer_conv(u, w_dw, b_dw, g_ln, b_ln):
    a, gate = jnp.split(u, 2, axis=-1)
    glu = a * jax.nn.sigmoid(gate)
    y = lax.conv_general_dilated(
        glu, w_dw[:, None, :].astype(glu.dtype), window_strides=(1,),
        padding=((CONV_WIDTH - 1, 0),),
        dimension_numbers=("NWC", "WIO", "NWC"),
        feature_group_count=D_CONV)
    y = y + b_dw
    y = layer_norm(y, g_ln, b_ln)
    return jax.nn.silu(y)


def dsa_attention(q, k, v, q_idx, k_idx, w_idx):
    B, S = q.shape[0], q.shape[1]
    topk = min(TOPK_MAX, S // 4)
    nb = S // Q_BLOCK
    R = N_HEADS // N_KV_HEADS
    f32 = jnp.float32
    slopes = jnp.exp2(-8.0 * jnp.arange(1, N_HEADS + 1, dtype=f32) / N_HEADS).reshape(N_KV_HEADS, R)
    k = k.reshape(B, S, N_KV_HEADS, HEAD_DIM)
    v = v.reshape(B, S, N_KV_HEADS, HEAD_DIM)
    qb_all = q.reshape(B, nb, Q_BLOCK, N_KV_HEADS, R, HEAD_DIM).swapaxes(0, 1)
    qi_all = q_idx.reshape(B, nb, Q_BLOCK, IDX_HEADS, IDX_DIM).swapaxes(0, 1)
    wi_all = w_idx.reshape(B, nb, Q_BLOCK, IDX_HEADS).swapaxes(0, 1)
    kf = k_idx.astype(f32)
    s_pos = jnp.arange(S, dtype=jnp.int32)

    def block(args):
        bi, qb, qib, wb = args
        t = bi * Q_BLOCK + jnp.arange(Q_BLOCK, dtype=jnp.int32)
        rel = jax.nn.relu(jnp.einsum("bthd,bsd->bhts", qib.astype(f32), kf) * IDX_DIM ** -0.5)
        score = jnp.einsum("bhts,bth->bts", rel, wb.astype(f32) * IDX_HEADS ** -0.5)
        score = jnp.where(s_pos[None, None, :] <= t[None, :, None], score, -jnp.inf)
        _, idx = lax.top_k(score, topk)
        valid = idx <= t[None, :, None]
        k_sel = jax.vmap(lambda kk, ii: kk[ii])(k, idx)
        v_sel = jax.vmap(lambda vv, ii: vv[ii])(v, idx)
        logits = jnp.einsum("btgrd,btkgd->btgrk", qb.astype(f32), k_sel.astype(f32)) * HEAD_DIM ** -0.5
        dist = (t[None, :, None] - idx).astype(f32)
        logits = logits - slopes[None, None, :, :, None] * dist[:, :, None, None, :]
        logits = jnp.where(valid[:, :, None, None, :], logits, -jnp.inf)
        p = jax.nn.softmax(logits, axis=-1)
        o = jnp.einsum("btgrk,btkgd->btgrd", p.astype(v.dtype), v_sel)
        return o.reshape(B, Q_BLOCK, D_ATTN)

    out = lax.map(block, (jnp.arange(nb, dtype=jnp.int32), qb_all, qi_all, wi_all))
    return out.swapaxes(0, 1).reshape(B, S, D_ATTN)


def hierarchical_moe(h, w_rg, b_rg, w_re, b_re, w1, w3, w2):
    B, S, D = h.shape
    N = B * S
    f32 = jnp.float32
    xf = h.reshape(N, D)
    g_logits = (xf @ w_rg + b_rg).astype(f32)
    g_sel = jnp.argmax(g_logits, axis=-1)
    p_group = jnp.take_along_axis(jax.nn.softmax(g_logits, axis=-1), g_sel[:, None], axis=1)
    e_logits = (xf @ w_re + b_re).astype(f32).reshape(N, N_GROUPS, EXPERTS_PER_GROUP)
    e_logits = jnp.take_along_axis(e_logits, g_sel[:, None, None], axis=1)[:, 0]
    top_p, top_local = lax.top_k(jax.nn.softmax(e_logits, axis=-1), TOP_K_IN_GROUP)
    gates = p_group * top_p / jnp.sum(top_p, axis=-1, keepdims=True)
    expert = g_sel[:, None].astype(jnp.int32) * EXPERTS_PER_GROUP + top_local.astype(jnp.int32)
    A = N * TOP_K_IN_GROUP
    e_flat = expert.reshape(A)
    tok_flat = jnp.repeat(jnp.arange(N, dtype=jnp.int32), TOP_K_IN_GROUP)
    gate_flat = gates.reshape(A)
    order = jnp.argsort(e_flat)
    e_sorted, tok_sorted, gate_sorted = e_flat[order], tok_flat[order], gate_flat[order]
    counts = jnp.zeros((N_EXPERTS,), jnp.int32).at[e_flat].add(1)
    padded = (counts + MOE_BLOCK - 1) // MOE_BLOCK * MOE_BLOCK
    start = jnp.cumsum(counts) - counts
    pend = jnp.cumsum(padded)
    pstart = pend - padded
    dest = pstart[e_sorted] + jnp.arange(A, dtype=jnp.int32) - start[e_sorted]
    P = (A + MOE_BLOCK - 1) // MOE_BLOCK * MOE_BLOCK + N_EXPERTS * MOE_BLOCK
    nblk = P // MOE_BLOCK
    buf_tok = jnp.zeros((P,), jnp.int32).at[dest].set(tok_sorted)
    buf_gate = jnp.zeros((P,), f32).at[dest].set(gate_sorted)
    blk_start = jnp.arange(nblk, dtype=jnp.int32) * MOE_BLOCK
    blk_expert = jnp.minimum(jnp.sum(blk_start[:, None] >= pend[None, :], axis=1), N_EXPERTS - 1)

    def expert_block(args):
        e, toks, g = args
        xb = xf[toks]
        hb = jax.nn.silu(xb @ w1[e]) * (xb @ w3[e])
        return (hb @ w2[e]) * g[:, None].astype(xb.dtype)

    y = lax.map(expert_block, (blk_expert, buf_tok.reshape(nblk, MOE_BLOCK), buf_gate.reshape(nblk, MOE_BLOCK)))
    out = jnp.zeros_like(xf).at[buf_tok].add(y.reshape(P, D))
    return out.reshape(B, S, D)


def setup_inputs(seed: int = 0) -> dict:
    key = jax.random.key(seed)
    ks = jax.random.split(key, 20)
    L, D = DEPTH, D_MODEL
    nrm = jax.random.normal
    x = nrm(ks[0], (BATCH, SEQ, D), jnp.float32)
    col_scale = jnp.concatenate([
        jnp.full((D_CONV,), BETA, jnp.float32), jnp.ones((D_CONV,), jnp.float32),
        jnp.ones((D_ATTN,), jnp.float32), jnp.ones((D_KV,), jnp.float32),
        jnp.full((D_KV,), BETA, jnp.float32),
        jnp.ones((IDX_HEADS * IDX_DIM + IDX_DIM + IDX_HEADS,), jnp.float32)])
    w_in = nrm(ks[1], (L, D, D_IN), jnp.float32) * (D ** -0.5) * col_scale
    conv_dw_w = nrm(ks[2], (L, CONV_WIDTH, D_CONV), jnp.float32) * (CONV_WIDTH ** -0.5)
    conv_dw_b = 0.01 * nrm(ks[3], (L, D_CONV), jnp.float32)
    conv_ln_g = 1.0 + 0.01 * nrm(ks[4], (L, D_CONV), jnp.float32)
    conv_ln_b = 0.01 * nrm(ks[5], (L, D_CONV), jnp.float32)
    w_out = nrm(ks[6], (L, D_CONV + D_ATTN, D), jnp.float32) * ((D_CONV + D_ATTN) ** -0.5) * BETA
    ln1_g = 1.0 + 0.01 * nrm(ks[7], (L, D), jnp.float32)
    ln1_b = 0.01 * nrm(ks[8], (L, D), jnp.float32)
    w_router_group = nrm(ks[9], (L, D, N_GROUPS), jnp.float32) * (D ** -0.5)
    b_router_group = 0.01 * nrm(ks[10], (L, N_GROUPS), jnp.float32)
    w_router_expert = nrm(ks[11], (L, D, N_EXPERTS), jnp.float32) * (D ** -0.5)
    b_router_expert = 0.01 * nrm(ks[12], (L, N_EXPERTS), jnp.float32)
    w_expert_gate = nrm(ks[13], (L, N_EXPERTS, D, D_EXPERT), jnp.float32) * (D ** -0.5) * BETA
    w_expert_up = nrm(ks[14], (L, N_EXPERTS, D, D_EXPERT), jnp.float32) * (D ** -0.5) * BETA
    w_expert_down = nrm(ks[15], (L, N_EXPERTS, D_EXPERT, D), jnp.float32) * (D_EXPERT ** -0.5) * BETA
    ln2_g = 1.0 + 0.01 * nrm(ks[16], (L, D), jnp.float32)
    ln2_b = 0.01 * nrm(ks[17], (L, D), jnp.float32)
    return {"x": x, "w_in": w_in, "conv_dw_w": conv_dw_w, "conv_dw_b": conv_dw_b,
            "conv_ln_g": conv_ln_g, "conv_ln_b": conv_ln_b, "w_out": w_out,
            "ln1_g": ln1_g, "ln1_b": ln1_b,
            "w_router_group": w_router_group, "b_router_group": b_router_group,
            "w_router_expert": w_router_expert, "b_router_expert": b_router_expert,
            "w_expert_gate": w_expert_gate, "w_expert_up": w_expert_up, "w_expert_down": w_expert_down,
            "ln2_g": ln2_g, "ln2_b": ln2_b}


def reference(x, w_in, conv_dw_w, conv_dw_b, conv_ln_g, conv_ln_b, w_out, ln1_g, ln1_b,
              w_router_group, b_router_group, w_router_expert, b_router_expert,
              w_expert_gate, w_expert_up, w_expert_down, ln2_g, ln2_b):
    split_points = np.cumsum(SPLITS)[:-1].tolist()
    h = x
    for l in range(DEPTH):
        u = h @ w_in[l]
        conv_u, q, k, v, q_idx, k_idx, w_idx = jnp.split(u, split_points, axis=-1)
        conv_out = conformer_conv(conv_u, conv_dw_w[l], conv_dw_b[l], conv_ln_g[l], conv_ln_b[l])
        attn_out = dsa_attention(q, k, v, q_idx, k_idx, w_idx)
        mix = jnp.concatenate([conv_out, attn_out], axis=-1) @ w_out[l]
        h = layer_norm(ALPHA * h + mix, ln1_g[l], ln1_b[l])
        ffn = hierarchical_moe(h, w_router_group[l], b_router_group[l], w_router_expert[l], b_router_expert[l],
                               w_expert_gate[l], w_expert_up[l], w_expert_down[l])
        h = layer_norm(ALPHA * h + ffn, ln2_g[l], ln2_b[l])
    return h
```

```python
import functools

import jax
import jax.numpy as jnp
from jax import lax
from jax.experimental import pallas as pl
from jax.experimental.pallas import tpu as pltpu

F32 = jnp.float32
BF16 = jnp.bfloat16
I32 = jnp.int32

CONV_WIDTH = 31
N_HEADS = 16
HEAD_DIM = 128
N_KV_HEADS = 4
IDX_HEADS = 16
IDX_DIM = 64
TOPK_MAX = 256
N_GROUPS = 4
EXPERTS_PER_GROUP = 8
N_EXPERTS = N_GROUPS * EXPERTS_PER_GROUP
LN_EPS = 1e-5

LANES = 128
SUBLANES = 8
BF16_ROWS = 16
MIB = 1024 * 1024

Q_TILE = 128
KEY_CHUNK = 512
CONV_HALO = 32
MOE_ROWS = 256
INT_MIN = -(2 ** 31)
NEG_BIG = -1e30
M_INIT = -1e20


def _cparams(semantics, vmem_mib):
    return pltpu.CompilerParams(dimension_semantics=semantics, vmem_limit_bytes=vmem_mib * MIB)


def _layer_norm_rows(z, g, b):
    mu = jnp.mean(z, axis=-1, keepdims=True)
    zc = z - mu
    var = jnp.mean(zc * zc, axis=-1, keepdims=True)
    return zc * lax.rsqrt(var + LN_EPS) * g + b


def _mm_glu_kernel(x_ref, wa_ref, wg_ref, o_ref):
    x = x_ref[...]
    a = jnp.dot(x, wa_ref[...], preferred_element_type=F32)
    g = jnp.dot(x, wg_ref[...], preferred_element_type=F32)
    o_ref[...] = (a * jax.nn.sigmoid(g)).astype(o_ref.dtype)


def _mm_scale_kernel(x_ref, w_ref, s_ref, o_ref):
    acc = jnp.dot(x_ref[...], w_ref[...], preferred_element_type=F32)
    o_ref[...] = (acc * s_ref[...]).astype(o_ref.dtype)


def _mm_tiles(n, k, cols):
    tm = min(1024, n)
    tn = min(512, cols)
    assert n % tm == 0 and cols % tn == 0
    return tm, tn


def _proj_glu(xb, wa, wg):
    n, k = xb.shape
    cols = wa.shape[1]
    tm, tn = _mm_tiles(n, k, cols)
    return pl.pallas_call(
        _mm_glu_kernel,
        grid=(n // tm, cols // tn),
        in_specs=[pl.BlockSpec((tm, k), lambda i, j: (i, 0)),
                  pl.BlockSpec((k, tn), lambda i, j: (0, j)),
                  pl.BlockSpec((k, tn), lambda i, j: (0, j))],
        out_specs=pl.BlockSpec((tm, tn), lambda i, j: (i, j)),
        out_shape=jax.ShapeDtypeStruct((n, cols), F32),
        compiler_params=_cparams(("parallel", "arbitrary"), 48),
        name="proj_glu",
    )(xb, wa, wg)


def _proj_scale(xb, w, scale, out_dtype, name):
    n, k = xb.shape
    cols = w.shape[1]
    tm, tn = _mm_tiles(n, k, cols)
    return pl.pallas_call(
        _mm_scale_kernel,
        grid=(n // tm, cols // tn),
        in_specs=[pl.BlockSpec((tm, k), lambda i, j: (i, 0)),
                  pl.BlockSpec((k, tn), lambda i, j: (0, j)),
                  pl.BlockSpec((1, tn), lambda i, j: (0, j))],
        out_specs=pl.BlockSpec((tm, tn), lambda i, j: (i, j)),
        out_shape=jax.ShapeDtypeStruct((n, cols), out_dtype),
        compiler_params=_cparams(("parallel", "arbitrary"), 48),
        name=name,
    )(xb, w, scale)


def _conv_kernel(cur_ref, prev_ref, w_ref, b_ref, g_ref, beta_ref, o_ref, ext_ref, y_ref, *, rows, lane_chunk):
    t = pl.program_id(1)
    dc = cur_ref.shape[2]
    row_sub = 32
    ext_ref[0:CONV_HALO, :] = jnp.where(t > 0, prev_ref[0], 0.0)
    ext_ref[CONV_HALO:CONV_HALO + rows, :] = cur_ref[0]
    first = CONV_HALO - (CONV_WIDTH - 1)

    for r0 in range(0, rows, row_sub):
        for lc in range(dc // lane_chunk):
            cols = slice(lc * lane_chunk, (lc + 1) * lane_chunk)
            acc = jnp.zeros((row_sub, lane_chunk), F32)
            for j in range(CONV_WIDTH):
                acc = acc + ext_ref[r0 + first + j:r0 + first + j + row_sub, cols] * w_ref[j:j + 1, cols]
            y_ref[r0:r0 + row_sub, cols] = acc + b_ref[:, cols]

    def norm_rows(i, carry):
        r0 = pl.multiple_of(i * BF16_ROWS, BF16_ROWS)
        yn = _layer_norm_rows(y_ref[pl.ds(r0, BF16_ROWS), :], g_ref[...], beta_ref[...])
        o_ref[0, pl.ds(r0, BF16_ROWS), :] = (yn * jax.nn.sigmoid(yn)).astype(o_ref.dtype)
        return carry

    lax.fori_loop(0, rows // BF16_ROWS, norm_rows, 0)


def _conformer_conv(glu, w_dw, b_dw, g_ln, b_ln):
    b, s, dc = glu.shape
    rows = min(128, s)
    halo_blocks = rows // CONV_HALO
    lane_chunk = min(512, dc)
    kern = functools.partial(_conv_kernel, rows=rows, lane_chunk=lane_chunk)
    return pl.pallas_call(
        kern,
        grid=(b, s // rows),
        in_specs=[pl.BlockSpec((1, rows, dc), lambda bi, t: (bi, t, 0)),
                  pl.BlockSpec((1, CONV_HALO, dc), lambda bi, t: (bi, jnp.maximum(t * halo_blocks - 1, 0), 0)),
                  pl.BlockSpec((CONV_WIDTH, dc), lambda bi, t: (0, 0)),
                  pl.BlockSpec((1, dc), lambda bi, t: (0, 0)),
                  pl.BlockSpec((1, dc), lambda bi, t: (0, 0)),
                  pl.BlockSpec((1, dc), lambda bi, t: (0, 0))],
        out_specs=pl.BlockSpec((1, rows, dc), lambda bi, t: (bi, t, 0)),
        out_shape=jax.ShapeDtypeStruct((b, s, dc), BF16),
        scratch_shapes=[pltpu.VMEM((CONV_HALO + rows, dc), F32), pltpu.VMEM((rows, dc), F32)],
        compiler_params=_cparams(("parallel", "arbitrary"), 32),
        name="conformer_conv",
    )(glu, glu, w_dw, b_dw, g_ln, b_ln)


def _dsa_kernel(q_ref, qi_ref, k_ref, v_ref, kidx_ref, w_ref, o_ref,
                ke_ref, ko_ref, key_ref, qs_ref, wb_ref, m_ref, l_ref, acc_ref, *, seq, chunk, topk):
    tb = pl.program_id(1)
    t0 = tb * Q_TILE
    n_chunks = (t0 + Q_TILE + chunk - 1) // chunk
    n_lt = chunk // LANES
    rep = N_HEADS // N_KV_HEADS
    nt = (((1,), (1,)), ((), ()))

    @pl.when(tb == 0)
    def _():
        def build(i, carry):
            r0 = pl.multiple_of(i * 256, 256)
            kx = kidx_ref[0, pl.ds(r0, 256), :]
            lane = lax.broadcasted_iota(I32, kx.shape, 1)
            ke_ref[pl.ds(r0, 256), :] = jnp.where(lane < IDX_DIM, kx, 0.0).astype(BF16)
            ko_ref[pl.ds(r0, 256), :] = jnp.where(lane >= IDX_DIM, pltpu.roll(kx, IDX_DIM, 1), 0.0).astype(BF16)
            return carry
        lax.fori_loop(0, seq // 256, build, 0)

    w_blk = w_ref[0]
    for h in range(IDX_HEADS):
        col = w_blk[:, IDX_DIM + h:IDX_DIM + h + 1] * (IDX_HEADS ** -0.5)
        wb_ref[h] = jnp.broadcast_to(col, (Q_TILE, LANES))

    row_pos = t0 + lax.broadcasted_iota(I32, (Q_TILE, LANES), 0)
    lane_pos = lax.broadcasted_iota(I32, (Q_TILE, LANES), 1)

    def score_chunk(c, carry):
        r0 = pl.multiple_of(c * chunk, chunk)
        ke = ke_ref[pl.ds(r0, chunk), :]
        ko = ko_ref[pl.ds(r0, chunk), :]
        accs = [jnp.zeros((Q_TILE, LANES), F32) for _ in range(n_lt)]
        for j in range(IDX_HEADS // 2):
            lhs = qi_ref[0, :, j * LANES:(j + 1) * LANES]
            de = lax.dot_general(lhs, ke, nt, preferred_element_type=F32)
            do = lax.dot_general(lhs, ko, nt, preferred_element_type=F32)
            wbe = wb_ref[2 * j]
            wbo = wb_ref[2 * j + 1]
            for lt in range(n_lt):
                cols = slice(lt * LANES, (lt + 1) * LANES)
                accs[lt] = accs[lt] + wbe * jnp.maximum(de[:, cols], 0.0) + wbo * jnp.maximum(do[:, cols], 0.0)
        for lt in range(n_lt):
            bits = lax.bitcast_convert_type(accs[lt], I32)
            key = bits ^ ((bits >> 31) & 0x7FFFFFFF)
            s_pos = r0 + lt * LANES + lane_pos
            key_ref[c, :, lt * LANES:(lt + 1) * LANES] = jnp.where(s_pos <= row_pos, key, INT_MIN)
        return carry

    lax.fori_loop(0, n_chunks, score_chunk, 0)

    def bit_step(i, t_u):
        cand_u = t_u | (jnp.int32(1) << (31 - i))
        cand = cand_u ^ INT_MIN

        def count_chunk(c, cnt):
            for lt in range(n_lt):
                kt = key_ref[c, :, lt * LANES:(lt + 1) * LANES]
                cnt = cnt + jnp.where(kt >= cand, 1.0, 0.0)
            return cnt

        cnt = lax.fori_loop(0, n_chunks, count_chunk, jnp.zeros((Q_TILE, LANES), F32))
        tot = jnp.sum(cnt, axis=1, keepdims=True)
        return jnp.where(tot >= float(topk), cand_u, t_u)

    t_u = lax.fori_loop(0, 32, bit_step, jnp.zeros((Q_TILE, LANES), I32))
    thr = jnp.maximum(t_u ^ INT_MIN, INT_MIN + 1)

    for g in range(N_KV_HEADS):
        for r in range(rep):
            h = g * rep + r
            qs_ref[g, r * Q_TILE:(r + 1) * Q_TILE, :] = q_ref[0, :, h * HEAD_DIM:(h + 1) * HEAD_DIM]
    m_ref[...] = jnp.full(m_ref.shape, M_INIT, F32)
    l_ref[...] = jnp.zeros(l_ref.shape, F32)
    acc_ref[...] = jnp.zeros(acc_ref.shape, F32)

    def attend_chunk(c, carry):
        r0 = pl.multiple_of(c * chunk, chunk)
        nd = []
        for lt in range(n_lt):
            kt = key_ref[c, :, lt * LANES:(lt + 1) * LANES]
            dist = (r0 + lt * LANES + lane_pos - row_pos).astype(F32)
            nd.append(jnp.where(kt >= thr, dist, NEG_BIG))
        for g in range(N_KV_HEADS):
            kg = k_ref[0, pl.ds(r0, chunk), g * HEAD_DIM:(g + 1) * HEAD_DIM]
            vg = v_ref[0, pl.ds(r0, chunk), g * HEAD_DIM:(g + 1) * HEAD_DIM]
            s_all = lax.dot_general(qs_ref[g], kg, nt, preferred_element_type=F32)
            for r in range(rep):
                h = g * rep + r
                slope = float(2.0 ** (-8.0 * (h + 1) / N_HEADS))
                rows = slice(r * Q_TILE, (r + 1) * Q_TILE)
                sr = [s_all[rows, lt * LANES:(lt + 1) * LANES] + slope * nd[lt] for lt in range(n_lt)]
                mx = sr[0]
                for lt in range(1, n_lt):
                    mx = jnp.maximum(mx, sr[lt])
                m_old = m_ref[h]
                m_new = jnp.maximum(m_old, jnp.max(mx, axis=1, keepdims=True))
                alpha = jnp.exp(m_old - m_new)
                p = [jnp.exp(sr[lt] - m_new) for lt in range(n_lt)]
                ps = p[0]
                for lt in range(1, n_lt):
                    ps = ps + p[lt]
                l_ref[h] = alpha * l_ref[h] + jnp.sum(ps, axis=1, keepdims=True)
                pv = jnp.dot(jnp.concatenate(p, axis=1).astype(BF16), vg, preferred_element_type=F32)
                acc_ref[h] = alpha * acc_ref[h] + pv
                m_ref[h] = m_new
        return carry

    lax.fori_loop(0, n_chunks, attend_chunk, 0)

    for h in range(N_HEADS):
        o_ref[0, :, h * HEAD_DIM:(h + 1) * HEAD_DIM] = (acc_ref[h] / l_ref[h]).astype(o_ref.dtype)


def _dsa_attention(qkvi, small, b, s):
    d_attn = N_HEADS * HEAD_DIM
    d_kv = N_KV_HEADS * HEAD_DIM
    d_qi = IDX_HEADS * IDX_DIM
    chunk = min(KEY_CHUNK, s)
    topk = min(TOPK_MAX, s // 4)
    assert s % chunk == 0 and chunk % Q_TILE == 0 and d_attn % d_qi == 0 and d_attn % d_kv == 0
    kern = functools.partial(_dsa_kernel, seq=s, chunk=chunk, topk=topk)
    return pl.pallas_call(
        kern,
        grid=(b, s // Q_TILE),
        in_specs=[pl.BlockSpec((1, Q_TILE, d_attn), lambda bi, t: (bi, t, 0)),
                  pl.BlockSpec((1, Q_TILE, d_qi), lambda bi, t: (bi, t, (d_attn + 2 * d_kv) // d_qi)),
                  pl.BlockSpec((1, s, d_kv), lambda bi, t: (bi, 0, d_attn // d_kv)),
                  pl.BlockSpec((1, s, d_kv), lambda bi, t: (bi, 0, d_attn // d_kv + 1)),
                  pl.BlockSpec((1, s, LANES), lambda bi, t: (bi, 0, 0)),
                  pl.BlockSpec((1, Q_TILE, LANES), lambda bi, t: (bi, t, 0))],
        out_specs=pl.BlockSpec((1, Q_TILE, d_attn), lambda bi, t: (bi, t, 0)),
        out_shape=jax.ShapeDtypeStruct((b, s, d_attn), BF16),
        scratch_shapes=[pltpu.VMEM((s, LANES), BF16),
                        pltpu.VMEM((s, LANES), BF16),
                        pltpu.VMEM((s // chunk, Q_TILE, chunk), I32),
                        pltpu.VMEM((N_KV_HEADS, (N_HEADS // N_KV_HEADS) * Q_TILE, HEAD_DIM), BF16),
                        pltpu.VMEM((IDX_HEADS, Q_TILE, LANES), F32),
                        pltpu.VMEM((N_HEADS, Q_TILE, 1), F32),
                        pltpu.VMEM((N_HEADS, Q_TILE, 1), F32),
                        pltpu.VMEM((N_HEADS, Q_TILE, HEAD_DIM), F32)],
        compiler_params=_cparams(("parallel", "arbitrary"), 48),
        name="dsa_attention",
    )(qkvi, qkvi, qkvi, qkvi, small, small)


def _outproj_kernel(conv_ref, attn_ref, w_ref, x_ref, g_ref, b_ref, wr_ref, br_ref, h_ref, lg_ref,
                    acc_ref, hb_ref, *, n_conv_steps, n_steps, alpha):
    kk = pl.program_id(1)

    @pl.when(kk == 0)
    def _():
        acc_ref[...] = jnp.zeros(acc_ref.shape, F32)

    @pl.when(kk < n_conv_steps)
    def _():
        acc_ref[...] += jnp.dot(conv_ref[...], w_ref[...], preferred_element_type=F32)

    @pl.when(kk >= n_conv_steps)
    def _():
        acc_ref[...] += jnp.dot(attn_ref[...], w_ref[...], preferred_element_type=F32)

    @pl.when(kk == n_steps - 1)
    def _():
        def rows(i, carry):
            r0 = pl.multiple_of(i * BF16_ROWS, BF16_ROWS)
            z = alpha * x_ref[pl.ds(r0, BF16_ROWS), :] + acc_ref[pl.ds(r0, BF16_ROWS), :]
            h = _layer_norm_rows(z, g_ref[...], b_ref[...])
            h_ref[pl.ds(r0, BF16_ROWS), :] = h
            hb_ref[pl.ds(r0, BF16_ROWS), :] = h.astype(BF16)
            return carry
        lax.fori_loop(0, acc_ref.shape[0] // BF16_ROWS, rows, 0)
        lg_ref[...] = jnp.dot(hb_ref[...], wr_ref[...], preferred_element_type=F32) + br_ref[...]


def _out_proj_ln(conv_out, attn_out, w_out_b, x2, ln_g, ln_b, w_router, b_router, alpha):
    n, d = x2.shape
    dc = conv_out.shape[1]
    da = attn_out.shape[1]
    tm = min(512, n)
    tk = min(512, dc)
    assert dc % tk == 0 and da % tk == 0 and n % tm == 0
    n_conv = dc // tk
    n_steps = n_conv + da // tk
    kern = functools.partial(_outproj_kernel, n_conv_steps=n_conv, n_steps=n_steps, alpha=alpha)
    return pl.pallas_call(
        kern,
        grid=(n // tm, n_steps),
        in_specs=[pl.BlockSpec((tm, tk), lambda i, kk: (i, jnp.minimum(kk, n_conv - 1))),
                  pl.BlockSpec((tm, tk), lambda i, kk: (i, jnp.maximum(kk - n_conv, 0))),
                  pl.BlockSpec((tk, d), lambda i, kk: (kk, 0)),
                  pl.BlockSpec((tm, d), lambda i, kk: (i, 0)),
                  pl.BlockSpec((1, d), lambda i, kk: (0, 0)),
                  pl.BlockSpec((1, d), lambda i, kk: (0, 0)),
                  pl.BlockSpec((d, LANES), lambda i, kk: (0, 0)),
                  pl.BlockSpec((1, LANES), lambda i, kk: (0, 0))],
        out_specs=[pl.BlockSpec((tm, d), lambda i, kk: (i, 0)),
                   pl.BlockSpec((tm, LANES), lambda i, kk: (i, 0))],
        out_shape=[jax.ShapeDtypeStruct((n, d), F32), jax.ShapeDtypeStruct((n, LANES), F32)],
        scratch_shapes=[pltpu.VMEM((tm, d), F32), pltpu.VMEM((tm, d), BF16)],
        compiler_params=_cparams(("parallel", "arbitrary"), 56),
        name="out_proj_ln1",
    )(conv_out, attn_out, w_out_b, x2, ln_g, ln_b, w_router, b_router)


def _first_argmax(vals, lane, valid):
    masked = jnp.where(valid, vals, -jnp.inf)
    mx = jnp.max(masked, axis=1, keepdims=True)
    idx = jnp.min(jnp.where(valid & (masked == mx), lane, LANES), axis=1, keepdims=True)
    return mx, idx


def _router_kernel(lg_ref, e_ref, gate_ref):
    lg = lg_ref[...]
    lane = lax.broadcasted_iota(I32, lg.shape, 1)
    is_group = lane < N_GROUPS
    g_max, g_sel = _first_argmax(lg, lane, is_group)
    g_den = jnp.sum(jnp.where(is_group, jnp.exp(lg - g_max), 0.0), axis=1, keepdims=True)
    p_group = 1.0 / g_den
    lo = N_GROUPS + g_sel * EXPERTS_PER_GROUP
    in_group = (lane >= lo) & (lane < lo + EXPERTS_PER_GROUP)
    e_max, _ = _first_argmax(lg, lane, in_group)
    e_exp = jnp.where(in_group, jnp.exp(lg - e_max), 0.0)
    prob = e_exp / jnp.sum(e_exp, axis=1, keepdims=True)
    p1, i1 = _first_argmax(prob, lane, in_group)
    p2, i2 = _first_argmax(prob, lane, in_group & (lane != i1))
    norm = p_group / (p1 + p2)
    e_ref[...] = jnp.where(lane == 0, i1 - N_GROUPS, jnp.where(lane == 1, i2 - N_GROUPS, 0))
    gate_ref[...] = jnp.where(lane == 0, p1 * norm, jnp.where(lane == 1, p2 * norm, 0.0))


def _route(logits):
    n = logits.shape[0]
    tm = min(1024, n)
    spec = pl.BlockSpec((tm, LANES), lambda i: (i, 0))
    return pl.pallas_call(
        _router_kernel,
        grid=(n // tm,),
        in_specs=[spec],
        out_specs=[spec, spec],
        out_shape=[jax.ShapeDtypeStruct((n, LANES), I32), jax.ShapeDtypeStruct((n, LANES), F32)],
        compiler_params=_cparams(("parallel",), 32),
        name="router",
    )(logits)


def _row_copy(src_hbm, row, dst_ref, slot, sem):
    return pltpu.make_async_copy(src_hbm.at[pl.ds(row, 1), :], dst_ref.at[pl.ds(slot, 1), :], sem)


def _gather_rows(src_hbm, idx_ref, idx_row, dst_ref, sem, count):
    def issue(r, carry):
        _row_copy(src_hbm, idx_ref[0, idx_row, r], dst_ref, r, sem).start()
        return carry
    lax.fori_loop(0, count, issue, 0)

    def drain(r, carry):
        _row_copy(src_hbm, 0, dst_ref, r, sem).wait()
        return carry
    lax.fori_loop(0, count, drain, 0)


def _gather_cast_kernel(tok_ref, h_hbm, o_ref, buf_ref, sem):
    _gather_rows(h_hbm, tok_ref, 0, buf_ref, sem, buf_ref.shape[0])
    o_ref[...] = buf_ref[...].astype(o_ref.dtype)


def _gather_tokens(h1, buf_tok):
    p = buf_tok.shape[0]
    d = h1.shape[1]
    nblk = p // MOE_ROWS
    return pl.pallas_call(
        _gather_cast_kernel,
        grid=(nblk,),
        in_specs=[pl.BlockSpec((1, 1, MOE_ROWS), lambda i: (i, 0, 0), memory_space=pltpu.SMEM),
                  pl.BlockSpec(memory_space=pl.ANY)],
        out_specs=pl.BlockSpec((MOE_ROWS, d), lambda i: (i, 0)),
        out_shape=jax.ShapeDtypeStruct((p, d), BF16),
        scratch_shapes=[pltpu.VMEM((MOE_ROWS, d), F32), pltpu.SemaphoreType.DMA(())],
        compiler_params=_cparams(("arbitrary",), 32),
        name="moe_gather",
    )(buf_tok.reshape(nblk, 1, MOE_ROWS), h1)


def _combine_kernel(pos_ref, y_hbm, h_ref, g_ref, b_ref, o_ref, buf0_ref, buf1_ref, sem, *, alpha):
    rows = buf0_ref.shape[0]
    _gather_rows(y_hbm, pos_ref, 0, buf0_ref, sem, rows)
    _gather_rows(y_hbm, pos_ref, 1, buf1_ref, sem, rows)

    def norm_rows(i, carry):
        r0 = pl.multiple_of(i * SUBLANES, SUBLANES)
        sl = pl.ds(r0, SUBLANES)
        z = alpha * h_ref[sl, :] + (buf0_ref[sl, :] + buf1_ref[sl, :])
        o_ref[sl, :] = _layer_norm_rows(z, g_ref[...], b_ref[...])
        return carry
    lax.fori_loop(0, rows // SUBLANES, norm_rows, 0)


def _combine_ln(y, pos2, h1, ln_g, ln_b, alpha):
    n, d = h1.shape
    tb = min(128, n)
    nblk = n // tb
    pos_blocks = pos2.reshape(nblk, tb, 2).transpose(0, 2, 1)
    kern = functools.partial(_combine_kernel, alpha=alpha)
    return pl.pallas_call(
        kern,
        grid=(nblk,),
        in_specs=[pl.BlockSpec((1, 2, tb), lambda i: (i, 0, 0), memory_space=pltpu.SMEM),
                  pl.BlockSpec(memory_space=pl.ANY),
                  pl.BlockSpec((tb, d), lambda i: (i, 0)),
                  pl.BlockSpec((1, d), lambda i: (0, 0)),
                  pl.BlockSpec((1, d), lambda i: (0, 0))],
        out_specs=pl.BlockSpec((tb, d), lambda i: (i, 0)),
        out_shape=jax.ShapeDtypeStruct((n, d), F32),
        scratch_shapes=[pltpu.VMEM((tb, d), F32), pltpu.VMEM((tb, d), F32), pltpu.SemaphoreType.DMA(())],
        compiler_params=_cparams(("arbitrary",), 32),
        name="moe_combine_ln2",
    )(pos_blocks, y, h1, ln_g, ln_b)


def _cast_weight(dst_ref, src_ref):
    rows = src_ref.shape[1]
    step = min(256, rows)

    def body(i, carry):
        r0 = pl.multiple_of(i * step, step)
        dst_ref[pl.ds(r0, step), :] = src_ref[0, pl.ds(r0, step), :].astype(BF16)
        return carry
    lax.fori_loop(0, rows // step, body, 0)


ITEM_COMPUTE = 1
ITEM_NEW_WEIGHTS = 2
ITEM_ZERO_FILL = 4


def _moe_up_kernel(e_ref, cw_ref, bi_ref, bo_ref, co_ref, flag_ref, x_ref, w1_ref, w3_ref, o_ref, w1b_ref, w3b_ref):
    flag = flag_ref[pl.program_id(0)]

    @pl.when((flag & ITEM_NEW_WEIGHTS) != 0)
    def _():
        _cast_weight(w1b_ref, w1_ref)
        _cast_weight(w3b_ref, w3_ref)

    @pl.when((flag & ITEM_COMPUTE) != 0)
    def _():
        x = x_ref[...]
        a = jnp.dot(x, w1b_ref[...], preferred_element_type=F32)
        b = jnp.dot(x, w3b_ref[...], preferred_element_type=F32)
        o_ref[...] = (a * jax.nn.sigmoid(a) * b).astype(o_ref.dtype)

    @pl.when((flag & ITEM_ZERO_FILL) != 0)
    def _():
        o_ref[...] = jnp.zeros(o_ref.shape, o_ref.dtype)


def _moe_down_kernel(e_ref, cw_ref, bi_ref, bo_ref, co_ref, flag_ref, h_ref, w2_ref, gate_ref, o_ref, w2b_ref):
    flag = flag_ref[pl.program_id(0)]

    @pl.when((flag & ITEM_NEW_WEIGHTS) != 0)
    def _():
        _cast_weight(w2b_ref, w2_ref)

    @pl.when((flag & ITEM_COMPUTE) != 0)
    def _():
        y = jnp.dot(h_ref[...], w2b_ref[...], preferred_element_type=F32)
        o_ref[...] = y * gate_ref[...]

    @pl.when((flag & ITEM_ZERO_FILL) != 0)
    def _():
        o_ref[...] = jnp.zeros(o_ref.shape, o_ref.dtype)


def _work_items(blocks_per_expert, block_start, n_chunks, n_blocks):
    n_items = n_chunks * n_blocks
    per_e = n_chunks * blocks_per_expert
    end = jnp.cumsum(per_e)
    start = end - per_e
    it = jnp.arange(n_items, dtype=I32)
    total = end[-1]
    used_blocks = total // n_chunks
    itc = jnp.minimum(it, total - 1)
    e = jnp.searchsorted(end, itc, side="right").astype(I32)
    local = itc - start[e]
    nb = jnp.maximum(blocks_per_expert[e], 1)
    c = local // nb
    r = local - c * nb
    b_in = block_start[e] + r
    active = it < total
    spare = jnp.maximum(it - total, 0)
    b_out = jnp.where(active, b_in, used_blocks + spare // n_chunks)
    c_out = jnp.where(active, c, spare % n_chunks)
    flag = jnp.where(active, ITEM_COMPUTE + ITEM_NEW_WEIGHTS * (r == 0).astype(I32), ITEM_ZERO_FILL)
    return tuple(v.astype(I32) for v in (e, c, b_in, b_out, c_out, flag))


def _moe_up(xs, w1, w3, items, f_chunk):
    p, d = xs.shape
    f = w1.shape[2]
    wspec = pl.BlockSpec((1, d, f_chunk), lambda it, e, cw, bi, bo, co, fl: (e[it], 0, cw[it]))
    return pl.pallas_call(
        _moe_up_kernel,
        grid_spec=pltpu.PrefetchScalarGridSpec(
            num_scalar_prefetch=6,
            grid=(items[0].shape[0],),
            in_specs=[pl.BlockSpec((MOE_ROWS, d), lambda it, e, cw, bi, bo, co, fl: (bi[it], 0)), wspec, wspec],
            out_specs=pl.BlockSpec((MOE_ROWS, f_chunk), lambda it, e, cw, bi, bo, co, fl: (bo[it], co[it])),
            scratch_shapes=[pltpu.VMEM((d, f_chunk), BF16), pltpu.VMEM((d, f_chunk), BF16)]),
        out_shape=jax.ShapeDtypeStruct((p, f), BF16),
        compiler_params=_cparams(("arbitrary",), 48),
        name="moe_up",
    )(*items, xs, w1, w3)


def _moe_down(hmid, w2, gate_col, items, d_chunk):
    p, f = hmid.shape
    d = w2.shape[2]
    return pl.pallas_call(
        _moe_down_kernel,
        grid_spec=pltpu.PrefetchScalarGridSpec(
            num_scalar_prefetch=6,
            grid=(items[0].shape[0],),
            in_specs=[pl.BlockSpec((MOE_ROWS, f), lambda it, e, cw, bi, bo, co, fl: (bi[it], 0)),
                      pl.BlockSpec((1, f, d_chunk), lambda it, e, cw, bi, bo, co, fl: (e[it], 0, cw[it])),
                      pl.BlockSpec((MOE_ROWS, 1), lambda it, e, cw, bi, bo, co, fl: (bi[it], 0))],
            out_specs=pl.BlockSpec((MOE_ROWS, d_chunk), lambda it, e, cw, bi, bo, co, fl: (bo[it], co[it])),
            scratch_shapes=[pltpu.VMEM((f, d_chunk), BF16)]),
        out_shape=jax.ShapeDtypeStruct((p, d), F32),
        compiler_params=_cparams(("arbitrary",), 48),
        name="moe_down",
    )(*items, hmid, w2, gate_col)


def _hier_moe_ln(h1, logits, w1, w3, w2, ln_g, ln_b, alpha):
    n, d = h1.shape
    e_lanes, gate_lanes = _route(logits)
    expert = e_lanes[:, :2]
    gates = gate_lanes[:, :2]
    a = 2 * n
    e_flat = expert.reshape(a)
    onehot = (e_flat[:, None] == jnp.arange(N_EXPERTS, dtype=I32)[None, :]).astype(I32)
    csum = jnp.cumsum(onehot, axis=0)
    rank = jnp.take_along_axis(csum, e_flat[:, None], axis=1)[:, 0] - 1
    counts = csum[-1]
    blocks_per_expert = (counts + MOE_ROWS - 1) // MOE_ROWS
    block_start = jnp.cumsum(blocks_per_expert) - blocks_per_expert
    pos = block_start[e_flat] * MOE_ROWS + rank
    n_blocks = (a + MOE_ROWS - 1) // MOE_ROWS + N_EXPERTS
    p = n_blocks * MOE_ROWS
    buf_tok = jnp.zeros((p,), I32).at[pos].set(jnp.arange(a, dtype=I32) // 2)
    buf_gate = jnp.zeros((p,), F32).at[pos].set(gates.reshape(a))

    f = w1.shape[2]
    f_chunk = min(256, f)
    d_chunk = min(2048, d)
    xs = _gather_tokens(h1, buf_tok)
    hmid = _moe_up(xs, w1, w3, _work_items(blocks_per_expert, block_start, f // f_chunk, n_blocks), f_chunk)
    y = _moe_down(hmid, w2, buf_gate[:, None],
                  _work_items(blocks_per_expert, block_start, d // d_chunk, n_blocks), d_chunk)
    return _combine_ln(y, pos.reshape(n, 2), h1, ln_g, ln_b, alpha)


def kernel(x, w_in, conv_dw_w, conv_dw_b, conv_ln_g, conv_ln_b, w_out, ln1_g, ln1_b, w_router_group, b_router_group, w_router_expert, b_router_expert, w_expert_gate, w_expert_up, w_expert_down, ln2_g, ln2_b):
    b, s, d = x.shape
    depth = w_in.shape[0]
    assert depth == 1
    alpha = float((2.0 * depth) ** 0.25)
    n = b * s
    dc = conv_dw_w.shape[2]
    d_attn = N_HEADS * HEAD_DIM
    d_kv = N_KV_HEADS * HEAD_DIM
    d_qi = IDX_HEADS * IDX_DIM
    n_small = IDX_DIM + IDX_HEADS
    assert w_in.shape[2] == 2 * dc + d_attn + 2 * d_kv + d_qi + n_small

    x2 = x.reshape(n, d)
    xb = x2.astype(BF16)
    w = w_in[0]
    o_qkvi = 2 * dc
    o_small = o_qkvi + d_attn + 2 * d_kv + d_qi
    wa = w[:, :dc].astype(BF16)
    wg = w[:, dc:2 * dc].astype(BF16)
    w_qkvi = w[:, o_qkvi:o_small].astype(BF16)
    w_small = jnp.pad(w[:, o_small:], ((0, 0), (0, LANES - n_small))).astype(BF16)
    qkvi_scale = jnp.concatenate([jnp.full((d_attn,), HEAD_DIM ** -0.5, F32), jnp.ones((2 * d_kv,), F32),
                                  jnp.full((d_qi,), IDX_DIM ** -0.5, F32)])[None, :]

    glu = _proj_glu(xb, wa, wg)
    qkvi = _proj_scale(xb, w_qkvi, qkvi_scale, BF16, "proj_qkvi")
    small = _proj_scale(xb, w_small, jnp.ones((1, LANES), F32), F32, "proj_idx")

    conv_out = _conformer_conv(glu.reshape(b, s, dc), conv_dw_w[0], conv_dw_b, conv_ln_g, conv_ln_b)
    attn_out = _dsa_attention(qkvi.reshape(b, s, -1), small.reshape(b, s, LANES), b, s)

    n_route = N_GROUPS + N_EXPERTS
    w_router = jnp.pad(jnp.concatenate([w_router_group[0], w_router_expert[0]], axis=1),
                       ((0, 0), (0, LANES - n_route))).astype(BF16)
    b_router = jnp.pad(jnp.concatenate([b_router_group[0], b_router_expert[0]]), (0, LANES - n_route))[None, :]
    h1, logits = _out_proj_ln(conv_out.reshape(n, dc), attn_out.reshape(n, d_attn), w_out[0].astype(BF16),
                              x2, ln1_g, ln1_b, w_router, b_router, alpha)

    out = _hier_moe_ln(h1, logits, w_expert_gate[0], w_expert_up[0], w_expert_down[0], ln2_g, ln2_b, alpha)
    return out.reshape(b, s, d)
```

```python
import functools
import math

import jax
import jax.numpy as jnp
from jax import lax
from jax.experimental import pallas as pl
from jax.experimental.pallas import tpu as pltpu

F32 = jnp.float32
BF16 = jnp.bfloat16
I32 = jnp.int32

CONV_WIDTH = 31
N_HEADS = 16
HEAD_DIM = 128
N_KV_HEADS = 4
IDX_HEADS = 16
IDX_DIM = 64
TOPK_MAX = 256
N_GROUPS = 4
EXPERTS_PER_GROUP = 8
N_EXPERTS = N_GROUPS * EXPERTS_PER_GROUP
LN_EPS = 1e-5

LANES = 128
SUBLANES = 8
BF16_ROWS = 16
MIB = 1024 * 1024

Q_TILE = 128
KEY_CHUNK = 512
CONV_HALO = 32
MOE_ROWS = 256
COMBINE_ROWS = 128
INT_MIN = -(2 ** 31)
LOG2E = math.log2(math.e)
NEG_BIG = -1e30
M_INIT = -1e20


def _cparams(semantics, vmem_mib):
    return pltpu.CompilerParams(dimension_semantics=semantics, vmem_limit_bytes=vmem_mib * MIB)


def _layer_norm_rows(z, g, b):
    mu = jnp.mean(z, axis=-1, keepdims=True)
    zc = z - mu
    var = jnp.mean(zc * zc, axis=-1, keepdims=True)
    return zc * lax.rsqrt(var + LN_EPS) * g + b


def _mm_glu_kernel(x_ref, wa_ref, wg_ref, o_ref):
    x = x_ref[...]
    a = jnp.dot(x, wa_ref[...], preferred_element_type=F32)
    g = jnp.dot(x, wg_ref[...], preferred_element_type=F32)
    o_ref[...] = (a * jax.nn.sigmoid(g)).astype(o_ref.dtype)


def _mm_scale_kernel(x_ref, w_ref, s_ref, o_ref):
    acc = jnp.dot(x_ref[...], w_ref[...], preferred_element_type=F32)
    o_ref[...] = (acc * s_ref[...]).astype(o_ref.dtype)


def _mm_tiles(n, k, cols):
    tm = min(1024, n)
    tn = min(512, cols)
    assert n % tm == 0 and cols % tn == 0
    return tm, tn


def _proj_glu(xb, wa, wg):
    n, k = xb.shape
    cols = wa.shape[1]
    tm, tn = _mm_tiles(n, k, cols)
    return pl.pallas_call(
        _mm_glu_kernel,
        grid=(n // tm, cols // tn),
        in_specs=[pl.BlockSpec((tm, k), lambda i, j: (i, 0)),
                  pl.BlockSpec((k, tn), lambda i, j: (0, j)),
                  pl.BlockSpec((k, tn), lambda i, j: (0, j))],
        out_specs=pl.BlockSpec((tm, tn), lambda i, j: (i, j)),
        out_shape=jax.ShapeDtypeStruct((n, cols), F32),
        compiler_params=_cparams(("parallel", "arbitrary"), 48),
        name="proj_glu",
    )(xb, wa, wg)


def _proj_scale(xb, w, scale, out_dtype, name):
    n, k = xb.shape
    cols = w.shape[1]
    tm, tn = _mm_tiles(n, k, cols)
    return pl.pallas_call(
        _mm_scale_kernel,
        grid=(n // tm, cols // tn),
        in_specs=[pl.BlockSpec((tm, k), lambda i, j: (i, 0)),
                  pl.BlockSpec((k, tn), lambda i, j: (0, j)),
                  pl.BlockSpec((1, tn), lambda i, j: (0, j))],
        out_specs=pl.BlockSpec((tm, tn), lambda i, j: (i, j)),
        out_shape=jax.ShapeDtypeStruct((n, cols), out_dtype),
        compiler_params=_cparams(("parallel", "arbitrary"), 48),
        name=name,
    )(xb, w, scale)


def _conv_kernel(cur_ref, prev_ref, w_ref, b_ref, g_ref, beta_ref, o_ref, ext_ref, y_ref, *, rows, lane_chunk):
    t = pl.program_id(1)
    dc = cur_ref.shape[2]
    row_sub = 32
    ext_ref[0:CONV_HALO, :] = jnp.where(t > 0, prev_ref[0], 0.0)
    ext_ref[CONV_HALO:CONV_HALO + rows, :] = cur_ref[0]
    first = CONV_HALO - (CONV_WIDTH - 1)

    for r0 in range(0, rows, row_sub):
        for lc in range(dc // lane_chunk):
            cols = slice(lc * lane_chunk, (lc + 1) * lane_chunk)
            acc = jnp.zeros((row_sub, lane_chunk), F32)
            for j in range(CONV_WIDTH):
                acc = acc + ext_ref[r0 + first + j:r0 + first + j + row_sub, cols] * w_ref[j:j + 1, cols]
            y_ref[r0:r0 + row_sub, cols] = acc + b_ref[:, cols]

    def norm_rows(i, carry):
        r0 = pl.multiple_of(i * BF16_ROWS, BF16_ROWS)
        yn = _layer_norm_rows(y_ref[pl.ds(r0, BF16_ROWS), :], g_ref[...], beta_ref[...])
        o_ref[0, pl.ds(r0, BF16_ROWS), :] = (yn * jax.nn.sigmoid(yn)).astype(o_ref.dtype)
        return carry

    lax.fori_loop(0, rows // BF16_ROWS, norm_rows, 0)


def _conformer_conv(glu, w_dw, b_dw, g_ln, b_ln):
    b, s, dc = glu.shape
    rows = min(128, s)
    halo_blocks = rows // CONV_HALO
    lane_chunk = min(512, dc)
    kern = functools.partial(_conv_kernel, rows=rows, lane_chunk=lane_chunk)
    return pl.pallas_call(
        kern,
        grid=(b, s // rows),
        in_specs=[pl.BlockSpec((1, rows, dc), lambda bi, t: (bi, t, 0)),
                  pl.BlockSpec((1, CONV_HALO, dc), lambda bi, t: (bi, jnp.maximum(t * halo_blocks - 1, 0), 0)),
                  pl.BlockSpec((CONV_WIDTH, dc), lambda bi, t: (0, 0)),
                  pl.BlockSpec((1, dc), lambda bi, t: (0, 0)),
                  pl.BlockSpec((1, dc), lambda bi, t: (0, 0)),
                  pl.BlockSpec((1, dc), lambda bi, t: (0, 0))],
        out_specs=pl.BlockSpec((1, rows, dc), lambda bi, t: (bi, t, 0)),
        out_shape=jax.ShapeDtypeStruct((b, s, dc), BF16),
        scratch_shapes=[pltpu.VMEM((CONV_HALO + rows, dc), F32), pltpu.VMEM((rows, dc), F32)],
        compiler_params=_cparams(("parallel", "arbitrary"), 32),
        name="conformer_conv",
    )(glu, glu, w_dw, b_dw, g_ln, b_ln)


def _dsa_kernel(q_ref, qi_ref, k_ref, v_ref, kidx_ref, w_ref, o_ref,
                ke_ref, ko_ref, vt_ref, key_ref, nd_ref, qs_ref, qis_ref, m_ref, l_ref, acc_ref,
                *, seq, chunk, topk):
    tb = pl.program_id(1)
    t0 = tb * Q_TILE
    n_chunks = (t0 + Q_TILE + chunk - 1) // chunk
    rep = N_HEADS // N_KV_HEADS
    nt = (((1,), (1,)), ((), ()))

    @pl.when(tb == 0)
    def _():
        def build(c, carry):
            r0 = pl.multiple_of(c * chunk, chunk)
            kx = kidx_ref[0, pl.ds(r0, chunk), :]
            lane = lax.broadcasted_iota(I32, kx.shape, 1)
            ke_ref[pl.ds(r0, chunk), :] = jnp.where(lane < IDX_DIM, kx, 0.0).astype(BF16)
            ko_ref[pl.ds(r0, chunk), :] = jnp.where(lane >= IDX_DIM, pltpu.roll(kx, IDX_DIM, 1), 0.0).astype(BF16)
            for g in range(N_KV_HEADS):
                vg = v_ref[0, pl.ds(r0, chunk), g * HEAD_DIM:(g + 1) * HEAD_DIM]
                vt_ref[c, g * HEAD_DIM:(g + 1) * HEAD_DIM, :] = vg.astype(F32).T.astype(BF16)
            return carry
        lax.fori_loop(0, seq // chunk, build, 0)

    for g in range(N_KV_HEADS):
        for r in range(rep):
            h = g * rep + r
            qs_ref[g, r * Q_TILE:(r + 1) * Q_TILE, :] = q_ref[0, :, h * HEAD_DIM:(h + 1) * HEAD_DIM]
    for jj in range(IDX_HEADS // 4):
        for half in range(2):
            pair = 2 * jj + half
            qis_ref[jj, half * Q_TILE:(half + 1) * Q_TILE, :] = qi_ref[0, :, pair * LANES:(pair + 1) * LANES]
    w_t = w_ref[0].T * (IDX_HEADS ** -0.5)

    key_row = lax.broadcasted_iota(I32, (chunk, Q_TILE), 0)
    q_pos = t0 + lax.broadcasted_iota(I32, (chunk, Q_TILE), 1)

    def score_chunk(c, carry):
        r0 = pl.multiple_of(c * chunk, chunk)
        ke = ke_ref[pl.ds(r0, chunk), :]
        ko = ko_ref[pl.ds(r0, chunk), :]
        acc = jnp.zeros((chunk, Q_TILE), F32)
        for jj in range(IDX_HEADS // 4):
            rhs = qis_ref[jj]
            de = lax.dot_general(ke, rhs, nt, preferred_element_type=F32)
            do = lax.dot_general(ko, rhs, nt, preferred_element_type=F32)
            for half in range(2):
                h_even = 2 * (2 * jj + half)
                cols = slice(half * Q_TILE, (half + 1) * Q_TILE)
                acc = acc + w_t[IDX_DIM + h_even:IDX_DIM + h_even + 1, :] * jnp.maximum(de[:, cols], 0.0)
                acc = acc + w_t[IDX_DIM + h_even + 1:IDX_DIM + h_even + 2, :] * jnp.maximum(do[:, cols], 0.0)
        bits = lax.bitcast_convert_type(acc, I32)
        key = bits ^ ((bits >> 31) & 0x7FFFFFFF)
        key_ref[c] = jnp.where(r0 + key_row <= q_pos, key, INT_MIN)
        return carry

    lax.fori_loop(0, n_chunks, score_chunk, 0)

    def bit_step(i, t_u):
        cand_u = t_u | (jnp.int32(1) << (31 - i))
        cand = cand_u ^ INT_MIN

        def count_chunk(c, cnts):
            cnts = list(cnts)
            for r in range(chunk // SUBLANES):
                slab = key_ref[c, r * SUBLANES:(r + 1) * SUBLANES, :]
                cnts[r % len(cnts)] = cnts[r % len(cnts)] + jnp.where(slab >= cand, 1.0, 0.0)
            return tuple(cnts)

        zero = jnp.zeros((SUBLANES, Q_TILE), F32)
        cnts = lax.fori_loop(0, n_chunks, count_chunk, (zero, zero, zero, zero))
        tot = jnp.sum((cnts[0] + cnts[1]) + (cnts[2] + cnts[3]), axis=0, keepdims=True)
        return jnp.where(tot >= float(topk), cand_u, t_u)

    t_u = lax.fori_loop(0, 32, bit_step, jnp.zeros((1, Q_TILE), I32))
    thr = jnp.maximum(t_u ^ INT_MIN, INT_MIN + 1)

    m_ref[...] = jnp.full(m_ref.shape, M_INIT, F32)
    l_ref[...] = jnp.zeros(l_ref.shape, F32)
    acc_ref[...] = jnp.zeros(acc_ref.shape, F32)

    def attend_chunk(c, carry):
        r0 = pl.multiple_of(c * chunk, chunk)
        dist = (r0 + key_row - q_pos).astype(F32)
        nd_ref[...] = jnp.where(key_ref[c] >= thr, dist, NEG_BIG)
        s_groups = []
        for g in range(N_KV_HEADS):
            kg = k_ref[0, pl.ds(r0, chunk), g * HEAD_DIM:(g + 1) * HEAD_DIM]
            s_groups.append(lax.dot_general(kg, qs_ref[g], nt, preferred_element_type=F32))
        for g in range(N_KV_HEADS):
            s_all = s_groups[g]
            probs = []
            alphas = []
            for r in range(rep):
                h = g * rep + r
                slope = float(2.0 ** (-8.0 * (h + 1) / N_HEADS)) * LOG2E
                sr = s_all[:, r * Q_TILE:(r + 1) * Q_TILE] + slope * nd_ref[...]
                m_old = m_ref[h]
                m_new = jnp.maximum(m_old, jnp.max(sr, axis=0, keepdims=True))
                alpha = jnp.exp2(m_old - m_new)
                p = jnp.exp2(sr - m_new)
                l_ref[h] = alpha * l_ref[h] + jnp.sum(p, axis=0, keepdims=True)
                m_ref[h] = m_new
                probs.append(p.astype(BF16))
                alphas.append(alpha)
            vt = vt_ref[c, g * HEAD_DIM:(g + 1) * HEAD_DIM, :]
            pv = jnp.dot(vt, jnp.concatenate(probs, axis=1), preferred_element_type=F32)
            acc_ref[g] = jnp.concatenate(alphas, axis=1) * acc_ref[g] + pv
        return carry

    lax.fori_loop(0, n_chunks, attend_chunk, 0)

    for g in range(N_KV_HEADS):
        for r in range(rep):
            h = g * rep + r
            o_t = acc_ref[g, :, r * Q_TILE:(r + 1) * Q_TILE] * (1.0 / l_ref[h])
            o_ref[0, :, h * HEAD_DIM:(h + 1) * HEAD_DIM] = o_t.T.astype(o_ref.dtype)


def _dsa_attention(qkvi, small, b, s):
    d_attn = N_HEADS * HEAD_DIM
    d_kv = N_KV_HEADS * HEAD_DIM
    d_qi = IDX_HEADS * IDX_DIM
    rep = N_HEADS // N_KV_HEADS
    chunk = min(KEY_CHUNK, s)
    topk = min(TOPK_MAX, s // 4)
    assert s % chunk == 0 and chunk % Q_TILE == 0 and d_attn % d_qi == 0 and d_attn % d_kv == 0
    assert Q_TILE == LANES and HEAD_DIM == LANES and 2 * IDX_DIM == LANES
    kern = functools.partial(_dsa_kernel, seq=s, chunk=chunk, topk=topk)
    return pl.pallas_call(
        kern,
        grid=(b, s // Q_TILE),
        in_specs=[pl.BlockSpec((1, Q_TILE, d_attn), lambda bi, t: (bi, t, 0)),
                  pl.BlockSpec((1, Q_TILE, d_qi), lambda bi, t: (bi, t, (d_attn + 2 * d_kv) // d_qi)),
                  pl.BlockSpec((1, s, d_kv), lambda bi, t: (bi, 0, d_attn // d_kv)),
                  pl.BlockSpec((1, s, d_kv), lambda bi, t: (bi, 0, d_attn // d_kv + 1)),
                  pl.BlockSpec((1, s, LANES), lambda bi, t: (bi, 0, 0)),
                  pl.BlockSpec((1, Q_TILE, LANES), lambda bi, t: (bi, t, 0))],
        out_specs=pl.BlockSpec((1, Q_TILE, d_attn), lambda bi, t: (bi, t, 0)),
        out_shape=jax.ShapeDtypeStruct((b, s, d_attn), BF16),
        scratch_shapes=[pltpu.VMEM((s, LANES), BF16),
                        pltpu.VMEM((s, LANES), BF16),
                        pltpu.VMEM((s // chunk, d_kv, chunk), BF16),
                        pltpu.VMEM((s // chunk, chunk, Q_TILE), I32),
                        pltpu.VMEM((chunk, Q_TILE), F32),
                        pltpu.VMEM((N_KV_HEADS, rep * Q_TILE, HEAD_DIM), BF16),
                        pltpu.VMEM((IDX_HEADS // 4, 2 * Q_TILE, LANES), BF16),
                        pltpu.VMEM((N_HEADS, 1, Q_TILE), F32),
                        pltpu.VMEM((N_HEADS, 1, Q_TILE), F32),
                        pltpu.VMEM((N_KV_HEADS, HEAD_DIM, rep * Q_TILE), F32)],
        compiler_params=_cparams(("parallel", "arbitrary"), 48),
        name="dsa_attention",
    )(qkvi, qkvi, qkvi, qkvi, small, small)


def _outproj_kernel(conv_ref, attn_ref, w_ref, x_ref, g_ref, b_ref, wr_ref, br_ref, h_ref, lg_ref,
                    acc_ref, hb_ref, *, n_conv_steps, n_steps, alpha):
    kk = pl.program_id(1)

    @pl.when(kk == 0)
    def _():
        acc_ref[...] = jnp.zeros(acc_ref.shape, F32)

    @pl.when(kk < n_conv_steps)
    def _():
        acc_ref[...] += jnp.dot(conv_ref[...], w_ref[...], preferred_element_type=F32)

    @pl.when(kk >= n_conv_steps)
    def _():
        acc_ref[...] += jnp.dot(attn_ref[...], w_ref[...], preferred_element_type=F32)

    @pl.when(kk == n_steps - 1)
    def _():
        def rows(i, carry):
            r0 = pl.multiple_of(i * BF16_ROWS, BF16_ROWS)
            z = alpha * x_ref[pl.ds(r0, BF16_ROWS), :] + acc_ref[pl.ds(r0, BF16_ROWS), :]
            h = _layer_norm_rows(z, g_ref[...], b_ref[...])
            h_ref[pl.ds(r0, BF16_ROWS), :] = h
            hb_ref[pl.ds(r0, BF16_ROWS), :] = h.astype(BF16)
            return carry
        lax.fori_loop(0, acc_ref.shape[0] // BF16_ROWS, rows, 0)
        lg_ref[...] = jnp.dot(hb_ref[...], wr_ref[...], preferred_element_type=F32) + br_ref[...]


def _out_proj_ln(conv_out, attn_out, w_out_b, x2, ln_g, ln_b, w_router, b_router, alpha):
    n, d = x2.shape
    dc = conv_out.shape[1]
    da = attn_out.shape[1]
    tm = min(512, n)
    tk = min(512, dc)
    assert dc % tk == 0 and da % tk == 0 and n % tm == 0
    n_conv = dc // tk
    n_steps = n_conv + da // tk
    kern = functools.partial(_outproj_kernel, n_conv_steps=n_conv, n_steps=n_steps, alpha=alpha)
    return pl.pallas_call(
        kern,
        grid=(n // tm, n_steps),
        in_specs=[pl.BlockSpec((tm, tk), lambda i, kk: (i, jnp.minimum(kk, n_conv - 1))),
                  pl.BlockSpec((tm, tk), lambda i, kk: (i, jnp.maximum(kk - n_conv, 0))),
                  pl.BlockSpec((tk, d), lambda i, kk: (kk, 0)),
                  pl.BlockSpec((tm, d), lambda i, kk: (i, 0)),
                  pl.BlockSpec((1, d), lambda i, kk: (0, 0)),
                  pl.BlockSpec((1, d), lambda i, kk: (0, 0)),
                  pl.BlockSpec((d, LANES), lambda i, kk: (0, 0)),
                  pl.BlockSpec((1, LANES), lambda i, kk: (0, 0))],
        out_specs=[pl.BlockSpec((tm, d), lambda i, kk: (i, 0)),
                   pl.BlockSpec((tm, LANES), lambda i, kk: (i, 0))],
        out_shape=[jax.ShapeDtypeStruct((n, d), F32), jax.ShapeDtypeStruct((n, LANES), F32)],
        scratch_shapes=[pltpu.VMEM((tm, d), F32), pltpu.VMEM((tm, d), BF16)],
        compiler_params=_cparams(("parallel", "arbitrary"), 56),
        name="out_proj_ln1",
    )(conv_out, attn_out, w_out_b, x2, ln_g, ln_b, w_router, b_router)


def _first_argmax(vals, lane, valid):
    masked = jnp.where(valid, vals, -jnp.inf)
    mx = jnp.max(masked, axis=1, keepdims=True)
    idx = jnp.min(jnp.where(valid & (masked == mx), lane, LANES), axis=1, keepdims=True)
    return mx, idx


def _router_kernel(lg_ref, e_ref, gate_ref):
    lg = lg_ref[...]
    lane = lax.broadcasted_iota(I32, lg.shape, 1)
    is_group = lane < N_GROUPS
    g_max, g_sel = _first_argmax(lg, lane, is_group)
    g_den = jnp.sum(jnp.where(is_group, jnp.exp(lg - g_max), 0.0), axis=1, keepdims=True)
    p_group = 1.0 / g_den
    lo = N_GROUPS + g_sel * EXPERTS_PER_GROUP
    in_group = (lane >= lo) & (lane < lo + EXPERTS_PER_GROUP)
    e_max, _ = _first_argmax(lg, lane, in_group)
    e_exp = jnp.where(in_group, jnp.exp(lg - e_max), 0.0)
    prob = e_exp / jnp.sum(e_exp, axis=1, keepdims=True)
    p1, i1 = _first_argmax(prob, lane, in_group)
    p2, i2 = _first_argmax(prob, lane, in_group & (lane != i1))
    norm = p_group / (p1 + p2)
    e_ref[...] = jnp.where(lane == 0, i1 - N_GROUPS, jnp.where(lane == 1, i2 - N_GROUPS, 0))
    gate_ref[...] = jnp.where(lane == 0, p1 * norm, jnp.where(lane == 1, p2 * norm, 0.0))


def _route(logits):
    n = logits.shape[0]
    tm = min(1024, n)
    spec = pl.BlockSpec((tm, LANES), lambda i: (i, 0))
    return pl.pallas_call(
        _router_kernel,
        grid=(n // tm,),
        in_specs=[spec],
        out_specs=[spec, spec],
        out_shape=[jax.ShapeDtypeStruct((n, LANES), I32), jax.ShapeDtypeStruct((n, LANES), F32)],
        compiler_params=_cparams(("parallel",), 32),
        name="router",
    )(logits)


def _start_row_copies(src_hbm, idx_ref, idx_row, dst_ref, slot, row0, sem, count):
    def issue(r, carry):
        pltpu.make_async_copy(src_hbm.at[pl.ds(idx_ref[0, idx_row, r], 1), :],
                              dst_ref.at[slot, pl.ds(row0 + r, 1), :], sem.at[slot]).start()
        return carry
    lax.fori_loop(0, count, issue, 0, unroll=8)


def _wait_slot(src_hbm, dst_ref, slot, sem):
    rows = dst_ref.shape[1]
    pltpu.make_async_copy(src_hbm.at[pl.ds(0, rows), :], dst_ref.at[slot], sem.at[slot]).wait()


def _gather_cast_kernel(used_ref, tok_ref, tok_next_ref, h_hbm, o_ref, buf_ref, sem):
    i = pl.program_id(0)
    slot = i % 2
    n_used = used_ref[0]
    rows = buf_ref.shape[1]

    @pl.when(i == 0)
    def _():
        _start_row_copies(h_hbm, tok_ref, 0, buf_ref, 0, 0, sem, rows)

    @pl.when(i + 1 < n_used)
    def _():
        _start_row_copies(h_hbm, tok_next_ref, 0, buf_ref, 1 - slot, 0, sem, rows)

    @pl.when(i < n_used)
    def _():
        _wait_slot(h_hbm, buf_ref, slot, sem)
        o_ref[...] = buf_ref[slot].astype(o_ref.dtype)

    @pl.when(i >= n_used)
    def _():
        o_ref[...] = jnp.zeros(o_ref.shape, o_ref.dtype)


def _gather_tokens(h1, buf_tok, n_used):
    p = buf_tok.shape[0]
    d = h1.shape[1]
    nblk = p // MOE_ROWS
    tok_blocks = buf_tok.reshape(nblk, 1, MOE_ROWS)
    return pl.pallas_call(
        _gather_cast_kernel,
        grid_spec=pltpu.PrefetchScalarGridSpec(
            num_scalar_prefetch=1,
            grid=(nblk,),
            in_specs=[pl.BlockSpec((1, 1, MOE_ROWS), lambda i, u: (i, 0, 0), memory_space=pltpu.SMEM),
                      pl.BlockSpec((1, 1, MOE_ROWS), lambda i, u: (jnp.minimum(i + 1, nblk - 1), 0, 0),
                                   memory_space=pltpu.SMEM),
                      pl.BlockSpec(memory_space=pl.ANY)],
            out_specs=pl.BlockSpec((MOE_ROWS, d), lambda i, u: (i, 0)),
            scratch_shapes=[pltpu.VMEM((2, MOE_ROWS, d), F32), pltpu.SemaphoreType.DMA((2,))]),
        out_shape=jax.ShapeDtypeStruct((p, d), BF16),
        compiler_params=_cparams(("arbitrary",), 32),
        name="moe_gather",
    )(n_used.reshape(1), tok_blocks, tok_blocks, h1)


def _combine_kernel(pos_ref, pos_next_ref, y_hbm, h_ref, g_ref, b_ref, o_ref, buf_ref, sem, *, alpha):
    i = pl.program_id(0)
    n = pl.num_programs(0)
    slot = i % 2
    rows = h_ref.shape[0]

    def start(idx_ref, into):
        for k in range(2):
            _start_row_copies(y_hbm, idx_ref, k, buf_ref, into, k * rows, sem, rows)

    @pl.when(i == 0)
    def _():
        start(pos_ref, 0)

    @pl.when(i + 1 < n)
    def _():
        start(pos_next_ref, 1 - slot)

    _wait_slot(y_hbm, buf_ref, slot, sem)

    def norm_rows(j, carry):
        r0 = pl.multiple_of(j * SUBLANES, SUBLANES)
        ffn = buf_ref[slot, pl.ds(r0, SUBLANES), :] + buf_ref[slot, pl.ds(rows + r0, SUBLANES), :]
        z = alpha * h_ref[pl.ds(r0, SUBLANES), :] + ffn
        o_ref[pl.ds(r0, SUBLANES), :] = _layer_norm_rows(z, g_ref[...], b_ref[...])
        return carry
    lax.fori_loop(0, rows // SUBLANES, norm_rows, 0)


def _combine_ln(y, pos2, h1, ln_g, ln_b, alpha):
    n, d = h1.shape
    tb = min(COMBINE_ROWS, n)
    nblk = n // tb
    pos_blocks = pos2.reshape(nblk, tb, 2).transpose(0, 2, 1)
    kern = functools.partial(_combine_kernel, alpha=alpha)
    return pl.pallas_call(
        kern,
        grid=(nblk,),
        in_specs=[pl.BlockSpec((1, 2, tb), lambda i: (i, 0, 0), memory_space=pltpu.SMEM),
                  pl.BlockSpec((1, 2, tb), lambda i: (jnp.minimum(i + 1, nblk - 1), 0, 0), memory_space=pltpu.SMEM),
                  pl.BlockSpec(memory_space=pl.ANY),
                  pl.BlockSpec((tb, d), lambda i: (i, 0)),
                  pl.BlockSpec((1, d), lambda i: (0, 0)),
                  pl.BlockSpec((1, d), lambda i: (0, 0))],
        out_specs=pl.BlockSpec((tb, d), lambda i: (i, 0)),
        out_shape=jax.ShapeDtypeStruct((n, d), F32),
        scratch_shapes=[pltpu.VMEM((2, 2 * tb, d), F32), pltpu.SemaphoreType.DMA((2,))],
        compiler_params=_cparams(("arbitrary",), 32),
        name="moe_combine_ln2",
    )(pos_blocks, pos_blocks, y, h1, ln_g, ln_b)


def _cast_weight(dst_ref, src_ref):
    rows = src_ref.shape[1]
    step = min(256, rows)

    def body(i, carry):
        r0 = pl.multiple_of(i * step, step)
        dst_ref[pl.ds(r0, step), :] = src_ref[0, pl.ds(r0, step), :].astype(BF16)
        return carry
    lax.fori_loop(0, rows // step, body, 0)


ITEM_COMPUTE = 1
ITEM_NEW_WEIGHTS = 2
ITEM_ZERO_FILL = 4


def _moe_up_kernel(e_ref, cw_ref, bi_ref, bo_ref, co_ref, flag_ref, x_ref, w1_ref, w3_ref, o_ref, w1b_ref, w3b_ref):
    flag = flag_ref[pl.program_id(0)]

    @pl.when((flag & ITEM_NEW_WEIGHTS) != 0)
    def _():
        _cast_weight(w1b_ref, w1_ref)
        _cast_weight(w3b_ref, w3_ref)

    @pl.when((flag & ITEM_COMPUTE) != 0)
    def _():
        x = x_ref[...]
        a = jnp.dot(x, w1b_ref[...], preferred_element_type=F32)
        b = jnp.dot(x, w3b_ref[...], preferred_element_type=F32)
        o_ref[...] = (a * jax.nn.sigmoid(a) * b).astype(o_ref.dtype)

    @pl.when((flag & ITEM_ZERO_FILL) != 0)
    def _():
        o_ref[...] = jnp.zeros(o_ref.shape, o_ref.dtype)


def _moe_down_kernel(e_ref, cw_ref, bi_ref, bo_ref, co_ref, flag_ref, h_ref, w2_ref, gate_ref, o_ref, w2b_ref):
    flag = flag_ref[pl.program_id(0)]

    @pl.when((flag & ITEM_NEW_WEIGHTS) != 0)
    def _():
        _cast_weight(w2b_ref, w2_ref)

    @pl.when((flag & ITEM_COMPUTE) != 0)
    def _():
        y = jnp.dot(h_ref[...], w2b_ref[...], preferred_element_type=F32)
        o_ref[...] = y * gate_ref[...]

    @pl.when((flag & ITEM_ZERO_FILL) != 0)
    def _():
        o_ref[...] = jnp.zeros(o_ref.shape, o_ref.dtype)


def _work_items(blocks_per_expert, block_start, n_chunks, n_blocks):
    n_items = n_chunks * n_blocks
    per_e = n_chunks * blocks_per_expert
    end = jnp.cumsum(per_e)
    start = end - per_e
    it = jnp.arange(n_items, dtype=I32)
    total = end[-1]
    used_blocks = total // n_chunks
    itc = jnp.minimum(it, total - 1)
    e = jnp.searchsorted(end, itc, side="right").astype(I32)
    local = itc - start[e]
    nb = jnp.maximum(blocks_per_expert[e], 1)
    c = local // nb
    r = local - c * nb
    b_in = block_start[e] + r
    active = it < total
    spare = jnp.maximum(it - total, 0)
    b_out = jnp.where(active, b_in, used_blocks + spare // n_chunks)
    c_out = jnp.where(active, c, spare % n_chunks)
    flag = jnp.where(active, ITEM_COMPUTE + ITEM_NEW_WEIGHTS * (r == 0).astype(I32), ITEM_ZERO_FILL)
    return tuple(v.astype(I32) for v in (e, c, b_in, b_out, c_out, flag))


def _moe_up(xs, w1, w3, items, f_chunk):
    p, d = xs.shape
    f = w1.shape[2]
    wspec = pl.BlockSpec((1, d, f_chunk), lambda it, e, cw, bi, bo, co, fl: (e[it], 0, cw[it]))
    return pl.pallas_call(
        _moe_up_kernel,
        grid_spec=pltpu.PrefetchScalarGridSpec(
            num_scalar_prefetch=6,
            grid=(items[0].shape[0],),
            in_specs=[pl.BlockSpec((MOE_ROWS, d), lambda it, e, cw, bi, bo, co, fl: (bi[it], 0)), wspec, wspec],
            out_specs=pl.BlockSpec((MOE_ROWS, f_chunk), lambda it, e, cw, bi, bo, co, fl: (bo[it], co[it])),
            scratch_shapes=[pltpu.VMEM((d, f_chunk), BF16), pltpu.VMEM((d, f_chunk), BF16)]),
        out_shape=jax.ShapeDtypeStruct((p, f), BF16),
        compiler_params=_cparams(("arbitrary",), 56),
        name="moe_up",
    )(*items, xs, w1, w3)


def _moe_down(hmid, w2, gate_col, items, d_chunk):
    p, f = hmid.shape
    d = w2.shape[2]
    return pl.pallas_call(
        _moe_down_kernel,
        grid_spec=pltpu.PrefetchScalarGridSpec(
            num_scalar_prefetch=6,
            grid=(items[0].shape[0],),
            in_specs=[pl.BlockSpec((MOE_ROWS, f), lambda it, e, cw, bi, bo, co, fl: (bi[it], 0)),
                      pl.BlockSpec((1, f, d_chunk), lambda it, e, cw, bi, bo, co, fl: (e[it], 0, cw[it])),
                      pl.BlockSpec((MOE_ROWS, 1), lambda it, e, cw, bi, bo, co, fl: (bi[it], 0))],
            out_specs=pl.BlockSpec((MOE_ROWS, d_chunk), lambda it, e, cw, bi, bo, co, fl: (bo[it], co[it])),
            scratch_shapes=[pltpu.VMEM((f, d_chunk), BF16)]),
        out_shape=jax.ShapeDtypeStruct((p, d), F32),
        compiler_params=_cparams(("arbitrary",), 48),
        name="moe_down",
    )(*items, hmid, w2, gate_col)


def _hier_moe_ln(h1, logits, w1, w3, w2, ln_g, ln_b, alpha):
    n, d = h1.shape
    e_lanes, gate_lanes = _route(logits)
    expert = e_lanes[:, :2]
    gates = gate_lanes[:, :2]
    a = 2 * n
    e_flat = expert.reshape(a)
    onehot = (e_flat[:, None] == jnp.arange(N_EXPERTS, dtype=I32)[None, :]).astype(I32)
    csum = jnp.cumsum(onehot, axis=0)
    rank = jnp.take_along_axis(csum, e_flat[:, None], axis=1)[:, 0] - 1
    counts = csum[-1]
    blocks_per_expert = (counts + MOE_ROWS - 1) // MOE_ROWS
    block_start = jnp.cumsum(blocks_per_expert) - blocks_per_expert
    pos = block_start[e_flat] * MOE_ROWS + rank
    n_blocks = (a + MOE_ROWS - 1) // MOE_ROWS + N_EXPERTS
    p = n_blocks * MOE_ROWS
    buf_tok = jnp.zeros((p,), I32).at[pos].set(jnp.arange(a, dtype=I32) // 2)
    buf_gate = jnp.zeros((p,), F32).at[pos].set(gates.reshape(a))

    f = w1.shape[2]
    f_chunk = min(512, f)
    d_chunk = min(2048, d)
    xs = _gather_tokens(h1, buf_tok, jnp.sum(blocks_per_expert).astype(I32))
    hmid = _moe_up(xs, w1, w3, _work_items(blocks_per_expert, block_start, f // f_chunk, n_blocks), f_chunk)
    y = _moe_down(hmid, w2, buf_gate[:, None],
                  _work_items(blocks_per_expert, block_start, d // d_chunk, n_blocks), d_chunk)
    return _combine_ln(y, pos.reshape(n, 2), h1, ln_g, ln_b, alpha)


def kernel(x, w_in, conv_dw_w, conv_dw_b, conv_ln_g, conv_ln_b, w_out, ln1_g, ln1_b, w_router_group, b_router_group, w_router_expert, b_router_expert, w_expert_gate, w_expert_up, w_expert_down, ln2_g, ln2_b):
    b, s, d = x.shape
    depth = w_in.shape[0]
    assert depth == 1
    alpha = float((2.0 * depth) ** 0.25)
    n = b * s
    dc = conv_dw_w.shape[2]
    d_attn = N_HEADS * HEAD_DIM
    d_kv = N_KV_HEADS * HEAD_DIM
    d_qi = IDX_HEADS * IDX_DIM
    n_small = IDX_DIM + IDX_HEADS
    assert w_in.shape[2] == 2 * dc + d_attn + 2 * d_kv + d_qi + n_small

    x2 = x.reshape(n, d)
    xb = x2.astype(BF16)
    w = w_in[0]
    o_qkvi = 2 * dc
    o_small = o_qkvi + d_attn + 2 * d_kv + d_qi
    wa = w[:, :dc].astype(BF16)
    wg = w[:, dc:2 * dc].astype(BF16)
    w_qkvi = w[:, o_qkvi:o_small].astype(BF16)
    w_small = jnp.pad(w[:, o_small:], ((0, 0), (0, LANES - n_small))).astype(BF16)
    qkvi_scale = jnp.concatenate([jnp.full((d_attn,), HEAD_DIM ** -0.5 * LOG2E, F32), jnp.ones((2 * d_kv,), F32),
                                  jnp.full((d_qi,), IDX_DIM ** -0.5, F32)])[None, :]

    glu = _proj_glu(xb, wa, wg)
    qkvi = _proj_scale(xb, w_qkvi, qkvi_scale, BF16, "proj_qkvi")
    small = _proj_scale(xb, w_small, jnp.ones((1, LANES), F32), F32, "proj_idx")

    conv_out = _conformer_conv(glu.reshape(b, s, dc), conv_dw_w[0], conv_dw_b, conv_ln_g, conv_ln_b)
    attn_out = _dsa_attention(qkvi.reshape(b, s, -1), small.reshape(b, s, LANES), b, s)

    n_route = N_GROUPS + N_EXPERTS
    w_router = jnp.pad(jnp.concatenate([w_router_group[0], w_router_expert[0]], axis=1),
                       ((0, 0), (0, LANES - n_route))).astype(BF16)
    b_router = jnp.pad(jnp.concatenate([b_router_group[0], b_router_expert[0]]), (0, LANES - n_route))[None, :]
    h1, logits = _out_proj_ln(conv_out.reshape(n, dc), attn_out.reshape(n, d_attn), w_out[0].astype(BF16),
                              x2, ln1_g, ln1_b, w_router, b_router, alpha)

    out = _hier_moe_ln(h1, logits, w_expert_gate[0], w_expert_up[0], w_expert_down[0], ln2_g, ln2_b, alpha)
    return out.reshape(b, s, d)
```

```python
import functools
import math

import jax
import jax.numpy as jnp
from jax import lax
from jax.experimental import pallas as pl
from jax.experimental.pallas import tpu as pltpu

F32 = jnp.float32
BF16 = jnp.bfloat16
I32 = jnp.int32

CONV_WIDTH = 31
N_HEADS = 16
HEAD_DIM = 128
N_KV_HEADS = 4
IDX_HEADS = 16
IDX_DIM = 64
TOPK_MAX = 256
N_GROUPS = 4
EXPERTS_PER_GROUP = 8
N_EXPERTS = N_GROUPS * EXPERTS_PER_GROUP
LN_EPS = 1e-5

LANES = 128
SUBLANES = 8
BF16_ROWS = 16
MIB = 1024 * 1024

Q_TILE = 128
KEY_CHUNK = 512
CONV_HALO = 32
MOE_ROWS = 256
COMBINE_ROWS = 128
INT_MIN = -(2 ** 31)
LOG2E = math.log2(math.e)
NEG_BIG = -1e30
M_INIT = -1e20


def _cparams(semantics, vmem_mib):
    return pltpu.CompilerParams(dimension_semantics=semantics, vmem_limit_bytes=vmem_mib * MIB)


def _layer_norm_rows(z, g, b):
    mu = jnp.mean(z, axis=-1, keepdims=True)
    zc = z - mu
    var = jnp.mean(zc * zc, axis=-1, keepdims=True)
    return zc * lax.rsqrt(var + LN_EPS) * g + b


def _mm_glu_kernel(x_ref, wa_ref, wg_ref, o_ref):
    x = x_ref[...]
    a = jnp.dot(x, wa_ref[...], preferred_element_type=F32)
    g = jnp.dot(x, wg_ref[...], preferred_element_type=F32)
    o_ref[...] = (a * jax.nn.sigmoid(g)).astype(o_ref.dtype)


def _mm_scale_kernel(x_ref, w_ref, s_ref, o_ref):
    acc = jnp.dot(x_ref[...], w_ref[...], preferred_element_type=F32)
    o_ref[...] = (acc * s_ref[...]).astype(o_ref.dtype)


def _mm_tiles(n, k, cols):
    tm = min(1024, n)
    tn = min(512, cols)
    assert n % tm == 0 and cols % tn == 0
    return tm, tn


def _proj_glu(xb, wa, wg):
    n, k = xb.shape
    cols = wa.shape[1]
    tm, tn = _mm_tiles(n, k, cols)
    return pl.pallas_call(
        _mm_glu_kernel,
        grid=(n // tm, cols // tn),
        in_specs=[pl.BlockSpec((tm, k), lambda i, j: (i, 0)),
                  pl.BlockSpec((k, tn), lambda i, j: (0, j)),
                  pl.BlockSpec((k, tn), lambda i, j: (0, j))],
        out_specs=pl.BlockSpec((tm, tn), lambda i, j: (i, j)),
        out_shape=jax.ShapeDtypeStruct((n, cols), F32),
        compiler_params=_cparams(("parallel", "arbitrary"), 48),
        name="proj_glu",
    )(xb, wa, wg)


def _proj_scale(xb, w, scale, out_dtype, name):
    n, k = xb.shape
    cols = w.shape[1]
    tm, tn = _mm_tiles(n, k, cols)
    return pl.pallas_call(
        _mm_scale_kernel,
        grid=(n // tm, cols // tn),
        in_specs=[pl.BlockSpec((tm, k), lambda i, j: (i, 0)),
                  pl.BlockSpec((k, tn), lambda i, j: (0, j)),
                  pl.BlockSpec((1, tn), lambda i, j: (0, j))],
        out_specs=pl.BlockSpec((tm, tn), lambda i, j: (i, j)),
        out_shape=jax.ShapeDtypeStruct((n, cols), out_dtype),
        compiler_params=_cparams(("parallel", "arbitrary"), 48),
        name=name,
    )(xb, w, scale)


def _conv_kernel(cur_ref, prev_ref, w_ref, b_ref, g_ref, beta_ref, o_ref, sh_ref, y_ref, *, rows, lane_chunk):
    t = pl.program_id(1)
    dc = cur_ref.shape[2]
    row_sub = 32
    sh_ref[0, 0:CONV_HALO, :] = jnp.where(t > 0, prev_ref[0], 0.0)
    sh_ref[0, CONV_HALO:CONV_HALO + rows, :] = cur_ref[0]
    first = CONV_HALO - (CONV_WIDTH - 1)
    shifted_rows = rows + CONV_HALO - SUBLANES
    for lc in range(dc // lane_chunk):
        cols = slice(lc * lane_chunk, (lc + 1) * lane_chunk)
        for r in range(1, SUBLANES):
            sh_ref[r, 0:shifted_rows, cols] = sh_ref[0, r:r + shifted_rows, cols]

    for r0 in range(0, rows, row_sub):
        for lc in range(dc // lane_chunk):
            cols = slice(lc * lane_chunk, (lc + 1) * lane_chunk)
            acc = jnp.zeros((row_sub, lane_chunk), F32)
            for j in range(CONV_WIDTH):
                shift, base = (first + j) % SUBLANES, (first + j) // SUBLANES * SUBLANES
                acc = acc + sh_ref[shift, r0 + base:r0 + base + row_sub, cols] * w_ref[j:j + 1, cols]
            y_ref[r0:r0 + row_sub, cols] = acc + b_ref[:, cols]

    def norm_rows(i, carry):
        r0 = pl.multiple_of(i * BF16_ROWS, BF16_ROWS)
        yn = _layer_norm_rows(y_ref[pl.ds(r0, BF16_ROWS), :], g_ref[...], beta_ref[...])
        o_ref[0, pl.ds(r0, BF16_ROWS), :] = (yn * jax.nn.sigmoid(yn)).astype(o_ref.dtype)
        return carry

    lax.fori_loop(0, rows // BF16_ROWS, norm_rows, 0, unroll=2)


def _conformer_conv(glu, w_dw, b_dw, g_ln, b_ln):
    b, s, dc = glu.shape
    rows = min(128, s)
    halo_blocks = rows // CONV_HALO
    lane_chunk = min(512, dc)
    kern = functools.partial(_conv_kernel, rows=rows, lane_chunk=lane_chunk)
    return pl.pallas_call(
        kern,
        grid=(b, s // rows),
        in_specs=[pl.BlockSpec((1, rows, dc), lambda bi, t: (bi, t, 0)),
                  pl.BlockSpec((1, CONV_HALO, dc), lambda bi, t: (bi, jnp.maximum(t * halo_blocks - 1, 0), 0)),
                  pl.BlockSpec((CONV_WIDTH, dc), lambda bi, t: (0, 0)),
                  pl.BlockSpec((1, dc), lambda bi, t: (0, 0)),
                  pl.BlockSpec((1, dc), lambda bi, t: (0, 0)),
                  pl.BlockSpec((1, dc), lambda bi, t: (0, 0))],
        out_specs=pl.BlockSpec((1, rows, dc), lambda bi, t: (bi, t, 0)),
        out_shape=jax.ShapeDtypeStruct((b, s, dc), BF16),
        scratch_shapes=[pltpu.VMEM((SUBLANES, CONV_HALO + rows, dc), F32), pltpu.VMEM((rows, dc), F32)],
        compiler_params=_cparams(("parallel", "arbitrary"), 32),
        name="conformer_conv",
    )(glu, glu, w_dw, b_dw, g_ln, b_ln)


def _dsa_kernel(q_ref, qi_ref, k_ref, v_ref, kidx_ref, w_ref, o_ref,
                ke_ref, ko_ref, vt_ref, key_ref, nd_ref, qs_ref, qis_ref, m_ref, l_ref, acc_ref,
                *, seq, chunk, topk):
    tb = pl.program_id(1)
    t0 = tb * Q_TILE
    n_chunks = (t0 + Q_TILE + chunk - 1) // chunk
    rep = N_HEADS // N_KV_HEADS
    nt = (((1,), (1,)), ((), ()))

    @pl.when(tb == 0)
    def _():
        def build(c, carry):
            r0 = pl.multiple_of(c * chunk, chunk)
            kx = kidx_ref[0, pl.ds(r0, chunk), :]
            lane = lax.broadcasted_iota(I32, kx.shape, 1)
            ke_ref[pl.ds(r0, chunk), :] = jnp.where(lane < IDX_DIM, kx, 0.0).astype(BF16)
            ko_ref[pl.ds(r0, chunk), :] = jnp.where(lane >= IDX_DIM, pltpu.roll(kx, IDX_DIM, 1), 0.0).astype(BF16)
            for g in range(N_KV_HEADS):
                vg = v_ref[0, pl.ds(r0, chunk), g * HEAD_DIM:(g + 1) * HEAD_DIM]
                vt_ref[c, g * HEAD_DIM:(g + 1) * HEAD_DIM, :] = vg.astype(F32).T.astype(BF16)
            return carry
        lax.fori_loop(0, seq // chunk, build, 0)

    for g in range(N_KV_HEADS):
        for r in range(rep):
            h = g * rep + r
            qs_ref[g, r * Q_TILE:(r + 1) * Q_TILE, :] = q_ref[0, :, h * HEAD_DIM:(h + 1) * HEAD_DIM]
    for jj in range(IDX_HEADS // 4):
        for half in range(2):
            pair = 2 * jj + half
            qis_ref[jj, half * Q_TILE:(half + 1) * Q_TILE, :] = qi_ref[0, :, pair * LANES:(pair + 1) * LANES]
    w_t = w_ref[0].T * (IDX_HEADS ** -0.5)

    key_row = lax.broadcasted_iota(I32, (chunk, Q_TILE), 0)
    q_pos = t0 + lax.broadcasted_iota(I32, (chunk, Q_TILE), 1)

    def score_chunk(c, carry):
        r0 = pl.multiple_of(c * chunk, chunk)
        ke = ke_ref[pl.ds(r0, chunk), :]
        ko = ko_ref[pl.ds(r0, chunk), :]
        acc = jnp.zeros((chunk, Q_TILE), F32)
        for jj in range(IDX_HEADS // 4):
            rhs = qis_ref[jj]
            de = lax.dot_general(ke, rhs, nt, preferred_element_type=F32)
            do = lax.dot_general(ko, rhs, nt, preferred_element_type=F32)
            for half in range(2):
                h_even = 2 * (2 * jj + half)
                cols = slice(half * Q_TILE, (half + 1) * Q_TILE)
                acc = acc + w_t[IDX_DIM + h_even:IDX_DIM + h_even + 1, :] * jnp.maximum(de[:, cols], 0.0)
                acc = acc + w_t[IDX_DIM + h_even + 1:IDX_DIM + h_even + 2, :] * jnp.maximum(do[:, cols], 0.0)
        bits = lax.bitcast_convert_type(acc, I32)
        key = bits ^ ((bits >> 31) & 0x7FFFFFFF)
        key_ref[c] = jnp.where(r0 + key_row <= q_pos, key, INT_MIN)
        return carry

    lax.fori_loop(0, n_chunks, score_chunk, 0)

    def bit_step(i, t_u):
        cand_u = t_u | (jnp.int32(1) << (31 - i))
        cand = cand_u ^ INT_MIN

        def count_chunk(c, cnts):
            cnts = list(cnts)
            for r in range(chunk // SUBLANES):
                slab = key_ref[c, r * SUBLANES:(r + 1) * SUBLANES, :]
                cnts[r % len(cnts)] = cnts[r % len(cnts)] + jnp.where(slab >= cand, 1.0, 0.0)
            return tuple(cnts)

        zero = jnp.zeros((SUBLANES, Q_TILE), F32)
        cnts = lax.fori_loop(0, n_chunks, count_chunk, (zero, zero, zero, zero))
        tot = jnp.sum((cnts[0] + cnts[1]) + (cnts[2] + cnts[3]), axis=0, keepdims=True)
        return jnp.where(tot >= float(topk), cand_u, t_u)

    t_u = lax.fori_loop(0, 32, bit_step, jnp.zeros((1, Q_TILE), I32))
    thr = jnp.maximum(t_u ^ INT_MIN, INT_MIN + 1)

    m_ref[...] = jnp.full(m_ref.shape, M_INIT, F32)
    l_ref[...] = jnp.zeros(l_ref.shape, F32)
    acc_ref[...] = jnp.zeros(acc_ref.shape, F32)

    def attend_chunk(c, carry):
        r0 = pl.multiple_of(c * chunk, chunk)
        dist = (r0 + key_row - q_pos).astype(F32)
        nd_ref[...] = jnp.where(key_ref[c] >= thr, dist, NEG_BIG)
        s_groups = []
        for g in range(N_KV_HEADS):
            kg = k_ref[0, pl.ds(r0, chunk), g * HEAD_DIM:(g + 1) * HEAD_DIM]
            s_groups.append(lax.dot_general(kg, qs_ref[g], nt, preferred_element_type=F32))
        for g in range(N_KV_HEADS):
            s_all = s_groups[g]
            probs = []
            alphas = []
            for r in range(rep):
                h = g * rep + r
                slope = float(2.0 ** (-8.0 * (h + 1) / N_HEADS)) * LOG2E
                sr = s_all[:, r * Q_TILE:(r + 1) * Q_TILE] + slope * nd_ref[...]
                m_old = m_ref[h]
                m_new = jnp.maximum(m_old, jnp.max(sr, axis=0, keepdims=True))
                alpha = jnp.exp2(m_old - m_new)
                p = jnp.exp2(sr - m_new)
                l_ref[h] = alpha * l_ref[h] + jnp.sum(p, axis=0, keepdims=True)
                m_ref[h] = m_new
                probs.append(p.astype(BF16))
                alphas.append(alpha)
            vt = vt_ref[c, g * HEAD_DIM:(g + 1) * HEAD_DIM, :]
            pv = jnp.dot(vt, jnp.concatenate(probs, axis=1), preferred_element_type=F32)
            acc_ref[g] = jnp.concatenate(alphas, axis=1) * acc_ref[g] + pv
        return carry

    lax.fori_loop(0, n_chunks, attend_chunk, 0)

    for g in range(N_KV_HEADS):
        for r in range(rep):
            h = g * rep + r
            o_t = acc_ref[g, :, r * Q_TILE:(r + 1) * Q_TILE] * (1.0 / l_ref[h])
            o_ref[0, :, h * HEAD_DIM:(h + 1) * HEAD_DIM] = o_t.T.astype(o_ref.dtype)


def _dsa_attention(qkvi, small, b, s):
    d_attn = N_HEADS * HEAD_DIM
    d_kv = N_KV_HEADS * HEAD_DIM
    d_qi = IDX_HEADS * IDX_DIM
    rep = N_HEADS // N_KV_HEADS
    chunk = min(KEY_CHUNK, s)
    topk = min(TOPK_MAX, s // 4)
    assert s % chunk == 0 and chunk % Q_TILE == 0 and d_attn % d_qi == 0 and d_attn % d_kv == 0
    assert Q_TILE == LANES and HEAD_DIM == LANES and 2 * IDX_DIM == LANES
    kern = functools.partial(_dsa_kernel, seq=s, chunk=chunk, topk=topk)
    return pl.pallas_call(
        kern,
        grid=(b, s // Q_TILE),
        in_specs=[pl.BlockSpec((1, Q_TILE, d_attn), lambda bi, t: (bi, t, 0)),
                  pl.BlockSpec((1, Q_TILE, d_qi), lambda bi, t: (bi, t, (d_attn + 2 * d_kv) // d_qi)),
                  pl.BlockSpec((1, s, d_kv), lambda bi, t: (bi, 0, d_attn // d_kv)),
                  pl.BlockSpec((1, s, d_kv), lambda bi, t: (bi, 0, d_attn // d_kv + 1)),
                  pl.BlockSpec((1, s, LANES), lambda bi, t: (bi, 0, 0)),
                  pl.BlockSpec((1, Q_TILE, LANES), lambda bi, t: (bi, t, 0))],
        out_specs=pl.BlockSpec((1, Q_TILE, d_attn), lambda bi, t: (bi, t, 0)),
        out_shape=jax.ShapeDtypeStruct((b, s, d_attn), BF16),
        scratch_shapes=[pltpu.VMEM((s, LANES), BF16),
                        pltpu.VMEM((s, LANES), BF16),
                        pltpu.VMEM((s // chunk, d_kv, chunk), BF16),
                        pltpu.VMEM((s // chunk, chunk, Q_TILE), I32),
                        pltpu.VMEM((chunk, Q_TILE), F32),
                        pltpu.VMEM((N_KV_HEADS, rep * Q_TILE, HEAD_DIM), BF16),
                        pltpu.VMEM((IDX_HEADS // 4, 2 * Q_TILE, LANES), BF16),
                        pltpu.VMEM((N_HEADS, 1, Q_TILE), F32),
                        pltpu.VMEM((N_HEADS, 1, Q_TILE), F32),
                        pltpu.VMEM((N_KV_HEADS, HEAD_DIM, rep * Q_TILE), F32)],
        compiler_params=_cparams(("parallel", "arbitrary"), 48),
        name="dsa_attention",
    )(qkvi, qkvi, qkvi, qkvi, small, small)


def _outproj_kernel(conv_ref, attn_ref, wc_ref, wa_ref, x_ref, g_ref, b_ref, wr_ref, br_ref, h_ref, lg_ref,
                    z_ref, hb_ref, *, alpha):
    j = pl.program_id(1)
    n_tiles, tm, tn = z_ref.shape
    mix = jnp.dot(conv_ref[...], wc_ref[...], preferred_element_type=F32)
    mix = mix + jnp.dot(attn_ref[...], wa_ref[...], preferred_element_type=F32)
    z_ref[j] = alpha * x_ref[...] + mix

    @pl.when(j == n_tiles - 1)
    def _():
        def rows(i, carry):
            r0 = pl.multiple_of(i * BF16_ROWS, BF16_ROWS)
            z = jnp.concatenate([z_ref[jj, pl.ds(r0, BF16_ROWS), :] for jj in range(n_tiles)], axis=1)
            h = _layer_norm_rows(z, g_ref[...], b_ref[...])
            h_ref[pl.ds(r0, BF16_ROWS), :] = h
            hb_ref[pl.ds(r0, BF16_ROWS), :] = h.astype(BF16)
            return carry
        lax.fori_loop(0, tm // BF16_ROWS, rows, 0, unroll=2)
        lg_ref[...] = jnp.dot(hb_ref[...], wr_ref[...], preferred_element_type=F32) + br_ref[...]


def _out_proj_ln(conv_out, attn_out, w_conv, w_attn, x2, ln_g, ln_b, w_router, b_router, alpha):
    n, d = x2.shape
    dc = conv_out.shape[1]
    da = attn_out.shape[1]
    tm = min(512, n)
    tn = min(512, d)
    assert n % tm == 0 and d % tn == 0
    kern = functools.partial(_outproj_kernel, alpha=alpha)
    return pl.pallas_call(
        kern,
        grid=(n // tm, d // tn),
        in_specs=[pl.BlockSpec((tm, dc), lambda i, j: (i, 0)),
                  pl.BlockSpec((tm, da), lambda i, j: (i, 0)),
                  pl.BlockSpec((dc, tn), lambda i, j: (0, j)),
                  pl.BlockSpec((da, tn), lambda i, j: (0, j)),
                  pl.BlockSpec((tm, tn), lambda i, j: (i, j)),
                  pl.BlockSpec((1, d), lambda i, j: (0, 0)),
                  pl.BlockSpec((1, d), lambda i, j: (0, 0)),
                  pl.BlockSpec((d, LANES), lambda i, j: (0, 0)),
                  pl.BlockSpec((1, LANES), lambda i, j: (0, 0))],
        out_specs=[pl.BlockSpec((tm, d), lambda i, j: (i, 0)),
                   pl.BlockSpec((tm, LANES), lambda i, j: (i, 0))],
        out_shape=[jax.ShapeDtypeStruct((n, d), F32), jax.ShapeDtypeStruct((n, LANES), F32)],
        scratch_shapes=[pltpu.VMEM((d // tn, tm, tn), F32), pltpu.VMEM((tm, d), BF16)],
        compiler_params=_cparams(("parallel", "arbitrary"), 56),
        name="out_proj_ln1",
    )(conv_out, attn_out, w_conv, w_attn, x2, ln_g, ln_b, w_router, b_router)


def _first_argmax(vals, lane, valid):
    masked = jnp.where(valid, vals, -jnp.inf)
    mx = jnp.max(masked, axis=1, keepdims=True)
    idx = jnp.min(jnp.where(valid & (masked == mx), lane, LANES), axis=1, keepdims=True)
    return mx, idx


def _router_kernel(lg_ref, tri_ref, e_ref, gate_ref, cnt_ref):
    @pl.when(pl.program_id(0) == 0)
    def _():
        cnt_ref[...] = jnp.zeros(cnt_ref.shape, F32)

    lg = lg_ref[...]
    lane = lax.broadcasted_iota(I32, lg.shape, 1)
    is_group = lane < N_GROUPS
    g_max, g_sel = _first_argmax(lg, lane, is_group)
    g_den = jnp.sum(jnp.where(is_group, jnp.exp(lg - g_max), 0.0), axis=1, keepdims=True)
    p_group = 1.0 / g_den
    lo = N_GROUPS + g_sel * EXPERTS_PER_GROUP
    in_group = (lane >= lo) & (lane < lo + EXPERTS_PER_GROUP)
    e_max, _ = _first_argmax(lg, lane, in_group)
    e_exp = jnp.where(in_group, jnp.exp(lg - e_max), 0.0)
    prob = e_exp / jnp.sum(e_exp, axis=1, keepdims=True)
    p1, i1 = _first_argmax(prob, lane, in_group)
    p2, i2 = _first_argmax(prob, lane, in_group & (lane != i1))
    norm = p_group / (p1 + p2)
    gate_ref[...] = jnp.where(lane == 0, p1 * norm, jnp.where(lane == 1, p2 * norm, 0.0))
    chosen = jnp.where((lane == i1) | (lane == i2), 1.0, 0.0)
    before = jnp.dot(tri_ref[...], chosen.astype(BF16), preferred_element_type=F32) + cnt_ref[...]
    rank1 = jnp.sum(jnp.where(lane == i1, before, 0.0), axis=1, keepdims=True).astype(I32)
    rank2 = jnp.sum(jnp.where(lane == i2, before, 0.0), axis=1, keepdims=True).astype(I32)
    cnt_ref[...] += jnp.sum(chosen, axis=0, keepdims=True)
    e_ref[...] = jnp.where(lane == 0, i1 - N_GROUPS,
                           jnp.where(lane == 1, i2 - N_GROUPS,
                                     jnp.where(lane == 2, rank1, jnp.where(lane == 3, rank2, 0))))


def _route(logits):
    n = logits.shape[0]
    tm = min(512, n)
    tri = jnp.tril(jnp.ones((tm, tm), BF16), -1)
    spec = pl.BlockSpec((tm, LANES), lambda i: (i, 0))
    return pl.pallas_call(
        _router_kernel,
        grid=(n // tm,),
        in_specs=[spec, pl.BlockSpec((tm, tm), lambda i: (0, 0))],
        out_specs=[spec, spec, pl.BlockSpec((1, LANES), lambda i: (0, 0))],
        out_shape=[jax.ShapeDtypeStruct((n, LANES), I32), jax.ShapeDtypeStruct((n, LANES), F32),
                   jax.ShapeDtypeStruct((1, LANES), F32)],
        compiler_params=_cparams(("arbitrary",), 32),
        name="router",
    )(logits, tri)


def _start_row_copies(src_hbm, idx_ref, idx_row, dst_ref, slot, row0, sem, count):
    group = 8

    def issue(i, carry):
        for k in range(group):
            r = i * group + k
            pltpu.make_async_copy(src_hbm.at[pl.ds(idx_ref[0, idx_row, r], 1), :],
                                  dst_ref.at[slot, pl.ds(row0 + r, 1), :], sem.at[slot]).start(priority=k % 2)
        return carry
    lax.fori_loop(0, count // group, issue, 0)


def _wait_slot(src_hbm, dst_ref, slot, sem):
    rows = dst_ref.shape[1]
    pltpu.make_async_copy(src_hbm.at[pl.ds(0, rows), :], dst_ref.at[slot], sem.at[slot]).wait()


def _gather_cast_kernel(used_ref, tok_ref, tok_next_ref, h_hbm, o_ref, buf_ref, sem):
    i = pl.program_id(0)
    slot = i % 2
    n_used = used_ref[0]
    rows = buf_ref.shape[1]

    @pl.when(i == 0)
    def _():
        _start_row_copies(h_hbm, tok_ref, 0, buf_ref, 0, 0, sem, rows)

    @pl.when(i + 1 < n_used)
    def _():
        _start_row_copies(h_hbm, tok_next_ref, 0, buf_ref, 1 - slot, 0, sem, rows)

    @pl.when(i < n_used)
    def _():
        _wait_slot(h_hbm, buf_ref, slot, sem)
        o_ref[...] = buf_ref[slot].astype(o_ref.dtype)

    @pl.when(i >= n_used)
    def _():
        o_ref[...] = jnp.zeros(o_ref.shape, o_ref.dtype)


def _gather_tokens(h1, buf_tok, n_used):
    p = buf_tok.shape[0]
    d = h1.shape[1]
    nblk = p // MOE_ROWS
    tok_blocks = buf_tok.reshape(nblk, 1, MOE_ROWS)
    return pl.pallas_call(
        _gather_cast_kernel,
        grid_spec=pltpu.PrefetchScalarGridSpec(
            num_scalar_prefetch=1,
            grid=(nblk,),
            in_specs=[pl.BlockSpec((1, 1, MOE_ROWS), lambda i, u: (i, 0, 0), memory_space=pltpu.SMEM),
                      pl.BlockSpec((1, 1, MOE_ROWS), lambda i, u: (jnp.minimum(i + 1, nblk - 1), 0, 0),
                                   memory_space=pltpu.SMEM),
                      pl.BlockSpec(memory_space=pl.ANY)],
            out_specs=pl.BlockSpec((MOE_ROWS, d), lambda i, u: (i, 0)),
            scratch_shapes=[pltpu.VMEM((2, MOE_ROWS, d), F32), pltpu.SemaphoreType.DMA((2,))]),
        out_shape=jax.ShapeDtypeStruct((p, d), BF16),
        compiler_params=_cparams(("arbitrary",), 32),
        name="moe_gather",
    )(n_used.reshape(1), tok_blocks, tok_blocks, h1)


def _combine_kernel(pos_ref, pos_next_ref, y_hbm, h_ref, gate_ref, g_ref, b_ref, o_ref, buf_ref, sem, *, alpha):
    i = pl.program_id(0)
    n = pl.num_programs(0)
    slot = i % 2
    rows = h_ref.shape[0]

    def start(idx_ref, into):
        for k in range(2):
            _start_row_copies(y_hbm, idx_ref, k, buf_ref, into, k * rows, sem, rows)

    @pl.when(i == 0)
    def _():
        start(pos_ref, 0)

    @pl.when(i + 1 < n)
    def _():
        start(pos_next_ref, 1 - slot)

    _wait_slot(y_hbm, buf_ref, slot, sem)

    def norm_rows(j, carry):
        r0 = pl.multiple_of(j * SUBLANES, SUBLANES)
        gates = gate_ref[pl.ds(r0, SUBLANES), :]
        ffn = (buf_ref[slot, pl.ds(r0, SUBLANES), :] * gates[:, 0:1]
               + buf_ref[slot, pl.ds(rows + r0, SUBLANES), :] * gates[:, 1:2])
        z = alpha * h_ref[pl.ds(r0, SUBLANES), :] + ffn
        o_ref[pl.ds(r0, SUBLANES), :] = _layer_norm_rows(z, g_ref[...], b_ref[...])
        return carry
    lax.fori_loop(0, rows // SUBLANES, norm_rows, 0, unroll=2)


def _combine_ln(y, pos2, h1, gate_lanes, ln_g, ln_b, alpha):
    n, d = h1.shape
    tb = min(COMBINE_ROWS, n)
    nblk = n // tb
    pos_blocks = pos2.reshape(nblk, tb, 2).transpose(0, 2, 1)
    kern = functools.partial(_combine_kernel, alpha=alpha)
    return pl.pallas_call(
        kern,
        grid=(nblk,),
        in_specs=[pl.BlockSpec((1, 2, tb), lambda i: (i, 0, 0), memory_space=pltpu.SMEM),
                  pl.BlockSpec((1, 2, tb), lambda i: (jnp.minimum(i + 1, nblk - 1), 0, 0), memory_space=pltpu.SMEM),
                  pl.BlockSpec(memory_space=pl.ANY),
                  pl.BlockSpec((tb, d), lambda i: (i, 0)),
                  pl.BlockSpec((tb, LANES), lambda i: (i, 0)),
                  pl.BlockSpec((1, d), lambda i: (0, 0)),
                  pl.BlockSpec((1, d), lambda i: (0, 0))],
        out_specs=pl.BlockSpec((tb, d), lambda i: (i, 0)),
        out_shape=jax.ShapeDtypeStruct((n, d), F32),
        scratch_shapes=[pltpu.VMEM((2, 2 * tb, d), F32), pltpu.SemaphoreType.DMA((2,))],
        compiler_params=_cparams(("arbitrary",), 32),
        name="moe_combine_ln2",
    )(pos_blocks, pos_blocks, y, h1, gate_lanes, ln_g, ln_b)


def _cast_weight(dst_ref, src_ref):
    rows = src_ref.shape[1]
    step = min(256, rows)

    def body(i, carry):
        r0 = pl.multiple_of(i * step, step)
        dst_ref[pl.ds(r0, step), :] = src_ref[0, pl.ds(r0, step), :].astype(BF16)
        return carry
    lax.fori_loop(0, rows // step, body, 0)


ITEM_COMPUTE = 1
ITEM_NEW_WEIGHTS = 2
ITEM_ZERO_FILL = 4


def _moe_up_kernel(e_ref, cw_ref, bi_ref, bo_ref, co_ref, flag_ref, x_ref, w1_ref, w3_ref, o_ref, w1b_ref, w3b_ref):
    flag = flag_ref[pl.program_id(0)]

    @pl.when((flag & ITEM_NEW_WEIGHTS) != 0)
    def _():
        _cast_weight(w1b_ref, w1_ref)
        _cast_weight(w3b_ref, w3_ref)

    @pl.when((flag & ITEM_COMPUTE) != 0)
    def _():
        x = x_ref[...]
        a = jnp.dot(x, w1b_ref[...], preferred_element_type=F32)
        b = jnp.dot(x, w3b_ref[...], preferred_element_type=F32)
        o_ref[...] = (a * jax.nn.sigmoid(a) * b).astype(o_ref.dtype)

    @pl.when((flag & ITEM_ZERO_FILL) != 0)
    def _():
        o_ref[...] = jnp.zeros(o_ref.shape, o_ref.dtype)


def _moe_down_kernel(e_ref, cw_ref, bi_ref, bo_ref, co_ref, flag_ref, h_ref, w2_ref, o_ref, w2b_ref):
    flag = flag_ref[pl.program_id(0)]

    @pl.when((flag & ITEM_NEW_WEIGHTS) != 0)
    def _():
        _cast_weight(w2b_ref, w2_ref)

    @pl.when((flag & ITEM_COMPUTE) != 0)
    def _():
        o_ref[...] = jnp.dot(h_ref[...], w2b_ref[...], preferred_element_type=F32)

    @pl.when((flag & ITEM_ZERO_FILL) != 0)
    def _():
        o_ref[...] = jnp.zeros(o_ref.shape, o_ref.dtype)


def _work_items(blocks_per_expert, block_start, n_chunks, n_blocks):
    n_items = n_chunks * n_blocks
    per_e = n_chunks * blocks_per_expert
    end = jnp.cumsum(per_e)
    start = end - per_e
    it = jnp.arange(n_items, dtype=I32)
    total = end[-1]
    used_blocks = total // n_chunks
    itc = jnp.minimum(it, total - 1)
    e = jnp.searchsorted(end, itc, side="right").astype(I32)
    local = itc - start[e]
    nb = jnp.maximum(blocks_per_expert[e], 1)
    c = local // nb
    r = local - c * nb
    b_in = block_start[e] + r
    active = it < total
    spare = jnp.maximum(it - total, 0)
    b_out = jnp.where(active, b_in, used_blocks + spare // n_chunks)
    c_out = jnp.where(active, c, spare % n_chunks)
    flag = jnp.where(active, ITEM_COMPUTE + ITEM_NEW_WEIGHTS * (r == 0).astype(I32), ITEM_ZERO_FILL)
    return tuple(v.astype(I32) for v in (e, c, b_in, b_out, c_out, flag))


def _moe_up(xs, w1, w3, items, f_chunk):
    p, d = xs.shape
    f = w1.shape[2]
    wspec = pl.BlockSpec((1, d, f_chunk), lambda it, e, cw, bi, bo, co, fl: (e[it], 0, cw[it]))
    return pl.pallas_call(
        _moe_up_kernel,
        grid_spec=pltpu.PrefetchScalarGridSpec(
            num_scalar_prefetch=6,
            grid=(items[0].shape[0],),
            in_specs=[pl.BlockSpec((MOE_ROWS, d), lambda it, e, cw, bi, bo, co, fl: (bi[it], 0)), wspec, wspec],
            out_specs=pl.BlockSpec((MOE_ROWS, f_chunk), lambda it, e, cw, bi, bo, co, fl: (bo[it], co[it])),
            scratch_shapes=[pltpu.VMEM((d, f_chunk), BF16), pltpu.VMEM((d, f_chunk), BF16)]),
        out_shape=jax.ShapeDtypeStruct((p, f), BF16),
        compiler_params=_cparams(("arbitrary",), 56),
        name="moe_up",
    )(*items, xs, w1, w3)


def _moe_down(hmid, w2, items, d_chunk):
    p, f = hmid.shape
    d = w2.shape[2]
    return pl.pallas_call(
        _moe_down_kernel,
        grid_spec=pltpu.PrefetchScalarGridSpec(
            num_scalar_prefetch=6,
            grid=(items[0].shape[0],),
            in_specs=[pl.BlockSpec((MOE_ROWS, f), lambda it, e, cw, bi, bo, co, fl: (bi[it], 0)),
                      pl.BlockSpec((1, f, d_chunk), lambda it, e, cw, bi, bo, co, fl: (e[it], 0, cw[it]))],
            out_specs=pl.BlockSpec((MOE_ROWS, d_chunk), lambda it, e, cw, bi, bo, co, fl: (bo[it], co[it])),
            scratch_shapes=[pltpu.VMEM((f, d_chunk), BF16)]),
        out_shape=jax.ShapeDtypeStruct((p, d), F32),
        compiler_params=_cparams(("arbitrary",), 48),
        name="moe_down",
    )(*items, hmid, w2)


def _hier_moe_ln(h1, logits, w1, w3, w2, ln_g, ln_b, alpha):
    n, d = h1.shape
    e_lanes, gate_lanes, lane_counts = _route(logits)
    a = 2 * n
    e_flat = e_lanes[:, 0:2].reshape(a)
    rank = e_lanes[:, 2:4].reshape(a)
    counts = lane_counts[0, N_GROUPS:N_GROUPS + N_EXPERTS].astype(I32)
    blocks_per_expert = (counts + MOE_ROWS - 1) // MOE_ROWS
    block_start = jnp.cumsum(blocks_per_expert) - blocks_per_expert
    onehot = e_flat[:, None] == jnp.arange(N_EXPERTS, dtype=I32)[None, :]
    pos = jnp.sum(jnp.where(onehot, block_start[None, :], 0), axis=1) * MOE_ROWS + rank
    n_blocks = (a + MOE_ROWS - 1) // MOE_ROWS + N_EXPERTS
    p = n_blocks * MOE_ROWS
    buf_tok = jnp.zeros((p,), I32).at[pos].set(jnp.arange(a, dtype=I32) // 2)

    f = w1.shape[2]
    f_chunk = min(512, f)
    d_chunk = min(2048, d)
    xs = _gather_tokens(h1, buf_tok, jnp.sum(blocks_per_expert).astype(I32))
    hmid = _moe_up(xs, w1, w3, _work_items(blocks_per_expert, block_start, f // f_chunk, n_blocks), f_chunk)
    y = _moe_down(hmid, w2, _work_items(blocks_per_expert, block_start, d // d_chunk, n_blocks), d_chunk)
    return _combine_ln(y, pos.reshape(n, 2), h1, gate_lanes, ln_g, ln_b, alpha)


def kernel(x, w_in, conv_dw_w, conv_dw_b, conv_ln_g, conv_ln_b, w_out, ln1_g, ln1_b, w_router_group, b_router_group, w_router_expert, b_router_expert, w_expert_gate, w_expert_up, w_expert_down, ln2_g, ln2_b):
    b, s, d = x.shape
    depth = w_in.shape[0]
    assert depth == 1
    alpha = float((2.0 * depth) ** 0.25)
    n = b * s
    dc = conv_dw_w.shape[2]
    d_attn = N_HEADS * HEAD_DIM
    d_kv = N_KV_HEADS * HEAD_DIM
    d_qi = IDX_HEADS * IDX_DIM
    n_small = IDX_DIM + IDX_HEADS
    assert w_in.shape[2] == 2 * dc + d_attn + 2 * d_kv + d_qi + n_small

    x2 = x.reshape(n, d)
    xb = x2.astype(BF16)
    w = w_in[0]
    o_qkvi = 2 * dc
    o_small = o_qkvi + d_attn + 2 * d_kv + d_qi
    wa = w[:, :dc].astype(BF16)
    wg = w[:, dc:2 * dc].astype(BF16)
    w_qkvi = w[:, o_qkvi:o_small].astype(BF16)
    w_small = jnp.pad(w[:, o_small:], ((0, 0), (0, LANES - n_small))).astype(BF16)
    qkvi_scale = jnp.concatenate([jnp.full((d_attn,), HEAD_DIM ** -0.5 * LOG2E, F32), jnp.ones((2 * d_kv,), F32),
                                  jnp.full((d_qi,), IDX_DIM ** -0.5, F32)])[None, :]

    glu = _proj_glu(xb, wa, wg)
    qkvi = _proj_scale(xb, w_qkvi, qkvi_scale, BF16, "proj_qkvi")
    small = _proj_scale(xb, w_small, jnp.ones((1, LANES), F32), F32, "proj_idx")

    conv_out = _conformer_conv(glu.reshape(b, s, dc), conv_dw_w[0], conv_dw_b, conv_ln_g, conv_ln_b)
    attn_out = _dsa_attention(qkvi.reshape(b, s, -1), small.reshape(b, s, LANES), b, s)

    n_route = N_GROUPS + N_EXPERTS
    w_router = jnp.pad(jnp.concatenate([w_router_group[0], w_router_expert[0]], axis=1),
                       ((0, 0), (0, LANES - n_route))).astype(BF16)
    b_router = jnp.pad(jnp.concatenate([b_router_group[0], b_router_expert[0]]), (0, LANES - n_route))[None, :]
    h1, logits = _out_proj_ln(conv_out.reshape(n, dc), attn_out.reshape(n, d_attn),
                              w_out[0, :dc].astype(BF16), w_out[0, dc:].astype(BF16),
                              x2, ln1_g, ln1_b, w_router, b_router, alpha)

    out = _hier_moe_ln(h1, logits, w_expert_gate[0], w_expert_up[0], w_expert_down[0], ln2_g, ln2_b, alpha)
    return out.reshape(b, s, d)
```

```python
import functools
import math

import jax
import jax.numpy as jnp
from jax import lax
from jax.experimental import pallas as pl
from jax.experimental.pallas import tpu as pltpu

F32 = jnp.float32
BF16 = jnp.bfloat16
I32 = jnp.int32

CONV_WIDTH = 31
N_HEADS = 16
HEAD_DIM = 128
N_KV_HEADS = 4
IDX_HEADS = 16
IDX_DIM = 64
TOPK_MAX = 256
N_GROUPS = 4
EXPERTS_PER_GROUP = 8
N_EXPERTS = N_GROUPS * EXPERTS_PER_GROUP
LN_EPS = 1e-5

LANES = 128
SUBLANES = 8
BF16_ROWS = 16
MIB = 1024 * 1024

Q_TILE = 128
KEY_CHUNK = 512
CONV_HALO = 32
MOE_ROWS = 256
COMBINE_ROWS = 128
INT_MIN = -(2 ** 31)
LOG2E = math.log2(math.e)
NEG_BIG = -1e30
M_INIT = -1e20


def _cparams(semantics, vmem_mib):
    return pltpu.CompilerParams(dimension_semantics=semantics, vmem_limit_bytes=vmem_mib * MIB)


def _layer_norm_rows(z, g, b):
    mu = jnp.mean(z, axis=-1, keepdims=True)
    zc = z - mu
    var = jnp.mean(zc * zc, axis=-1, keepdims=True)
    return zc * lax.rsqrt(var + LN_EPS) * g + b


def _mm_glu_kernel(x_ref, wa_ref, wg_ref, o_ref):
    x = x_ref[...]
    a = jnp.dot(x, wa_ref[...], preferred_element_type=F32)
    g = jnp.dot(x, wg_ref[...], preferred_element_type=F32)
    o_ref[...] = (a * jax.nn.sigmoid(g)).astype(o_ref.dtype)


def _mm_scale_kernel(x_ref, w_ref, s_ref, o_ref):
    acc = jnp.dot(x_ref[...], w_ref[...], preferred_element_type=F32)
    o_ref[...] = (acc * s_ref[...]).astype(o_ref.dtype)


def _mm_tiles(n, k, cols):
    tm = min(1024, n)
    tn = min(512, cols)
    assert n % tm == 0 and cols % tn == 0
    return tm, tn


def _proj_glu(xb, wa, wg):
    n, k = xb.shape
    cols = wa.shape[1]
    tm, tn = _mm_tiles(n, k, cols)
    return pl.pallas_call(
        _mm_glu_kernel,
        grid=(n // tm, cols // tn),
        in_specs=[pl.BlockSpec((tm, k), lambda i, j: (i, 0)),
                  pl.BlockSpec((k, tn), lambda i, j: (0, j)),
                  pl.BlockSpec((k, tn), lambda i, j: (0, j))],
        out_specs=pl.BlockSpec((tm, tn), lambda i, j: (i, j)),
        out_shape=jax.ShapeDtypeStruct((n, cols), F32),
        compiler_params=_cparams(("parallel", "arbitrary"), 48),
        name="proj_glu",
    )(xb, wa, wg)


def _proj_scale(xb, w, scale, out_dtype, name):
    n, k = xb.shape
    cols = w.shape[1]
    tm, tn = _mm_tiles(n, k, cols)
    return pl.pallas_call(
        _mm_scale_kernel,
        grid=(n // tm, cols // tn),
        in_specs=[pl.BlockSpec((tm, k), lambda i, j: (i, 0)),
                  pl.BlockSpec((k, tn), lambda i, j: (0, j)),
                  pl.BlockSpec((1, tn), lambda i, j: (0, j))],
        out_specs=pl.BlockSpec((tm, tn), lambda i, j: (i, j)),
        out_shape=jax.ShapeDtypeStruct((n, cols), out_dtype),
        compiler_params=_cparams(("parallel", "arbitrary"), 48),
        name=name,
    )(xb, w, scale)


def _conv_kernel(cur_ref, prev_ref, w_ref, b_ref, g_ref, beta_ref, o_ref, sh_ref, y_ref, *, rows, lane_chunk):
    t = pl.program_id(1)
    dc = cur_ref.shape[2]
    row_sub = 32
    sh_ref[0, 0:CONV_HALO, :] = jnp.where(t > 0, prev_ref[0], 0.0)
    sh_ref[0, CONV_HALO:CONV_HALO + rows, :] = cur_ref[0]
    first = CONV_HALO - (CONV_WIDTH - 1)
    shifted_rows = rows + CONV_HALO - SUBLANES
    for lc in range(dc // lane_chunk):
        cols = slice(lc * lane_chunk, (lc + 1) * lane_chunk)
        for r in range(1, SUBLANES):
            sh_ref[r, 0:shifted_rows, cols] = sh_ref[0, r:r + shifted_rows, cols]

    for r0 in range(0, rows, row_sub):
        for lc in range(dc // lane_chunk):
            cols = slice(lc * lane_chunk, (lc + 1) * lane_chunk)
            acc = jnp.zeros((row_sub, lane_chunk), F32)
            for j in range(CONV_WIDTH):
                shift, base = (first + j) % SUBLANES, (first + j) // SUBLANES * SUBLANES
                acc = acc + sh_ref[shift, r0 + base:r0 + base + row_sub, cols] * w_ref[j:j + 1, cols]
            y_ref[r0:r0 + row_sub, cols] = acc + b_ref[:, cols]

    def norm_rows(i, carry):
        r0 = pl.multiple_of(i * BF16_ROWS, BF16_ROWS)
        yn = _layer_norm_rows(y_ref[pl.ds(r0, BF16_ROWS), :], g_ref[...], beta_ref[...])
        o_ref[0, pl.ds(r0, BF16_ROWS), :] = (yn * jax.nn.sigmoid(yn)).astype(o_ref.dtype)
        return carry

    lax.fori_loop(0, rows // BF16_ROWS, norm_rows, 0, unroll=2)


def _conformer_conv(glu, w_dw, b_dw, g_ln, b_ln):
    b, s, dc = glu.shape
    rows = min(128, s)
    halo_blocks = rows // CONV_HALO
    lane_chunk = min(512, dc)
    kern = functools.partial(_conv_kernel, rows=rows, lane_chunk=lane_chunk)
    return pl.pallas_call(
        kern,
        grid=(b, s // rows),
        in_specs=[pl.BlockSpec((1, rows, dc), lambda bi, t: (bi, t, 0)),
                  pl.BlockSpec((1, CONV_HALO, dc), lambda bi, t: (bi, jnp.maximum(t * halo_blocks - 1, 0), 0)),
                  pl.BlockSpec((CONV_WIDTH, dc), lambda bi, t: (0, 0)),
                  pl.BlockSpec((1, dc), lambda bi, t: (0, 0)),
                  pl.BlockSpec((1, dc), lambda bi, t: (0, 0)),
                  pl.BlockSpec((1, dc), lambda bi, t: (0, 0))],
        out_specs=pl.BlockSpec((1, rows, dc), lambda bi, t: (bi, t, 0)),
        out_shape=jax.ShapeDtypeStruct((b, s, dc), BF16),
        scratch_shapes=[pltpu.VMEM((SUBLANES, CONV_HALO + rows, dc), F32), pltpu.VMEM((rows, dc), F32)],
        compiler_params=_cparams(("parallel", "arbitrary"), 32),
        name="conformer_conv",
    )(glu, glu, w_dw, b_dw, g_ln, b_ln)


def _dsa_kernel(q_ref, qi_ref, k_ref, v_ref, kidx_ref, w_ref, o_ref,
                ke_ref, ko_ref, vt_ref, key_ref, nd_ref, qs_ref, qis_ref, m_ref, l_ref, acc_ref,
                *, seq, chunk, topk):
    tb = pl.program_id(1)
    t0 = tb * Q_TILE
    n_chunks = (t0 + Q_TILE + chunk - 1) // chunk
    rep = N_HEADS // N_KV_HEADS
    nt = (((1,), (1,)), ((), ()))

    @pl.when(tb == 0)
    def _():
        def build(c, carry):
            r0 = pl.multiple_of(c * chunk, chunk)
            kx = kidx_ref[0, pl.ds(r0, chunk), :]
            lane = lax.broadcasted_iota(I32, kx.shape, 1)
            ke_ref[pl.ds(r0, chunk), :] = jnp.where(lane < IDX_DIM, kx, 0.0).astype(BF16)
            ko_ref[pl.ds(r0, chunk), :] = jnp.where(lane >= IDX_DIM, pltpu.roll(kx, IDX_DIM, 1), 0.0).astype(BF16)
            for g in range(N_KV_HEADS):
                vg = v_ref[0, pl.ds(r0, chunk), g * HEAD_DIM:(g + 1) * HEAD_DIM]
                vt_ref[c, g * HEAD_DIM:(g + 1) * HEAD_DIM, :] = vg.astype(F32).T.astype(BF16)
            return carry
        lax.fori_loop(0, seq // chunk, build, 0)

    for g in range(N_KV_HEADS):
        for r in range(rep):
            h = g * rep + r
            qs_ref[g, r * Q_TILE:(r + 1) * Q_TILE, :] = q_ref[0, :, h * HEAD_DIM:(h + 1) * HEAD_DIM]
    for jj in range(IDX_HEADS // 4):
        for half in range(2):
            pair = 2 * jj + half
            qis_ref[jj, half * Q_TILE:(half + 1) * Q_TILE, :] = qi_ref[0, :, pair * LANES:(pair + 1) * LANES]
    w_t = w_ref[0].T * (IDX_HEADS ** -0.5)

    key_row = lax.broadcasted_iota(I32, (chunk, Q_TILE), 0)
    q_pos = t0 + lax.broadcasted_iota(I32, (chunk, Q_TILE), 1)

    def score_chunk(c, carry):
        r0 = pl.multiple_of(c * chunk, chunk)
        ke = ke_ref[pl.ds(r0, chunk), :]
        ko = ko_ref[pl.ds(r0, chunk), :]
        acc = jnp.zeros((chunk, Q_TILE), F32)
        for jj in range(IDX_HEADS // 4):
            rhs = qis_ref[jj]
            de = lax.dot_general(ke, rhs, nt, preferred_element_type=F32)
            do = lax.dot_general(ko, rhs, nt, preferred_element_type=F32)
            for half in range(2):
                h_even = 2 * (2 * jj + half)
                cols = slice(half * Q_TILE, (half + 1) * Q_TILE)
                acc = acc + w_t[IDX_DIM + h_even:IDX_DIM + h_even + 1, :] * jnp.maximum(de[:, cols], 0.0)
                acc = acc + w_t[IDX_DIM + h_even + 1:IDX_DIM + h_even + 2, :] * jnp.maximum(do[:, cols], 0.0)
        bits = lax.bitcast_convert_type(acc, I32)
        key = bits ^ ((bits >> 31) & 0x7FFFFFFF)
        key_ref[c] = jnp.where(r0 + key_row <= q_pos, key, INT_MIN)
        return carry

    lax.fori_loop(0, n_chunks, score_chunk, 0)

    def bit_step(i, t_u):
        cand_u = t_u | (jnp.int32(1) << (31 - i))
        cand = cand_u ^ INT_MIN

        def count_chunk(c, cnts):
            cnts = list(cnts)
            for r in range(chunk // SUBLANES):
                slab = key_ref[c, r * SUBLANES:(r + 1) * SUBLANES, :]
                cnts[r % len(cnts)] = cnts[r % len(cnts)] + jnp.where(slab >= cand, 1.0, 0.0)
            return tuple(cnts)

        zero = jnp.zeros((SUBLANES, Q_TILE), F32)
        cnts = lax.fori_loop(0, n_chunks, count_chunk, (zero, zero, zero, zero))
        tot = jnp.sum((cnts[0] + cnts[1]) + (cnts[2] + cnts[3]), axis=0, keepdims=True)
        return jnp.where(tot >= float(topk), cand_u, t_u)

    t_u = lax.fori_loop(0, 32, bit_step, jnp.zeros((1, Q_TILE), I32))
    thr = jnp.maximum(t_u ^ INT_MIN, INT_MIN + 1)

    m_ref[...] = jnp.full(m_ref.shape, M_INIT, F32)
    l_ref[...] = jnp.zeros(l_ref.shape, F32)
    acc_ref[...] = jnp.zeros(acc_ref.shape, F32)

    def attend_chunk(c, carry):
        r0 = pl.multiple_of(c * chunk, chunk)
        dist = (r0 + key_row - q_pos).astype(F32)
        nd_ref[...] = jnp.where(key_ref[c] >= thr, dist, NEG_BIG)
        s_groups = []
        for g in range(N_KV_HEADS):
            kg = k_ref[0, pl.ds(r0, chunk), g * HEAD_DIM:(g + 1) * HEAD_DIM]
            s_groups.append(lax.dot_general(kg, qs_ref[g], nt, preferred_element_type=F32))
        for g in range(N_KV_HEADS):
            s_all = s_groups[g]
            probs = []
            alphas = []
            for r in range(rep):
                h = g * rep + r
                slope = float(2.0 ** (-8.0 * (h + 1) / N_HEADS)) * LOG2E
                sr = s_all[:, r * Q_TILE:(r + 1) * Q_TILE] + slope * nd_ref[...]
                m_old = m_ref[h]
                m_new = jnp.maximum(m_old, jnp.max(sr, axis=0, keepdims=True))
                alpha = jnp.exp2(m_old - m_new)
                p = jnp.exp2(sr - m_new)
                l_ref[h] = alpha * l_ref[h] + jnp.sum(p, axis=0, keepdims=True)
                m_ref[h] = m_new
                probs.append(p.astype(BF16))
                alphas.append(alpha)
            vt = vt_ref[c, g * HEAD_DIM:(g + 1) * HEAD_DIM, :]
            pv = jnp.dot(vt, jnp.concatenate(probs, axis=1), preferred_element_type=F32)
            acc_ref[g] = jnp.concatenate(alphas, axis=1) * acc_ref[g] + pv
        return carry

    lax.fori_loop(0, n_chunks, attend_chunk, 0)

    for g in range(N_KV_HEADS):
        for r in range(rep):
            h = g * rep + r
            o_t = acc_ref[g, :, r * Q_TILE:(r + 1) * Q_TILE] * (1.0 / l_ref[h])
            o_ref[0, :, h * HEAD_DIM:(h + 1) * HEAD_DIM] = o_t.T.astype(o_ref.dtype)


def _dsa_attention(qkvi, small, b, s):
    d_attn = N_HEADS * HEAD_DIM
    d_kv = N_KV_HEADS * HEAD_DIM
    d_qi = IDX_HEADS * IDX_DIM
    rep = N_HEADS // N_KV_HEADS
    chunk = min(KEY_CHUNK, s)
    topk = min(TOPK_MAX, s // 4)
    assert s % chunk == 0 and chunk % Q_TILE == 0 and d_attn % d_qi == 0 and d_attn % d_kv == 0
    assert Q_TILE == LANES and HEAD_DIM == LANES and 2 * IDX_DIM == LANES
    kern = functools.partial(_dsa_kernel, seq=s, chunk=chunk, topk=topk)
    return pl.pallas_call(
        kern,
        grid=(b, s // Q_TILE),
        in_specs=[pl.BlockSpec((1, Q_TILE, d_attn), lambda bi, t: (bi, t, 0)),
                  pl.BlockSpec((1, Q_TILE, d_qi), lambda bi, t: (bi, t, (d_attn + 2 * d_kv) // d_qi)),
                  pl.BlockSpec((1, s, d_kv), lambda bi, t: (bi, 0, d_attn // d_kv)),
                  pl.BlockSpec((1, s, d_kv), lambda bi, t: (bi, 0, d_attn // d_kv + 1)),
                  pl.BlockSpec((1, s, LANES), lambda bi, t: (bi, 0, 0)),
                  pl.BlockSpec((1, Q_TILE, LANES), lambda bi, t: (bi, t, 0))],
        out_specs=pl.BlockSpec((1, Q_TILE, d_attn), lambda bi, t: (bi, t, 0)),
        out_shape=jax.ShapeDtypeStruct((b, s, d_attn), BF16),
        scratch_shapes=[pltpu.VMEM((s, LANES), BF16),
                        pltpu.VMEM((s, LANES), BF16),
                        pltpu.VMEM((s // chunk, d_kv, chunk), BF16),
                        pltpu.VMEM((s // chunk, chunk, Q_TILE), I32),
                        pltpu.VMEM((chunk, Q_TILE), F32),
                        pltpu.VMEM((N_KV_HEADS, rep * Q_TILE, HEAD_DIM), BF16),
                        pltpu.VMEM((IDX_HEADS // 4, 2 * Q_TILE, LANES), BF16),
                        pltpu.VMEM((N_HEADS, 1, Q_TILE), F32),
                        pltpu.VMEM((N_HEADS, 1, Q_TILE), F32),
                        pltpu.VMEM((N_KV_HEADS, HEAD_DIM, rep * Q_TILE), F32)],
        compiler_params=_cparams(("parallel", "arbitrary"), 48),
        name="dsa_attention",
    )(qkvi, qkvi, qkvi, qkvi, small, small)


def _pack_bf16_halves(hb):
    half = hb.shape[1] // 2
    bits = lax.bitcast_convert_type(hb.astype(F32), I32)
    return lax.shift_right_logical(bits[:, :half], 16) | (bits[:, half:] & -65536)


def _unpack_bf16_halves(words):
    lo = lax.bitcast_convert_type(words << 16, F32).astype(BF16)
    hi = lax.bitcast_convert_type(words & -65536, F32).astype(BF16)
    return lo, hi


def _outproj_kernel(conv_ref, attn_ref, wc_ref, wa_ref, x_ref, g_ref, b_ref, wr_ref, br_ref,
                    h_ref, hp_ref, lg_ref, hb_ref, *, alpha):
    j = pl.program_id(1)
    tm, d = h_ref.shape
    tn = x_ref.shape[1]
    n_tiles = d // tn
    mix = jnp.dot(conv_ref[...], wc_ref[...], preferred_element_type=F32)
    mix = mix + jnp.dot(attn_ref[...], wa_ref[...], preferred_element_type=F32)
    z = alpha * x_ref[...] + mix
    for jj in range(n_tiles):
        @pl.when(j == jj)
        def _(jj=jj):
            h_ref[:, jj * tn:(jj + 1) * tn] = z

    @pl.when(j == n_tiles - 1)
    def _():
        def rows(i, carry):
            r0 = pl.multiple_of(i * BF16_ROWS, BF16_ROWS)
            h = _layer_norm_rows(h_ref[pl.ds(r0, BF16_ROWS), :], g_ref[...], b_ref[...])
            hb = h.astype(BF16)
            h_ref[pl.ds(r0, BF16_ROWS), :] = h
            hb_ref[pl.ds(r0, BF16_ROWS), :] = hb
            hp_ref[pl.ds(r0, BF16_ROWS), :] = _pack_bf16_halves(hb)
            return carry
        lax.fori_loop(0, tm // BF16_ROWS, rows, 0, unroll=2)
        lg_ref[...] = jnp.dot(hb_ref[...], wr_ref[...], preferred_element_type=F32) + br_ref[...]


def _out_proj_ln(conv_out, attn_out, w_conv, w_attn, x2, ln_g, ln_b, w_router, b_router, alpha):
    n, d = x2.shape
    dc = conv_out.shape[1]
    da = attn_out.shape[1]
    tm = min(512, n)
    tn = min(512, d)
    assert n % tm == 0 and d % tn == 0
    kern = functools.partial(_outproj_kernel, alpha=alpha)
    return pl.pallas_call(
        kern,
        grid=(n // tm, d // tn),
        in_specs=[pl.BlockSpec((tm, dc), lambda i, j: (i, 0)),
                  pl.BlockSpec((tm, da), lambda i, j: (i, 0)),
                  pl.BlockSpec((dc, tn), lambda i, j: (0, j)),
                  pl.BlockSpec((da, tn), lambda i, j: (0, j)),
                  pl.BlockSpec((tm, tn), lambda i, j: (i, j)),
                  pl.BlockSpec((1, d), lambda i, j: (0, 0)),
                  pl.BlockSpec((1, d), lambda i, j: (0, 0)),
                  pl.BlockSpec((d, LANES), lambda i, j: (0, 0)),
                  pl.BlockSpec((1, LANES), lambda i, j: (0, 0))],
        out_specs=[pl.BlockSpec((tm, d), lambda i, j: (i, 0)),
                   pl.BlockSpec((tm, d // 2), lambda i, j: (i, 0)),
                   pl.BlockSpec((tm, LANES), lambda i, j: (i, 0))],
        out_shape=[jax.ShapeDtypeStruct((n, d), F32), jax.ShapeDtypeStruct((n, d // 2), I32),
                   jax.ShapeDtypeStruct((n, LANES), F32)],
        scratch_shapes=[pltpu.VMEM((tm, d), BF16)],
        compiler_params=_cparams(("parallel", "arbitrary"), 56),
        name="out_proj_ln1",
    )(conv_out, attn_out, w_conv, w_attn, x2, ln_g, ln_b, w_router, b_router)


def _first_argmax(vals, lane, valid):
    masked = jnp.where(valid, vals, -jnp.inf)
    mx = jnp.max(masked, axis=1, keepdims=True)
    idx = jnp.min(jnp.where(valid & (masked == mx), lane, LANES), axis=1, keepdims=True)
    return mx, idx


def _router_kernel(lg_ref, tri_ref, e_ref, gate_ref, cnt_ref):
    @pl.when(pl.program_id(0) == 0)
    def _():
        cnt_ref[...] = jnp.zeros(cnt_ref.shape, F32)

    lg = lg_ref[...]
    lane = lax.broadcasted_iota(I32, lg.shape, 1)
    is_group = lane < N_GROUPS
    g_max, g_sel = _first_argmax(lg, lane, is_group)
    g_den = jnp.sum(jnp.where(is_group, jnp.exp(lg - g_max), 0.0), axis=1, keepdims=True)
    p_group = 1.0 / g_den
    lo = N_GROUPS + g_sel * EXPERTS_PER_GROUP
    in_group = (lane >= lo) & (lane < lo + EXPERTS_PER_GROUP)
    e_max, _ = _first_argmax(lg, lane, in_group)
    e_exp = jnp.where(in_group, jnp.exp(lg - e_max), 0.0)
    prob = e_exp / jnp.sum(e_exp, axis=1, keepdims=True)
    p1, i1 = _first_argmax(prob, lane, in_group)
    p2, i2 = _first_argmax(prob, lane, in_group & (lane != i1))
    norm = p_group / (p1 + p2)
    gate_ref[...] = jnp.where(lane == 0, p1 * norm, jnp.where(lane == 1, p2 * norm, 0.0))
    chosen = jnp.where((lane == i1) | (lane == i2), 1.0, 0.0)
    before = jnp.dot(tri_ref[...], chosen.astype(BF16), preferred_element_type=F32) + cnt_ref[...]
    rank1 = jnp.sum(jnp.where(lane == i1, before, 0.0), axis=1, keepdims=True).astype(I32)
    rank2 = jnp.sum(jnp.where(lane == i2, before, 0.0), axis=1, keepdims=True).astype(I32)
    cnt_ref[...] += jnp.sum(chosen, axis=0, keepdims=True)
    e_ref[...] = jnp.where(lane == 0, i1 - N_GROUPS,
                           jnp.where(lane == 1, i2 - N_GROUPS,
                                     jnp.where(lane == 2, rank1, jnp.where(lane == 3, rank2, 0))))


def _route(logits):
    n = logits.shape[0]
    tm = min(512, n)
    tri = jnp.tril(jnp.ones((tm, tm), BF16), -1)
    spec = pl.BlockSpec((tm, LANES), lambda i: (i, 0))
    return pl.pallas_call(
        _router_kernel,
        grid=(n // tm,),
        in_specs=[spec, pl.BlockSpec((tm, tm), lambda i: (0, 0))],
        out_specs=[spec, spec, pl.BlockSpec((1, LANES), lambda i: (0, 0))],
        out_shape=[jax.ShapeDtypeStruct((n, LANES), I32), jax.ShapeDtypeStruct((n, LANES), F32),
                   jax.ShapeDtypeStruct((1, LANES), F32)],
        compiler_params=_cparams(("arbitrary",), 32),
        name="router",
    )(logits, tri)


def _start_row_copies(src_hbm, idx_ref, idx_row, dst_ref, slot, row0, sem, count):
    group = 8

    def issue(i, carry):
        for k in range(group):
            r = i * group + k
            pltpu.make_async_copy(src_hbm.at[pl.ds(idx_ref[0, idx_row, r], 1), :],
                                  dst_ref.at[slot, pl.ds(row0 + r, 1), :], sem.at[slot]).start(priority=k % 2)
        return carry
    lax.fori_loop(0, count // group, issue, 0)


def _wait_slot(src_hbm, dst_ref, slot, sem):
    rows = dst_ref.shape[1]
    pltpu.make_async_copy(src_hbm.at[pl.ds(0, rows), :], dst_ref.at[slot], sem.at[slot]).wait()


def _dispatch_kernel(fill_ref, pos_ref, hp_hbm, xs_hbm, zero_ref, fill_sem, sem, *, toks):
    i = pl.program_id(0)
    slot = i % 2
    n_rows = 2 * toks

    def fill_copy(k):
        blk = jnp.maximum(fill_ref[k], 0)
        return pltpu.make_async_copy(zero_ref, xs_hbm.at[pl.ds(blk * MOE_ROWS, MOE_ROWS), :], fill_sem)

    @pl.when(i == 0)
    def _():
        zero_ref[...] = jnp.zeros(zero_ref.shape, zero_ref.dtype)

        def start(k, carry):
            @pl.when(fill_ref[k] >= 0)
            def _():
                fill_copy(k).start()
            return carry
        lax.fori_loop(0, fill_ref.shape[0], start, 0)

        def wait(k, carry):
            @pl.when(fill_ref[k] >= 0)
            def _():
                fill_copy(k).wait()
            return carry
        lax.fori_loop(0, fill_ref.shape[0], wait, 0)

    group = 8

    def issue(gi, carry):
        for k in range(group):
            r = gi * group + k
            tok = i * toks + gi * (group // 2) + k // 2
            pltpu.make_async_copy(hp_hbm.at[pl.ds(tok, 1), :], xs_hbm.at[pl.ds(pos_ref[0, 0, r], 1), :],
                                  sem.at[slot]).start(priority=k % 2)
        return carry
    lax.fori_loop(0, n_rows // group, issue, 0)

    def wait_step(which):
        pltpu.make_async_copy(hp_hbm.at[pl.ds(0, n_rows), :], xs_hbm.at[pl.ds(0, n_rows), :], sem.at[which]).wait()

    @pl.when(i > 0)
    def _():
        wait_step(1 - slot)

    @pl.when(i == pl.num_programs(0) - 1)
    def _():
        wait_step(slot)


def _dispatch_tokens(hp, pos, fill_blocks, n_blocks):
    n, half = hp.shape
    toks = min(256, n)
    steps = n // toks
    kern = functools.partial(_dispatch_kernel, toks=toks)
    return pl.pallas_call(
        kern,
        grid_spec=pltpu.PrefetchScalarGridSpec(
            num_scalar_prefetch=1,
            grid=(steps,),
            in_specs=[pl.BlockSpec((1, 1, 2 * toks), lambda i, f: (i, 0, 0), memory_space=pltpu.SMEM),
                      pl.BlockSpec(memory_space=pl.ANY)],
            out_specs=pl.BlockSpec(memory_space=pl.ANY),
            scratch_shapes=[pltpu.VMEM((MOE_ROWS, half), I32), pltpu.SemaphoreType.DMA(()),
                            pltpu.SemaphoreType.DMA((2,))]),
        out_shape=jax.ShapeDtypeStruct((n_blocks * MOE_ROWS, half), I32),
        compiler_params=_cparams(("arbitrary",), 32),
        name="moe_dispatch",
    )(fill_blocks, pos.reshape(steps, 1, 2 * toks), hp)


def _combine_kernel(pos_ref, pos_next_ref, y_hbm, h_ref, gate_ref, g_ref, b_ref, o_ref, buf_ref, sem, *, alpha):
    i = pl.program_id(0)
    n = pl.num_programs(0)
    slot = i % 2
    rows = h_ref.shape[0]

    def start(idx_ref, into):
        for k in range(2):
            _start_row_copies(y_hbm, idx_ref, k, buf_ref, into, k * rows, sem, rows)

    @pl.when(i == 0)
    def _():
        start(pos_ref, 0)

    @pl.when(i + 1 < n)
    def _():
        start(pos_next_ref, 1 - slot)

    _wait_slot(y_hbm, buf_ref, slot, sem)

    def norm_rows(j, carry):
        r0 = pl.multiple_of(j * SUBLANES, SUBLANES)
        gates = gate_ref[pl.ds(r0, SUBLANES), :]
        ffn = (buf_ref[slot, pl.ds(r0, SUBLANES), :] * gates[:, 0:1]
               + buf_ref[slot, pl.ds(rows + r0, SUBLANES), :] * gates[:, 1:2])
        z = alpha * h_ref[pl.ds(r0, SUBLANES), :] + ffn
        o_ref[pl.ds(r0, SUBLANES), :] = _layer_norm_rows(z, g_ref[...], b_ref[...])
        return carry
    lax.fori_loop(0, rows // SUBLANES, norm_rows, 0, unroll=2)


def _combine_ln(y, pos2, h1, gate_lanes, ln_g, ln_b, alpha):
    n, d = h1.shape
    tb = min(COMBINE_ROWS, n)
    nblk = n // tb
    pos_blocks = pos2.reshape(nblk, tb, 2).transpose(0, 2, 1)
    kern = functools.partial(_combine_kernel, alpha=alpha)
    return pl.pallas_call(
        kern,
        grid=(nblk,),
        in_specs=[pl.BlockSpec((1, 2, tb), lambda i: (i, 0, 0), memory_space=pltpu.SMEM),
                  pl.BlockSpec((1, 2, tb), lambda i: (jnp.minimum(i + 1, nblk - 1), 0, 0), memory_space=pltpu.SMEM),
                  pl.BlockSpec(memory_space=pl.ANY),
                  pl.BlockSpec((tb, d), lambda i: (i, 0)),
                  pl.BlockSpec((tb, LANES), lambda i: (i, 0)),
                  pl.BlockSpec((1, d), lambda i: (0, 0)),
                  pl.BlockSpec((1, d), lambda i: (0, 0))],
        out_specs=pl.BlockSpec((tb, d), lambda i: (i, 0)),
        out_shape=jax.ShapeDtypeStruct((n, d), F32),
        scratch_shapes=[pltpu.VMEM((2, 2 * tb, d), F32), pltpu.SemaphoreType.DMA((2,))],
        compiler_params=_cparams(("arbitrary",), 32),
        name="moe_combine_ln2",
    )(pos_blocks, pos_blocks, y, h1, gate_lanes, ln_g, ln_b)


def _cast_weight(dst_ref, src_ref):
    rows = src_ref.shape[1]
    step = min(256, rows)

    def body(i, carry):
        r0 = pl.multiple_of(i * step, step)
        dst_ref[pl.ds(r0, step), :] = src_ref[0, pl.ds(r0, step), :].astype(BF16)
        return carry
    lax.fori_loop(0, rows // step, body, 0)


ITEM_COMPUTE = 1
ITEM_NEW_WEIGHTS = 2
ITEM_ZERO_FILL = 4


def _moe_up_kernel(e_ref, cw_ref, bi_ref, bo_ref, co_ref, flag_ref, x_ref, w1_ref, w3_ref, o_ref, w1b_ref, w3b_ref):
    flag = flag_ref[pl.program_id(0)]

    @pl.when((flag & ITEM_NEW_WEIGHTS) != 0)
    def _():
        _cast_weight(w1b_ref, w1_ref)
        _cast_weight(w3b_ref, w3_ref)

    @pl.when((flag & ITEM_COMPUTE) != 0)
    def _():
        x_lo, x_hi = _unpack_bf16_halves(x_ref[...])
        half = x_lo.shape[1]

        def project(w_ref):
            return (jnp.dot(x_lo, w_ref[0:half, :], preferred_element_type=F32)
                    + jnp.dot(x_hi, w_ref[half:2 * half, :], preferred_element_type=F32))

        a = project(w1b_ref)
        b = project(w3b_ref)
        o_ref[...] = (a * jax.nn.sigmoid(a) * b).astype(o_ref.dtype)

    @pl.when((flag & ITEM_ZERO_FILL) != 0)
    def _():
        o_ref[...] = jnp.zeros(o_ref.shape, o_ref.dtype)


def _moe_down_kernel(e_ref, cw_ref, bi_ref, bo_ref, co_ref, flag_ref, h_ref, w2_ref, o_ref, w2b_ref):
    flag = flag_ref[pl.program_id(0)]

    @pl.when((flag & ITEM_NEW_WEIGHTS) != 0)
    def _():
        _cast_weight(w2b_ref, w2_ref)

    @pl.when((flag & ITEM_COMPUTE) != 0)
    def _():
        o_ref[...] = jnp.dot(h_ref[...], w2b_ref[...], preferred_element_type=F32)

    @pl.when((flag & ITEM_ZERO_FILL) != 0)
    def _():
        o_ref[...] = jnp.zeros(o_ref.shape, o_ref.dtype)


def _work_items(blocks_per_expert, block_start, n_chunks, n_blocks):
    n_items = n_chunks * n_blocks
    per_e = n_chunks * blocks_per_expert
    end = jnp.cumsum(per_e)
    start = end - per_e
    it = jnp.arange(n_items, dtype=I32)
    total = end[-1]
    used_blocks = total // n_chunks
    itc = jnp.minimum(it, total - 1)
    e = jnp.sum(itc[:, None] >= end[None, :], axis=1).astype(I32)
    local = itc - start[e]
    nb = jnp.maximum(blocks_per_expert[e], 1)
    c = local // nb
    r = local - c * nb
    b_in = block_start[e] + r
    active = it < total
    spare = jnp.maximum(it - total, 0)
    b_out = jnp.where(active, b_in, used_blocks + spare // n_chunks)
    c_out = jnp.where(active, c, spare % n_chunks)
    flag = jnp.where(active, ITEM_COMPUTE + ITEM_NEW_WEIGHTS * (r == 0).astype(I32), ITEM_ZERO_FILL)
    return tuple(v.astype(I32) for v in (e, c, b_in, b_out, c_out, flag))


def _moe_up(xs, w1, w3, items, f_chunk):
    p, half = xs.shape
    d, f = w1.shape[1], w1.shape[2]
    assert d == 2 * half
    wspec = pl.BlockSpec((1, d, f_chunk), lambda it, e, cw, bi, bo, co, fl: (e[it], 0, cw[it]))
    return pl.pallas_call(
        _moe_up_kernel,
        grid_spec=pltpu.PrefetchScalarGridSpec(
            num_scalar_prefetch=6,
            grid=(items[0].shape[0],),
            in_specs=[pl.BlockSpec((MOE_ROWS, half), lambda it, e, cw, bi, bo, co, fl: (bi[it], 0)), wspec, wspec],
            out_specs=pl.BlockSpec((MOE_ROWS, f_chunk), lambda it, e, cw, bi, bo, co, fl: (bo[it], co[it])),
            scratch_shapes=[pltpu.VMEM((d, f_chunk), BF16), pltpu.VMEM((d, f_chunk), BF16)]),
        out_shape=jax.ShapeDtypeStruct((p, f), BF16),
        compiler_params=_cparams(("arbitrary",), 56),
        name="moe_up",
    )(*items, xs, w1, w3)


def _moe_down(hmid, w2, items, d_chunk):
    p, f = hmid.shape
    d = w2.shape[2]
    return pl.pallas_call(
        _moe_down_kernel,
        grid_spec=pltpu.PrefetchScalarGridSpec(
            num_scalar_prefetch=6,
            grid=(items[0].shape[0],),
            in_specs=[pl.BlockSpec((MOE_ROWS, f), lambda it, e, cw, bi, bo, co, fl: (bi[it], 0)),
                      pl.BlockSpec((1, f, d_chunk), lambda it, e, cw, bi, bo, co, fl: (e[it], 0, cw[it]))],
            out_specs=pl.BlockSpec((MOE_ROWS, d_chunk), lambda it, e, cw, bi, bo, co, fl: (bo[it], co[it])),
            scratch_shapes=[pltpu.VMEM((f, d_chunk), BF16)]),
        out_shape=jax.ShapeDtypeStruct((p, d), F32),
        compiler_params=_cparams(("arbitrary",), 48),
        name="moe_down",
    )(*items, hmid, w2)


def _hier_moe_ln(h1, hp, logits, w1, w3, w2, ln_g, ln_b, alpha):
    n, d = h1.shape
    e_lanes, gate_lanes, lane_counts = _route(logits)
    a = 2 * n
    e_flat = e_lanes[:, 0:2].reshape(a)
    rank = e_lanes[:, 2:4].reshape(a)
    counts = lane_counts[0, N_GROUPS:N_GROUPS + N_EXPERTS].astype(I32)
    blocks_per_expert = (counts + MOE_ROWS - 1) // MOE_ROWS
    block_start = jnp.cumsum(blocks_per_expert) - blocks_per_expert
    onehot = e_flat[:, None] == jnp.arange(N_EXPERTS, dtype=I32)[None, :]
    pos = jnp.sum(jnp.where(onehot, block_start[None, :], 0), axis=1) * MOE_ROWS + rank
    n_blocks = (a + MOE_ROWS - 1) // MOE_ROWS + N_EXPERTS
    last_block = jnp.where(counts > 0, block_start + blocks_per_expert - 1, -1)
    all_blocks = jnp.arange(n_blocks, dtype=I32)
    spare_blocks = jnp.where(all_blocks >= jnp.sum(blocks_per_expert), all_blocks, -1)
    fill_blocks = jnp.concatenate([last_block, spare_blocks]).astype(I32)

    f = w1.shape[2]
    f_chunk = min(512, f)
    d_chunk = min(2048, d)
    xs = _dispatch_tokens(hp, pos, fill_blocks, n_blocks)
    hmid = _moe_up(xs, w1, w3, _work_items(blocks_per_expert, block_start, f // f_chunk, n_blocks), f_chunk)
    y = _moe_down(hmid, w2, _work_items(blocks_per_expert, block_start, d // d_chunk, n_blocks), d_chunk)
    return _combine_ln(y, pos.reshape(n, 2), h1, gate_lanes, ln_g, ln_b, alpha)


def kernel(x, w_in, conv_dw_w, conv_dw_b, conv_ln_g, conv_ln_b, w_out, ln1_g, ln1_b, w_router_group, b_router_group, w_router_expert, b_router_expert, w_expert_gate, w_expert_up, w_expert_down, ln2_g, ln2_b):
    b, s, d = x.shape
    depth = w_in.shape[0]
    assert depth == 1
    alpha = float((2.0 * depth) ** 0.25)
    n = b * s
    dc = conv_dw_w.shape[2]
    d_attn = N_HEADS * HEAD_DIM
    d_kv = N_KV_HEADS * HEAD_DIM
    d_qi = IDX_HEADS * IDX_DIM
    n_small = IDX_DIM + IDX_HEADS
    assert w_in.shape[2] == 2 * dc + d_attn + 2 * d_kv + d_qi + n_small

    x2 = x.reshape(n, d)
    xb = x2.astype(BF16)
    w = w_in[0]
    o_qkvi = 2 * dc
    o_small = o_qkvi + d_attn + 2 * d_kv + d_qi
    wa = w[:, :dc].astype(BF16)
    wg = w[:, dc:2 * dc].astype(BF16)
    w_qkvi = w[:, o_qkvi:o_small].astype(BF16)
    w_small = jnp.pad(w[:, o_small:], ((0, 0), (0, LANES - n_small))).astype(BF16)
    qkvi_scale = jnp.concatenate([jnp.full((d_attn,), HEAD_DIM ** -0.5 * LOG2E, F32), jnp.ones((2 * d_kv,), F32),
                                  jnp.full((d_qi,), IDX_DIM ** -0.5, F32)])[None, :]

    glu = _proj_glu(xb, wa, wg)
    qkvi = _proj_scale(xb, w_qkvi, qkvi_scale, BF16, "proj_qkvi")
    small = _proj_scale(xb, w_small, jnp.ones((1, LANES), F32), F32, "proj_idx")

    conv_out = _conformer_conv(glu.reshape(b, s, dc), conv_dw_w[0], conv_dw_b, conv_ln_g, conv_ln_b)
    attn_out = _dsa_attention(qkvi.reshape(b, s, -1), small.reshape(b, s, LANES), b, s)

    n_route = N_GROUPS + N_EXPERTS
    w_router = jnp.pad(jnp.concatenate([w_router_group[0], w_router_expert[0]], axis=1),
                       ((0, 0), (0, LANES - n_route))).astype(BF16)
    b_router = jnp.pad(jnp.concatenate([b_router_group[0], b_router_expert[0]]), (0, LANES - n_route))[None, :]
    h1, hp, logits = _out_proj_ln(conv_out.reshape(n, dc), attn_out.reshape(n, d_attn),
                                  w_out[0, :dc].astype(BF16), w_out[0, dc:].astype(BF16),
                                  x2, ln1_g, ln1_b, w_router, b_router, alpha)

    out = _hier_moe_ln(h1, hp, logits, w_expert_gate[0], w_expert_up[0], w_expert_down[0], ln2_g, ln2_b, alpha)
    return out.reshape(b, s, d)
```

```python
import functools
import math

import jax
import jax.numpy as jnp
from jax import lax
from jax.experimental import pallas as pl
from jax.experimental.pallas import tpu as pltpu

F32 = jnp.float32
BF16 = jnp.bfloat16
I32 = jnp.int32

CONV_WIDTH = 31
N_HEADS = 16
HEAD_DIM = 128
N_KV_HEADS = 4
IDX_HEADS = 16
IDX_DIM = 64
TOPK_MAX = 256
N_GROUPS = 4
EXPERTS_PER_GROUP = 8
N_EXPERTS = N_GROUPS * EXPERTS_PER_GROUP
LN_EPS = 1e-5

LANES = 128
SUBLANES = 8
BF16_ROWS = 16
MIB = 1024 * 1024

Q_TILE = 128
KEY_CHUNK = 512
CONV_HALO = 32
MOE_ROWS = 256
COMBINE_ROWS = 128
INT_MIN = -(2 ** 31)
LOG2E = math.log2(math.e)
NEG_BIG = -1e30
M_INIT = -1e20


def _cparams(semantics, vmem_mib):
    return pltpu.CompilerParams(dimension_semantics=semantics, vmem_limit_bytes=vmem_mib * MIB)


def _layer_norm_rows(z, g, b):
    mu = jnp.mean(z, axis=-1, keepdims=True)
    zc = z - mu
    var = jnp.mean(zc * zc, axis=-1, keepdims=True)
    return zc * lax.rsqrt(var + LN_EPS) * g + b


def _mm_glu_kernel(x_ref, wa_ref, wg_ref, o_ref):
    x = x_ref[...]
    a = jnp.dot(x, wa_ref[...], preferred_element_type=F32)
    g = jnp.dot(x, wg_ref[...], preferred_element_type=F32)
    o_ref[...] = (a * jax.nn.sigmoid(g)).astype(o_ref.dtype)


def _mm_scale_kernel(x_ref, w_ref, s_ref, o_ref):
    acc = jnp.dot(x_ref[...], w_ref[...], preferred_element_type=F32)
    o_ref[...] = (acc * s_ref[...]).astype(o_ref.dtype)


def _mm_tiles(n, k, cols):
    tm = min(1024, n)
    tn = min(512, cols)
    assert n % tm == 0 and cols % tn == 0
    return tm, tn


def _proj_glu(xb, wa, wg):
    n, k = xb.shape
    cols = wa.shape[1]
    tm, tn = _mm_tiles(n, k, cols)
    return pl.pallas_call(
        _mm_glu_kernel,
        grid=(n // tm, cols // tn),
        in_specs=[pl.BlockSpec((tm, k), lambda i, j: (i, 0)),
                  pl.BlockSpec((k, tn), lambda i, j: (0, j)),
                  pl.BlockSpec((k, tn), lambda i, j: (0, j))],
        out_specs=pl.BlockSpec((tm, tn), lambda i, j: (i, j)),
        out_shape=jax.ShapeDtypeStruct((n, cols), F32),
        compiler_params=_cparams(("parallel", "arbitrary"), 48),
        name="proj_glu",
    )(xb, wa, wg)


def _proj_scale(xb, w, scale, out_dtype, name):
    n, k = xb.shape
    cols = w.shape[1]
    tm, tn = _mm_tiles(n, k, cols)
    return pl.pallas_call(
        _mm_scale_kernel,
        grid=(n // tm, cols // tn),
        in_specs=[pl.BlockSpec((tm, k), lambda i, j: (i, 0)),
                  pl.BlockSpec((k, tn), lambda i, j: (0, j)),
                  pl.BlockSpec((1, tn), lambda i, j: (0, j))],
        out_specs=pl.BlockSpec((tm, tn), lambda i, j: (i, j)),
        out_shape=jax.ShapeDtypeStruct((n, cols), out_dtype),
        compiler_params=_cparams(("parallel", "arbitrary"), 48),
        name=name,
    )(xb, w, scale)


def _conv_kernel(cur_ref, prev_ref, w_ref, b_ref, g_ref, beta_ref, o_ref, sh_ref, y_ref, *, rows, lane_chunk):
    t = pl.program_id(1)
    dc = cur_ref.shape[2]
    row_sub = 32
    sh_ref[0, 0:CONV_HALO, :] = jnp.where(t > 0, prev_ref[0], 0.0)
    sh_ref[0, CONV_HALO:CONV_HALO + rows, :] = cur_ref[0]
    first = CONV_HALO - (CONV_WIDTH - 1)
    shifted_rows = rows + CONV_HALO - SUBLANES
    for lc in range(dc // lane_chunk):
        cols = slice(lc * lane_chunk, (lc + 1) * lane_chunk)
        for r in range(1, SUBLANES):
            sh_ref[r, 0:shifted_rows, cols] = sh_ref[0, r:r + shifted_rows, cols]

    for r0 in range(0, rows, row_sub):
        for lc in range(dc // lane_chunk):
            cols = slice(lc * lane_chunk, (lc + 1) * lane_chunk)
            acc = jnp.zeros((row_sub, lane_chunk), F32)
            for j in range(CONV_WIDTH):
                shift, base = (first + j) % SUBLANES, (first + j) // SUBLANES * SUBLANES
                acc = acc + sh_ref[shift, r0 + base:r0 + base + row_sub, cols] * w_ref[j:j + 1, cols]
            y_ref[r0:r0 + row_sub, cols] = acc + b_ref[:, cols]

    def norm_rows(i, carry):
        r0 = pl.multiple_of(i * BF16_ROWS, BF16_ROWS)
        yn = _layer_norm_rows(y_ref[pl.ds(r0, BF16_ROWS), :], g_ref[...], beta_ref[...])
        o_ref[0, pl.ds(r0, BF16_ROWS), :] = (yn * jax.nn.sigmoid(yn)).astype(o_ref.dtype)
        return carry

    lax.fori_loop(0, rows // BF16_ROWS, norm_rows, 0, unroll=2)


def _conformer_conv(glu, w_dw, b_dw, g_ln, b_ln):
    b, s, dc = glu.shape
    rows = min(128, s)
    halo_blocks = rows // CONV_HALO
    lane_chunk = min(512, dc)
    kern = functools.partial(_conv_kernel, rows=rows, lane_chunk=lane_chunk)
    return pl.pallas_call(
        kern,
        grid=(b, s // rows),
        in_specs=[pl.BlockSpec((1, rows, dc), lambda bi, t: (bi, t, 0)),
                  pl.BlockSpec((1, CONV_HALO, dc), lambda bi, t: (bi, jnp.maximum(t * halo_blocks - 1, 0), 0)),
                  pl.BlockSpec((CONV_WIDTH, dc), lambda bi, t: (0, 0)),
                  pl.BlockSpec((1, dc), lambda bi, t: (0, 0)),
                  pl.BlockSpec((1, dc), lambda bi, t: (0, 0)),
                  pl.BlockSpec((1, dc), lambda bi, t: (0, 0))],
        out_specs=pl.BlockSpec((1, rows, dc), lambda bi, t: (bi, t, 0)),
        out_shape=jax.ShapeDtypeStruct((b, s, dc), BF16),
        scratch_shapes=[pltpu.VMEM((SUBLANES, CONV_HALO + rows, dc), F32), pltpu.VMEM((rows, dc), F32)],
        compiler_params=_cparams(("parallel", "arbitrary"), 32),
        name="conformer_conv",
    )(glu, glu, w_dw, b_dw, g_ln, b_ln)


def _dsa_kernel(q_ref, qi_ref, k_ref, v_ref, kidx_ref, w_ref, o_ref,
                ke_ref, ko_ref, vt_ref, key_ref, nd_ref, qs_ref, qis_ref, m_ref, l_ref, acc_ref,
                *, seq, chunk, topk):
    tb = pl.program_id(1)
    t0 = tb * Q_TILE
    n_chunks = (t0 + Q_TILE + chunk - 1) // chunk
    rep = N_HEADS // N_KV_HEADS
    nt = (((1,), (1,)), ((), ()))

    @pl.when(tb == 0)
    def _():
        def build(c, carry):
            r0 = pl.multiple_of(c * chunk, chunk)
            kx = kidx_ref[0, pl.ds(r0, chunk), :]
            lane = lax.broadcasted_iota(I32, kx.shape, 1)
            ke_ref[pl.ds(r0, chunk), :] = jnp.where(lane < IDX_DIM, kx, 0.0).astype(BF16)
            ko_ref[pl.ds(r0, chunk), :] = jnp.where(lane >= IDX_DIM, pltpu.roll(kx, IDX_DIM, 1), 0.0).astype(BF16)
            for g in range(N_KV_HEADS):
                vg = v_ref[0, pl.ds(r0, chunk), g * HEAD_DIM:(g + 1) * HEAD_DIM]
                vt_ref[c, g * HEAD_DIM:(g + 1) * HEAD_DIM, :] = vg.astype(F32).T.astype(BF16)
            return carry
        lax.fori_loop(0, seq // chunk, build, 0)

    for g in range(N_KV_HEADS):
        for r in range(rep):
            h = g * rep + r
            qs_ref[g, r * Q_TILE:(r + 1) * Q_TILE, :] = q_ref[0, :, h * HEAD_DIM:(h + 1) * HEAD_DIM]
    for jj in range(IDX_HEADS // 4):
        for half in range(2):
            pair = 2 * jj + half
            qis_ref[jj, half * Q_TILE:(half + 1) * Q_TILE, :] = qi_ref[0, :, pair * LANES:(pair + 1) * LANES]
    w_t = w_ref[0].T * (IDX_HEADS ** -0.5)

    key_row = lax.broadcasted_iota(I32, (chunk, Q_TILE), 0)
    q_pos = t0 + lax.broadcasted_iota(I32, (chunk, Q_TILE), 1)

    def score_chunk(c, carry):
        r0 = pl.multiple_of(c * chunk, chunk)
        ke = ke_ref[pl.ds(r0, chunk), :]
        ko = ko_ref[pl.ds(r0, chunk), :]
        acc = jnp.zeros((chunk, Q_TILE), F32)
        for jj in range(IDX_HEADS // 4):
            rhs = qis_ref[jj]
            de = lax.dot_general(ke, rhs, nt, preferred_element_type=F32)
            do = lax.dot_general(ko, rhs, nt, preferred_element_type=F32)
            for half in range(2):
                h_even = 2 * (2 * jj + half)
                cols = slice(half * Q_TILE, (half + 1) * Q_TILE)
                acc = acc + w_t[IDX_DIM + h_even:IDX_DIM + h_even + 1, :] * jnp.maximum(de[:, cols], 0.0)
                acc = acc + w_t[IDX_DIM + h_even + 1:IDX_DIM + h_even + 2, :] * jnp.maximum(do[:, cols], 0.0)
        bits = lax.bitcast_convert_type(acc, I32)
        key = bits ^ ((bits >> 31) & 0x7FFFFFFF)
        key_ref[c] = jnp.where(r0 + key_row <= q_pos, key, INT_MIN)
        return carry

    lax.fori_loop(0, n_chunks, score_chunk, 0)

    def bit_step(i, t_u):
        cand_u = t_u | (jnp.int32(1) << (31 - i))
        cand = cand_u ^ INT_MIN

        def count_chunk(c, cnts):
            cnts = list(cnts)
            for r in range(chunk // SUBLANES):
                slab = key_ref[c, r * SUBLANES:(r + 1) * SUBLANES, :]
                cnts[r % len(cnts)] = cnts[r % len(cnts)] + jnp.where(slab >= cand, 1.0, 0.0)
            return tuple(cnts)

        zero = jnp.zeros((SUBLANES, Q_TILE), F32)
        cnts = lax.fori_loop(0, n_chunks, count_chunk, (zero, zero, zero, zero))
        tot = jnp.sum((cnts[0] + cnts[1]) + (cnts[2] + cnts[3]), axis=0, keepdims=True)
        return jnp.where(tot >= float(topk), cand_u, t_u)

    t_u = lax.fori_loop(0, 32, bit_step, jnp.zeros((1, Q_TILE), I32))
    thr = jnp.maximum(t_u ^ INT_MIN, INT_MIN + 1)

    m_ref[...] = jnp.full(m_ref.shape, M_INIT, F32)
    l_ref[...] = jnp.zeros(l_ref.shape, F32)
    acc_ref[...] = jnp.zeros(acc_ref.shape, F32)

    def attend_chunk(c, carry):
        r0 = pl.multiple_of(c * chunk, chunk)
        dist = (r0 + key_row - q_pos).astype(F32)
        nd_ref[...] = jnp.where(key_ref[c] >= thr, dist, NEG_BIG)
        s_groups = []
        for g in range(N_KV_HEADS):
            kg = k_ref[0, pl.ds(r0, chunk), g * HEAD_DIM:(g + 1) * HEAD_DIM]
            s_groups.append(lax.dot_general(kg, qs_ref[g], nt, preferred_element_type=F32))
        for g in range(N_KV_HEADS):
            s_all = s_groups[g]
            probs = []
            alphas = []
            for r in range(rep):
                h = g * rep + r
                slope = float(2.0 ** (-8.0 * (h + 1) / N_HEADS)) * LOG2E
                sr = s_all[:, r * Q_TILE:(r + 1) * Q_TILE] + slope * nd_ref[...]
                m_old = m_ref[h]
                m_new = jnp.maximum(m_old, jnp.max(sr, axis=0, keepdims=True))
                alpha = jnp.exp2(m_old - m_new)
                p = jnp.exp2(sr - m_new)
                l_ref[h] = alpha * l_ref[h] + jnp.sum(p, axis=0, keepdims=True)
                m_ref[h] = m_new
                probs.append(p.astype(BF16))
                alphas.append(alpha)
            vt = vt_ref[c, g * HEAD_DIM:(g + 1) * HEAD_DIM, :]
            pv = jnp.dot(vt, jnp.concatenate(probs, axis=1), preferred_element_type=F32)
            acc_ref[g] = jnp.concatenate(alphas, axis=1) * acc_ref[g] + pv
        return carry

    lax.fori_loop(0, n_chunks, attend_chunk, 0)

    for g in range(N_KV_HEADS):
        for r in range(rep):
            h = g * rep + r
            o_t = acc_ref[g, :, r * Q_TILE:(r + 1) * Q_TILE] * (1.0 / l_ref[h])
            o_ref[0, :, h * HEAD_DIM:(h + 1) * HEAD_DIM] = o_t.T.astype(o_ref.dtype)


def _dsa_attention(qkvi, small, b, s):
    d_attn = N_HEADS * HEAD_DIM
    d_kv = N_KV_HEADS * HEAD_DIM
    d_qi = IDX_HEADS * IDX_DIM
    rep = N_HEADS // N_KV_HEADS
    chunk = min(KEY_CHUNK, s)
    topk = min(TOPK_MAX, s // 4)
    assert s % chunk == 0 and chunk % Q_TILE == 0 and d_attn % d_qi == 0 and d_attn % d_kv == 0
    assert Q_TILE == LANES and HEAD_DIM == LANES and 2 * IDX_DIM == LANES
    kern = functools.partial(_dsa_kernel, seq=s, chunk=chunk, topk=topk)
    return pl.pallas_call(
        kern,
        grid=(b, s // Q_TILE),
        in_specs=[pl.BlockSpec((1, Q_TILE, d_attn), lambda bi, t: (bi, t, 0)),
                  pl.BlockSpec((1, Q_TILE, d_qi), lambda bi, t: (bi, t, (d_attn + 2 * d_kv) // d_qi)),
                  pl.BlockSpec((1, s, d_kv), lambda bi, t: (bi, 0, d_attn // d_kv)),
                  pl.BlockSpec((1, s, d_kv), lambda bi, t: (bi, 0, d_attn // d_kv + 1)),
                  pl.BlockSpec((1, s, LANES), lambda bi, t: (bi, 0, 0)),
                  pl.BlockSpec((1, Q_TILE, LANES), lambda bi, t: (bi, t, 0))],
        out_specs=pl.BlockSpec((1, Q_TILE, d_attn), lambda bi, t: (bi, t, 0)),
        out_shape=jax.ShapeDtypeStruct((b, s, d_attn), BF16),
        scratch_shapes=[pltpu.VMEM((s, LANES), BF16),
                        pltpu.VMEM((s, LANES), BF16),
                        pltpu.VMEM((s // chunk, d_kv, chunk), BF16),
                        pltpu.VMEM((s // chunk, chunk, Q_TILE), I32),
                        pltpu.VMEM((chunk, Q_TILE), F32),
                        pltpu.VMEM((N_KV_HEADS, rep * Q_TILE, HEAD_DIM), BF16),
                        pltpu.VMEM((IDX_HEADS // 4, 2 * Q_TILE, LANES), BF16),
                        pltpu.VMEM((N_HEADS, 1, Q_TILE), F32),
                        pltpu.VMEM((N_HEADS, 1, Q_TILE), F32),
                        pltpu.VMEM((N_KV_HEADS, HEAD_DIM, rep * Q_TILE), F32)],
        compiler_params=_cparams(("parallel", "arbitrary"), 48),
        name="dsa_attention",
    )(qkvi, qkvi, qkvi, qkvi, small, small)


def _pack_bf16_halves(hb):
    half = hb.shape[1] // 2
    bits = lax.bitcast_convert_type(hb.astype(F32), I32)
    return lax.shift_right_logical(bits[:, :half], 16) | (bits[:, half:] & -65536)


def _unpack_bf16_halves(words):
    lo = lax.bitcast_convert_type(words << 16, F32).astype(BF16)
    hi = lax.bitcast_convert_type(words & -65536, F32).astype(BF16)
    return lo, hi


def _outproj_kernel(conv_ref, attn_ref, wc_ref, wa_ref, x_ref, g_ref, b_ref, wr_ref, br_ref,
                    h_ref, hp_ref, lg_ref, hb_ref, *, alpha):
    j = pl.program_id(1)
    tm, d = h_ref.shape
    tn = x_ref.shape[1]
    n_tiles = d // tn
    mix = jnp.dot(conv_ref[...], wc_ref[...], preferred_element_type=F32)
    mix = mix + jnp.dot(attn_ref[...], wa_ref[...], preferred_element_type=F32)
    z = alpha * x_ref[...] + mix
    for jj in range(n_tiles):
        @pl.when(j == jj)
        def _(jj=jj):
            h_ref[:, jj * tn:(jj + 1) * tn] = z

    @pl.when(j == n_tiles - 1)
    def _():
        def rows(i, carry):
            starts = [pl.multiple_of((2 * i + k) * BF16_ROWS, BF16_ROWS) for k in range(2)]
            zs = [h_ref[pl.ds(r0, BF16_ROWS), :] for r0 in starts]
            hs = [_layer_norm_rows(z, g_ref[...], b_ref[...]) for z in zs]
            for r0, h in zip(starts, hs):
                h_ref[pl.ds(r0, BF16_ROWS), :] = h
                hb_ref[pl.ds(r0, BF16_ROWS), :] = h.astype(BF16)
            return carry
        lax.fori_loop(0, tm // (2 * BF16_ROWS), rows, 0)

        def pack_rows(i, carry):
            r0 = pl.multiple_of(i * BF16_ROWS, BF16_ROWS)
            hp_ref[pl.ds(r0, BF16_ROWS), :] = _pack_bf16_halves(hb_ref[pl.ds(r0, BF16_ROWS), :])
            return carry
        lax.fori_loop(0, tm // BF16_ROWS, pack_rows, 0, unroll=2)
        lg_ref[...] = jnp.dot(hb_ref[...], wr_ref[...], preferred_element_type=F32) + br_ref[...]


def _out_proj_ln(conv_out, attn_out, w_conv, w_attn, x2, ln_g, ln_b, w_router, b_router, alpha):
    n, d = x2.shape
    dc = conv_out.shape[1]
    da = attn_out.shape[1]
    tm = min(512, n)
    tn = min(512, d)
    assert n % tm == 0 and d % tn == 0
    kern = functools.partial(_outproj_kernel, alpha=alpha)
    return pl.pallas_call(
        kern,
        grid=(n // tm, d // tn),
        in_specs=[pl.BlockSpec((tm, dc), lambda i, j: (i, 0)),
                  pl.BlockSpec((tm, da), lambda i, j: (i, 0)),
                  pl.BlockSpec((dc, tn), lambda i, j: (0, j)),
                  pl.BlockSpec((da, tn), lambda i, j: (0, j)),
                  pl.BlockSpec((tm, tn), lambda i, j: (i, j)),
                  pl.BlockSpec((1, d), lambda i, j: (0, 0)),
                  pl.BlockSpec((1, d), lambda i, j: (0, 0)),
                  pl.BlockSpec((d, LANES), lambda i, j: (0, 0)),
                  pl.BlockSpec((1, LANES), lambda i, j: (0, 0))],
        out_specs=[pl.BlockSpec((tm, d), lambda i, j: (i, 0)),
                   pl.BlockSpec((tm, d // 2), lambda i, j: (i, 0)),
                   pl.BlockSpec((tm, LANES), lambda i, j: (i, 0))],
        out_shape=[jax.ShapeDtypeStruct((n, d), F32), jax.ShapeDtypeStruct((n, d // 2), I32),
                   jax.ShapeDtypeStruct((n, LANES), F32)],
        scratch_shapes=[pltpu.VMEM((tm, d), BF16)],
        compiler_params=_cparams(("parallel", "arbitrary"), 56),
        name="out_proj_ln1",
    )(conv_out, attn_out, w_conv, w_attn, x2, ln_g, ln_b, w_router, b_router)


def _first_argmax(vals, lane, valid):
    masked = jnp.where(valid, vals, -jnp.inf)
    mx = jnp.max(masked, axis=1, keepdims=True)
    idx = jnp.min(jnp.where(valid & (masked == mx), lane, LANES), axis=1, keepdims=True)
    return mx, idx


def _router_kernel(lg_ref, tri_ref, e_ref, gate_ref, cnt_ref):
    @pl.when(pl.program_id(0) == 0)
    def _():
        cnt_ref[...] = jnp.zeros(cnt_ref.shape, F32)

    lg = lg_ref[...]
    lane = lax.broadcasted_iota(I32, lg.shape, 1)
    is_group = lane < N_GROUPS
    g_max, g_sel = _first_argmax(lg, lane, is_group)
    g_den = jnp.sum(jnp.where(is_group, jnp.exp(lg - g_max), 0.0), axis=1, keepdims=True)
    p_group = 1.0 / g_den
    lo = N_GROUPS + g_sel * EXPERTS_PER_GROUP
    in_group = (lane >= lo) & (lane < lo + EXPERTS_PER_GROUP)
    e_max, _ = _first_argmax(lg, lane, in_group)
    e_exp = jnp.where(in_group, jnp.exp(lg - e_max), 0.0)
    prob = e_exp / jnp.sum(e_exp, axis=1, keepdims=True)
    p1, i1 = _first_argmax(prob, lane, in_group)
    p2, i2 = _first_argmax(prob, lane, in_group & (lane != i1))
    norm = p_group / (p1 + p2)
    gate_ref[...] = jnp.where(lane == 0, p1 * norm, jnp.where(lane == 1, p2 * norm, 0.0))
    chosen = jnp.where((lane == i1) | (lane == i2), 1.0, 0.0)
    before = jnp.dot(tri_ref[...], chosen.astype(BF16), preferred_element_type=F32) + cnt_ref[...]
    rank1 = jnp.sum(jnp.where(lane == i1, before, 0.0), axis=1, keepdims=True).astype(I32)
    rank2 = jnp.sum(jnp.where(lane == i2, before, 0.0), axis=1, keepdims=True).astype(I32)
    cnt_ref[...] += jnp.sum(chosen, axis=0, keepdims=True)
    e_ref[...] = jnp.where(lane == 0, i1 - N_GROUPS,
                           jnp.where(lane == 1, i2 - N_GROUPS,
                                     jnp.where(lane == 2, rank1, jnp.where(lane == 3, rank2, 0))))


def _route(logits):
    n = logits.shape[0]
    tm = min(512, n)
    tri = jnp.tril(jnp.ones((tm, tm), BF16), -1)
    spec = pl.BlockSpec((tm, LANES), lambda i: (i, 0))
    return pl.pallas_call(
        _router_kernel,
        grid=(n // tm,),
        in_specs=[spec, pl.BlockSpec((tm, tm), lambda i: (0, 0))],
        out_specs=[spec, spec, pl.BlockSpec((1, LANES), lambda i: (0, 0))],
        out_shape=[jax.ShapeDtypeStruct((n, LANES), I32), jax.ShapeDtypeStruct((n, LANES), F32),
                   jax.ShapeDtypeStruct((1, LANES), F32)],
        compiler_params=_cparams(("arbitrary",), 32),
        name="router",
    )(logits, tri)


def _start_row_copies(src_hbm, idx_ref, idx_row, dst_ref, slot, row0, sem, count):
    group = 8

    def issue(i, carry):
        for k in range(group):
            r = i * group + k
            pltpu.make_async_copy(src_hbm.at[pl.ds(idx_ref[0, idx_row, r], 1), :],
                                  dst_ref.at[slot, pl.ds(row0 + r, 1), :], sem.at[slot]).start(priority=k % 2)
        return carry
    lax.fori_loop(0, count // group, issue, 0)


def _wait_slot(src_hbm, dst_ref, slot, sem):
    rows = dst_ref.shape[1]
    pltpu.make_async_copy(src_hbm.at[pl.ds(0, rows), :], dst_ref.at[slot], sem.at[slot]).wait()


def _gather_kernel(used_ref, tok_ref, tok_next_ref, hp_hbm, o_ref, buf_ref, sem):
    i = pl.program_id(0)
    slot = i % 2
    n_used = used_ref[0]
    rows = buf_ref.shape[1]

    @pl.when(i == 0)
    def _():
        _start_row_copies(hp_hbm, tok_ref, 0, buf_ref, 0, 0, sem, rows)

    @pl.when(i + 1 < n_used)
    def _():
        _start_row_copies(hp_hbm, tok_next_ref, 0, buf_ref, 1 - slot, 0, sem, rows)

    @pl.when(i < n_used)
    def _():
        _wait_slot(hp_hbm, buf_ref, slot, sem)
        o_ref[...] = buf_ref[slot]

    @pl.when(i >= n_used)
    def _():
        o_ref[...] = jnp.zeros(o_ref.shape, o_ref.dtype)


def _gather_tokens(hp, buf_tok, n_used):
    p = buf_tok.shape[0]
    half = hp.shape[1]
    nblk = p // MOE_ROWS
    tok_blocks = buf_tok.reshape(nblk, 1, MOE_ROWS)
    return pl.pallas_call(
        _gather_kernel,
        grid_spec=pltpu.PrefetchScalarGridSpec(
            num_scalar_prefetch=1,
            grid=(nblk,),
            in_specs=[pl.BlockSpec((1, 1, MOE_ROWS), lambda i, u: (i, 0, 0), memory_space=pltpu.SMEM),
                      pl.BlockSpec((1, 1, MOE_ROWS), lambda i, u: (jnp.minimum(i + 1, nblk - 1), 0, 0),
                                   memory_space=pltpu.SMEM),
                      pl.BlockSpec(memory_space=pl.ANY)],
            out_specs=pl.BlockSpec((MOE_ROWS, half), lambda i, u: (i, 0)),
            scratch_shapes=[pltpu.VMEM((2, MOE_ROWS, half), I32), pltpu.SemaphoreType.DMA((2,))]),
        out_shape=jax.ShapeDtypeStruct((p, half), I32),
        compiler_params=_cparams(("arbitrary",), 32),
        name="moe_gather",
    )(n_used.reshape(1), tok_blocks, tok_blocks, hp)


def _combine_kernel(pos_ref, pos_next_ref, y_hbm, h_ref, gate_ref, g_ref, b_ref, o_ref, buf_ref, sem, *, alpha):
    i = pl.program_id(0)
    n = pl.num_programs(0)
    slot = i % 2
    rows = h_ref.shape[0]

    def start(idx_ref, into):
        for k in range(2):
            _start_row_copies(y_hbm, idx_ref, k, buf_ref, into, k * rows, sem, rows)

    @pl.when(i == 0)
    def _():
        start(pos_ref, 0)

    @pl.when(i + 1 < n)
    def _():
        start(pos_next_ref, 1 - slot)

    _wait_slot(y_hbm, buf_ref, slot, sem)

    def norm_rows(j, carry):
        r0 = pl.multiple_of(j * SUBLANES, SUBLANES)
        gates = gate_ref[pl.ds(r0, SUBLANES), :]
        ffn = (buf_ref[slot, pl.ds(r0, SUBLANES), :] * gates[:, 0:1]
               + buf_ref[slot, pl.ds(rows + r0, SUBLANES), :] * gates[:, 1:2])
        z = alpha * h_ref[pl.ds(r0, SUBLANES), :] + ffn
        o_ref[pl.ds(r0, SUBLANES), :] = _layer_norm_rows(z, g_ref[...], b_ref[...])
        return carry
    lax.fori_loop(0, rows // SUBLANES, norm_rows, 0, unroll=2)


def _combine_ln(y, pos2, h1, gate_lanes, ln_g, ln_b, alpha):
    n, d = h1.shape
    tb = min(COMBINE_ROWS, n)
    nblk = n // tb
    pos_blocks = pos2.reshape(nblk, tb, 2).transpose(0, 2, 1)
    kern = functools.partial(_combine_kernel, alpha=alpha)
    return pl.pallas_call(
        kern,
        grid=(nblk,),
        in_specs=[pl.BlockSpec((1, 2, tb), lambda i: (i, 0, 0), memory_space=pltpu.SMEM),
                  pl.BlockSpec((1, 2, tb), lambda i: (jnp.minimum(i + 1, nblk - 1), 0, 0), memory_space=pltpu.SMEM),
                  pl.BlockSpec(memory_space=pl.ANY),
                  pl.BlockSpec((tb, d), lambda i: (i, 0)),
                  pl.BlockSpec((tb, LANES), lambda i: (i, 0)),
                  pl.BlockSpec((1, d), lambda i: (0, 0)),
                  pl.BlockSpec((1, d), lambda i: (0, 0))],
        out_specs=pl.BlockSpec((tb, d), lambda i: (i, 0)),
        out_shape=jax.ShapeDtypeStruct((n, d), F32),
        scratch_shapes=[pltpu.VMEM((2, 2 * tb, d), F32), pltpu.SemaphoreType.DMA((2,))],
        compiler_params=_cparams(("arbitrary",), 32),
        name="moe_combine_ln2",
    )(pos_blocks, pos_blocks, y, h1, gate_lanes, ln_g, ln_b)


def _cast_weight(dst_ref, src_ref):
    rows = src_ref.shape[1]
    step = min(256, rows)

    def body(i, carry):
        r0 = pl.multiple_of(i * step, step)
        dst_ref[pl.ds(r0, step), :] = src_ref[0, pl.ds(r0, step), :].astype(BF16)
        return carry
    lax.fori_loop(0, rows // step, body, 0)


ITEM_COMPUTE = 1
ITEM_NEW_WEIGHTS = 2
ITEM_ZERO_FILL = 4


def _moe_up_kernel(e_ref, cw_ref, bi_ref, bo_ref, co_ref, flag_ref, x_ref, w1_ref, w3_ref, o_ref, w1b_ref, w3b_ref):
    flag = flag_ref[pl.program_id(0)]

    @pl.when((flag & ITEM_NEW_WEIGHTS) != 0)
    def _():
        _cast_weight(w1b_ref, w1_ref)
        _cast_weight(w3b_ref, w3_ref)

    @pl.when((flag & ITEM_COMPUTE) != 0)
    def _():
        x_lo, x_hi = _unpack_bf16_halves(x_ref[...])
        half = x_lo.shape[1]

        def project(w_ref):
            return (jnp.dot(x_lo, w_ref[0:half, :], preferred_element_type=F32)
                    + jnp.dot(x_hi, w_ref[half:2 * half, :], preferred_element_type=F32))

        a = project(w1b_ref)
        b = project(w3b_ref)
        o_ref[...] = (a * jax.nn.sigmoid(a) * b).astype(o_ref.dtype)

    @pl.when((flag & ITEM_ZERO_FILL) != 0)
    def _():
        o_ref[...] = jnp.zeros(o_ref.shape, o_ref.dtype)


def _moe_down_kernel(e_ref, cw_ref, bi_ref, bo_ref, co_ref, flag_ref, h_ref, w2_ref, o_ref, w2b_ref):
    flag = flag_ref[pl.program_id(0)]

    @pl.when((flag & ITEM_NEW_WEIGHTS) != 0)
    def _():
        _cast_weight(w2b_ref, w2_ref)

    @pl.when((flag & ITEM_COMPUTE) != 0)
    def _():
        o_ref[...] = jnp.dot(h_ref[...], w2b_ref[...], preferred_element_type=F32)

    @pl.when((flag & ITEM_ZERO_FILL) != 0)
    def _():
        o_ref[...] = jnp.zeros(o_ref.shape, o_ref.dtype)


def _work_items(blocks_per_expert, block_start, n_chunks, n_blocks):
    n_items = n_chunks * n_blocks
    per_e = n_chunks * blocks_per_expert
    end = jnp.cumsum(per_e)
    start = end - per_e
    it = jnp.arange(n_items, dtype=I32)
    total = end[-1]
    used_blocks = total // n_chunks
    itc = jnp.minimum(it, total - 1)
    e = jnp.sum(itc[:, None] >= end[None, :], axis=1).astype(I32)
    local = itc - start[e]
    nb = jnp.maximum(blocks_per_expert[e], 1)
    c = local // nb
    r = local - c * nb
    b_in = block_start[e] + r
    active = it < total
    spare = jnp.maximum(it - total, 0)
    b_out = jnp.where(active, b_in, used_blocks + spare // n_chunks)
    c_out = jnp.where(active, c, spare % n_chunks)
    first = active & (r == 0)
    flag = jnp.where(active, ITEM_COMPUTE + ITEM_NEW_WEIGHTS * first.astype(I32), ITEM_ZERO_FILL)
    later_first = lax.cummin(jnp.where(first, it, n_items)[::-1])[::-1]
    next_start = jnp.concatenate([later_first[1:], jnp.full((1,), n_items, I32)])
    ahead = jnp.minimum(next_start, n_items - 1)
    use_next = active & ~first & (next_start < n_items)
    e_w = jnp.where(use_next, e[ahead], e)
    c_w = jnp.where(use_next, c[ahead], c)
    return tuple(v.astype(I32) for v in (e_w, c_w, b_in, b_out, c_out, flag))


def _moe_up(xs, w1, w3, items, f_chunk):
    p, half = xs.shape
    d, f = w1.shape[1], w1.shape[2]
    assert d == 2 * half
    wspec = pl.BlockSpec((1, d, f_chunk), lambda it, e, cw, bi, bo, co, fl: (e[it], 0, cw[it]))
    return pl.pallas_call(
        _moe_up_kernel,
        grid_spec=pltpu.PrefetchScalarGridSpec(
            num_scalar_prefetch=6,
            grid=(items[0].shape[0],),
            in_specs=[pl.BlockSpec((MOE_ROWS, half), lambda it, e, cw, bi, bo, co, fl: (bi[it], 0)), wspec, wspec],
            out_specs=pl.BlockSpec((MOE_ROWS, f_chunk), lambda it, e, cw, bi, bo, co, fl: (bo[it], co[it])),
            scratch_shapes=[pltpu.VMEM((d, f_chunk), BF16), pltpu.VMEM((d, f_chunk), BF16)]),
        out_shape=jax.ShapeDtypeStruct((p, f), BF16),
        compiler_params=_cparams(("arbitrary",), 56),
        name="moe_up",
    )(*items, xs, w1, w3)


def _moe_down(hmid, w2, items, d_chunk):
    p, f = hmid.shape
    d = w2.shape[2]
    return pl.pallas_call(
        _moe_down_kernel,
        grid_spec=pltpu.PrefetchScalarGridSpec(
            num_scalar_prefetch=6,
            grid=(items[0].shape[0],),
            in_specs=[pl.BlockSpec((MOE_ROWS, f), lambda it, e, cw, bi, bo, co, fl: (bi[it], 0)),
                      pl.BlockSpec((1, f, d_chunk), lambda it, e, cw, bi, bo, co, fl: (e[it], 0, cw[it]))],
            out_specs=pl.BlockSpec((MOE_ROWS, d_chunk), lambda it, e, cw, bi, bo, co, fl: (bo[it], co[it])),
            scratch_shapes=[pltpu.VMEM((f, d_chunk), BF16)]),
        out_shape=jax.ShapeDtypeStruct((p, d), F32),
        compiler_params=_cparams(("arbitrary",), 48),
        name="moe_down",
    )(*items, hmid, w2)


def _hier_moe_ln(h1, hp, logits, w1, w3, w2, ln_g, ln_b, alpha):
    n, d = h1.shape
    e_lanes, gate_lanes, lane_counts = _route(logits)
    a = 2 * n
    e_flat = e_lanes[:, 0:2].reshape(a)
    rank = e_lanes[:, 2:4].reshape(a)
    counts = lane_counts[0, N_GROUPS:N_GROUPS + N_EXPERTS].astype(I32)
    blocks_per_expert = (counts + MOE_ROWS - 1) // MOE_ROWS
    block_start = jnp.cumsum(blocks_per_expert) - blocks_per_expert
    onehot = e_flat[:, None] == jnp.arange(N_EXPERTS, dtype=I32)[None, :]
    pos = jnp.sum(jnp.where(onehot, block_start[None, :], 0), axis=1) * MOE_ROWS + rank
    n_blocks = (a + MOE_ROWS - 1) // MOE_ROWS + N_EXPERTS
    buf_tok = jnp.zeros((n_blocks * MOE_ROWS,), I32).at[pos].set(jnp.arange(a, dtype=I32) // 2)

    f = w1.shape[2]
    f_chunk = min(512, f)
    d_chunk = min(2048, d)
    xs = _gather_tokens(hp, buf_tok, jnp.sum(blocks_per_expert).astype(I32))
    hmid = _moe_up(xs, w1, w3, _work_items(blocks_per_expert, block_start, f // f_chunk, n_blocks), f_chunk)
    y = _moe_down(hmid, w2, _work_items(blocks_per_expert, block_start, d // d_chunk, n_blocks), d_chunk)
    return _combine_ln(y, pos.reshape(n, 2), h1, gate_lanes, ln_g, ln_b, alpha)


def kernel(x, w_in, conv_dw_w, conv_dw_b, conv_ln_g, conv_ln_b, w_out, ln1_g, ln1_b, w_router_group, b_router_group, w_router_expert, b_router_expert, w_expert_gate, w_expert_up, w_expert_down, ln2_g, ln2_b):
    b, s, d = x.shape
    depth = w_in.shape[0]
    assert depth == 1
    alpha = float((2.0 * depth) ** 0.25)
    n = b * s
    dc = conv_dw_w.shape[2]
    d_attn = N_HEADS * HEAD_DIM
    d_kv = N_KV_HEADS * HEAD_DIM
    d_qi = IDX_HEADS * IDX_DIM
    n_small = IDX_DIM + IDX_HEADS
    assert w_in.shape[2] == 2 * dc + d_attn + 2 * d_kv + d_qi + n_small

    x2 = x.reshape(n, d)
    xb = x2.astype(BF16)
    w = w_in[0]
    o_qkvi = 2 * dc
    o_small = o_qkvi + d_attn + 2 * d_kv + d_qi
    wa = w[:, :dc].astype(BF16)
    wg = w[:, dc:2 * dc].astype(BF16)
    w_qkvi = w[:, o_qkvi:o_small].astype(BF16)
    w_small = jnp.pad(w[:, o_small:], ((0, 0), (0, LANES - n_small))).astype(BF16)
    qkvi_scale = jnp.concatenate([jnp.full((d_attn,), HEAD_DIM ** -0.5 * LOG2E, F32), jnp.ones((2 * d_kv,), F32),
                                  jnp.full((d_qi,), IDX_DIM ** -0.5, F32)])[None, :]

    glu = _proj_glu(xb, wa, wg)
    qkvi = _proj_scale(xb, w_qkvi, qkvi_scale, BF16, "proj_qkvi")
    small = _proj_scale(xb, w_small, jnp.ones((1, LANES), F32), F32, "proj_idx")

    conv_out = _conformer_conv(glu.reshape(b, s, dc), conv_dw_w[0], conv_dw_b, conv_ln_g, conv_ln_b)
    attn_out = _dsa_attention(qkvi.reshape(b, s, -1), small.reshape(b, s, LANES), b, s)

    n_route = N_GROUPS + N_EXPERTS
    w_router = jnp.pad(jnp.concatenate([w_router_group[0], w_router_expert[0]], axis=1),
                       ((0, 0), (0, LANES - n_route))).astype(BF16)
    b_router = jnp.pad(jnp.concatenate([b_router_group[0], b_router_expert[0]]), (0, LANES - n_route))[None, :]
    h1, hp, logits = _out_proj_ln(conv_out.reshape(n, dc), attn_out.reshape(n, d_attn),
                                  w_out[0, :dc].astype(BF16), w_out[0, dc:].astype(BF16),
                                  x2, ln1_g, ln1_b, w_router, b_router, alpha)

    out = _hier_moe_ln(h1, hp, logits, w_expert_gate[0], w_expert_up[0], w_expert_down[0], ln2_g, ln2_b, alpha)
    return out.reshape(b, s, d)
```

```python
import functools
import math

import jax
import jax.numpy as jnp
from jax import lax
from jax.experimental import pallas as pl
from jax.experimental.pallas import tpu as pltpu

F32 = jnp.float32
BF16 = jnp.bfloat16
I32 = jnp.int32
I16 = jnp.int16

CONV_WIDTH = 31
N_HEADS = 16
HEAD_DIM = 128
N_KV_HEADS = 4
IDX_HEADS = 16
IDX_DIM = 64
TOPK_MAX = 256
N_GROUPS = 4
EXPERTS_PER_GROUP = 8
N_EXPERTS = N_GROUPS * EXPERTS_PER_GROUP
LN_EPS = 1e-5

LANES = 128
SUBLANES = 8
BF16_ROWS = 16
MIB = 1024 * 1024

Q_TILE = 128
KEY_CHUNK = 512
CONV_HALO = 32
MOE_ROWS = 256
COMBINE_ROWS = 128
INT_MIN = -(2 ** 31)
INT_MAX = 2 ** 31 - 1
LOG2E = math.log2(math.e)
NEG_BIG = -1e30
M_INIT = -1e20


def _cparams(semantics, vmem_mib):
    return pltpu.CompilerParams(dimension_semantics=semantics, vmem_limit_bytes=vmem_mib * MIB)


def _layer_norm_rows(z, g, b):
    mu = jnp.mean(z, axis=-1, keepdims=True)
    zc = z - mu
    var = jnp.mean(zc * zc, axis=-1, keepdims=True)
    return zc * lax.rsqrt(var + LN_EPS) * g + b


def _mm_glu_kernel(x_ref, wa_ref, wg_ref, o_ref):
    x = x_ref[...]
    a = jnp.dot(x, wa_ref[...], preferred_element_type=F32)
    g = jnp.dot(x, wg_ref[...], preferred_element_type=F32)
    o_ref[...] = (a * jax.nn.sigmoid(g)).astype(o_ref.dtype)


def _mm_scale_kernel(x_ref, w_ref, s_ref, o_ref):
    acc = jnp.dot(x_ref[...], w_ref[...], preferred_element_type=F32)
    o_ref[...] = (acc * s_ref[...]).astype(o_ref.dtype)


def _mm_tiles(n, k, cols):
    tm = min(1024, n)
    tn = min(512, cols)
    assert n % tm == 0 and cols % tn == 0
    return tm, tn


def _proj_glu(xb, wa, wg):
    n, k = xb.shape
    cols = wa.shape[1]
    tm, tn = _mm_tiles(n, k, cols)
    return pl.pallas_call(
        _mm_glu_kernel,
        grid=(n // tm, cols // tn),
        in_specs=[pl.BlockSpec((tm, k), lambda i, j: (i, 0)),
                  pl.BlockSpec((k, tn), lambda i, j: (0, j)),
                  pl.BlockSpec((k, tn), lambda i, j: (0, j))],
        out_specs=pl.BlockSpec((tm, tn), lambda i, j: (i, j)),
        out_shape=jax.ShapeDtypeStruct((n, cols), F32),
        compiler_params=_cparams(("parallel", "arbitrary"), 48),
        name="proj_glu",
    )(xb, wa, wg)


def _proj_scale(xb, w, scale, out_dtype, name):
    n, k = xb.shape
    cols = w.shape[1]
    tm, tn = _mm_tiles(n, k, cols)
    return pl.pallas_call(
        _mm_scale_kernel,
        grid=(n // tm, cols // tn),
        in_specs=[pl.BlockSpec((tm, k), lambda i, j: (i, 0)),
                  pl.BlockSpec((k, tn), lambda i, j: (0, j)),
                  pl.BlockSpec((1, tn), lambda i, j: (0, j))],
        out_specs=pl.BlockSpec((tm, tn), lambda i, j: (i, j)),
        out_shape=jax.ShapeDtypeStruct((n, cols), out_dtype),
        compiler_params=_cparams(("parallel", "arbitrary"), 48),
        name=name,
    )(xb, w, scale)


def _conv_kernel(cur_ref, prev_ref, w_ref, b_ref, g_ref, beta_ref, o_ref, sh_ref, y_ref, *, rows, lane_chunk):
    t = pl.program_id(1)
    dc = cur_ref.shape[2]
    row_sub = 32
    sh_ref[0, 0:CONV_HALO, :] = jnp.where(t > 0, prev_ref[0], 0.0)
    sh_ref[0, CONV_HALO:CONV_HALO + rows, :] = cur_ref[0]
    first = CONV_HALO - (CONV_WIDTH - 1)
    shifted_rows = rows + CONV_HALO - SUBLANES
    for lc in range(dc // lane_chunk):
        cols = slice(lc * lane_chunk, (lc + 1) * lane_chunk)
        for r in range(1, SUBLANES):
            sh_ref[r, 0:shifted_rows, cols] = sh_ref[0, r:r + shifted_rows, cols]

    for r0 in range(0, rows, row_sub):
        for lc in range(dc // lane_chunk):
            cols = slice(lc * lane_chunk, (lc + 1) * lane_chunk)
            acc = jnp.zeros((row_sub, lane_chunk), F32)
            for j in range(CONV_WIDTH):
                shift, base = (first + j) % SUBLANES, (first + j) // SUBLANES * SUBLANES
                acc = acc + sh_ref[shift, r0 + base:r0 + base + row_sub, cols] * w_ref[j:j + 1, cols]
            y_ref[r0:r0 + row_sub, cols] = acc + b_ref[:, cols]

    def norm_rows(i, carry):
        r0 = pl.multiple_of(i * BF16_ROWS, BF16_ROWS)
        yn = _layer_norm_rows(y_ref[pl.ds(r0, BF16_ROWS), :], g_ref[...], beta_ref[...])
        o_ref[0, pl.ds(r0, BF16_ROWS), :] = (yn * jax.nn.sigmoid(yn)).astype(o_ref.dtype)
        return carry

    lax.fori_loop(0, rows // BF16_ROWS, norm_rows, 0, unroll=2)


def _conformer_conv(glu, w_dw, b_dw, g_ln, b_ln):
    b, s, dc = glu.shape
    rows = min(128, s)
    halo_blocks = rows // CONV_HALO
    lane_chunk = min(512, dc)
    kern = functools.partial(_conv_kernel, rows=rows, lane_chunk=lane_chunk)
    return pl.pallas_call(
        kern,
        grid=(b, s // rows),
        in_specs=[pl.BlockSpec((1, rows, dc), lambda bi, t: (bi, t, 0)),
                  pl.BlockSpec((1, CONV_HALO, dc), lambda bi, t: (bi, jnp.maximum(t * halo_blocks - 1, 0), 0)),
                  pl.BlockSpec((CONV_WIDTH, dc), lambda bi, t: (0, 0)),
                  pl.BlockSpec((1, dc), lambda bi, t: (0, 0)),
                  pl.BlockSpec((1, dc), lambda bi, t: (0, 0)),
                  pl.BlockSpec((1, dc), lambda bi, t: (0, 0))],
        out_specs=pl.BlockSpec((1, rows, dc), lambda bi, t: (bi, t, 0)),
        out_shape=jax.ShapeDtypeStruct((b, s, dc), BF16),
        scratch_shapes=[pltpu.VMEM((SUBLANES, CONV_HALO + rows, dc), F32), pltpu.VMEM((rows, dc), F32)],
        compiler_params=_cparams(("parallel", "arbitrary"), 32),
        name="conformer_conv",
    )(glu, glu, w_dw, b_dw, g_ln, b_ln)


def _dsa_kernel(q_ref, qi_ref, k_ref, v_ref, kidx_ref, w_ref, o_ref,
                ke_ref, ko_ref, vt_ref, key_ref, hi_ref, lo_ref, lo2_ref, cut_ref, nd_ref, qs_ref, qis_ref,
                m_ref, l_ref, acc_ref,
                *, seq, chunk, topk):
    tb = pl.program_id(1)
    t0 = tb * Q_TILE
    n_chunks = (t0 + Q_TILE + chunk - 1) // chunk
    rep = N_HEADS // N_KV_HEADS
    nt = (((1,), (1,)), ((), ()))

    @pl.when(tb == 0)
    def _():
        def build(c, carry):
            r0 = pl.multiple_of(c * chunk, chunk)
            kx = kidx_ref[0, pl.ds(r0, chunk), :]
            lane = lax.broadcasted_iota(I32, kx.shape, 1)
            ke_ref[pl.ds(r0, chunk), :] = jnp.where(lane < IDX_DIM, kx, 0.0).astype(BF16)
            ko_ref[pl.ds(r0, chunk), :] = jnp.where(lane >= IDX_DIM, pltpu.roll(kx, IDX_DIM, 1), 0.0).astype(BF16)
            for g in range(N_KV_HEADS):
                vg = v_ref[0, pl.ds(r0, chunk), g * HEAD_DIM:(g + 1) * HEAD_DIM]
                vt_ref[c, g * HEAD_DIM:(g + 1) * HEAD_DIM, :] = vg.astype(F32).T.astype(BF16)
            return carry
        lax.fori_loop(0, seq // chunk, build, 0)

    for g in range(N_KV_HEADS):
        for r in range(rep):
            h = g * rep + r
            qs_ref[g, r * Q_TILE:(r + 1) * Q_TILE, :] = q_ref[0, :, h * HEAD_DIM:(h + 1) * HEAD_DIM]
    for jj in range(IDX_HEADS // 4):
        for half in range(2):
            pair = 2 * jj + half
            qis_ref[jj, half * Q_TILE:(half + 1) * Q_TILE, :] = qi_ref[0, :, pair * LANES:(pair + 1) * LANES]
    w_t = w_ref[0].T * (IDX_HEADS ** -0.5)

    key_row = lax.broadcasted_iota(I32, (chunk, Q_TILE), 0)
    q_pos = t0 + lax.broadcasted_iota(I32, (chunk, Q_TILE), 1)

    def score_chunk(c, carry):
        r0 = pl.multiple_of(c * chunk, chunk)
        ke = ke_ref[pl.ds(r0, chunk), :]
        ko = ko_ref[pl.ds(r0, chunk), :]
        acc = jnp.zeros((chunk, Q_TILE), F32)
        for jj in range(IDX_HEADS // 4):
            rhs = qis_ref[jj]
            de = lax.dot_general(ke, rhs, nt, preferred_element_type=F32)
            do = lax.dot_general(ko, rhs, nt, preferred_element_type=F32)
            for half in range(2):
                h_even = 2 * (2 * jj + half)
                cols = slice(half * Q_TILE, (half + 1) * Q_TILE)
                acc = acc + w_t[IDX_DIM + h_even:IDX_DIM + h_even + 1, :] * jnp.maximum(de[:, cols], 0.0)
                acc = acc + w_t[IDX_DIM + h_even + 1:IDX_DIM + h_even + 2, :] * jnp.maximum(do[:, cols], 0.0)
        bits = lax.bitcast_convert_type(acc, I32)
        key = bits ^ ((bits >> 31) & 0x7FFFFFFF)
        key = jnp.where(r0 + key_row <= q_pos, key, INT_MIN)
        key_ref[c] = key
        hi_ref[c] = (key >> 16).astype(I16)
        lo_ref[c] = ((key & 0xFFFF) - 32768).astype(I16)
        return carry

    lax.fori_loop(0, n_chunks, score_chunk, 0)

    def count_ge(half_ref, cand16):
        def count_chunk(c, cnts):
            cnts = list(cnts)
            for r in range(chunk // BF16_ROWS):
                slab = half_ref[c, r * BF16_ROWS:(r + 1) * BF16_ROWS, :]
                cnts[r % len(cnts)] = cnts[r % len(cnts)] + jnp.where(slab >= cand16, jnp.int16(1), jnp.int16(0))
            return tuple(cnts)

        zero = jnp.zeros((BF16_ROWS, Q_TILE), I16)
        cnts = lax.fori_loop(0, n_chunks, count_chunk, (zero, zero, zero, zero))
        return jnp.sum(((cnts[0] + cnts[1]) + (cnts[2] + cnts[3])).astype(I32), axis=0, keepdims=True)

    def bisect_half(half_ref, need):
        def bit_step(i, t_u):
            cand_u = t_u | (jnp.int32(1) << (15 - i))
            tot = count_ge(half_ref, (cand_u - 32768).astype(I16))
            return jnp.where(tot >= need, cand_u, t_u)
        return lax.fori_loop(0, 16, bit_step, jnp.zeros((1, Q_TILE), I32))

    hi_u = bisect_half(hi_ref, topk)
    hi16 = (hi_u - 32768).astype(I16)
    above = count_ge(hi_ref, jnp.minimum(hi_u - 32767, 32767).astype(I16))
    above = jnp.where(hi_u == 65535, 0, above)

    def restrict_chunk(c, carry):
        lo2_ref[c] = jnp.where(hi_ref[c] == hi16, lo_ref[c], jnp.int16(-32768))
        return carry
    lax.fori_loop(0, n_chunks, restrict_chunk, 0)
    lo_u = bisect_half(lo2_ref, topk - above)
    thr = jnp.maximum((hi_u - 32768) * 65536 + lo_u, INT_MIN + 1)

    n_ge = above + count_ge(lo2_ref, (lo_u - 32768).astype(I16))
    cut_ref[...] = jnp.full(cut_ref.shape, INT_MAX, I32)

    @pl.when(jnp.max(n_ge) > topk)
    def _():
        def count_where(pred):
            def count_chunk(c, cnt):
                r0 = pl.multiple_of(c * chunk, chunk)
                hit = jnp.where(pred(key_ref[c], r0 + key_row), 1, 0)
                return cnt + jnp.sum(hit, axis=0, keepdims=True)
            return lax.fori_loop(0, n_chunks, count_chunk, jnp.zeros((1, Q_TILE), I32))

        keep = topk - count_where(lambda kc, pos: kc > thr)
        pos_bits = seq.bit_length()

        def bit_step(i, cut):
            cand = cut | (jnp.int32(1) << (pos_bits - 1 - i))
            taken = count_where(lambda kc, pos: jnp.where(kc == thr, pos, INT_MAX) < cand)
            return jnp.where(taken <= keep, cand, cut)
        cut_ref[...] = lax.fori_loop(0, pos_bits, bit_step, jnp.zeros((1, Q_TILE), I32))

    cut = cut_ref[...]

    m_ref[...] = jnp.full(m_ref.shape, M_INIT, F32)
    l_ref[...] = jnp.zeros(l_ref.shape, F32)
    acc_ref[...] = jnp.zeros(acc_ref.shape, F32)

    def attend_chunk(c, carry):
        r0 = pl.multiple_of(c * chunk, chunk)
        pos = r0 + key_row
        kc = key_ref[c]
        rank_pos = jnp.where(kc > thr, -1, jnp.where(kc == thr, pos, INT_MAX))
        nd_ref[...] = jnp.where(rank_pos < cut, (pos - q_pos).astype(F32), NEG_BIG)
        s_groups = []
        for g in range(N_KV_HEADS):
            kg = k_ref[0, pl.ds(r0, chunk), g * HEAD_DIM:(g + 1) * HEAD_DIM]
            s_groups.append(lax.dot_general(kg, qs_ref[g], nt, preferred_element_type=F32))
        for g in range(N_KV_HEADS):
            s_all = s_groups[g]
            probs = []
            alphas = []
            for r in range(rep):
                h = g * rep + r
                slope = float(2.0 ** (-8.0 * (h + 1) / N_HEADS)) * LOG2E
                sr = s_all[:, r * Q_TILE:(r + 1) * Q_TILE] + slope * nd_ref[...]
                m_old = m_ref[h]
                m_new = jnp.maximum(m_old, jnp.max(sr, axis=0, keepdims=True))
                alpha = jnp.exp2(m_old - m_new)
                p = jnp.exp2(sr - m_new)
                l_ref[h] = alpha * l_ref[h] + jnp.sum(p, axis=0, keepdims=True)
                m_ref[h] = m_new
                probs.append(p.astype(BF16))
                alphas.append(alpha)
            vt = vt_ref[c, g * HEAD_DIM:(g + 1) * HEAD_DIM, :]
            pv = jnp.dot(vt, jnp.concatenate(probs, axis=1), preferred_element_type=F32)
            acc_ref[g] = jnp.concatenate(alphas, axis=1) * acc_ref[g] + pv
        return carry

    lax.fori_loop(0, n_chunks, attend_chunk, 0)

    for g in range(N_KV_HEADS):
        for r in range(rep):
            h = g * rep + r
            o_t = acc_ref[g, :, r * Q_TILE:(r + 1) * Q_TILE] * (1.0 / l_ref[h])
            o_ref[0, :, h * HEAD_DIM:(h + 1) * HEAD_DIM] = o_t.T.astype(o_ref.dtype)


def _dsa_attention(qkvi, small, b, s):
    d_attn = N_HEADS * HEAD_DIM
    d_kv = N_KV_HEADS * HEAD_DIM
    d_qi = IDX_HEADS * IDX_DIM
    rep = N_HEADS // N_KV_HEADS
    chunk = min(KEY_CHUNK, s)
    topk = min(TOPK_MAX, s // 4)
    assert s % chunk == 0 and chunk % Q_TILE == 0 and d_attn % d_qi == 0 and d_attn % d_kv == 0
    assert Q_TILE == LANES and HEAD_DIM == LANES and 2 * IDX_DIM == LANES
    kern = functools.partial(_dsa_kernel, seq=s, chunk=chunk, topk=topk)
    return pl.pallas_call(
        kern,
        grid=(b, s // Q_TILE),
        in_specs=[pl.BlockSpec((1, Q_TILE, d_attn), lambda bi, t: (bi, t, 0)),
                  pl.BlockSpec((1, Q_TILE, d_qi), lambda bi, t: (bi, t, (d_attn + 2 * d_kv) // d_qi)),
                  pl.BlockSpec((1, s, d_kv), lambda bi, t: (bi, 0, d_attn // d_kv)),
                  pl.BlockSpec((1, s, d_kv), lambda bi, t: (bi, 0, d_attn // d_kv + 1)),
                  pl.BlockSpec((1, s, LANES), lambda bi, t: (bi, 0, 0)),
                  pl.BlockSpec((1, Q_TILE, LANES), lambda bi, t: (bi, t, 0))],
        out_specs=pl.BlockSpec((1, Q_TILE, d_attn), lambda bi, t: (bi, t, 0)),
        out_shape=jax.ShapeDtypeStruct((b, s, d_attn), BF16),
        scratch_shapes=[pltpu.VMEM((s, LANES), BF16),
                        pltpu.VMEM((s, LANES), BF16),
                        pltpu.VMEM((s // chunk, d_kv, chunk), BF16),
                        pltpu.VMEM((s // chunk, chunk, Q_TILE), I32),
                        pltpu.VMEM((s // chunk, chunk, Q_TILE), I16),
                        pltpu.VMEM((s // chunk, chunk, Q_TILE), I16),
                        pltpu.VMEM((s // chunk, chunk, Q_TILE), I16),
                        pltpu.VMEM((1, Q_TILE), I32),
                        pltpu.VMEM((chunk, Q_TILE), F32),
                        pltpu.VMEM((N_KV_HEADS, rep * Q_TILE, HEAD_DIM), BF16),
                        pltpu.VMEM((IDX_HEADS // 4, 2 * Q_TILE, LANES), BF16),
                        pltpu.VMEM((N_HEADS, 1, Q_TILE), F32),
                        pltpu.VMEM((N_HEADS, 1, Q_TILE), F32),
                        pltpu.VMEM((N_KV_HEADS, HEAD_DIM, rep * Q_TILE), F32)],
        compiler_params=_cparams(("parallel", "arbitrary"), 48),
        name="dsa_attention",
    )(qkvi, qkvi, qkvi, qkvi, small, small)


def _pack_bf16_halves(hb):
    half = hb.shape[1] // 2
    bits = lax.bitcast_convert_type(hb.astype(F32), I32)
    return lax.shift_right_logical(bits[:, :half], 16) | (bits[:, half:] & -65536)


def _unpack_bf16_halves(words):
    lo = lax.bitcast_convert_type(words << 16, F32).astype(BF16)
    hi = lax.bitcast_convert_type(words & -65536, F32).astype(BF16)
    return lo, hi


def _outproj_kernel(conv_ref, attn_ref, wc_ref, wa_ref, x_ref, g_ref, b_ref, wr_ref, br_ref,
                    h_ref, hp_ref, lg_ref, hb_ref, *, alpha):
    j = pl.program_id(1)
    tm, d = h_ref.shape
    tn = x_ref.shape[1]
    n_tiles = d // tn
    mix = jnp.dot(conv_ref[...], wc_ref[...], preferred_element_type=F32)
    mix = mix + jnp.dot(attn_ref[...], wa_ref[...], preferred_element_type=F32)
    z = alpha * x_ref[...] + mix
    for jj in range(n_tiles):
        @pl.when(j == jj)
        def _(jj=jj):
            h_ref[:, jj * tn:(jj + 1) * tn] = z

    @pl.when(j == n_tiles - 1)
    def _():
        def rows(i, carry):
            starts = [pl.multiple_of((2 * i + k) * BF16_ROWS, BF16_ROWS) for k in range(2)]
            zs = [h_ref[pl.ds(r0, BF16_ROWS), :] for r0 in starts]
            hs = [_layer_norm_rows(z, g_ref[...], b_ref[...]) for z in zs]
            for r0, h in zip(starts, hs):
                h_ref[pl.ds(r0, BF16_ROWS), :] = h
                hb_ref[pl.ds(r0, BF16_ROWS), :] = h.astype(BF16)
            return carry
        lax.fori_loop(0, tm // (2 * BF16_ROWS), rows, 0)

        def pack_rows(i, carry):
            r0 = pl.multiple_of(i * BF16_ROWS, BF16_ROWS)
            hp_ref[pl.ds(r0, BF16_ROWS), :] = _pack_bf16_halves(hb_ref[pl.ds(r0, BF16_ROWS), :])
            return carry
        lax.fori_loop(0, tm // BF16_ROWS, pack_rows, 0, unroll=2)
        lg_ref[...] = jnp.dot(hb_ref[...], wr_ref[...], preferred_element_type=F32) + br_ref[...]


def _out_proj_ln(conv_out, attn_out, w_conv, w_attn, x2, ln_g, ln_b, w_router, b_router, alpha):
    n, d = x2.shape
    dc = conv_out.shape[1]
    da = attn_out.shape[1]
    tm = min(512, n)
    tn = min(512, d)
    assert n % tm == 0 and d % tn == 0
    kern = functools.partial(_outproj_kernel, alpha=alpha)
    return pl.pallas_call(
        kern,
        grid=(n // tm, d // tn),
        in_specs=[pl.BlockSpec((tm, dc), lambda i, j: (i, 0)),
                  pl.BlockSpec((tm, da), lambda i, j: (i, 0)),
                  pl.BlockSpec((dc, tn), lambda i, j: (0, j)),
                  pl.BlockSpec((da, tn), lambda i, j: (0, j)),
                  pl.BlockSpec((tm, tn), lambda i, j: (i, j)),
                  pl.BlockSpec((1, d), lambda i, j: (0, 0)),
                  pl.BlockSpec((1, d), lambda i, j: (0, 0)),
                  pl.BlockSpec((d, LANES), lambda i, j: (0, 0)),
                  pl.BlockSpec((1, LANES), lambda i, j: (0, 0))],
        out_specs=[pl.BlockSpec((tm, d), lambda i, j: (i, 0)),
                   pl.BlockSpec((tm, d // 2), lambda i, j: (i, 0)),
                   pl.BlockSpec((tm, LANES), lambda i, j: (i, 0))],
        out_shape=[jax.ShapeDtypeStruct((n, d), F32), jax.ShapeDtypeStruct((n, d // 2), I32),
                   jax.ShapeDtypeStruct((n, LANES), F32)],
        scratch_shapes=[pltpu.VMEM((tm, d), BF16)],
        compiler_params=_cparams(("parallel", "arbitrary"), 56),
        name="out_proj_ln1",
    )(conv_out, attn_out, w_conv, w_attn, x2, ln_g, ln_b, w_router, b_router)


def _first_argmax(vals, lane, valid):
    masked = jnp.where(valid, vals, -jnp.inf)
    mx = jnp.max(masked, axis=1, keepdims=True)
    idx = jnp.min(jnp.where(valid & (masked == mx), lane, LANES), axis=1, keepdims=True)
    return mx, idx


def _router_kernel(lg_ref, tri_ref, e_ref, gate_ref, cnt_ref):
    @pl.when(pl.program_id(0) == 0)
    def _():
        cnt_ref[...] = jnp.zeros(cnt_ref.shape, F32)

    lg = lg_ref[...]
    lane = lax.broadcasted_iota(I32, lg.shape, 1)
    is_group = lane < N_GROUPS
    g_max, g_sel = _first_argmax(lg, lane, is_group)
    g_den = jnp.sum(jnp.where(is_group, jnp.exp(lg - g_max), 0.0), axis=1, keepdims=True)
    p_group = 1.0 / g_den
    lo = N_GROUPS + g_sel * EXPERTS_PER_GROUP
    in_group = (lane >= lo) & (lane < lo + EXPERTS_PER_GROUP)
    e_max, _ = _first_argmax(lg, lane, in_group)
    e_exp = jnp.where(in_group, jnp.exp(lg - e_max), 0.0)
    prob = e_exp / jnp.sum(e_exp, axis=1, keepdims=True)
    p1, i1 = _first_argmax(prob, lane, in_group)
    p2, i2 = _first_argmax(prob, lane, in_group & (lane != i1))
    norm = p_group / (p1 + p2)
    gate_ref[...] = jnp.where(lane == 0, p1 * norm, jnp.where(lane == 1, p2 * norm, 0.0))
    chosen = jnp.where((lane == i1) | (lane == i2), 1.0, 0.0)
    before = jnp.dot(tri_ref[...], chosen.astype(BF16), preferred_element_type=F32) + cnt_ref[...]
    rank1 = jnp.sum(jnp.where(lane == i1, before, 0.0), axis=1, keepdims=True).astype(I32)
    rank2 = jnp.sum(jnp.where(lane == i2, before, 0.0), axis=1, keepdims=True).astype(I32)
    cnt_ref[...] += jnp.sum(chosen, axis=0, keepdims=True)
    e_ref[...] = jnp.where(lane == 0, i1 - N_GROUPS,
                           jnp.where(lane == 1, i2 - N_GROUPS,
                                     jnp.where(lane == 2, rank1, jnp.where(lane == 3, rank2, 0))))


def _route(logits):
    n = logits.shape[0]
    tm = min(512, n)
    tri = jnp.tril(jnp.ones((tm, tm), BF16), -1)
    spec = pl.BlockSpec((tm, LANES), lambda i: (i, 0))
    return pl.pallas_call(
        _router_kernel,
        grid=(n // tm,),
        in_specs=[spec, pl.BlockSpec((tm, tm), lambda i: (0, 0))],
        out_specs=[spec, spec, pl.BlockSpec((1, LANES), lambda i: (0, 0))],
        out_shape=[jax.ShapeDtypeStruct((n, LANES), I32), jax.ShapeDtypeStruct((n, LANES), F32),
                   jax.ShapeDtypeStruct((1, LANES), F32)],
        compiler_params=_cparams(("arbitrary",), 32),
        name="router",
    )(logits, tri)


def _start_row_copies(src_hbm, idx_ref, idx_row, dst_ref, slot, row0, sem, count):
    group = 8

    def issue(i, carry):
        for k in range(group):
            r = i * group + k
            pltpu.make_async_copy(src_hbm.at[pl.ds(idx_ref[0, idx_row, r], 1), :],
                                  dst_ref.at[slot, pl.ds(row0 + r, 1), :], sem.at[slot]).start(priority=k % 2)
        return carry
    lax.fori_loop(0, count // group, issue, 0)


def _wait_slot(src_hbm, dst_ref, slot, sem):
    rows = dst_ref.shape[1]
    pltpu.make_async_copy(src_hbm.at[pl.ds(0, rows), :], dst_ref.at[slot], sem.at[slot]).wait()


def _gather_kernel(used_ref, tok_ref, tok_next_ref, hp_hbm, o_ref, buf_ref, sem):
    i = pl.program_id(0)
    slot = i % 2
    n_used = used_ref[0]
    rows = buf_ref.shape[1]

    @pl.when(i == 0)
    def _():
        _start_row_copies(hp_hbm, tok_ref, 0, buf_ref, 0, 0, sem, rows)

    @pl.when(i + 1 < n_used)
    def _():
        _start_row_copies(hp_hbm, tok_next_ref, 0, buf_ref, 1 - slot, 0, sem, rows)

    @pl.when(i < n_used)
    def _():
        _wait_slot(hp_hbm, buf_ref, slot, sem)
        o_ref[...] = buf_ref[slot]

    @pl.when(i >= n_used)
    def _():
        o_ref[...] = jnp.zeros(o_ref.shape, o_ref.dtype)


def _gather_tokens(hp, buf_tok, n_used):
    p = buf_tok.shape[0]
    half = hp.shape[1]
    nblk = p // MOE_ROWS
    tok_blocks = buf_tok.reshape(nblk, 1, MOE_ROWS)
    return pl.pallas_call(
        _gather_kernel,
        grid_spec=pltpu.PrefetchScalarGridSpec(
            num_scalar_prefetch=1,
            grid=(nblk,),
            in_specs=[pl.BlockSpec((1, 1, MOE_ROWS), lambda i, u: (i, 0, 0), memory_space=pltpu.SMEM),
                      pl.BlockSpec((1, 1, MOE_ROWS), lambda i, u: (jnp.minimum(i + 1, nblk - 1), 0, 0),
                                   memory_space=pltpu.SMEM),
                      pl.BlockSpec(memory_space=pl.ANY)],
            out_specs=pl.BlockSpec((MOE_ROWS, half), lambda i, u: (i, 0)),
            scratch_shapes=[pltpu.VMEM((2, MOE_ROWS, half), I32), pltpu.SemaphoreType.DMA((2,))]),
        out_shape=jax.ShapeDtypeStruct((p, half), I32),
        compiler_params=_cparams(("arbitrary",), 32),
        name="moe_gather",
    )(n_used.reshape(1), tok_blocks, tok_blocks, hp)


def _combine_kernel(pos_ref, pos_next_ref, y_hbm, h_ref, gate_ref, g_ref, b_ref, o_ref, buf_ref, sem, *, alpha):
    i = pl.program_id(0)
    n = pl.num_programs(0)
    slot = i % 2
    rows = h_ref.shape[0]

    def start(idx_ref, into):
        for k in range(2):
            _start_row_copies(y_hbm, idx_ref, k, buf_ref, into, k * rows, sem, rows)

    @pl.when(i == 0)
    def _():
        start(pos_ref, 0)

    @pl.when(i + 1 < n)
    def _():
        start(pos_next_ref, 1 - slot)

    _wait_slot(y_hbm, buf_ref, slot, sem)

    def norm_rows(j, carry):
        r0 = pl.multiple_of(j * SUBLANES, SUBLANES)
        gates = gate_ref[pl.ds(r0, SUBLANES), :]
        ffn = (buf_ref[slot, pl.ds(r0, SUBLANES), :] * gates[:, 0:1]
               + buf_ref[slot, pl.ds(rows + r0, SUBLANES), :] * gates[:, 1:2])
        z = alpha * h_ref[pl.ds(r0, SUBLANES), :] + ffn
        o_ref[pl.ds(r0, SUBLANES), :] = _layer_norm_rows(z, g_ref[...], b_ref[...])
        return carry
    lax.fori_loop(0, rows // SUBLANES, norm_rows, 0, unroll=2)


def _combine_ln(y, pos2, h1, gate_lanes, ln_g, ln_b, alpha):
    n, d = h1.shape
    tb = min(COMBINE_ROWS, n)
    nblk = n // tb
    pos_blocks = pos2.reshape(nblk, tb, 2).transpose(0, 2, 1)
    kern = functools.partial(_combine_kernel, alpha=alpha)
    return pl.pallas_call(
        kern,
        grid=(nblk,),
        in_specs=[pl.BlockSpec((1, 2, tb), lambda i: (i, 0, 0), memory_space=pltpu.SMEM),
                  pl.BlockSpec((1, 2, tb), lambda i: (jnp.minimum(i + 1, nblk - 1), 0, 0), memory_space=pltpu.SMEM),
                  pl.BlockSpec(memory_space=pl.ANY),
                  pl.BlockSpec((tb, d), lambda i: (i, 0)),
                  pl.BlockSpec((tb, LANES), lambda i: (i, 0)),
                  pl.BlockSpec((1, d), lambda i: (0, 0)),
                  pl.BlockSpec((1, d), lambda i: (0, 0))],
        out_specs=pl.BlockSpec((tb, d), lambda i: (i, 0)),
        out_shape=jax.ShapeDtypeStruct((n, d), F32),
        scratch_shapes=[pltpu.VMEM((2, 2 * tb, d), F32), pltpu.SemaphoreType.DMA((2,))],
        compiler_params=_cparams(("arbitrary",), 32),
        name="moe_combine_ln2",
    )(pos_blocks, pos_blocks, y, h1, gate_lanes, ln_g, ln_b)


def _cast_weight(dst_ref, src_ref):
    rows = src_ref.shape[1]
    step = min(256, rows)

    def body(i, carry):
        r0 = pl.multiple_of(i * step, step)
        dst_ref[pl.ds(r0, step), :] = src_ref[0, pl.ds(r0, step), :].astype(BF16)
        return carry
    lax.fori_loop(0, rows // step, body, 0)


ITEM_COMPUTE = 1
ITEM_NEW_WEIGHTS = 2
ITEM_ZERO_FILL = 4


def _moe_up_kernel(e_ref, cw_ref, bi_ref, bo_ref, co_ref, flag_ref, x_ref, w1_ref, w3_ref, o_ref, w1b_ref, w3b_ref):
    flag = flag_ref[pl.program_id(0)]

    @pl.when((flag & ITEM_NEW_WEIGHTS) != 0)
    def _():
        _cast_weight(w1b_ref, w1_ref)
        _cast_weight(w3b_ref, w3_ref)

    @pl.when((flag & ITEM_COMPUTE) != 0)
    def _():
        x_lo, x_hi = _unpack_bf16_halves(x_ref[...])
        half = x_lo.shape[1]

        def project(w_ref):
            return (jnp.dot(x_lo, w_ref[0:half, :], preferred_element_type=F32)
                    + jnp.dot(x_hi, w_ref[half:2 * half, :], preferred_element_type=F32))

        a = project(w1b_ref)
        b = project(w3b_ref)
        o_ref[...] = (a * jax.nn.sigmoid(a) * b).astype(o_ref.dtype)

    @pl.when((flag & ITEM_ZERO_FILL) != 0)
    def _():
        o_ref[...] = jnp.zeros(o_ref.shape, o_ref.dtype)


def _moe_down_kernel(e_ref, cw_ref, bi_ref, bo_ref, co_ref, flag_ref, h_ref, w2_ref, o_ref, w2b_ref):
    flag = flag_ref[pl.program_id(0)]

    @pl.when((flag & ITEM_NEW_WEIGHTS) != 0)
    def _():
        _cast_weight(w2b_ref, w2_ref)

    @pl.when((flag & ITEM_COMPUTE) != 0)
    def _():
        o_ref[...] = jnp.dot(h_ref[...], w2b_ref[...], preferred_element_type=F32)

    @pl.when((flag & ITEM_ZERO_FILL) != 0)
    def _():
        o_ref[...] = jnp.zeros(o_ref.shape, o_ref.dtype)


def _work_items(blocks_per_expert, block_start, n_chunks, n_blocks):
    n_items = n_chunks * n_blocks
    per_e = n_chunks * blocks_per_expert
    end = jnp.cumsum(per_e)
    start = end - per_e
    it = jnp.arange(n_items, dtype=I32)
    total = end[-1]
    used_blocks = total // n_chunks
    itc = jnp.minimum(it, total - 1)
    e = jnp.sum(itc[:, None] >= end[None, :], axis=1).astype(I32)
    local = itc - start[e]
    nb = jnp.maximum(blocks_per_expert[e], 1)
    c = local // nb
    r = local - c * nb
    b_in = block_start[e] + r
    active = it < total
    spare = jnp.maximum(it - total, 0)
    b_out = jnp.where(active, b_in, used_blocks + spare // n_chunks)
    c_out = jnp.where(active, c, spare % n_chunks)
    first = active & (r == 0)
    flag = jnp.where(active, ITEM_COMPUTE + ITEM_NEW_WEIGHTS * first.astype(I32), ITEM_ZERO_FILL)
    later_first = lax.cummin(jnp.where(first, it, n_items)[::-1])[::-1]
    next_start = jnp.concatenate([later_first[1:], jnp.full((1,), n_items, I32)])
    ahead = jnp.minimum(next_start, n_items - 1)
    use_next = active & ~first & (next_start < n_items)
    e_w = jnp.where(use_next, e[ahead], e)
    c_w = jnp.where(use_next, c[ahead], c)
    return tuple(v.astype(I32) for v in (e_w, c_w, b_in, b_out, c_out, flag))


def _moe_up(xs, w1, w3, items, f_chunk):
    p, half = xs.shape
    d, f = w1.shape[1], w1.shape[2]
    assert d == 2 * half
    wspec = pl.BlockSpec((1, d, f_chunk), lambda it, e, cw, bi, bo, co, fl: (e[it], 0, cw[it]))
    return pl.pallas_call(
        _moe_up_kernel,
        grid_spec=pltpu.PrefetchScalarGridSpec(
            num_scalar_prefetch=6,
            grid=(items[0].shape[0],),
            in_specs=[pl.BlockSpec((MOE_ROWS, half), lambda it, e, cw, bi, bo, co, fl: (bi[it], 0)), wspec, wspec],
            out_specs=pl.BlockSpec((MOE_ROWS, f_chunk), lambda it, e, cw, bi, bo, co, fl: (bo[it], co[it])),
            scratch_shapes=[pltpu.VMEM((d, f_chunk), BF16), pltpu.VMEM((d, f_chunk), BF16)]),
        out_shape=jax.ShapeDtypeStruct((p, f), BF16),
        compiler_params=_cparams(("arbitrary",), 56),
        name="moe_up",
    )(*items, xs, w1, w3)


def _moe_down(hmid, w2, items, d_chunk):
    p, f = hmid.shape
    d = w2.shape[2]
    return pl.pallas_call(
        _moe_down_kernel,
        grid_spec=pltpu.PrefetchScalarGridSpec(
            num_scalar_prefetch=6,
            grid=(items[0].shape[0],),
            in_specs=[pl.BlockSpec((MOE_ROWS, f), lambda it, e, cw, bi, bo, co, fl: (bi[it], 0)),
                      pl.BlockSpec((1, f, d_chunk), lambda it, e, cw, bi, bo, co, fl: (e[it], 0, cw[it]))],
            out_specs=pl.BlockSpec((MOE_ROWS, d_chunk), lambda it, e, cw, bi, bo, co, fl: (bo[it], co[it])),
            scratch_shapes=[pltpu.VMEM((f, d_chunk), BF16)]),
        out_shape=jax.ShapeDtypeStruct((p, d), F32),
        compiler_params=_cparams(("arbitrary",), 48),
        name="moe_down",
    )(*items, hmid, w2)


def _hier_moe_ln(h1, hp, logits, w1, w3, w2, ln_g, ln_b, alpha):
    n, d = h1.shape
    e_lanes, gate_lanes, lane_counts = _route(logits)
    a = 2 * n
    e_flat = e_lanes[:, 0:2].reshape(a)
    rank = e_lanes[:, 2:4].reshape(a)
    counts = lane_counts[0, N_GROUPS:N_GROUPS + N_EXPERTS].astype(I32)
    blocks_per_expert = (counts + MOE_ROWS - 1) // MOE_ROWS
    block_start = jnp.cumsum(blocks_per_expert) - blocks_per_expert
    onehot = e_flat[:, None] == jnp.arange(N_EXPERTS, dtype=I32)[None, :]
    pos = jnp.sum(jnp.where(onehot, block_start[None, :], 0), axis=1) * MOE_ROWS + rank
    n_blocks = (a + MOE_ROWS - 1) // MOE_ROWS + N_EXPERTS
    buf_tok = jnp.zeros((n_blocks * MOE_ROWS,), I32).at[pos].set(jnp.arange(a, dtype=I32) // 2)

    f = w1.shape[2]
    f_chunk = min(512, f)
    d_chunk = min(2048, d)
    xs = _gather_tokens(hp, buf_tok, jnp.sum(blocks_per_expert).astype(I32))
    hmid = _moe_up(xs, w1, w3, _work_items(blocks_per_expert, block_start, f // f_chunk, n_blocks), f_chunk)
    y = _moe_down(hmid, w2, _work_items(blocks_per_expert, block_start, d // d_chunk, n_blocks), d_chunk)
    return _combine_ln(y, pos.reshape(n, 2), h1, gate_lanes, ln_g, ln_b, alpha)


def kernel(x, w_in, conv_dw_w, conv_dw_b, conv_ln_g, conv_ln_b, w_out, ln1_g, ln1_b, w_router_group, b_router_group, w_router_expert, b_router_expert, w_expert_gate, w_expert_up, w_expert_down, ln2_g, ln2_b):
    b, s, d = x.shape
    depth = w_in.shape[0]
    assert depth == 1
    alpha = float((2.0 * depth) ** 0.25)
    n = b * s
    dc = conv_dw_w.shape[2]
    d_attn = N_HEADS * HEAD_DIM
    d_kv = N_KV_HEADS * HEAD_DIM
    d_qi = IDX_HEADS * IDX_DIM
    n_small = IDX_DIM + IDX_HEADS
    assert w_in.shape[2] == 2 * dc + d_attn + 2 * d_kv + d_qi + n_small

    x2 = x.reshape(n, d)
    xb = x2.astype(BF16)
    w = w_in[0]
    o_qkvi = 2 * dc
    o_small = o_qkvi + d_attn + 2 * d_kv + d_qi
    wa = w[:, :dc].astype(BF16)
    wg = w[:, dc:2 * dc].astype(BF16)
    w_qkvi = w[:, o_qkvi:o_small].astype(BF16)
    w_small = jnp.pad(w[:, o_small:], ((0, 0), (0, LANES - n_small))).astype(BF16)
    qkvi_scale = jnp.concatenate([jnp.full((d_attn,), HEAD_DIM ** -0.5 * LOG2E, F32), jnp.ones((2 * d_kv,), F32),
                                  jnp.full((d_qi,), IDX_DIM ** -0.5, F32)])[None, :]

    glu = _proj_glu(xb, wa, wg)
    qkvi = _proj_scale(xb, w_qkvi, qkvi_scale, BF16, "proj_qkvi")
    small = _proj_scale(xb, w_small, jnp.ones((1, LANES), F32), F32, "proj_idx")

    conv_out = _conformer_conv(glu.reshape(b, s, dc), conv_dw_w[0], conv_dw_b, conv_ln_g, conv_ln_b)
    attn_out = _dsa_attention(qkvi.reshape(b, s, -1), small.reshape(b, s, LANES), b, s)

    n_route = N_GROUPS + N_EXPERTS
    w_router = jnp.pad(jnp.concatenate([w_router_group[0], w_router_expert[0]], axis=1),
                       ((0, 0), (0, LANES - n_route))).astype(BF16)
    b_router = jnp.pad(jnp.concatenate([b_router_group[0], b_router_expert[0]]), (0, LANES - n_route))[None, :]
    h1, hp, logits = _out_proj_ln(conv_out.reshape(n, dc), attn_out.reshape(n, d_attn),
                                  w_out[0, :dc].astype(BF16), w_out[0, dc:].astype(BF16),
                                  x2, ln1_g, ln1_b, w_router, b_router, alpha)

    out = _hier_moe_ln(h1, hp, logits, w_expert_gate[0], w_expert_up[0], w_expert_down[0], ln2_g, ln2_b, alpha)
    return out.reshape(b, s, d)
```

```python
import functools
import math

import jax
import jax.numpy as jnp
from jax import lax
from jax.experimental import pallas as pl
from jax.experimental.pallas import tpu as pltpu

F32 = jnp.float32
BF16 = jnp.bfloat16
I32 = jnp.int32

CONV_WIDTH = 31
N_HEADS = 16
HEAD_DIM = 128
N_KV_HEADS = 4
IDX_HEADS = 16
IDX_DIM = 64
TOPK_MAX = 256
N_GROUPS = 4
EXPERTS_PER_GROUP = 8
N_EXPERTS = N_GROUPS * EXPERTS_PER_GROUP
LN_EPS = 1e-5

LANES = 128
SUBLANES = 8
BF16_ROWS = 16
MIB = 1024 * 1024

Q_TILE = 128
KEY_CHUNK = 512
CONV_HALO = 32
MOE_ROWS = 256
COMBINE_ROWS = 128
INT_MIN = -(2 ** 31)
INT_MAX = 2 ** 31 - 1
LOG2E = math.log2(math.e)
NEG_BIG = -1e30
M_INIT = -1e20


def _cparams(semantics, vmem_mib):
    return pltpu.CompilerParams(dimension_semantics=semantics, vmem_limit_bytes=vmem_mib * MIB)


def _layer_norm_rows(z, g, b):
    mu = jnp.mean(z, axis=-1, keepdims=True)
    zc = z - mu
    var = jnp.mean(zc * zc, axis=-1, keepdims=True)
    return zc * lax.rsqrt(var + LN_EPS) * g + b


def _mm_glu_kernel(x_ref, wa_ref, wg_ref, o_ref):
    x = x_ref[...]
    a = jnp.dot(x, wa_ref[...], preferred_element_type=F32)
    g = jnp.dot(x, wg_ref[...], preferred_element_type=F32)
    o_ref[...] = (a * jax.nn.sigmoid(g)).astype(o_ref.dtype)


def _mm_scale_kernel(x_ref, w_ref, s_ref, o_ref):
    acc = jnp.dot(x_ref[...], w_ref[...], preferred_element_type=F32)
    o_ref[...] = (acc * s_ref[...]).astype(o_ref.dtype)


def _mm_tiles(n, k, cols):
    tm = min(1024, n)
    tn = min(512, cols)
    assert n % tm == 0 and cols % tn == 0
    return tm, tn


def _proj_glu(xb, w, cols):
    n, k = xb.shape
    tm, tn = _mm_tiles(n, k, cols)
    gate_block = cols // tn
    return pl.pallas_call(
        _mm_glu_kernel,
        grid=(n // tm, cols // tn),
        in_specs=[pl.BlockSpec((tm, k), lambda i, j: (i, 0)),
                  pl.BlockSpec((k, tn), lambda i, j: (0, j)),
                  pl.BlockSpec((k, tn), lambda i, j: (0, j + gate_block))],
        out_specs=pl.BlockSpec((tm, tn), lambda i, j: (i, j)),
        out_shape=jax.ShapeDtypeStruct((n, cols), F32),
        compiler_params=_cparams(("parallel", "arbitrary"), 48),
        name="proj_glu",
    )(xb, w, w)


def _proj_scale(xb, w, first_col, cols, scale, out_dtype, name):
    n, k = xb.shape
    tm, tn = _mm_tiles(n, k, cols)
    assert first_col % tn == 0
    first_block = first_col // tn
    return pl.pallas_call(
        _mm_scale_kernel,
        grid=(n // tm, cols // tn),
        in_specs=[pl.BlockSpec((tm, k), lambda i, j: (i, 0)),
                  pl.BlockSpec((k, tn), lambda i, j: (0, j + first_block)),
                  pl.BlockSpec((1, tn), lambda i, j: (0, j))],
        out_specs=pl.BlockSpec((tm, tn), lambda i, j: (i, j)),
        out_shape=jax.ShapeDtypeStruct((n, cols), out_dtype),
        compiler_params=_cparams(("parallel", "arbitrary"), 48),
        name=name,
    )(xb, w, scale)


def _idx_cast_kernel(x_ref, w_ref, xb_ref, o_ref):
    xb = x_ref[...].astype(BF16)
    xb_ref[...] = xb
    o_ref[...] = jnp.dot(xb, w_ref[...], preferred_element_type=F32)


def _proj_idx_and_cast(x2, w_small):
    n, k = x2.shape
    tm = min(256, n)
    return pl.pallas_call(
        _idx_cast_kernel,
        grid=(n // tm,),
        in_specs=[pl.BlockSpec((tm, k), lambda i: (i, 0)),
                  pl.BlockSpec((k, LANES), lambda i: (0, 0))],
        out_specs=[pl.BlockSpec((tm, k), lambda i: (i, 0)),
                   pl.BlockSpec((tm, LANES), lambda i: (i, 0))],
        out_shape=[jax.ShapeDtypeStruct((n, k), BF16), jax.ShapeDtypeStruct((n, LANES), F32)],
        compiler_params=_cparams(("parallel",), 32),
        name="proj_idx",
    )(x2, w_small)


def _conv_kernel(cur_ref, prev_ref, w_ref, b_ref, g_ref, beta_ref, o_ref, sh_ref, y_ref, *, rows, lane_chunk):
    t = pl.program_id(1)
    dc = cur_ref.shape[2]
    row_sub = 32
    sh_ref[0, 0:CONV_HALO, :] = jnp.where(t > 0, prev_ref[0], 0.0)
    sh_ref[0, CONV_HALO:CONV_HALO + rows, :] = cur_ref[0]
    first = CONV_HALO - (CONV_WIDTH - 1)
    shifted_rows = rows + CONV_HALO - SUBLANES
    for lc in range(dc // lane_chunk):
        cols = slice(lc * lane_chunk, (lc + 1) * lane_chunk)
        for r in range(1, SUBLANES):
            sh_ref[r, 0:shifted_rows, cols] = sh_ref[0, r:r + shifted_rows, cols]

    for r0 in range(0, rows, row_sub):
        for lc in range(dc // lane_chunk):
            cols = slice(lc * lane_chunk, (lc + 1) * lane_chunk)
            acc = jnp.zeros((row_sub, lane_chunk), F32)
            for j in range(CONV_WIDTH):
                shift, base = (first + j) % SUBLANES, (first + j) // SUBLANES * SUBLANES
                acc = acc + sh_ref[shift, r0 + base:r0 + base + row_sub, cols] * w_ref[j:j + 1, cols]
            y_ref[r0:r0 + row_sub, cols] = acc + b_ref[:, cols]

    def norm_rows(i, carry):
        r0 = pl.multiple_of(i * BF16_ROWS, BF16_ROWS)
        yn = _layer_norm_rows(y_ref[pl.ds(r0, BF16_ROWS), :], g_ref[...], beta_ref[...])
        o_ref[0, pl.ds(r0, BF16_ROWS), :] = (yn * jax.nn.sigmoid(yn)).astype(o_ref.dtype)
        return carry

    lax.fori_loop(0, rows // BF16_ROWS, norm_rows, 0, unroll=2)


def _conformer_conv(glu, w_dw, b_dw, g_ln, b_ln):
    b, s, dc = glu.shape
    rows = min(128, s)
    halo_blocks = rows // CONV_HALO
    lane_chunk = min(512, dc)
    kern = functools.partial(_conv_kernel, rows=rows, lane_chunk=lane_chunk)
    return pl.pallas_call(
        kern,
        grid=(b, s // rows),
        in_specs=[pl.BlockSpec((1, rows, dc), lambda bi, t: (bi, t, 0)),
                  pl.BlockSpec((1, CONV_HALO, dc), lambda bi, t: (bi, jnp.maximum(t * halo_blocks - 1, 0), 0)),
                  pl.BlockSpec((CONV_WIDTH, dc), lambda bi, t: (0, 0)),
                  pl.BlockSpec((1, dc), lambda bi, t: (0, 0)),
                  pl.BlockSpec((1, dc), lambda bi, t: (0, 0)),
                  pl.BlockSpec((1, dc), lambda bi, t: (0, 0))],
        out_specs=pl.BlockSpec((1, rows, dc), lambda bi, t: (bi, t, 0)),
        out_shape=jax.ShapeDtypeStruct((b, s, dc), BF16),
        scratch_shapes=[pltpu.VMEM((SUBLANES, CONV_HALO + rows, dc), F32), pltpu.VMEM((rows, dc), F32)],
        compiler_params=_cparams(("parallel", "arbitrary"), 32),
        name="conformer_conv",
    )(glu, glu, w_dw, b_dw, g_ln, b_ln)


def _dsa_kernel(q_ref, qi_ref, k_ref, v_ref, kidx_ref, w_ref, o_ref,
                ke_ref, ko_ref, vt_ref, key_ref, cut_ref, nd_ref, qs_ref, qis_ref, m_ref, l_ref, acc_ref,
                *, seq, chunk, topk):
    tb = pl.program_id(1)
    t0 = tb * Q_TILE
    n_chunks = (t0 + Q_TILE + chunk - 1) // chunk
    rep = N_HEADS // N_KV_HEADS
    nt = (((1,), (1,)), ((), ()))

    @pl.when(tb == 0)
    def _():
        def build(c, carry):
            r0 = pl.multiple_of(c * chunk, chunk)
            kx = kidx_ref[0, pl.ds(r0, chunk), :]
            lane = lax.broadcasted_iota(I32, kx.shape, 1)
            ke_ref[pl.ds(r0, chunk), :] = jnp.where(lane < IDX_DIM, kx, 0.0).astype(BF16)
            ko_ref[pl.ds(r0, chunk), :] = jnp.where(lane >= IDX_DIM, pltpu.roll(kx, IDX_DIM, 1), 0.0).astype(BF16)
            for g in range(N_KV_HEADS):
                vg = v_ref[0, pl.ds(r0, chunk), g * HEAD_DIM:(g + 1) * HEAD_DIM]
                vt_ref[c, g * HEAD_DIM:(g + 1) * HEAD_DIM, :] = vg.astype(F32).T.astype(BF16)
            return carry
        lax.fori_loop(0, seq // chunk, build, 0)

    for g in range(N_KV_HEADS):
        for r in range(rep):
            h = g * rep + r
            qs_ref[g, r * Q_TILE:(r + 1) * Q_TILE, :] = q_ref[0, :, h * HEAD_DIM:(h + 1) * HEAD_DIM]
    for jj in range(IDX_HEADS // 4):
        for half in range(2):
            pair = 2 * jj + half
            qis_ref[jj, half * Q_TILE:(half + 1) * Q_TILE, :] = qi_ref[0, :, pair * LANES:(pair + 1) * LANES]
    w_t = w_ref[0].T * (IDX_HEADS ** -0.5)

    key_row = lax.broadcasted_iota(I32, (chunk, Q_TILE), 0)
    q_pos = t0 + lax.broadcasted_iota(I32, (chunk, Q_TILE), 1)

    def score_chunk(c, carry):
        r0 = pl.multiple_of(c * chunk, chunk)
        ke = ke_ref[pl.ds(r0, chunk), :]
        ko = ko_ref[pl.ds(r0, chunk), :]
        acc = jnp.zeros((chunk, Q_TILE), F32)
        for jj in range(IDX_HEADS // 4):
            rhs = qis_ref[jj]
            de = lax.dot_general(ke, rhs, nt, preferred_element_type=F32)
            do = lax.dot_general(ko, rhs, nt, preferred_element_type=F32)
            for half in range(2):
                h_even = 2 * (2 * jj + half)
                cols = slice(half * Q_TILE, (half + 1) * Q_TILE)
                acc = acc + w_t[IDX_DIM + h_even:IDX_DIM + h_even + 1, :] * jnp.maximum(de[:, cols], 0.0)
                acc = acc + w_t[IDX_DIM + h_even + 1:IDX_DIM + h_even + 2, :] * jnp.maximum(do[:, cols], 0.0)
        bits = lax.bitcast_convert_type(acc, I32)
        key = bits ^ ((bits >> 31) & 0x7FFFFFFF)
        key_ref[c] = jnp.where(r0 + key_row <= q_pos, key, INT_MIN)
        return carry

    lax.fori_loop(0, n_chunks, score_chunk, 0)

    def count_ge(cand):
        def count_chunk(c, cnts):
            cnts = list(cnts)
            for r in range(chunk // SUBLANES):
                slab = key_ref[c, r * SUBLANES:(r + 1) * SUBLANES, :]
                cnts[r % len(cnts)] = cnts[r % len(cnts)] + jnp.where(slab >= cand, 1.0, 0.0)
            return tuple(cnts)

        zero = jnp.zeros((SUBLANES, Q_TILE), F32)
        cnts = lax.fori_loop(0, n_chunks, count_chunk, (zero, zero, zero, zero))
        return jnp.sum((cnts[0] + cnts[1]) + (cnts[2] + cnts[3]), axis=0, keepdims=True)

    def bit_step(i, t_u):
        cand_u = t_u | (jnp.int32(1) << (31 - i))
        return jnp.where(count_ge(cand_u ^ INT_MIN) >= float(topk), cand_u, t_u)

    t_u = lax.fori_loop(0, 32, bit_step, jnp.zeros((1, Q_TILE), I32))
    thr = jnp.maximum(t_u ^ INT_MIN, INT_MIN + 1)

    n_ge = count_ge(thr)
    cut_ref[...] = jnp.full(cut_ref.shape, INT_MAX, I32)

    @pl.when(jnp.max(n_ge) > topk)
    def _():
        def count_where(pred):
            def count_chunk(c, cnt):
                r0 = pl.multiple_of(c * chunk, chunk)
                hit = jnp.where(pred(key_ref[c], r0 + key_row), 1, 0)
                return cnt + jnp.sum(hit, axis=0, keepdims=True)
            return lax.fori_loop(0, n_chunks, count_chunk, jnp.zeros((1, Q_TILE), I32))

        keep = topk - count_where(lambda kc, pos: kc > thr)
        pos_bits = seq.bit_length()

        def bit_step(i, cut):
            cand = cut | (jnp.int32(1) << (pos_bits - 1 - i))
            taken = count_where(lambda kc, pos: jnp.where(kc == thr, pos, INT_MAX) < cand)
            return jnp.where(taken <= keep, cand, cut)
        cut_ref[...] = lax.fori_loop(0, pos_bits, bit_step, jnp.zeros((1, Q_TILE), I32))

    cut = cut_ref[...]

    m_ref[...] = jnp.full(m_ref.shape, M_INIT, F32)
    l_ref[...] = jnp.zeros(l_ref.shape, F32)
    acc_ref[...] = jnp.zeros(acc_ref.shape, F32)

    def attend_chunk(c, carry):
        r0 = pl.multiple_of(c * chunk, chunk)
        pos = r0 + key_row
        kc = key_ref[c]
        rank_pos = jnp.where(kc > thr, -1, jnp.where(kc == thr, pos, INT_MAX))
        nd_ref[...] = jnp.where(rank_pos < cut, (pos - q_pos).astype(F32), NEG_BIG)
        def logits(g):
            kg = k_ref[0, pl.ds(r0, chunk), g * HEAD_DIM:(g + 1) * HEAD_DIM]
            return lax.dot_general(kg, qs_ref[g], nt, preferred_element_type=F32)

        s_next = logits(0)
        for g in range(N_KV_HEADS):
            s_all = s_next
            if g + 1 < N_KV_HEADS:
                s_next = logits(g + 1)
            probs = []
            alphas = []
            for r in range(rep):
                h = g * rep + r
                slope = float(2.0 ** (-8.0 * (h + 1) / N_HEADS)) * LOG2E
                sr = s_all[:, r * Q_TILE:(r + 1) * Q_TILE] + slope * nd_ref[...]
                m_old = m_ref[h]
                m_new = jnp.maximum(m_old, jnp.max(sr, axis=0, keepdims=True))
                alpha = jnp.exp2(m_old - m_new)
                p = jnp.exp2(sr - m_new)
                l_ref[h] = alpha * l_ref[h] + jnp.sum(p, axis=0, keepdims=True)
                m_ref[h] = m_new
                probs.append(p.astype(BF16))
                alphas.append(alpha)
            vt = vt_ref[c, g * HEAD_DIM:(g + 1) * HEAD_DIM, :]
            pv = jnp.dot(vt, jnp.concatenate(probs, axis=1), preferred_element_type=F32)
            acc_ref[g] = jnp.concatenate(alphas, axis=1) * acc_ref[g] + pv
        return carry

    lax.fori_loop(0, n_chunks, attend_chunk, 0)

    for g in range(N_KV_HEADS):
        for r in range(rep):
            h = g * rep + r
            o_t = acc_ref[g, :, r * Q_TILE:(r + 1) * Q_TILE] * (1.0 / l_ref[h])
            o_ref[0, :, h * HEAD_DIM:(h + 1) * HEAD_DIM] = o_t.T.astype(o_ref.dtype)


def _dsa_attention(qkvi, small, b, s):
    d_attn = N_HEADS * HEAD_DIM
    d_kv = N_KV_HEADS * HEAD_DIM
    d_qi = IDX_HEADS * IDX_DIM
    rep = N_HEADS // N_KV_HEADS
    chunk = min(KEY_CHUNK, s)
    topk = min(TOPK_MAX, s // 4)
    assert s % chunk == 0 and chunk % Q_TILE == 0 and d_attn % d_qi == 0 and d_attn % d_kv == 0
    assert Q_TILE == LANES and HEAD_DIM == LANES and 2 * IDX_DIM == LANES
    kern = functools.partial(_dsa_kernel, seq=s, chunk=chunk, topk=topk)
    return pl.pallas_call(
        kern,
        grid=(b, s // Q_TILE),
        in_specs=[pl.BlockSpec((1, Q_TILE, d_attn), lambda bi, t: (bi, t, 0)),
                  pl.BlockSpec((1, Q_TILE, d_qi), lambda bi, t: (bi, t, (d_attn + 2 * d_kv) // d_qi)),
                  pl.BlockSpec((1, s, d_kv), lambda bi, t: (bi, 0, d_attn // d_kv)),
                  pl.BlockSpec((1, s, d_kv), lambda bi, t: (bi, 0, d_attn // d_kv + 1)),
                  pl.BlockSpec((1, s, LANES), lambda bi, t: (bi, 0, 0)),
                  pl.BlockSpec((1, Q_TILE, LANES), lambda bi, t: (bi, t, 0))],
        out_specs=pl.BlockSpec((1, Q_TILE, d_attn), lambda bi, t: (bi, t, 0)),
        out_shape=jax.ShapeDtypeStruct((b, s, d_attn), BF16),
        scratch_shapes=[pltpu.VMEM((s, LANES), BF16),
                        pltpu.VMEM((s, LANES), BF16),
                        pltpu.VMEM((s // chunk, d_kv, chunk), BF16),
                        pltpu.VMEM((s // chunk, chunk, Q_TILE), I32),
                        pltpu.VMEM((1, Q_TILE), I32),
                        pltpu.VMEM((chunk, Q_TILE), F32),
                        pltpu.VMEM((N_KV_HEADS, rep * Q_TILE, HEAD_DIM), BF16),
                        pltpu.VMEM((IDX_HEADS // 4, 2 * Q_TILE, LANES), BF16),
                        pltpu.VMEM((N_HEADS, 1, Q_TILE), F32),
                        pltpu.VMEM((N_HEADS, 1, Q_TILE), F32),
                        pltpu.VMEM((N_KV_HEADS, HEAD_DIM, rep * Q_TILE), F32)],
        compiler_params=_cparams(("parallel", "arbitrary"), 48),
        name="dsa_attention",
    )(qkvi, qkvi, qkvi, qkvi, small, small)


def _pack_bf16_halves(hb):
    half = hb.shape[1] // 2
    bits = lax.bitcast_convert_type(hb.astype(F32), I32)
    return lax.shift_right_logical(bits[:, :half], 16) | (bits[:, half:] & -65536)


def _unpack_bf16_halves(words):
    lo = lax.bitcast_convert_type(words << 16, F32).astype(BF16)
    hi = lax.bitcast_convert_type(words & -65536, F32).astype(BF16)
    return lo, hi


def _outproj_kernel(conv_ref, attn_ref, wc_ref, wa_ref, x_ref, g_ref, b_ref, wr_ref, br_ref,
                    h_ref, hp_ref, lg_ref, hb_ref, *, alpha):
    j = pl.program_id(1)
    tm, d = h_ref.shape
    tn = x_ref.shape[1]
    n_tiles = d // tn
    mix = jnp.dot(conv_ref[...], wc_ref[...], preferred_element_type=F32)
    mix = mix + jnp.dot(attn_ref[...], wa_ref[...], preferred_element_type=F32)
    z = alpha * x_ref[...] + mix
    for jj in range(n_tiles):
        @pl.when(j == jj)
        def _(jj=jj):
            h_ref[:, jj * tn:(jj + 1) * tn] = z

    @pl.when(j == n_tiles - 1)
    def _():
        def rows(i, carry):
            starts = [pl.multiple_of((2 * i + k) * BF16_ROWS, BF16_ROWS) for k in range(2)]
            zs = [h_ref[pl.ds(r0, BF16_ROWS), :] for r0 in starts]
            hs = [_layer_norm_rows(z, g_ref[...], b_ref[...]) for z in zs]
            for r0, h in zip(starts, hs):
                h_ref[pl.ds(r0, BF16_ROWS), :] = h
                hb_ref[pl.ds(r0, BF16_ROWS), :] = h.astype(BF16)
            return carry
        lax.fori_loop(0, tm // (2 * BF16_ROWS), rows, 0)

        def pack_rows(i, carry):
            r0 = pl.multiple_of(i * BF16_ROWS, BF16_ROWS)
            hp_ref[pl.ds(r0, BF16_ROWS), :] = _pack_bf16_halves(hb_ref[pl.ds(r0, BF16_ROWS), :])
            return carry
        lax.fori_loop(0, tm // BF16_ROWS, pack_rows, 0, unroll=2)
        lg_ref[...] = jnp.dot(hb_ref[...], wr_ref[...], preferred_element_type=F32) + br_ref[...]


def _out_proj_ln(conv_out, attn_out, w_conv, w_attn, x2, ln_g, ln_b, w_router, b_router, alpha):
    n, d = x2.shape
    dc = conv_out.shape[1]
    da = attn_out.shape[1]
    tm = min(512, n)
    tn = min(512, d)
    assert n % tm == 0 and d % tn == 0
    kern = functools.partial(_outproj_kernel, alpha=alpha)
    return pl.pallas_call(
        kern,
        grid=(n // tm, d // tn),
        in_specs=[pl.BlockSpec((tm, dc), lambda i, j: (i, 0)),
                  pl.BlockSpec((tm, da), lambda i, j: (i, 0)),
                  pl.BlockSpec((dc, tn), lambda i, j: (0, j)),
                  pl.BlockSpec((da, tn), lambda i, j: (0, j)),
                  pl.BlockSpec((tm, tn), lambda i, j: (i, j)),
                  pl.BlockSpec((1, d), lambda i, j: (0, 0)),
                  pl.BlockSpec((1, d), lambda i, j: (0, 0)),
                  pl.BlockSpec((d, LANES), lambda i, j: (0, 0)),
                  pl.BlockSpec((1, LANES), lambda i, j: (0, 0))],
        out_specs=[pl.BlockSpec((tm, d), lambda i, j: (i, 0)),
                   pl.BlockSpec((tm, d // 2), lambda i, j: (i, 0)),
                   pl.BlockSpec((tm, LANES), lambda i, j: (i, 0))],
        out_shape=[jax.ShapeDtypeStruct((n, d), F32), jax.ShapeDtypeStruct((n, d // 2), I32),
                   jax.ShapeDtypeStruct((n, LANES), F32)],
        scratch_shapes=[pltpu.VMEM((tm, d), BF16)],
        compiler_params=_cparams(("parallel", "arbitrary"), 56),
        name="out_proj_ln1",
    )(conv_out, attn_out, w_conv, w_attn, x2, ln_g, ln_b, w_router, b_router)


def _first_argmax(vals, lane, valid):
    masked = jnp.where(valid, vals, -jnp.inf)
    mx = jnp.max(masked, axis=1, keepdims=True)
    idx = jnp.min(jnp.where(valid & (masked == mx), lane, LANES), axis=1, keepdims=True)
    return mx, idx


def _router_kernel(lg_ref, tri_ref, e_ref, gate_ref, cnt_ref):
    @pl.when(pl.program_id(0) == 0)
    def _():
        cnt_ref[...] = jnp.zeros(cnt_ref.shape, F32)

    lg = lg_ref[...]
    lane = lax.broadcasted_iota(I32, lg.shape, 1)
    is_group = lane < N_GROUPS
    g_max, g_sel = _first_argmax(lg, lane, is_group)
    g_den = jnp.sum(jnp.where(is_group, jnp.exp(lg - g_max), 0.0), axis=1, keepdims=True)
    p_group = 1.0 / g_den
    lo = N_GROUPS + g_sel * EXPERTS_PER_GROUP
    in_group = (lane >= lo) & (lane < lo + EXPERTS_PER_GROUP)
    e_max, _ = _first_argmax(lg, lane, in_group)
    e_exp = jnp.where(in_group, jnp.exp(lg - e_max), 0.0)
    prob = e_exp / jnp.sum(e_exp, axis=1, keepdims=True)
    p1, i1 = _first_argmax(prob, lane, in_group)
    p2, i2 = _first_argmax(prob, lane, in_group & (lane != i1))
    norm = p_group / (p1 + p2)
    gate_ref[...] = jnp.where(lane == 0, p1 * norm, jnp.where(lane == 1, p2 * norm, 0.0))
    chosen = jnp.where((lane == i1) | (lane == i2), 1.0, 0.0)
    before = jnp.dot(tri_ref[...], chosen.astype(BF16), preferred_element_type=F32) + cnt_ref[...]
    rank1 = jnp.sum(jnp.where(lane == i1, before, 0.0), axis=1, keepdims=True).astype(I32)
    rank2 = jnp.sum(jnp.where(lane == i2, before, 0.0), axis=1, keepdims=True).astype(I32)
    cnt_ref[...] += jnp.sum(chosen, axis=0, keepdims=True)
    e_ref[...] = jnp.where(lane == 0, i1 - N_GROUPS,
                           jnp.where(lane == 1, i2 - N_GROUPS,
                                     jnp.where(lane == 2, rank1, jnp.where(lane == 3, rank2, 0))))


def _route(logits):
    n = logits.shape[0]
    tm = min(512, n)
    tri = jnp.tril(jnp.ones((tm, tm), BF16), -1)
    spec = pl.BlockSpec((tm, LANES), lambda i: (i, 0))
    return pl.pallas_call(
        _router_kernel,
        grid=(n // tm,),
        in_specs=[spec, pl.BlockSpec((tm, tm), lambda i: (0, 0))],
        out_specs=[spec, spec, pl.BlockSpec((1, LANES), lambda i: (0, 0))],
        out_shape=[jax.ShapeDtypeStruct((n, LANES), I32), jax.ShapeDtypeStruct((n, LANES), F32),
                   jax.ShapeDtypeStruct((1, LANES), F32)],
        compiler_params=_cparams(("arbitrary",), 32),
        name="router",
    )(logits, tri)


def _start_row_copies(src_hbm, idx_ref, idx_row, dst_ref, slot, row0, sem, count):
    group = 8

    def issue(i, carry):
        for k in range(group):
            r = i * group + k
            pltpu.make_async_copy(src_hbm.at[pl.ds(idx_ref[0, idx_row, r], 1), :],
                                  dst_ref.at[slot, pl.ds(row0 + r, 1), :], sem.at[slot]).start(priority=k % 2)
        return carry
    lax.fori_loop(0, count // group, issue, 0)


def _wait_slot(src_hbm, dst_ref, slot, sem):
    rows = dst_ref.shape[1]
    pltpu.make_async_copy(src_hbm.at[pl.ds(0, rows), :], dst_ref.at[slot], sem.at[slot]).wait()


def _gather_kernel(used_ref, tok_ref, tok_next_ref, hp_hbm, o_ref, buf_ref, sem):
    i = pl.program_id(0)
    slot = i % 2
    n_used = used_ref[0]
    rows = buf_ref.shape[1]

    @pl.when(i == 0)
    def _():
        _start_row_copies(hp_hbm, tok_ref, 0, buf_ref, 0, 0, sem, rows)

    @pl.when(i + 1 < n_used)
    def _():
        _start_row_copies(hp_hbm, tok_next_ref, 0, buf_ref, 1 - slot, 0, sem, rows)

    @pl.when(i < n_used)
    def _():
        _wait_slot(hp_hbm, buf_ref, slot, sem)
        o_ref[...] = buf_ref[slot]

    @pl.when(i >= n_used)
    def _():
        o_ref[...] = jnp.zeros(o_ref.shape, o_ref.dtype)


def _gather_tokens(hp, buf_tok, n_used):
    p = buf_tok.shape[0]
    half = hp.shape[1]
    nblk = p // MOE_ROWS
    tok_blocks = buf_tok.reshape(nblk, 1, MOE_ROWS)
    return pl.pallas_call(
        _gather_kernel,
        grid_spec=pltpu.PrefetchScalarGridSpec(
            num_scalar_prefetch=1,
            grid=(nblk,),
            in_specs=[pl.BlockSpec((1, 1, MOE_ROWS), lambda i, u: (i, 0, 0), memory_space=pltpu.SMEM),
                      pl.BlockSpec((1, 1, MOE_ROWS), lambda i, u: (jnp.minimum(i + 1, nblk - 1), 0, 0),
                                   memory_space=pltpu.SMEM),
                      pl.BlockSpec(memory_space=pl.ANY)],
            out_specs=pl.BlockSpec((MOE_ROWS, half), lambda i, u: (i, 0)),
            scratch_shapes=[pltpu.VMEM((2, MOE_ROWS, half), I32), pltpu.SemaphoreType.DMA((2,))]),
        out_shape=jax.ShapeDtypeStruct((p, half), I32),
        compiler_params=_cparams(("arbitrary",), 32),
        name="moe_gather",
    )(n_used.reshape(1), tok_blocks, tok_blocks, hp)


def _combine_kernel(pos_ref, pos_next_ref, y_hbm, h_ref, gate_ref, g_ref, b_ref, o_ref, buf_ref, sem, *, alpha):
    i = pl.program_id(0)
    n = pl.num_programs(0)
    slot = i % 2
    rows = h_ref.shape[0]

    def start(idx_ref, into):
        for k in range(2):
            _start_row_copies(y_hbm, idx_ref, k, buf_ref, into, k * rows, sem, rows)

    @pl.when(i == 0)
    def _():
        start(pos_ref, 0)

    @pl.when(i + 1 < n)
    def _():
        start(pos_next_ref, 1 - slot)

    _wait_slot(y_hbm, buf_ref, slot, sem)

    def norm_rows(j, carry):
        r0 = pl.multiple_of(j * SUBLANES, SUBLANES)
        gates = gate_ref[pl.ds(r0, SUBLANES), :]
        ffn = (buf_ref[slot, pl.ds(r0, SUBLANES), :] * gates[:, 0:1]
               + buf_ref[slot, pl.ds(rows + r0, SUBLANES), :] * gates[:, 1:2])
        z = alpha * h_ref[pl.ds(r0, SUBLANES), :] + ffn
        o_ref[pl.ds(r0, SUBLANES), :] = _layer_norm_rows(z, g_ref[...], b_ref[...])
        return carry
    lax.fori_loop(0, rows // SUBLANES, norm_rows, 0, unroll=2)


def _combine_ln(y, pos2, h1, gate_lanes, ln_g, ln_b, alpha):
    n, d = h1.shape
    tb = min(COMBINE_ROWS, n)
    nblk = n // tb
    pos_blocks = pos2.reshape(nblk, tb, 2).transpose(0, 2, 1)
    kern = functools.partial(_combine_kernel, alpha=alpha)
    return pl.pallas_call(
        kern,
        grid=(nblk,),
        in_specs=[pl.BlockSpec((1, 2, tb), lambda i: (i, 0, 0), memory_space=pltpu.SMEM),
                  pl.BlockSpec((1, 2, tb), lambda i: (jnp.minimum(i + 1, nblk - 1), 0, 0), memory_space=pltpu.SMEM),
                  pl.BlockSpec(memory_space=pl.ANY),
                  pl.BlockSpec((tb, d), lambda i: (i, 0)),
                  pl.BlockSpec((tb, LANES), lambda i: (i, 0)),
                  pl.BlockSpec((1, d), lambda i: (0, 0)),
                  pl.BlockSpec((1, d), lambda i: (0, 0))],
        out_specs=pl.BlockSpec((tb, d), lambda i: (i, 0)),
        out_shape=jax.ShapeDtypeStruct((n, d), F32),
        scratch_shapes=[pltpu.VMEM((2, 2 * tb, d), F32), pltpu.SemaphoreType.DMA((2,))],
        compiler_params=_cparams(("arbitrary",), 32),
        name="moe_combine_ln2",
    )(pos_blocks, pos_blocks, y, h1, gate_lanes, ln_g, ln_b)


def _cast_weight(dst_ref, src_ref):
    rows = src_ref.shape[1]
    step = min(256, rows)

    def body(i, carry):
        r0 = pl.multiple_of(i * step, step)
        dst_ref[pl.ds(r0, step), :] = src_ref[0, pl.ds(r0, step), :].astype(BF16)
        return carry
    lax.fori_loop(0, rows // step, body, 0)


ITEM_COMPUTE = 1
ITEM_NEW_WEIGHTS = 2
ITEM_ZERO_FILL = 4


def _moe_up_kernel(e_ref, cw_ref, bi_ref, bo_ref, co_ref, flag_ref, x_ref, w1_ref, w3_ref, o_ref, w1b_ref, w3b_ref):
    flag = flag_ref[pl.program_id(0)]

    @pl.when((flag & ITEM_NEW_WEIGHTS) != 0)
    def _():
        _cast_weight(w1b_ref, w1_ref)
        _cast_weight(w3b_ref, w3_ref)

    @pl.when((flag & ITEM_COMPUTE) != 0)
    def _():
        x_lo, x_hi = _unpack_bf16_halves(x_ref[...])
        half = x_lo.shape[1]

        def project(w_ref):
            return (jnp.dot(x_lo, w_ref[0:half, :], preferred_element_type=F32)
                    + jnp.dot(x_hi, w_ref[half:2 * half, :], preferred_element_type=F32))

        a = project(w1b_ref)
        b = project(w3b_ref)
        o_ref[...] = (a * jax.nn.sigmoid(a) * b).astype(o_ref.dtype)

    @pl.when((flag & ITEM_ZERO_FILL) != 0)
    def _():
        o_ref[...] = jnp.zeros(o_ref.shape, o_ref.dtype)


def _moe_down_kernel(e_ref, cw_ref, bi_ref, bo_ref, co_ref, flag_ref, h_ref, w2_ref, o_ref, w2b_ref):
    flag = flag_ref[pl.program_id(0)]

    @pl.when((flag & ITEM_NEW_WEIGHTS) != 0)
    def _():
        _cast_weight(w2b_ref, w2_ref)

    @pl.when((flag & ITEM_COMPUTE) != 0)
    def _():
        o_ref[...] = jnp.dot(h_ref[...], w2b_ref[...], preferred_element_type=F32)

    @pl.when((flag & ITEM_ZERO_FILL) != 0)
    def _():
        o_ref[...] = jnp.zeros(o_ref.shape, o_ref.dtype)


def _work_items(blocks_per_expert, block_start, n_chunks, n_blocks):
    n_items = n_chunks * n_blocks
    per_e = n_chunks * blocks_per_expert
    end = jnp.cumsum(per_e)
    start = end - per_e
    it = jnp.arange(n_items, dtype=I32)
    total = end[-1]
    used_blocks = total // n_chunks
    itc = jnp.minimum(it, total - 1)
    e = jnp.sum(itc[:, None] >= end[None, :], axis=1).astype(I32)
    local = itc - start[e]
    nb = jnp.maximum(blocks_per_expert[e], 1)
    c = local // nb
    r = local - c * nb
    b_in = block_start[e] + r
    active = it < total
    spare = jnp.maximum(it - total, 0)
    b_out = jnp.where(active, b_in, used_blocks + spare // n_chunks)
    c_out = jnp.where(active, c, spare % n_chunks)
    first = active & (r == 0)
    flag = jnp.where(active, ITEM_COMPUTE + ITEM_NEW_WEIGHTS * first.astype(I32), ITEM_ZERO_FILL)
    later_first = lax.cummin(jnp.where(first, it, n_items)[::-1])[::-1]
    next_start = jnp.concatenate([later_first[1:], jnp.full((1,), n_items, I32)])
    ahead = jnp.minimum(next_start, n_items - 1)
    use_next = active & ~first & (next_start < n_items)
    e_w = jnp.where(use_next, e[ahead], e)
    c_w = jnp.where(use_next, c[ahead], c)
    return tuple(v.astype(I32) for v in (e_w, c_w, b_in, b_out, c_out, flag))


def _moe_up(xs, w1, w3, items, f_chunk):
    p, half = xs.shape
    d, f = w1.shape[1], w1.shape[2]
    assert d == 2 * half
    wspec = pl.BlockSpec((1, d, f_chunk), lambda it, e, cw, bi, bo, co, fl: (e[it], 0, cw[it]))
    return pl.pallas_call(
        _moe_up_kernel,
        grid_spec=pltpu.PrefetchScalarGridSpec(
            num_scalar_prefetch=6,
            grid=(items[0].shape[0],),
            in_specs=[pl.BlockSpec((MOE_ROWS, half), lambda it, e, cw, bi, bo, co, fl: (bi[it], 0)), wspec, wspec],
            out_specs=pl.BlockSpec((MOE_ROWS, f_chunk), lambda it, e, cw, bi, bo, co, fl: (bo[it], co[it])),
            scratch_shapes=[pltpu.VMEM((d, f_chunk), BF16), pltpu.VMEM((d, f_chunk), BF16)]),
        out_shape=jax.ShapeDtypeStruct((p, f), BF16),
        compiler_params=_cparams(("arbitrary",), 56),
        name="moe_up",
    )(*items, xs, w1, w3)


def _moe_down(hmid, w2, items, d_chunk):
    p, f = hmid.shape
    d = w2.shape[2]
    return pl.pallas_call(
        _moe_down_kernel,
        grid_spec=pltpu.PrefetchScalarGridSpec(
            num_scalar_prefetch=6,
            grid=(items[0].shape[0],),
            in_specs=[pl.BlockSpec((MOE_ROWS, f), lambda it, e, cw, bi, bo, co, fl: (bi[it], 0)),
                      pl.BlockSpec((1, f, d_chunk), lambda it, e, cw, bi, bo, co, fl: (e[it], 0, cw[it]))],
            out_specs=pl.BlockSpec((MOE_ROWS, d_chunk), lambda it, e, cw, bi, bo, co, fl: (bo[it], co[it])),
            scratch_shapes=[pltpu.VMEM((f, d_chunk), BF16)]),
        out_shape=jax.ShapeDtypeStruct((p, d), F32),
        compiler_params=_cparams(("arbitrary",), 48),
        name="moe_down",
    )(*items, hmid, w2)


def _hier_moe_ln(h1, hp, logits, w1, w3, w2, ln_g, ln_b, alpha):
    n, d = h1.shape
    e_lanes, gate_lanes, lane_counts = _route(logits)
    a = 2 * n
    e_flat = e_lanes[:, 0:2].reshape(a)
    rank = e_lanes[:, 2:4].reshape(a)
    counts = lane_counts[0, N_GROUPS:N_GROUPS + N_EXPERTS].astype(I32)
    blocks_per_expert = (counts + MOE_ROWS - 1) // MOE_ROWS
    block_start = jnp.cumsum(blocks_per_expert) - blocks_per_expert
    onehot = e_flat[:, None] == jnp.arange(N_EXPERTS, dtype=I32)[None, :]
    pos = jnp.sum(jnp.where(onehot, block_start[None, :], 0), axis=1) * MOE_ROWS + rank
    n_blocks = (a + MOE_ROWS - 1) // MOE_ROWS + N_EXPERTS
    buf_tok = jnp.zeros((n_blocks * MOE_ROWS,), I32).at[pos].set(jnp.arange(a, dtype=I32) // 2)

    f = w1.shape[2]
    f_chunk = min(512, f)
    d_chunk = min(2048, d)
    xs = _gather_tokens(hp, buf_tok, jnp.sum(blocks_per_expert).astype(I32))
    hmid = _moe_up(xs, w1, w3, _work_items(blocks_per_expert, block_start, f // f_chunk, n_blocks), f_chunk)
    y = _moe_down(hmid, w2, _work_items(blocks_per_expert, block_start, d // d_chunk, n_blocks), d_chunk)
    return _combine_ln(y, pos.reshape(n, 2), h1, gate_lanes, ln_g, ln_b, alpha)


def kernel(x, w_in, conv_dw_w, conv_dw_b, conv_ln_g, conv_ln_b, w_out, ln1_g, ln1_b, w_router_group, b_router_group, w_router_expert, b_router_expert, w_expert_gate, w_expert_up, w_expert_down, ln2_g, ln2_b):
    b, s, d = x.shape
    depth = w_in.shape[0]
    assert depth == 1
    alpha = float((2.0 * depth) ** 0.25)
    n = b * s
    dc = conv_dw_w.shape[2]
    d_attn = N_HEADS * HEAD_DIM
    d_kv = N_KV_HEADS * HEAD_DIM
    d_qi = IDX_HEADS * IDX_DIM
    n_small = IDX_DIM + IDX_HEADS
    assert w_in.shape[2] == 2 * dc + d_attn + 2 * d_kv + d_qi + n_small

    x2 = x.reshape(n, d)
    w = w_in[0].astype(BF16)
    o_qkvi = 2 * dc
    o_small = o_qkvi + d_attn + 2 * d_kv + d_qi
    w_small = jnp.pad(w[:, o_small:], ((0, 0), (0, LANES - n_small)))
    qkvi_scale = jnp.concatenate([jnp.full((d_attn,), HEAD_DIM ** -0.5 * LOG2E, F32), jnp.ones((2 * d_kv,), F32),
                                  jnp.full((d_qi,), IDX_DIM ** -0.5, F32)])[None, :]

    xb, small = _proj_idx_and_cast(x2, w_small)
    glu = _proj_glu(xb, w, dc)
    qkvi = _proj_scale(xb, w, o_qkvi, o_small - o_qkvi, qkvi_scale, BF16, "proj_qkvi")

    conv_out = _conformer_conv(glu.reshape(b, s, dc), conv_dw_w[0], conv_dw_b, conv_ln_g, conv_ln_b)
    attn_out = _dsa_attention(qkvi.reshape(b, s, -1), small.reshape(b, s, LANES), b, s)

    n_route = N_GROUPS + N_EXPERTS
    w_router = jnp.pad(jnp.concatenate([w_router_group[0], w_router_expert[0]], axis=1),
                       ((0, 0), (0, LANES - n_route))).astype(BF16)
    b_router = jnp.pad(jnp.concatenate([b_router_group[0], b_router_expert[0]]), (0, LANES - n_route))[None, :]
    h1, hp, logits = _out_proj_ln(conv_out.reshape(n, dc), attn_out.reshape(n, d_attn),
                                  w_out[0, :dc].astype(BF16), w_out[0, dc:].astype(BF16),
                                  x2, ln1_g, ln1_b, w_router, b_router, alpha)

    out = _hier_moe_ln(h1, hp, logits, w_expert_gate[0], w_expert_up[0], w_expert_down[0], ln2_g, ln2_b, alpha)
    return out.reshape(b, s, d)
```

```python
import functools
import math

import jax
import jax.numpy as jnp
from jax import lax
from jax.experimental import pallas as pl
from jax.experimental.pallas import tpu as pltpu

F32 = jnp.float32
BF16 = jnp.bfloat16
I32 = jnp.int32

CONV_WIDTH = 31
N_HEADS = 16
HEAD_DIM = 128
N_KV_HEADS = 4
IDX_HEADS = 16
IDX_DIM = 64
TOPK_MAX = 256
N_GROUPS = 4
EXPERTS_PER_GROUP = 8
N_EXPERTS = N_GROUPS * EXPERTS_PER_GROUP
LN_EPS = 1e-5

LANES = 128
SUBLANES = 8
BF16_ROWS = 16
MIB = 1024 * 1024

Q_TILE = 128
KEY_CHUNK = 512
CONV_HALO = 32
MOE_ROWS = 256
COMBINE_ROWS = 128
LN_GROUPS = 4
INT_MIN = -(2 ** 31)
INT_MAX = 2 ** 31 - 1
LOG2E = math.log2(math.e)
NEG_BIG = -1e30
M_INIT = -1e20


def _cparams(semantics, vmem_mib):
    return pltpu.CompilerParams(dimension_semantics=semantics, vmem_limit_bytes=vmem_mib * MIB)


def _layer_norm_rows(z, g, b):
    mu = jnp.mean(z, axis=-1, keepdims=True)
    zc = z - mu
    var = jnp.mean(zc * zc, axis=-1, keepdims=True)
    return zc * lax.rsqrt(var + LN_EPS) * g + b


def _mm_glu_kernel(x_ref, wa_ref, wg_ref, o_ref):
    x = x_ref[...]
    a = jnp.dot(x, wa_ref[...], preferred_element_type=F32)
    g = jnp.dot(x, wg_ref[...], preferred_element_type=F32)
    o_ref[...] = (a * jax.nn.sigmoid(g)).astype(o_ref.dtype)


def _mm_scale_kernel(x_ref, w_ref, s_ref, o_ref):
    acc = jnp.dot(x_ref[...], w_ref[...], preferred_element_type=F32)
    o_ref[...] = (acc * s_ref[...]).astype(o_ref.dtype)


def _mm_tiles(n, k, cols):
    tm = min(1024, n)
    tn = min(512, cols)
    assert n % tm == 0 and cols % tn == 0
    return tm, tn


def _proj_glu(xb, w, cols):
    n, k = xb.shape
    tm, tn = _mm_tiles(n, k, cols)
    gate_block = cols // tn
    return pl.pallas_call(
        _mm_glu_kernel,
        grid=(n // tm, cols // tn),
        in_specs=[pl.BlockSpec((tm, k), lambda i, j: (i, 0)),
                  pl.BlockSpec((k, tn), lambda i, j: (0, j)),
                  pl.BlockSpec((k, tn), lambda i, j: (0, j + gate_block))],
        out_specs=pl.BlockSpec((tm, tn), lambda i, j: (i, j)),
        out_shape=jax.ShapeDtypeStruct((n, cols), F32),
        compiler_params=_cparams(("parallel", "arbitrary"), 48),
        name="proj_glu",
    )(xb, w, w)


def _proj_scale(xb, w, first_col, cols, scale, out_dtype, name):
    n, k = xb.shape
    tm, tn = _mm_tiles(n, k, cols)
    assert first_col % tn == 0
    first_block = first_col // tn
    return pl.pallas_call(
        _mm_scale_kernel,
        grid=(n // tm, cols // tn),
        in_specs=[pl.BlockSpec((tm, k), lambda i, j: (i, 0)),
                  pl.BlockSpec((k, tn), lambda i, j: (0, j + first_block)),
                  pl.BlockSpec((1, tn), lambda i, j: (0, j))],
        out_specs=pl.BlockSpec((tm, tn), lambda i, j: (i, j)),
        out_shape=jax.ShapeDtypeStruct((n, cols), out_dtype),
        compiler_params=_cparams(("parallel", "arbitrary"), 48),
        name=name,
    )(xb, w, scale)


def _idx_cast_kernel(x_ref, w_ref, xb_ref, o_ref):
    xb = x_ref[...].astype(BF16)
    xb_ref[...] = xb
    o_ref[...] = jnp.dot(xb, w_ref[...], preferred_element_type=F32)


def _proj_idx_and_cast(x2, w_small):
    n, k = x2.shape
    tm = min(256, n)
    return pl.pallas_call(
        _idx_cast_kernel,
        grid=(n // tm,),
        in_specs=[pl.BlockSpec((tm, k), lambda i: (i, 0)),
                  pl.BlockSpec((k, LANES), lambda i: (0, 0))],
        out_specs=[pl.BlockSpec((tm, k), lambda i: (i, 0)),
                   pl.BlockSpec((tm, LANES), lambda i: (i, 0))],
        out_shape=[jax.ShapeDtypeStruct((n, k), BF16), jax.ShapeDtypeStruct((n, LANES), F32)],
        compiler_params=_cparams(("parallel",), 32),
        name="proj_idx",
    )(x2, w_small)


def _conv_kernel(cur_ref, prev_ref, w_ref, b_ref, g_ref, beta_ref, o_ref, sh_ref, y_ref, *, rows, lane_chunk):
    t = pl.program_id(1)
    dc = cur_ref.shape[2]
    row_sub = 32
    sh_ref[0, 0:CONV_HALO, :] = jnp.where(t > 0, prev_ref[0], 0.0)
    sh_ref[0, CONV_HALO:CONV_HALO + rows, :] = cur_ref[0]
    first = CONV_HALO - (CONV_WIDTH - 1)
    shifted_rows = rows + CONV_HALO - SUBLANES
    for lc in range(dc // lane_chunk):
        cols = slice(lc * lane_chunk, (lc + 1) * lane_chunk)
        for r in range(1, SUBLANES):
            sh_ref[r, 0:shifted_rows, cols] = sh_ref[0, r:r + shifted_rows, cols]

    for r0 in range(0, rows, row_sub):
        for lc in range(dc // lane_chunk):
            cols = slice(lc * lane_chunk, (lc + 1) * lane_chunk)
            acc = jnp.zeros((row_sub, lane_chunk), F32)
            for j in range(CONV_WIDTH):
                shift, base = (first + j) % SUBLANES, (first + j) // SUBLANES * SUBLANES
                acc = acc + sh_ref[shift, r0 + base:r0 + base + row_sub, cols] * w_ref[j:j + 1, cols]
            y_ref[r0:r0 + row_sub, cols] = acc + b_ref[:, cols]

    def norm_rows(i, carry):
        r0 = pl.multiple_of(i * BF16_ROWS, BF16_ROWS)
        yn = _layer_norm_rows(y_ref[pl.ds(r0, BF16_ROWS), :], g_ref[...], beta_ref[...])
        o_ref[0, pl.ds(r0, BF16_ROWS), :] = (yn * jax.nn.sigmoid(yn)).astype(o_ref.dtype)
        return carry

    lax.fori_loop(0, rows // BF16_ROWS, norm_rows, 0, unroll=4)


def _conformer_conv(glu, w_dw, b_dw, g_ln, b_ln):
    b, s, dc = glu.shape
    rows = min(128, s)
    halo_blocks = rows // CONV_HALO
    lane_chunk = min(512, dc)
    kern = functools.partial(_conv_kernel, rows=rows, lane_chunk=lane_chunk)
    return pl.pallas_call(
        kern,
        grid=(b, s // rows),
        in_specs=[pl.BlockSpec((1, rows, dc), lambda bi, t: (bi, t, 0)),
                  pl.BlockSpec((1, CONV_HALO, dc), lambda bi, t: (bi, jnp.maximum(t * halo_blocks - 1, 0), 0)),
                  pl.BlockSpec((CONV_WIDTH, dc), lambda bi, t: (0, 0)),
                  pl.BlockSpec((1, dc), lambda bi, t: (0, 0)),
                  pl.BlockSpec((1, dc), lambda bi, t: (0, 0)),
                  pl.BlockSpec((1, dc), lambda bi, t: (0, 0))],
        out_specs=pl.BlockSpec((1, rows, dc), lambda bi, t: (bi, t, 0)),
        out_shape=jax.ShapeDtypeStruct((b, s, dc), BF16),
        scratch_shapes=[pltpu.VMEM((SUBLANES, CONV_HALO + rows, dc), F32), pltpu.VMEM((rows, dc), F32)],
        compiler_params=_cparams(("parallel", "arbitrary"), 32),
        name="conformer_conv",
    )(glu, glu, w_dw, b_dw, g_ln, b_ln)


def _dsa_kernel(q_ref, qi_ref, k_ref, v_ref, kidx_ref, w_ref, o_ref,
                ke_ref, ko_ref, vt_ref, key_ref, cut_ref, nd_ref, qs_ref, qis_ref, m_ref, l_ref, acc_ref,
                *, seq, chunk, topk):
    tb = pl.program_id(1)
    t0 = tb * Q_TILE
    n_chunks = (t0 + Q_TILE + chunk - 1) // chunk
    rep = N_HEADS // N_KV_HEADS
    nt = (((1,), (1,)), ((), ()))

    @pl.when(tb == 0)
    def _():
        def build(c, carry):
            r0 = pl.multiple_of(c * chunk, chunk)
            kx = kidx_ref[0, pl.ds(r0, chunk), :]
            lane = lax.broadcasted_iota(I32, kx.shape, 1)
            ke_ref[pl.ds(r0, chunk), :] = jnp.where(lane < IDX_DIM, kx, 0.0).astype(BF16)
            ko_ref[pl.ds(r0, chunk), :] = jnp.where(lane >= IDX_DIM, pltpu.roll(kx, IDX_DIM, 1), 0.0).astype(BF16)
            for g in range(N_KV_HEADS):
                vg = v_ref[0, pl.ds(r0, chunk), g * HEAD_DIM:(g + 1) * HEAD_DIM]
                vt_ref[c, g * HEAD_DIM:(g + 1) * HEAD_DIM, :] = vg.astype(F32).T.astype(BF16)
            return carry
        lax.fori_loop(0, seq // chunk, build, 0)

    for g in range(N_KV_HEADS):
        for r in range(rep):
            h = g * rep + r
            qs_ref[g, r * Q_TILE:(r + 1) * Q_TILE, :] = q_ref[0, :, h * HEAD_DIM:(h + 1) * HEAD_DIM]
    for jj in range(IDX_HEADS // 4):
        for half in range(2):
            pair = 2 * jj + half
            qis_ref[jj, half * Q_TILE:(half + 1) * Q_TILE, :] = qi_ref[0, :, pair * LANES:(pair + 1) * LANES]
    w_t = w_ref[0].T * (IDX_HEADS ** -0.5)

    key_row = lax.broadcasted_iota(I32, (chunk, Q_TILE), 0)
    q_pos = t0 + lax.broadcasted_iota(I32, (chunk, Q_TILE), 1)

    def score_chunk(c, carry):
        r0 = pl.multiple_of(c * chunk, chunk)
        ke = ke_ref[pl.ds(r0, chunk), :]
        ko = ko_ref[pl.ds(r0, chunk), :]
        acc = jnp.zeros((chunk, Q_TILE), F32)
        for jj in range(IDX_HEADS // 4):
            rhs = qis_ref[jj]
            de = lax.dot_general(ke, rhs, nt, preferred_element_type=F32)
            do = lax.dot_general(ko, rhs, nt, preferred_element_type=F32)
            for half in range(2):
                h_even = 2 * (2 * jj + half)
                cols = slice(half * Q_TILE, (half + 1) * Q_TILE)
                acc = acc + w_t[IDX_DIM + h_even:IDX_DIM + h_even + 1, :] * jnp.maximum(de[:, cols], 0.0)
                acc = acc + w_t[IDX_DIM + h_even + 1:IDX_DIM + h_even + 2, :] * jnp.maximum(do[:, cols], 0.0)
        bits = lax.bitcast_convert_type(acc, I32)
        key = bits ^ ((bits >> 31) & 0x7FFFFFFF)
        key_ref[c] = jnp.where(r0 + key_row <= q_pos, key, INT_MIN)
        return carry

    lax.fori_loop(0, n_chunks, score_chunk, 0)

    def count_ge(cand):
        def count_chunk(c, cnts):
            cnts = list(cnts)
            for r in range(chunk // SUBLANES):
                slab = key_ref[c, r * SUBLANES:(r + 1) * SUBLANES, :]
                cnts[r % len(cnts)] = cnts[r % len(cnts)] + jnp.where(slab >= cand, 1.0, 0.0)
            return tuple(cnts)

        zero = jnp.zeros((SUBLANES, Q_TILE), F32)
        cnts = lax.fori_loop(0, n_chunks, count_chunk, (zero, zero, zero, zero))
        return jnp.sum((cnts[0] + cnts[1]) + (cnts[2] + cnts[3]), axis=0, keepdims=True)

    def bit_step(i, t_u):
        cand_u = t_u | (jnp.int32(1) << (31 - i))
        return jnp.where(count_ge(cand_u ^ INT_MIN) >= float(topk), cand_u, t_u)

    t_u = lax.fori_loop(0, 32, bit_step, jnp.zeros((1, Q_TILE), I32))
    thr = jnp.maximum(t_u ^ INT_MIN, INT_MIN + 1)

    n_ge = count_ge(thr)
    cut_ref[...] = jnp.full(cut_ref.shape, INT_MAX, I32)

    @pl.when(jnp.max(n_ge) > topk)
    def _():
        def count_where(pred):
            def count_chunk(c, cnt):
                r0 = pl.multiple_of(c * chunk, chunk)
                hit = jnp.where(pred(key_ref[c], r0 + key_row), 1, 0)
                return cnt + jnp.sum(hit, axis=0, keepdims=True)
            return lax.fori_loop(0, n_chunks, count_chunk, jnp.zeros((1, Q_TILE), I32))

        keep = topk - count_where(lambda kc, pos: kc > thr)
        pos_bits = seq.bit_length()

        def bit_step(i, cut):
            cand = cut | (jnp.int32(1) << (pos_bits - 1 - i))
            taken = count_where(lambda kc, pos: jnp.where(kc == thr, pos, INT_MAX) < cand)
            return jnp.where(taken <= keep, cand, cut)
        cut_ref[...] = lax.fori_loop(0, pos_bits, bit_step, jnp.zeros((1, Q_TILE), I32))

    cut = cut_ref[...]

    m_ref[...] = jnp.full(m_ref.shape, M_INIT, F32)
    l_ref[...] = jnp.zeros(l_ref.shape, F32)
    acc_ref[...] = jnp.zeros(acc_ref.shape, F32)

    def attend_chunk(c, carry):
        r0 = pl.multiple_of(c * chunk, chunk)
        pos = r0 + key_row
        kc = key_ref[c]
        rank_pos = jnp.where(kc > thr, -1, jnp.where(kc == thr, pos, INT_MAX))
        nd_ref[...] = jnp.where(rank_pos < cut, (pos - q_pos).astype(F32), NEG_BIG)
        def logits(g):
            kg = k_ref[0, pl.ds(r0, chunk), g * HEAD_DIM:(g + 1) * HEAD_DIM]
            return lax.dot_general(kg, qs_ref[g], nt, preferred_element_type=F32)

        s_next = logits(0)
        for g in range(N_KV_HEADS):
            s_all = s_next
            if g + 1 < N_KV_HEADS:
                s_next = logits(g + 1)
            probs = []
            alphas = []
            for r in range(rep):
                h = g * rep + r
                slope = float(2.0 ** (-8.0 * (h + 1) / N_HEADS)) * LOG2E
                sr = s_all[:, r * Q_TILE:(r + 1) * Q_TILE] + slope * nd_ref[...]
                m_old = m_ref[h]
                m_new = jnp.maximum(m_old, jnp.max(sr, axis=0, keepdims=True))
                alpha = jnp.exp2(m_old - m_new)
                p = jnp.exp2(sr - m_new)
                l_ref[h] = alpha * l_ref[h] + jnp.sum(p, axis=0, keepdims=True)
                m_ref[h] = m_new
                probs.append(p.astype(BF16))
                alphas.append(alpha)
            vt = vt_ref[c, g * HEAD_DIM:(g + 1) * HEAD_DIM, :]
            pv = jnp.dot(vt, jnp.concatenate(probs, axis=1), preferred_element_type=F32)
            acc_ref[g] = jnp.concatenate(alphas, axis=1) * acc_ref[g] + pv
        return carry

    lax.fori_loop(0, n_chunks, attend_chunk, 0)

    for g in range(N_KV_HEADS):
        for r in range(rep):
            h = g * rep + r
            o_t = acc_ref[g, :, r * Q_TILE:(r + 1) * Q_TILE] * (1.0 / l_ref[h])
            o_ref[0, :, h * HEAD_DIM:(h + 1) * HEAD_DIM] = o_t.T.astype(o_ref.dtype)


def _dsa_attention(qkvi, small, b, s):
    d_attn = N_HEADS * HEAD_DIM
    d_kv = N_KV_HEADS * HEAD_DIM
    d_qi = IDX_HEADS * IDX_DIM
    rep = N_HEADS // N_KV_HEADS
    chunk = min(KEY_CHUNK, s)
    topk = min(TOPK_MAX, s // 4)
    assert s % chunk == 0 and chunk % Q_TILE == 0 and d_attn % d_qi == 0 and d_attn % d_kv == 0
    assert Q_TILE == LANES and HEAD_DIM == LANES and 2 * IDX_DIM == LANES
    kern = functools.partial(_dsa_kernel, seq=s, chunk=chunk, topk=topk)
    return pl.pallas_call(
        kern,
        grid=(b, s // Q_TILE),
        in_specs=[pl.BlockSpec((1, Q_TILE, d_attn), lambda bi, t: (bi, t, 0)),
                  pl.BlockSpec((1, Q_TILE, d_qi), lambda bi, t: (bi, t, (d_attn + 2 * d_kv) // d_qi)),
                  pl.BlockSpec((1, s, d_kv), lambda bi, t: (bi, 0, d_attn // d_kv)),
                  pl.BlockSpec((1, s, d_kv), lambda bi, t: (bi, 0, d_attn // d_kv + 1)),
                  pl.BlockSpec((1, s, LANES), lambda bi, t: (bi, 0, 0)),
                  pl.BlockSpec((1, Q_TILE, LANES), lambda bi, t: (bi, t, 0))],
        out_specs=pl.BlockSpec((1, Q_TILE, d_attn), lambda bi, t: (bi, t, 0)),
        out_shape=jax.ShapeDtypeStruct((b, s, d_attn), BF16),
        scratch_shapes=[pltpu.VMEM((s, LANES), BF16),
                        pltpu.VMEM((s, LANES), BF16),
                        pltpu.VMEM((s // chunk, d_kv, chunk), BF16),
                        pltpu.VMEM((s // chunk, chunk, Q_TILE), I32),
                        pltpu.VMEM((1, Q_TILE), I32),
                        pltpu.VMEM((chunk, Q_TILE), F32),
                        pltpu.VMEM((N_KV_HEADS, rep * Q_TILE, HEAD_DIM), BF16),
                        pltpu.VMEM((IDX_HEADS // 4, 2 * Q_TILE, LANES), BF16),
                        pltpu.VMEM((N_HEADS, 1, Q_TILE), F32),
                        pltpu.VMEM((N_HEADS, 1, Q_TILE), F32),
                        pltpu.VMEM((N_KV_HEADS, HEAD_DIM, rep * Q_TILE), F32)],
        compiler_params=_cparams(("parallel", "arbitrary"), 48),
        name="dsa_attention",
    )(qkvi, qkvi, qkvi, qkvi, small, small)


def _pack_bf16_halves(hb):
    half = hb.shape[1] // 2
    bits = lax.bitcast_convert_type(hb.astype(F32), I32)
    return lax.shift_right_logical(bits[:, :half], 16) | (bits[:, half:] & -65536)


def _unpack_bf16_halves(words):
    lo = lax.bitcast_convert_type(words << 16, F32).astype(BF16)
    hi = lax.bitcast_convert_type(words & -65536, F32).astype(BF16)
    return lo, hi


def _outproj_kernel(conv_ref, attn_ref, wc_ref, wa_ref, x_ref, g_ref, b_ref, wr_ref, br_ref,
                    h_ref, hp_ref, lg_ref, hb_ref, *, alpha):
    j = pl.program_id(1)
    tm, d = h_ref.shape
    tn = x_ref.shape[1]
    n_tiles = d // tn
    mix = jnp.dot(conv_ref[...], wc_ref[...], preferred_element_type=F32)
    mix = mix + jnp.dot(attn_ref[...], wa_ref[...], preferred_element_type=F32)
    z = alpha * x_ref[...] + mix
    for jj in range(n_tiles):
        @pl.when(j == jj)
        def _(jj=jj):
            h_ref[:, jj * tn:(jj + 1) * tn] = z

    @pl.when(j == n_tiles - 1)
    def _():
        def rows(i, carry):
            starts = [pl.multiple_of((LN_GROUPS * i + k) * BF16_ROWS, BF16_ROWS) for k in range(LN_GROUPS)]
            zs = [h_ref[pl.ds(r0, BF16_ROWS), :] for r0 in starts]
            hs = [_layer_norm_rows(z, g_ref[...], b_ref[...]) for z in zs]
            for r0, h in zip(starts, hs):
                h_ref[pl.ds(r0, BF16_ROWS), :] = h
                hb_ref[pl.ds(r0, BF16_ROWS), :] = h.astype(BF16)
            return carry
        lax.fori_loop(0, tm // (LN_GROUPS * BF16_ROWS), rows, 0)

        def pack_rows(i, carry):
            r0 = pl.multiple_of(i * BF16_ROWS, BF16_ROWS)
            hp_ref[pl.ds(r0, BF16_ROWS), :] = _pack_bf16_halves(hb_ref[pl.ds(r0, BF16_ROWS), :])
            return carry
        lax.fori_loop(0, tm // BF16_ROWS, pack_rows, 0, unroll=2)
        lg_ref[...] = jnp.dot(hb_ref[...], wr_ref[...], preferred_element_type=F32) + br_ref[...]


def _out_proj_ln(conv_out, attn_out, w_conv, w_attn, x2, ln_g, ln_b, w_router, b_router, alpha):
    n, d = x2.shape
    dc = conv_out.shape[1]
    da = attn_out.shape[1]
    tm = min(512, n)
    tn = min(512, d)
    assert n % tm == 0 and d % tn == 0
    kern = functools.partial(_outproj_kernel, alpha=alpha)
    return pl.pallas_call(
        kern,
        grid=(n // tm, d // tn),
        in_specs=[pl.BlockSpec((tm, dc), lambda i, j: (i, 0)),
                  pl.BlockSpec((tm, da), lambda i, j: (i, 0)),
                  pl.BlockSpec((dc, tn), lambda i, j: (0, j)),
                  pl.BlockSpec((da, tn), lambda i, j: (0, j)),
                  pl.BlockSpec((tm, tn), lambda i, j: (i, j)),
                  pl.BlockSpec((1, d), lambda i, j: (0, 0)),
                  pl.BlockSpec((1, d), lambda i, j: (0, 0)),
                  pl.BlockSpec((d, LANES), lambda i, j: (0, 0)),
                  pl.BlockSpec((1, LANES), lambda i, j: (0, 0))],
        out_specs=[pl.BlockSpec((tm, d), lambda i, j: (i, 0)),
                   pl.BlockSpec((tm, d // 2), lambda i, j: (i, 0)),
                   pl.BlockSpec((tm, LANES), lambda i, j: (i, 0))],
        out_shape=[jax.ShapeDtypeStruct((n, d), F32), jax.ShapeDtypeStruct((n, d // 2), I32),
                   jax.ShapeDtypeStruct((n, LANES), F32)],
        scratch_shapes=[pltpu.VMEM((tm, d), BF16)],
        compiler_params=_cparams(("parallel", "arbitrary"), 56),
        name="out_proj_ln1",
    )(conv_out, attn_out, w_conv, w_attn, x2, ln_g, ln_b, w_router, b_router)


def _first_argmax(vals, lane, valid):
    masked = jnp.where(valid, vals, -jnp.inf)
    mx = jnp.max(masked, axis=1, keepdims=True)
    idx = jnp.min(jnp.where(valid & (masked == mx), lane, LANES), axis=1, keepdims=True)
    return mx, idx


def _router_kernel(lg_ref, tri_ref, e_ref, gate_ref, cnt_ref):
    @pl.when(pl.program_id(0) == 0)
    def _():
        cnt_ref[...] = jnp.zeros(cnt_ref.shape, F32)

    lg = lg_ref[...]
    lane = lax.broadcasted_iota(I32, lg.shape, 1)
    is_group = lane < N_GROUPS
    g_max, g_sel = _first_argmax(lg, lane, is_group)
    g_den = jnp.sum(jnp.where(is_group, jnp.exp(lg - g_max), 0.0), axis=1, keepdims=True)
    p_group = 1.0 / g_den
    lo = N_GROUPS + g_sel * EXPERTS_PER_GROUP
    in_group = (lane >= lo) & (lane < lo + EXPERTS_PER_GROUP)
    e_max, _ = _first_argmax(lg, lane, in_group)
    e_exp = jnp.where(in_group, jnp.exp(lg - e_max), 0.0)
    prob = e_exp / jnp.sum(e_exp, axis=1, keepdims=True)
    p1, i1 = _first_argmax(prob, lane, in_group)
    p2, i2 = _first_argmax(prob, lane, in_group & (lane != i1))
    norm = p_group / (p1 + p2)
    gate_ref[...] = jnp.where(lane == 0, p1 * norm, jnp.where(lane == 1, p2 * norm, 0.0))
    chosen = jnp.where((lane == i1) | (lane == i2), 1.0, 0.0)
    before = jnp.dot(tri_ref[...], chosen.astype(BF16), preferred_element_type=F32) + cnt_ref[...]
    rank1 = jnp.sum(jnp.where(lane == i1, before, 0.0), axis=1, keepdims=True).astype(I32)
    rank2 = jnp.sum(jnp.where(lane == i2, before, 0.0), axis=1, keepdims=True).astype(I32)
    cnt_ref[...] += jnp.sum(chosen, axis=0, keepdims=True)
    e_ref[...] = jnp.where(lane == 0, i1 - N_GROUPS,
                           jnp.where(lane == 1, i2 - N_GROUPS,
                                     jnp.where(lane == 2, rank1, jnp.where(lane == 3, rank2, 0))))


def _route(logits):
    n = logits.shape[0]
    tm = min(512, n)
    tri = jnp.tril(jnp.ones((tm, tm), BF16), -1)
    spec = pl.BlockSpec((tm, LANES), lambda i: (i, 0))
    return pl.pallas_call(
        _router_kernel,
        grid=(n // tm,),
        in_specs=[spec, pl.BlockSpec((tm, tm), lambda i: (0, 0))],
        out_specs=[spec, spec, pl.BlockSpec((1, LANES), lambda i: (0, 0))],
        out_shape=[jax.ShapeDtypeStruct((n, LANES), I32), jax.ShapeDtypeStruct((n, LANES), F32),
                   jax.ShapeDtypeStruct((1, LANES), F32)],
        compiler_params=_cparams(("arbitrary",), 32),
        name="router",
    )(logits, tri)


def _start_row_copies(src_hbm, idx_ref, idx_row, dst_ref, slot, row0, sem, count):
    group = 8

    def issue(i, carry):
        for k in range(group):
            r = i * group + k
            pltpu.make_async_copy(src_hbm.at[pl.ds(idx_ref[0, idx_row, r], 1), :],
                                  dst_ref.at[slot, pl.ds(row0 + r, 1), :], sem.at[slot]).start(priority=k % 2)
        return carry
    lax.fori_loop(0, count // group, issue, 0)


def _wait_slot(src_hbm, dst_ref, slot, sem):
    rows = dst_ref.shape[1]
    pltpu.make_async_copy(src_hbm.at[pl.ds(0, rows), :], dst_ref.at[slot], sem.at[slot]).wait()


def _gather_kernel(used_ref, tok_ref, tok_next_ref, hp_hbm, o_ref, buf_ref, sem):
    i = pl.program_id(0)
    slot = i % 2
    n_used = used_ref[0]
    rows = buf_ref.shape[1]

    @pl.when(i == 0)
    def _():
        _start_row_copies(hp_hbm, tok_ref, 0, buf_ref, 0, 0, sem, rows)

    @pl.when(i + 1 < n_used)
    def _():
        _start_row_copies(hp_hbm, tok_next_ref, 0, buf_ref, 1 - slot, 0, sem, rows)

    @pl.when(i < n_used)
    def _():
        _wait_slot(hp_hbm, buf_ref, slot, sem)
        o_ref[...] = buf_ref[slot]

    @pl.when(i >= n_used)
    def _():
        o_ref[...] = jnp.zeros(o_ref.shape, o_ref.dtype)


def _gather_tokens(hp, buf_tok, n_used):
    p = buf_tok.shape[0]
    half = hp.shape[1]
    nblk = p // MOE_ROWS
    tok_blocks = buf_tok.reshape(nblk, 1, MOE_ROWS)
    return pl.pallas_call(
        _gather_kernel,
        grid_spec=pltpu.PrefetchScalarGridSpec(
            num_scalar_prefetch=1,
            grid=(nblk,),
            in_specs=[pl.BlockSpec((1, 1, MOE_ROWS), lambda i, u: (i, 0, 0), memory_space=pltpu.SMEM),
                      pl.BlockSpec((1, 1, MOE_ROWS), lambda i, u: (jnp.minimum(i + 1, nblk - 1), 0, 0),
                                   memory_space=pltpu.SMEM),
                      pl.BlockSpec(memory_space=pl.ANY)],
            out_specs=pl.BlockSpec((MOE_ROWS, half), lambda i, u: (i, 0)),
            scratch_shapes=[pltpu.VMEM((2, MOE_ROWS, half), I32), pltpu.SemaphoreType.DMA((2,))]),
        out_shape=jax.ShapeDtypeStruct((p, half), I32),
        compiler_params=_cparams(("arbitrary",), 32),
        name="moe_gather",
    )(n_used.reshape(1), tok_blocks, tok_blocks, hp)


def _combine_kernel(pos_ref, pos_next_ref, y_hbm, h_ref, gate_ref, g_ref, b_ref, o_ref, buf_ref, sem, *, alpha):
    i = pl.program_id(0)
    n = pl.num_programs(0)
    slot = i % 2
    rows = h_ref.shape[0]

    def start(idx_ref, into):
        for k in range(2):
            _start_row_copies(y_hbm, idx_ref, k, buf_ref, into, k * rows, sem, rows)

    @pl.when(i == 0)
    def _():
        start(pos_ref, 0)

    @pl.when(i + 1 < n)
    def _():
        start(pos_next_ref, 1 - slot)

    _wait_slot(y_hbm, buf_ref, slot, sem)

    def norm_rows(j, carry):
        r0 = pl.multiple_of(j * SUBLANES, SUBLANES)
        gates = gate_ref[pl.ds(r0, SUBLANES), :]
        ffn = (buf_ref[slot, pl.ds(r0, SUBLANES), :] * gates[:, 0:1]
               + buf_ref[slot, pl.ds(rows + r0, SUBLANES), :] * gates[:, 1:2])
        z = alpha * h_ref[pl.ds(r0, SUBLANES), :] + ffn
        o_ref[pl.ds(r0, SUBLANES), :] = _layer_norm_rows(z, g_ref[...], b_ref[...])
        return carry
    lax.fori_loop(0, rows // SUBLANES, norm_rows, 0, unroll=8)


def _combine_ln(y, pos2, h1, gate_lanes, ln_g, ln_b, alpha):
    n, d = h1.shape
    tb = min(COMBINE_ROWS, n)
    nblk = n // tb
    pos_blocks = pos2.reshape(nblk, tb, 2).transpose(0, 2, 1)
    kern = functools.partial(_combine_kernel, alpha=alpha)
    return pl.pallas_call(
        kern,
        grid=(nblk,),
        in_specs=[pl.BlockSpec((1, 2, tb), lambda i: (i, 0, 0), memory_space=pltpu.SMEM),
                  pl.BlockSpec((1, 2, tb), lambda i: (jnp.minimum(i + 1, nblk - 1), 0, 0), memory_space=pltpu.SMEM),
                  pl.BlockSpec(memory_space=pl.ANY),
                  pl.BlockSpec((tb, d), lambda i: (i, 0)),
                  pl.BlockSpec((tb, LANES), lambda i: (i, 0)),
                  pl.BlockSpec((1, d), lambda i: (0, 0)),
                  pl.BlockSpec((1, d), lambda i: (0, 0))],
        out_specs=pl.BlockSpec((tb, d), lambda i: (i, 0)),
        out_shape=jax.ShapeDtypeStruct((n, d), F32),
        scratch_shapes=[pltpu.VMEM((2, 2 * tb, d), F32), pltpu.SemaphoreType.DMA((2,))],
        compiler_params=_cparams(("arbitrary",), 32),
        name="moe_combine_ln2",
    )(pos_blocks, pos_blocks, y, h1, gate_lanes, ln_g, ln_b)


def _cast_weight(dst_ref, src_ref):
    rows = src_ref.shape[1]
    step = min(256, rows)

    def body(i, carry):
        r0 = pl.multiple_of(i * step, step)
        dst_ref[pl.ds(r0, step), :] = src_ref[0, pl.ds(r0, step), :].astype(BF16)
        return carry
    lax.fori_loop(0, rows // step, body, 0)


ITEM_COMPUTE = 1
ITEM_NEW_WEIGHTS = 2
ITEM_ZERO_FILL = 4


def _moe_up_kernel(e_ref, cw_ref, bi_ref, bo_ref, co_ref, flag_ref, x_ref, w1_ref, w3_ref, o_ref, w1b_ref, w3b_ref):
    flag = flag_ref[pl.program_id(0)]

    @pl.when((flag & ITEM_NEW_WEIGHTS) != 0)
    def _():
        _cast_weight(w1b_ref, w1_ref)
        _cast_weight(w3b_ref, w3_ref)

    @pl.when((flag & ITEM_COMPUTE) != 0)
    def _():
        x_lo, x_hi = _unpack_bf16_halves(x_ref[...])
        half = x_lo.shape[1]

        def project(w_ref):
            return (jnp.dot(x_lo, w_ref[0:half, :], preferred_element_type=F32)
                    + jnp.dot(x_hi, w_ref[half:2 * half, :], preferred_element_type=F32))

        a = project(w1b_ref)
        b = project(w3b_ref)
        o_ref[...] = (a * jax.nn.sigmoid(a) * b).astype(o_ref.dtype)

    @pl.when((flag & ITEM_ZERO_FILL) != 0)
    def _():
        o_ref[...] = jnp.zeros(o_ref.shape, o_ref.dtype)


def _moe_down_kernel(e_ref, cw_ref, bi_ref, bo_ref, co_ref, flag_ref, h_ref, w2_ref, o_ref, w2b_ref):
    flag = flag_ref[pl.program_id(0)]

    @pl.when((flag & ITEM_NEW_WEIGHTS) != 0)
    def _():
        _cast_weight(w2b_ref, w2_ref)

    @pl.when((flag & ITEM_COMPUTE) != 0)
    def _():
        o_ref[...] = jnp.dot(h_ref[...], w2b_ref[...], preferred_element_type=F32)

    @pl.when((flag & ITEM_ZERO_FILL) != 0)
    def _():
        o_ref[...] = jnp.zeros(o_ref.shape, o_ref.dtype)


def _work_items(blocks_per_expert, block_start, n_chunks, n_blocks):
    n_items = n_chunks * n_blocks
    per_e = n_chunks * blocks_per_expert
    end = jnp.cumsum(per_e)
    start = end - per_e
    it = jnp.arange(n_items, dtype=I32)
    total = end[-1]
    used_blocks = total // n_chunks
    itc = jnp.minimum(it, total - 1)
    e = jnp.sum(itc[:, None] >= end[None, :], axis=1).astype(I32)
    local = itc - start[e]
    nb = jnp.maximum(blocks_per_expert[e], 1)
    c = local // nb
    r = local - c * nb
    b_in = block_start[e] + r
    active = it < total
    spare = jnp.maximum(it - total, 0)
    b_out = jnp.where(active, b_in, used_blocks + spare // n_chunks)
    c_out = jnp.where(active, c, spare % n_chunks)
    first = active & (r == 0)
    flag = jnp.where(active, ITEM_COMPUTE + ITEM_NEW_WEIGHTS * first.astype(I32), ITEM_ZERO_FILL)
    later_first = lax.cummin(jnp.where(first, it, n_items)[::-1])[::-1]
    next_start = jnp.concatenate([later_first[1:], jnp.full((1,), n_items, I32)])
    ahead = jnp.minimum(next_start, n_items - 1)
    use_next = active & ~first & (next_start < n_items)
    e_w = jnp.where(use_next, e[ahead], e)
    c_w = jnp.where(use_next, c[ahead], c)
    return tuple(v.astype(I32) for v in (e_w, c_w, b_in, b_out, c_out, flag))


def _moe_up(xs, w1, w3, items, f_chunk):
    p, half = xs.shape
    d, f = w1.shape[1], w1.shape[2]
    assert d == 2 * half
    wspec = pl.BlockSpec((1, d, f_chunk), lambda it, e, cw, bi, bo, co, fl: (e[it], 0, cw[it]))
    return pl.pallas_call(
        _moe_up_kernel,
        grid_spec=pltpu.PrefetchScalarGridSpec(
            num_scalar_prefetch=6,
            grid=(items[0].shape[0],),
            in_specs=[pl.BlockSpec((MOE_ROWS, half), lambda it, e, cw, bi, bo, co, fl: (bi[it], 0)), wspec, wspec],
            out_specs=pl.BlockSpec((MOE_ROWS, f_chunk), lambda it, e, cw, bi, bo, co, fl: (bo[it], co[it])),
            scratch_shapes=[pltpu.VMEM((d, f_chunk), BF16), pltpu.VMEM((d, f_chunk), BF16)]),
        out_shape=jax.ShapeDtypeStruct((p, f), BF16),
        compiler_params=_cparams(("arbitrary",), 56),
        name="moe_up",
    )(*items, xs, w1, w3)


def _moe_down(hmid, w2, items, d_chunk):
    p, f = hmid.shape
    d = w2.shape[2]
    return pl.pallas_call(
        _moe_down_kernel,
        grid_spec=pltpu.PrefetchScalarGridSpec(
            num_scalar_prefetch=6,
            grid=(items[0].shape[0],),
            in_specs=[pl.BlockSpec((MOE_ROWS, f), lambda it, e, cw, bi, bo, co, fl: (bi[it], 0)),
                      pl.BlockSpec((1, f, d_chunk), lambda it, e, cw, bi, bo, co, fl: (e[it], 0, cw[it]))],
            out_specs=pl.BlockSpec((MOE_ROWS, d_chunk), lambda it, e, cw, bi, bo, co, fl: (bo[it], co[it])),
            scratch_shapes=[pltpu.VMEM((f, d_chunk), BF16)]),
        out_shape=jax.ShapeDtypeStruct((p, d), F32),
        compiler_params=_cparams(("arbitrary",), 48),
        name="moe_down",
    )(*items, hmid, w2)


def _hier_moe_ln(h1, hp, logits, w1, w3, w2, ln_g, ln_b, alpha):
    n, d = h1.shape
    e_lanes, gate_lanes, lane_counts = _route(logits)
    a = 2 * n
    e_flat = e_lanes[:, 0:2].reshape(a)
    rank = e_lanes[:, 2:4].reshape(a)
    counts = lane_counts[0, N_GROUPS:N_GROUPS + N_EXPERTS].astype(I32)
    blocks_per_expert = (counts + MOE_ROWS - 1) // MOE_ROWS
    block_start = jnp.cumsum(blocks_per_expert) - blocks_per_expert
    onehot = e_flat[:, None] == jnp.arange(N_EXPERTS, dtype=I32)[None, :]
    pos = jnp.sum(jnp.where(onehot, block_start[None, :], 0), axis=1) * MOE_ROWS + rank
    n_blocks = (a + MOE_ROWS - 1) // MOE_ROWS + N_EXPERTS
    buf_tok = jnp.zeros((n_blocks * MOE_ROWS,), I32).at[pos].set(jnp.arange(a, dtype=I32) // 2)

    f = w1.shape[2]
    f_chunk = min(512, f)
    d_chunk = min(2048, d)
    xs = _gather_tokens(hp, buf_tok, jnp.sum(blocks_per_expert).astype(I32))
    hmid = _moe_up(xs, w1, w3, _work_items(blocks_per_expert, block_start, f // f_chunk, n_blocks), f_chunk)
    y = _moe_down(hmid, w2, _work_items(blocks_per_expert, block_start, d // d_chunk, n_blocks), d_chunk)
    return _combine_ln(y, pos.reshape(n, 2), h1, gate_lanes, ln_g, ln_b, alpha)


def kernel(x, w_in, conv_dw_w, conv_dw_b, conv_ln_g, conv_ln_b, w_out, ln1_g, ln1_b, w_router_group, b_router_group, w_router_expert, b_router_expert, w_expert_gate, w_expert_up, w_expert_down, ln2_g, ln2_b):
    b, s, d = x.shape
    depth = w_in.shape[0]
    assert depth == 1
    alpha = float((2.0 * depth) ** 0.25)
    n = b * s
    dc = conv_dw_w.shape[2]
    d_attn = N_HEADS * HEAD_DIM
    d_kv = N_KV_HEADS * HEAD_DIM
    d_qi = IDX_HEADS * IDX_DIM
    n_small = IDX_DIM + IDX_HEADS
    assert w_in.shape[2] == 2 * dc + d_attn + 2 * d_kv + d_qi + n_small

    x2 = x.reshape(n, d)
    w = w_in[0].astype(BF16)
    o_qkvi = 2 * dc
    o_small = o_qkvi + d_attn + 2 * d_kv + d_qi
    w_small = jnp.pad(w[:, o_small:], ((0, 0), (0, LANES - n_small)))
    qkvi_scale = jnp.concatenate([jnp.full((d_attn,), HEAD_DIM ** -0.5 * LOG2E, F32), jnp.ones((2 * d_kv,), F32),
                                  jnp.full((d_qi,), IDX_DIM ** -0.5, F32)])[None, :]

    xb, small = _proj_idx_and_cast(x2, w_small)
    glu = _proj_glu(xb, w, dc)
    qkvi = _proj_scale(xb, w, o_qkvi, o_small - o_qkvi, qkvi_scale, BF16, "proj_qkvi")

    conv_out = _conformer_conv(glu.reshape(b, s, dc), conv_dw_w[0], conv_dw_b, conv_ln_g, conv_ln_b)
    attn_out = _dsa_attention(qkvi.reshape(b, s, -1), small.reshape(b, s, LANES), b, s)

    n_route = N_GROUPS + N_EXPERTS
    w_router = jnp.pad(jnp.concatenate([w_router_group[0], w_router_expert[0]], axis=1),
                       ((0, 0), (0, LANES - n_route))).astype(BF16)
    b_router = jnp.pad(jnp.concatenate([b_router_group[0], b_router_expert[0]]), (0, LANES - n_route))[None, :]
    h1, hp, logits = _out_proj_ln(conv_out.reshape(n, dc), attn_out.reshape(n, d_attn),
                                  w_out[0, :dc].astype(BF16), w_out[0, dc:].astype(BF16),
                                  x2, ln1_g, ln1_b, w_router, b_router, alpha)

    out = _hier_moe_ln(h1, hp, logits, w_expert_gate[0], w_expert_up[0], w_expert_down[0], ln2_g, ln2_b, alpha)
    return out.reshape(b, s, d)
```

```python
import functools
import math

import jax
import jax.numpy as jnp
from jax import lax
from jax.experimental import pallas as pl
from jax.experimental.pallas import tpu as pltpu

F32 = jnp.float32
BF16 = jnp.bfloat16
I32 = jnp.int32

CONV_WIDTH = 31
N_HEADS = 16
HEAD_DIM = 128
N_KV_HEADS = 4
IDX_HEADS = 16
IDX_DIM = 64
TOPK_MAX = 256
N_GROUPS = 4
EXPERTS_PER_GROUP = 8
N_EXPERTS = N_GROUPS * EXPERTS_PER_GROUP
LN_EPS = 1e-5

LANES = 128
SUBLANES = 8
BF16_ROWS = 16
MIB = 1024 * 1024

Q_TILE = 128
KEY_CHUNK = 512
CONV_HALO = 32
MOE_ROWS = 256
COMBINE_ROWS = 128
LN_GROUPS = 4
INT_MIN = -(2 ** 31)
INT_MAX = 2 ** 31 - 1
LOG2E = math.log2(math.e)
NEG_BIG = -1e30
M_INIT = -1e20


def _cparams(semantics, vmem_mib):
    return pltpu.CompilerParams(dimension_semantics=semantics, vmem_limit_bytes=vmem_mib * MIB)


def _layer_norm_rows(z, g, b):
    mu = jnp.mean(z, axis=-1, keepdims=True)
    zc = z - mu
    var = jnp.mean(zc * zc, axis=-1, keepdims=True)
    return zc * lax.rsqrt(var + LN_EPS) * g + b


def _mm_glu_kernel(x_ref, wa_ref, wg_ref, o_ref):
    x = x_ref[...]
    a = jnp.dot(x, wa_ref[...], preferred_element_type=F32)
    g = jnp.dot(x, wg_ref[...], preferred_element_type=F32)
    o_ref[...] = (a * jax.nn.sigmoid(g)).astype(o_ref.dtype)


def _mm_scale_kernel(x_ref, w_ref, s_ref, o_ref):
    acc = jnp.dot(x_ref[...], w_ref[...], preferred_element_type=F32)
    o_ref[...] = (acc * s_ref[...]).astype(o_ref.dtype)


def _mm_tiles(n, k, cols):
    tm = min(1024, n)
    tn = min(512, cols)
    assert n % tm == 0 and cols % tn == 0
    return tm, tn


def _proj_glu(xb, w, cols):
    n, k = xb.shape
    tm, tn = _mm_tiles(n, k, cols)
    gate_block = cols // tn
    return pl.pallas_call(
        _mm_glu_kernel,
        grid=(n // tm, cols // tn),
        in_specs=[pl.BlockSpec((tm, k), lambda i, j: (i, 0)),
                  pl.BlockSpec((k, tn), lambda i, j: (0, j)),
                  pl.BlockSpec((k, tn), lambda i, j: (0, j + gate_block))],
        out_specs=pl.BlockSpec((tm, tn), lambda i, j: (i, j)),
        out_shape=jax.ShapeDtypeStruct((n, cols), F32),
        compiler_params=_cparams(("parallel", "arbitrary"), 48),
        name="proj_glu",
    )(xb, w, w)


def _proj_scale(xb, w, first_col, cols, scale, out_dtype, name):
    n, k = xb.shape
    tm, tn = _mm_tiles(n, k, cols)
    assert first_col % tn == 0
    first_block = first_col // tn
    return pl.pallas_call(
        _mm_scale_kernel,
        grid=(n // tm, cols // tn),
        in_specs=[pl.BlockSpec((tm, k), lambda i, j: (i, 0)),
                  pl.BlockSpec((k, tn), lambda i, j: (0, j + first_block)),
                  pl.BlockSpec((1, tn), lambda i, j: (0, j))],
        out_specs=pl.BlockSpec((tm, tn), lambda i, j: (i, j)),
        out_shape=jax.ShapeDtypeStruct((n, cols), out_dtype),
        compiler_params=_cparams(("parallel", "arbitrary"), 48),
        name=name,
    )(xb, w, scale)


def _idx_cast_kernel(x_ref, w_ref, xb_ref, o_ref):
    xb = x_ref[...].astype(BF16)
    xb_ref[...] = xb
    o_ref[...] = jnp.dot(xb, w_ref[...], preferred_element_type=F32)


def _proj_idx_and_cast(x2, w_small):
    n, k = x2.shape
    tm = min(256, n)
    return pl.pallas_call(
        _idx_cast_kernel,
        grid=(n // tm,),
        in_specs=[pl.BlockSpec((tm, k), lambda i: (i, 0)),
                  pl.BlockSpec((k, LANES), lambda i: (0, 0))],
        out_specs=[pl.BlockSpec((tm, k), lambda i: (i, 0)),
                   pl.BlockSpec((tm, LANES), lambda i: (i, 0))],
        out_shape=[jax.ShapeDtypeStruct((n, k), BF16), jax.ShapeDtypeStruct((n, LANES), F32)],
        compiler_params=_cparams(("parallel",), 32),
        name="proj_idx",
    )(x2, w_small)


def _conv_kernel(cur_ref, prev_ref, w_ref, b_ref, g_ref, beta_ref, o_ref, sh_ref, y_ref, *, rows, lane_chunk):
    t = pl.program_id(1)
    dc = cur_ref.shape[2]
    row_sub = 32
    sh_ref[0, 0:CONV_HALO, :] = jnp.where(t > 0, prev_ref[0], 0.0)
    sh_ref[0, CONV_HALO:CONV_HALO + rows, :] = cur_ref[0]
    first = CONV_HALO - (CONV_WIDTH - 1)
    shifted_rows = rows + CONV_HALO - SUBLANES
    for lc in range(dc // lane_chunk):
        cols = slice(lc * lane_chunk, (lc + 1) * lane_chunk)
        for r in range(1, SUBLANES):
            sh_ref[r, 0:shifted_rows, cols] = sh_ref[0, r:r + shifted_rows, cols]

    for r0 in range(0, rows, row_sub):
        for lc in range(dc // lane_chunk):
            cols = slice(lc * lane_chunk, (lc + 1) * lane_chunk)
            acc = jnp.zeros((row_sub, lane_chunk), F32)
            for j in range(CONV_WIDTH):
                shift, base = (first + j) % SUBLANES, (first + j) // SUBLANES * SUBLANES
                acc = acc + sh_ref[shift, r0 + base:r0 + base + row_sub, cols] * w_ref[j:j + 1, cols]
            y_ref[r0:r0 + row_sub, cols] = acc + b_ref[:, cols]

    def norm_rows(i, carry):
        r0 = pl.multiple_of(i * BF16_ROWS, BF16_ROWS)
        yn = _layer_norm_rows(y_ref[pl.ds(r0, BF16_ROWS), :], g_ref[...], beta_ref[...])
        o_ref[0, pl.ds(r0, BF16_ROWS), :] = (yn * jax.nn.sigmoid(yn)).astype(o_ref.dtype)
        return carry

    lax.fori_loop(0, rows // BF16_ROWS, norm_rows, 0, unroll=4)


def _conformer_conv(glu, w_dw, b_dw, g_ln, b_ln):
    b, s, dc = glu.shape
    rows = min(128, s)
    halo_blocks = rows // CONV_HALO
    lane_chunk = min(512, dc)
    kern = functools.partial(_conv_kernel, rows=rows, lane_chunk=lane_chunk)
    return pl.pallas_call(
        kern,
        grid=(b, s // rows),
        in_specs=[pl.BlockSpec((1, rows, dc), lambda bi, t: (bi, t, 0)),
                  pl.BlockSpec((1, CONV_HALO, dc), lambda bi, t: (bi, jnp.maximum(t * halo_blocks - 1, 0), 0)),
                  pl.BlockSpec((CONV_WIDTH, dc), lambda bi, t: (0, 0)),
                  pl.BlockSpec((1, dc), lambda bi, t: (0, 0)),
                  pl.BlockSpec((1, dc), lambda bi, t: (0, 0)),
                  pl.BlockSpec((1, dc), lambda bi, t: (0, 0))],
        out_specs=pl.BlockSpec((1, rows, dc), lambda bi, t: (bi, t, 0)),
        out_shape=jax.ShapeDtypeStruct((b, s, dc), BF16),
        scratch_shapes=[pltpu.VMEM((SUBLANES, CONV_HALO + rows, dc), F32), pltpu.VMEM((rows, dc), F32)],
        compiler_params=_cparams(("parallel", "arbitrary"), 32),
        name="conformer_conv",
    )(glu, glu, w_dw, b_dw, g_ln, b_ln)


def _dsa_kernel(q_ref, qi_ref, k_ref, v_ref, kidx_ref, w_ref, o_ref,
                ke_ref, ko_ref, vt_ref, key_ref, cut_ref, nd_ref, qs_ref, qis_ref, m_ref, l_ref, acc_ref,
                *, seq, chunk, topk):
    tb = pl.program_id(1)
    t0 = tb * Q_TILE
    n_chunks = (t0 + Q_TILE + chunk - 1) // chunk
    rep = N_HEADS // N_KV_HEADS
    nt = (((1,), (1,)), ((), ()))

    @pl.when(tb == 0)
    def _():
        def build(c, carry):
            r0 = pl.multiple_of(c * chunk, chunk)
            kx = kidx_ref[0, pl.ds(r0, chunk), :]
            lane = lax.broadcasted_iota(I32, kx.shape, 1)
            ke_ref[pl.ds(r0, chunk), :] = jnp.where(lane < IDX_DIM, kx, 0.0).astype(BF16)
            ko_ref[pl.ds(r0, chunk), :] = jnp.where(lane >= IDX_DIM, pltpu.roll(kx, IDX_DIM, 1), 0.0).astype(BF16)
            for g in range(N_KV_HEADS):
                vg = v_ref[0, pl.ds(r0, chunk), g * HEAD_DIM:(g + 1) * HEAD_DIM]
                vt_ref[c, g * HEAD_DIM:(g + 1) * HEAD_DIM, :] = vg.astype(F32).T.astype(BF16)
            return carry
        lax.fori_loop(0, seq // chunk, build, 0)

    for g in range(N_KV_HEADS):
        for r in range(rep):
            h = g * rep + r
            qs_ref[g, r * Q_TILE:(r + 1) * Q_TILE, :] = q_ref[0, :, h * HEAD_DIM:(h + 1) * HEAD_DIM]
    for jj in range(IDX_HEADS // 4):
        for half in range(2):
            pair = 2 * jj + half
            qis_ref[jj, half * Q_TILE:(half + 1) * Q_TILE, :] = qi_ref[0, :, pair * LANES:(pair + 1) * LANES]
    w_t = w_ref[0].T * (IDX_HEADS ** -0.5)

    key_row = lax.broadcasted_iota(I32, (chunk, Q_TILE), 0)
    q_pos = t0 + lax.broadcasted_iota(I32, (chunk, Q_TILE), 1)

    def score_chunk(c, carry):
        r0 = pl.multiple_of(c * chunk, chunk)
        ke = ke_ref[pl.ds(r0, chunk), :]
        ko = ko_ref[pl.ds(r0, chunk), :]
        acc = jnp.zeros((chunk, Q_TILE), F32)
        for jj in range(IDX_HEADS // 4):
            rhs = qis_ref[jj]
            de = lax.dot_general(ke, rhs, nt, preferred_element_type=F32)
            do = lax.dot_general(ko, rhs, nt, preferred_element_type=F32)
            for half in range(2):
                h_even = 2 * (2 * jj + half)
                cols = slice(half * Q_TILE, (half + 1) * Q_TILE)
                acc = acc + w_t[IDX_DIM + h_even:IDX_DIM + h_even + 1, :] * jnp.maximum(de[:, cols], 0.0)
                acc = acc + w_t[IDX_DIM + h_even + 1:IDX_DIM + h_even + 2, :] * jnp.maximum(do[:, cols], 0.0)
        bits = lax.bitcast_convert_type(acc, I32)
        key = bits ^ ((bits >> 31) & 0x7FFFFFFF)
        key_ref[c] = jnp.where(r0 + key_row <= q_pos, key, INT_MIN)
        return carry

    lax.fori_loop(0, n_chunks, score_chunk, 0)

    def count_ge(cand):
        def count_chunk(c, cnts):
            cnts = list(cnts)
            for r in range(chunk // SUBLANES):
                slab = key_ref[c, r * SUBLANES:(r + 1) * SUBLANES, :]
                cnts[r % len(cnts)] = cnts[r % len(cnts)] + jnp.where(slab >= cand, 1.0, 0.0)
            return tuple(cnts)

        zero = jnp.zeros((SUBLANES, Q_TILE), F32)
        cnts = lax.fori_loop(0, n_chunks, count_chunk, (zero, zero, zero, zero))
        return jnp.sum((cnts[0] + cnts[1]) + (cnts[2] + cnts[3]), axis=0, keepdims=True)

    def bit_step(i, t_u):
        cand_u = t_u | (jnp.int32(1) << (31 - i))
        return jnp.where(count_ge(cand_u ^ INT_MIN) >= float(topk), cand_u, t_u)

    t_u = lax.fori_loop(0, 32, bit_step, jnp.zeros((1, Q_TILE), I32))
    thr = jnp.maximum(t_u ^ INT_MIN, INT_MIN + 1)

    n_ge = count_ge(thr)
    cut_ref[...] = jnp.full(cut_ref.shape, INT_MAX, I32)

    @pl.when(jnp.max(n_ge) > topk)
    def _():
        def count_where(pred):
            def count_chunk(c, cnt):
                r0 = pl.multiple_of(c * chunk, chunk)
                hit = jnp.where(pred(key_ref[c], r0 + key_row), 1, 0)
                return cnt + jnp.sum(hit, axis=0, keepdims=True)
            return lax.fori_loop(0, n_chunks, count_chunk, jnp.zeros((1, Q_TILE), I32))

        keep = topk - count_where(lambda kc, pos: kc > thr)
        pos_bits = seq.bit_length()

        def bit_step(i, cut):
            cand = cut | (jnp.int32(1) << (pos_bits - 1 - i))
            taken = count_where(lambda kc, pos: jnp.where(kc == thr, pos, INT_MAX) < cand)
            return jnp.where(taken <= keep, cand, cut)
        cut_ref[...] = lax.fori_loop(0, pos_bits, bit_step, jnp.zeros((1, Q_TILE), I32))

    cut = cut_ref[...]

    m_ref[...] = jnp.full(m_ref.shape, M_INIT, F32)
    l_ref[...] = jnp.zeros(l_ref.shape, F32)
    acc_ref[...] = jnp.zeros(acc_ref.shape, F32)

    def attend_chunk(c, carry):
        r0 = pl.multiple_of(c * chunk, chunk)
        pos = r0 + key_row
        kc = key_ref[c]
        rank_pos = jnp.where(kc > thr, -1, jnp.where(kc == thr, pos, INT_MAX))
        nd_ref[...] = jnp.where(rank_pos < cut, (pos - q_pos).astype(F32), NEG_BIG)
        def logits(g):
            kg = k_ref[0, pl.ds(r0, chunk), g * HEAD_DIM:(g + 1) * HEAD_DIM]
            return lax.dot_general(kg, qs_ref[g], nt, preferred_element_type=F32)

        s_next = logits(0)
        for g in range(N_KV_HEADS):
            s_all = s_next
            if g + 1 < N_KV_HEADS:
                s_next = logits(g + 1)
            probs = []
            alphas = []
            for r in range(rep):
                h = g * rep + r
                slope = float(2.0 ** (-8.0 * (h + 1) / N_HEADS)) * LOG2E
                sr = s_all[:, r * Q_TILE:(r + 1) * Q_TILE] + slope * nd_ref[...]
                m_old = m_ref[h]
                m_new = jnp.maximum(m_old, jnp.max(sr, axis=0, keepdims=True))
                alpha = jnp.exp2(m_old - m_new)
                p = jnp.exp2(sr - m_new)
                l_ref[h] = alpha * l_ref[h] + jnp.sum(p, axis=0, keepdims=True)
                m_ref[h] = m_new
                probs.append(p.astype(BF16))
                alphas.append(alpha)
            vt = vt_ref[c, g * HEAD_DIM:(g + 1) * HEAD_DIM, :]
            pv = jnp.dot(vt, jnp.concatenate(probs, axis=1), preferred_element_type=F32)
            acc_ref[g] = jnp.concatenate(alphas, axis=1) * acc_ref[g] + pv
        return carry

    lax.fori_loop(0, n_chunks, attend_chunk, 0)

    for g in range(N_KV_HEADS):
        for r in range(rep):
            h = g * rep + r
            o_t = acc_ref[g, :, r * Q_TILE:(r + 1) * Q_TILE] * (1.0 / l_ref[h])
            o_ref[0, :, h * HEAD_DIM:(h + 1) * HEAD_DIM] = o_t.T.astype(o_ref.dtype)


def _dsa_attention(qkvi, small, b, s):
    d_attn = N_HEADS * HEAD_DIM
    d_kv = N_KV_HEADS * HEAD_DIM
    d_qi = IDX_HEADS * IDX_DIM
    rep = N_HEADS // N_KV_HEADS
    chunk = min(KEY_CHUNK, s)
    topk = min(TOPK_MAX, s // 4)
    assert s % chunk == 0 and chunk % Q_TILE == 0 and d_attn % d_qi == 0 and d_attn % d_kv == 0
    assert Q_TILE == LANES and HEAD_DIM == LANES and 2 * IDX_DIM == LANES
    kern = functools.partial(_dsa_kernel, seq=s, chunk=chunk, topk=topk)
    return pl.pallas_call(
        kern,
        grid=(b, s // Q_TILE),
        in_specs=[pl.BlockSpec((1, Q_TILE, d_attn), lambda bi, t: (bi, t, 0)),
                  pl.BlockSpec((1, Q_TILE, d_qi), lambda bi, t: (bi, t, (d_attn + 2 * d_kv) // d_qi)),
                  pl.BlockSpec((1, s, d_kv), lambda bi, t: (bi, 0, d_attn // d_kv)),
                  pl.BlockSpec((1, s, d_kv), lambda bi, t: (bi, 0, d_attn // d_kv + 1)),
                  pl.BlockSpec((1, s, LANES), lambda bi, t: (bi, 0, 0)),
                  pl.BlockSpec((1, Q_TILE, LANES), lambda bi, t: (bi, t, 0))],
        out_specs=pl.BlockSpec((1, Q_TILE, d_attn), lambda bi, t: (bi, t, 0)),
        out_shape=jax.ShapeDtypeStruct((b, s, d_attn), BF16),
        scratch_shapes=[pltpu.VMEM((s, LANES), BF16),
                        pltpu.VMEM((s, LANES), BF16),
                        pltpu.VMEM((s // chunk, d_kv, chunk), BF16),
                        pltpu.VMEM((s // chunk, chunk, Q_TILE), I32),
                        pltpu.VMEM((1, Q_TILE), I32),
                        pltpu.VMEM((chunk, Q_TILE), F32),
                        pltpu.VMEM((N_KV_HEADS, rep * Q_TILE, HEAD_DIM), BF16),
                        pltpu.VMEM((IDX_HEADS // 4, 2 * Q_TILE, LANES), BF16),
                        pltpu.VMEM((N_HEADS, 1, Q_TILE), F32),
                        pltpu.VMEM((N_HEADS, 1, Q_TILE), F32),
                        pltpu.VMEM((N_KV_HEADS, HEAD_DIM, rep * Q_TILE), F32)],
        compiler_params=_cparams(("parallel", "arbitrary"), 48),
        name="dsa_attention",
    )(qkvi, qkvi, qkvi, qkvi, small, small)


def _pack_bf16_halves(hb):
    half = hb.shape[1] // 2
    bits = lax.bitcast_convert_type(hb.astype(F32), I32)
    return lax.shift_right_logical(bits[:, :half], 16) | (bits[:, half:] & -65536)


def _unpack_bf16_halves(words):
    lo = lax.bitcast_convert_type(words << 16, F32).astype(BF16)
    hi = lax.bitcast_convert_type(words & -65536, F32).astype(BF16)
    return lo, hi


def _outproj_kernel(conv_ref, attn_ref, wc_ref, wa_ref, x_ref, g_ref, b_ref, wr_ref, br_ref,
                    h_ref, hp_ref, lg_ref, hb_ref, *, alpha):
    j = pl.program_id(1)
    tm, d = h_ref.shape
    tn = x_ref.shape[1]
    n_tiles = d // tn
    mix = jnp.dot(conv_ref[...], wc_ref[...], preferred_element_type=F32)
    mix = mix + jnp.dot(attn_ref[...], wa_ref[...], preferred_element_type=F32)
    z = alpha * x_ref[...] + mix
    for jj in range(n_tiles):
        @pl.when(j == jj)
        def _(jj=jj):
            h_ref[:, jj * tn:(jj + 1) * tn] = z

    @pl.when(j == n_tiles - 1)
    def _():
        def rows(i, carry):
            starts = [pl.multiple_of((LN_GROUPS * i + k) * BF16_ROWS, BF16_ROWS) for k in range(LN_GROUPS)]
            zs = [h_ref[pl.ds(r0, BF16_ROWS), :] for r0 in starts]
            hs = [_layer_norm_rows(z, g_ref[...], b_ref[...]) for z in zs]
            for r0, h in zip(starts, hs):
                h_ref[pl.ds(r0, BF16_ROWS), :] = h
                hb_ref[pl.ds(r0, BF16_ROWS), :] = h.astype(BF16)
            return carry
        lax.fori_loop(0, tm // (LN_GROUPS * BF16_ROWS), rows, 0)

        def pack_rows(i, carry):
            r0 = pl.multiple_of(i * BF16_ROWS, BF16_ROWS)
            hp_ref[pl.ds(r0, BF16_ROWS), :] = _pack_bf16_halves(hb_ref[pl.ds(r0, BF16_ROWS), :])
            return carry
        lax.fori_loop(0, tm // BF16_ROWS, pack_rows, 0, unroll=2)
        lg_ref[...] = jnp.dot(hb_ref[...], wr_ref[...], preferred_element_type=F32) + br_ref[...]


def _out_proj_ln(conv_out, attn_out, w_conv, w_attn, x2, ln_g, ln_b, w_router, b_router, alpha):
    n, d = x2.shape
    dc = conv_out.shape[1]
    da = attn_out.shape[1]
    tm = min(512, n)
    tn = min(512, d)
    assert n % tm == 0 and d % tn == 0
    kern = functools.partial(_outproj_kernel, alpha=alpha)
    return pl.pallas_call(
        kern,
        grid=(n // tm, d // tn),
        in_specs=[pl.BlockSpec((tm, dc), lambda i, j: (i, 0)),
                  pl.BlockSpec((tm, da), lambda i, j: (i, 0)),
                  pl.BlockSpec((dc, tn), lambda i, j: (0, j)),
                  pl.BlockSpec((da, tn), lambda i, j: (0, j)),
                  pl.BlockSpec((tm, tn), lambda i, j: (i, j)),
                  pl.BlockSpec((1, d), lambda i, j: (0, 0)),
                  pl.BlockSpec((1, d), lambda i, j: (0, 0)),
                  pl.BlockSpec((d, LANES), lambda i, j: (0, 0)),
                  pl.BlockSpec((1, LANES), lambda i, j: (0, 0))],
        out_specs=[pl.BlockSpec((tm, d), lambda i, j: (i, 0)),
                   pl.BlockSpec((tm, d // 2), lambda i, j: (i, 0)),
                   pl.BlockSpec((tm, LANES), lambda i, j: (i, 0))],
        out_shape=[jax.ShapeDtypeStruct((n, d), F32), jax.ShapeDtypeStruct((n, d // 2), I32),
                   jax.ShapeDtypeStruct((n, LANES), F32)],
        scratch_shapes=[pltpu.VMEM((tm, d), BF16)],
        compiler_params=_cparams(("parallel", "arbitrary"), 56),
        name="out_proj_ln1",
    )(conv_out, attn_out, w_conv, w_attn, x2, ln_g, ln_b, w_router, b_router)


def _first_argmax(vals, lane, valid):
    masked = jnp.where(valid, vals, -jnp.inf)
    mx = jnp.max(masked, axis=1, keepdims=True)
    idx = jnp.min(jnp.where(valid & (masked == mx), lane, LANES), axis=1, keepdims=True)
    return mx, idx


def _router_kernel(lg_ref, tri_ref, e_ref, gate_ref, cnt_ref):
    @pl.when(pl.program_id(0) == 0)
    def _():
        cnt_ref[...] = jnp.zeros(cnt_ref.shape, F32)

    lg = lg_ref[...]
    lane = lax.broadcasted_iota(I32, lg.shape, 1)
    is_group = lane < N_GROUPS
    g_max, g_sel = _first_argmax(lg, lane, is_group)
    g_den = jnp.sum(jnp.where(is_group, jnp.exp(lg - g_max), 0.0), axis=1, keepdims=True)
    p_group = 1.0 / g_den
    lo = N_GROUPS + g_sel * EXPERTS_PER_GROUP
    in_group = (lane >= lo) & (lane < lo + EXPERTS_PER_GROUP)
    e_max, _ = _first_argmax(lg, lane, in_group)
    e_exp = jnp.where(in_group, jnp.exp(lg - e_max), 0.0)
    prob = e_exp / jnp.sum(e_exp, axis=1, keepdims=True)
    p1, i1 = _first_argmax(prob, lane, in_group)
    p2, i2 = _first_argmax(prob, lane, in_group & (lane != i1))
    norm = p_group / (p1 + p2)
    gate_ref[...] = jnp.where(lane == 0, p1 * norm, jnp.where(lane == 1, p2 * norm, 0.0))
    chosen = jnp.where((lane == i1) | (lane == i2), 1.0, 0.0)
    before = jnp.dot(tri_ref[...], chosen.astype(BF16), preferred_element_type=F32) + cnt_ref[...]
    rank1 = jnp.sum(jnp.where(lane == i1, before, 0.0), axis=1, keepdims=True).astype(I32)
    rank2 = jnp.sum(jnp.where(lane == i2, before, 0.0), axis=1, keepdims=True).astype(I32)
    cnt_ref[...] += jnp.sum(chosen, axis=0, keepdims=True)
    e_ref[...] = jnp.where(lane == 0, i1 - N_GROUPS,
                           jnp.where(lane == 1, i2 - N_GROUPS,
                                     jnp.where(lane == 2, rank1, jnp.where(lane == 3, rank2, 0))))


def _route(logits):
    n = logits.shape[0]
    tm = min(512, n)
    tri = jnp.tril(jnp.ones((tm, tm), BF16), -1)
    spec = pl.BlockSpec((tm, LANES), lambda i: (i, 0))
    return pl.pallas_call(
        _router_kernel,
        grid=(n // tm,),
        in_specs=[spec, pl.BlockSpec((tm, tm), lambda i: (0, 0))],
        out_specs=[spec, spec, pl.BlockSpec((1, LANES), lambda i: (0, 0))],
        out_shape=[jax.ShapeDtypeStruct((n, LANES), I32), jax.ShapeDtypeStruct((n, LANES), F32),
                   jax.ShapeDtypeStruct((1, LANES), F32)],
        compiler_params=_cparams(("arbitrary",), 32),
        name="router",
    )(logits, tri)


def _start_row_copies(src_hbm, idx_ref, idx_row, dst_ref, slot, row0, sem, count):
    group = 8

    def issue(i, carry):
        for k in range(group):
            r = i * group + k
            pltpu.make_async_copy(src_hbm.at[pl.ds(idx_ref[0, idx_row, r], 1), :],
                                  dst_ref.at[slot, pl.ds(row0 + r, 1), :], sem.at[slot]).start(priority=k % 2)
        return carry
    lax.fori_loop(0, count // group, issue, 0)


def _wait_slot(src_hbm, dst_ref, slot, sem):
    rows = dst_ref.shape[1]
    pltpu.make_async_copy(src_hbm.at[pl.ds(0, rows), :], dst_ref.at[slot], sem.at[slot]).wait()


def _combine_kernel(pos_ref, pos_next_ref, y_hbm, h_ref, gate_ref, g_ref, b_ref, o_ref, buf_ref, sem, *, alpha):
    i = pl.program_id(0)
    n = pl.num_programs(0)
    slot = i % 2
    rows = h_ref.shape[0]

    def start(idx_ref, into):
        for k in range(2):
            _start_row_copies(y_hbm, idx_ref, k, buf_ref, into, k * rows, sem, rows)

    @pl.when(i == 0)
    def _():
        start(pos_ref, 0)

    @pl.when(i + 1 < n)
    def _():
        start(pos_next_ref, 1 - slot)

    _wait_slot(y_hbm, buf_ref, slot, sem)

    def norm_rows(j, carry):
        r0 = pl.multiple_of(j * SUBLANES, SUBLANES)
        gates = gate_ref[pl.ds(r0, SUBLANES), :]
        ffn = (buf_ref[slot, pl.ds(r0, SUBLANES), :] * gates[:, 0:1]
               + buf_ref[slot, pl.ds(rows + r0, SUBLANES), :] * gates[:, 1:2])
        z = alpha * h_ref[pl.ds(r0, SUBLANES), :] + ffn
        o_ref[pl.ds(r0, SUBLANES), :] = _layer_norm_rows(z, g_ref[...], b_ref[...])
        return carry
    lax.fori_loop(0, rows // SUBLANES, norm_rows, 0, unroll=8)


def _combine_ln(y, pos2, h1, gate_lanes, ln_g, ln_b, alpha):
    n, d = h1.shape
    tb = min(COMBINE_ROWS, n)
    nblk = n // tb
    pos_blocks = pos2.reshape(nblk, tb, 2).transpose(0, 2, 1)
    kern = functools.partial(_combine_kernel, alpha=alpha)
    return pl.pallas_call(
        kern,
        grid=(nblk,),
        in_specs=[pl.BlockSpec((1, 2, tb), lambda i: (i, 0, 0), memory_space=pltpu.SMEM),
                  pl.BlockSpec((1, 2, tb), lambda i: (jnp.minimum(i + 1, nblk - 1), 0, 0), memory_space=pltpu.SMEM),
                  pl.BlockSpec(memory_space=pl.ANY),
                  pl.BlockSpec((tb, d), lambda i: (i, 0)),
                  pl.BlockSpec((tb, LANES), lambda i: (i, 0)),
                  pl.BlockSpec((1, d), lambda i: (0, 0)),
                  pl.BlockSpec((1, d), lambda i: (0, 0))],
        out_specs=pl.BlockSpec((tb, d), lambda i: (i, 0)),
        out_shape=jax.ShapeDtypeStruct((n, d), F32),
        scratch_shapes=[pltpu.VMEM((2, 2 * tb, d), F32), pltpu.SemaphoreType.DMA((2,))],
        compiler_params=_cparams(("arbitrary",), 32),
        name="moe_combine_ln2",
    )(pos_blocks, pos_blocks, y, h1, gate_lanes, ln_g, ln_b)


def _cast_weight(dst_ref, src_ref):
    rows = src_ref.shape[1]
    step = min(256, rows)

    def body(i, carry):
        r0 = pl.multiple_of(i * step, step)
        dst_ref[pl.ds(r0, step), :] = src_ref[0, pl.ds(r0, step), :].astype(BF16)
        return carry
    lax.fori_loop(0, rows // step, body, 0)


ITEM_COMPUTE = 1
ITEM_NEW_WEIGHTS = 2
ITEM_ZERO_FILL = 4


def _expert_hidden(words, w1b_ref, w3b_ref):
    x_lo, x_hi = _unpack_bf16_halves(words)
    half = x_lo.shape[1]

    def project(w_ref):
        return (jnp.dot(x_lo, w_ref[0:half, :], preferred_element_type=F32)
                + jnp.dot(x_hi, w_ref[half:2 * half, :], preferred_element_type=F32))

    a = project(w1b_ref)
    b = project(w3b_ref)
    return (a * jax.nn.sigmoid(a) * b).astype(BF16)


def _moe_up_gather_kernel(e_ref, cw_ref, bi_ref, bo_ref, co_ref, flag_ref, tok_ref, tok_next_ref, hp_hbm,
                          w1_ref, w3_ref, o_ref, xs_ref, buf_ref, w1b_ref, w3b_ref, sem):
    it = pl.program_id(0)
    last = pl.num_programs(0) - 1
    flag = flag_ref[it]
    slot = it % 2
    rows = buf_ref.shape[1]

    @pl.when(it == 0)
    def _():
        _start_row_copies(hp_hbm, tok_ref, 0, buf_ref, 0, 0, sem, rows)

    @pl.when((it < last) & ((flag_ref[jnp.minimum(it + 1, last)] & ITEM_COMPUTE) != 0))
    def _():
        _start_row_copies(hp_hbm, tok_next_ref, 0, buf_ref, 1 - slot, 0, sem, rows)

    @pl.when((flag & ITEM_NEW_WEIGHTS) != 0)
    def _():
        _cast_weight(w1b_ref, w1_ref)
        _cast_weight(w3b_ref, w3_ref)

    @pl.when((flag & ITEM_COMPUTE) != 0)
    def _():
        _wait_slot(hp_hbm, buf_ref, slot, sem)
        words = buf_ref[slot]
        xs_ref[...] = words
        o_ref[...] = _expert_hidden(words, w1b_ref, w3b_ref)

    @pl.when((flag & ITEM_ZERO_FILL) != 0)
    def _():
        o_ref[...] = jnp.zeros(o_ref.shape, o_ref.dtype)
        xs_ref[...] = jnp.zeros(xs_ref.shape, xs_ref.dtype)


def _moe_up_kernel(e_ref, cw_ref, bi_ref, bo_ref, co_ref, flag_ref, x_ref, w1_ref, w3_ref, o_ref, w1b_ref, w3b_ref):
    flag = flag_ref[pl.program_id(0)]

    @pl.when((flag & ITEM_NEW_WEIGHTS) != 0)
    def _():
        _cast_weight(w1b_ref, w1_ref)
        _cast_weight(w3b_ref, w3_ref)

    @pl.when((flag & ITEM_COMPUTE) != 0)
    def _():
        o_ref[...] = _expert_hidden(x_ref[...], w1b_ref, w3b_ref)

    @pl.when((flag & ITEM_ZERO_FILL) != 0)
    def _():
        o_ref[...] = jnp.zeros(o_ref.shape, o_ref.dtype)


def _moe_down_kernel(e_ref, cw_ref, bi_ref, bo_ref, co_ref, flag_ref, h0_ref, h1_ref, w2_ref, o_ref, w2b_ref):
    flag = flag_ref[pl.program_id(0)]

    @pl.when((flag & ITEM_NEW_WEIGHTS) != 0)
    def _():
        _cast_weight(w2b_ref, w2_ref)

    @pl.when((flag & ITEM_COMPUTE) != 0)
    def _():
        fc = h0_ref.shape[1]
        o_ref[...] = (jnp.dot(h0_ref[...], w2b_ref[0:fc, :], preferred_element_type=F32)
                      + jnp.dot(h1_ref[...], w2b_ref[fc:2 * fc, :], preferred_element_type=F32))

    @pl.when((flag & ITEM_ZERO_FILL) != 0)
    def _():
        o_ref[...] = jnp.zeros(o_ref.shape, o_ref.dtype)


def _work_items(blocks_per_expert, block_start, n_chunks, n_blocks):
    n_items = n_chunks * n_blocks
    per_e = n_chunks * blocks_per_expert
    end = jnp.cumsum(per_e)
    start = end - per_e
    it = jnp.arange(n_items, dtype=I32)
    total = end[-1]
    used_blocks = total // n_chunks
    itc = jnp.minimum(it, total - 1)
    e = jnp.sum(itc[:, None] >= end[None, :], axis=1).astype(I32)
    local = itc - start[e]
    nb = jnp.maximum(blocks_per_expert[e], 1)
    c = local // nb
    r = local - c * nb
    b_in = block_start[e] + r
    active = it < total
    spare = jnp.maximum(it - total, 0)
    b_out = jnp.where(active, b_in, used_blocks + spare // n_chunks)
    c_out = jnp.where(active, c, spare % n_chunks)
    first = active & (r == 0)
    flag = jnp.where(active, ITEM_COMPUTE + ITEM_NEW_WEIGHTS * first.astype(I32), ITEM_ZERO_FILL)
    later_first = lax.cummin(jnp.where(first, it, n_items)[::-1])[::-1]
    next_start = jnp.concatenate([later_first[1:], jnp.full((1,), n_items, I32)])
    ahead = jnp.minimum(next_start, n_items - 1)
    use_next = active & ~first & (next_start < n_items)
    e_w = jnp.where(use_next, e[ahead], e)
    c_w = jnp.where(use_next, c[ahead], c)
    return tuple(v.astype(I32) for v in (e_w, c_w, b_in, b_out, c_out, flag))


def _moe_up_gather(hp, buf_tok, w1, w3, items):
    p = buf_tok.shape[0]
    half = hp.shape[1]
    d, f = w1.shape[1], w1.shape[2]
    fc = f // 2
    n_items = items[0].shape[0]
    nblk = p // MOE_ROWS
    tok_blocks = buf_tok.reshape(nblk, 1, MOE_ROWS)
    wspec = pl.BlockSpec((1, d, fc), lambda it, e, cw, bi, bo, co, fl: (e[it], 0, 0))
    return pl.pallas_call(
        _moe_up_gather_kernel,
        grid_spec=pltpu.PrefetchScalarGridSpec(
            num_scalar_prefetch=6,
            grid=(n_items,),
            in_specs=[pl.BlockSpec((1, 1, MOE_ROWS), lambda it, e, cw, bi, bo, co, fl: (bi[it], 0, 0),
                                   memory_space=pltpu.SMEM),
                      pl.BlockSpec((1, 1, MOE_ROWS),
                                   lambda it, e, cw, bi, bo, co, fl: (bi[jnp.minimum(it + 1, n_items - 1)], 0, 0),
                                   memory_space=pltpu.SMEM),
                      pl.BlockSpec(memory_space=pl.ANY), wspec, wspec],
            out_specs=[pl.BlockSpec((MOE_ROWS, fc), lambda it, e, cw, bi, bo, co, fl: (bo[it], 0)),
                       pl.BlockSpec((MOE_ROWS, half), lambda it, e, cw, bi, bo, co, fl: (bo[it], 0))],
            scratch_shapes=[pltpu.VMEM((2, MOE_ROWS, half), I32), pltpu.VMEM((d, fc), BF16),
                            pltpu.VMEM((d, fc), BF16), pltpu.SemaphoreType.DMA((2,))]),
        out_shape=[jax.ShapeDtypeStruct((p, fc), BF16), jax.ShapeDtypeStruct((p, half), I32)],
        compiler_params=_cparams(("arbitrary",), 56),
        name="moe_up_gather",
    )(*items, tok_blocks, tok_blocks, hp, w1, w3)


def _moe_up(xs, w1, w3, items):
    p, half = xs.shape
    d, f = w1.shape[1], w1.shape[2]
    fc = f // 2
    assert d == 2 * half
    wspec = pl.BlockSpec((1, d, fc), lambda it, e, cw, bi, bo, co, fl: (e[it], 0, 1))
    return pl.pallas_call(
        _moe_up_kernel,
        grid_spec=pltpu.PrefetchScalarGridSpec(
            num_scalar_prefetch=6,
            grid=(items[0].shape[0],),
            in_specs=[pl.BlockSpec((MOE_ROWS, half), lambda it, e, cw, bi, bo, co, fl: (bi[it], 0)), wspec, wspec],
            out_specs=pl.BlockSpec((MOE_ROWS, fc), lambda it, e, cw, bi, bo, co, fl: (bo[it], 0)),
            scratch_shapes=[pltpu.VMEM((d, fc), BF16), pltpu.VMEM((d, fc), BF16)]),
        out_shape=jax.ShapeDtypeStruct((p, fc), BF16),
        compiler_params=_cparams(("arbitrary",), 56),
        name="moe_up",
    )(*items, xs, w1, w3)


def _moe_down(h0, h1, w2, items, d_chunk):
    p, fc = h0.shape
    f = 2 * fc
    d = w2.shape[2]
    hspec = pl.BlockSpec((MOE_ROWS, fc), lambda it, e, cw, bi, bo, co, fl: (bi[it], 0))
    return pl.pallas_call(
        _moe_down_kernel,
        grid_spec=pltpu.PrefetchScalarGridSpec(
            num_scalar_prefetch=6,
            grid=(items[0].shape[0],),
            in_specs=[hspec, hspec,
                      pl.BlockSpec((1, f, d_chunk), lambda it, e, cw, bi, bo, co, fl: (e[it], 0, cw[it]))],
            out_specs=pl.BlockSpec((MOE_ROWS, d_chunk), lambda it, e, cw, bi, bo, co, fl: (bo[it], co[it])),
            scratch_shapes=[pltpu.VMEM((f, d_chunk), BF16)]),
        out_shape=jax.ShapeDtypeStruct((p, d), F32),
        compiler_params=_cparams(("arbitrary",), 48),
        name="moe_down",
    )(*items, h0, h1, w2)


def _hier_moe_ln(h1, hp, logits, w1, w3, w2, ln_g, ln_b, alpha):
    n, d = h1.shape
    e_lanes, gate_lanes, lane_counts = _route(logits)
    a = 2 * n
    e_flat = e_lanes[:, 0:2].reshape(a)
    rank = e_lanes[:, 2:4].reshape(a)
    counts = lane_counts[0, N_GROUPS:N_GROUPS + N_EXPERTS].astype(I32)
    blocks_per_expert = (counts + MOE_ROWS - 1) // MOE_ROWS
    block_start = jnp.cumsum(blocks_per_expert) - blocks_per_expert
    onehot = e_flat[:, None] == jnp.arange(N_EXPERTS, dtype=I32)[None, :]
    pos = jnp.sum(jnp.where(onehot, block_start[None, :], 0), axis=1) * MOE_ROWS + rank
    n_blocks = (a + MOE_ROWS - 1) // MOE_ROWS + N_EXPERTS
    buf_tok = jnp.zeros((n_blocks * MOE_ROWS,), I32).at[pos].set(jnp.arange(a, dtype=I32) // 2)

    d_chunk = min(2048, d)
    up_items = _work_items(blocks_per_expert, block_start, 1, n_blocks)
    h0, xs = _moe_up_gather(hp, buf_tok, w1, w3, up_items)
    h1_mid = _moe_up(xs, w1, w3, up_items)
    y = _moe_down(h0, h1_mid, w2, _work_items(blocks_per_expert, block_start, d // d_chunk, n_blocks), d_chunk)
    return _combine_ln(y, pos.reshape(n, 2), h1, gate_lanes, ln_g, ln_b, alpha)


def kernel(x, w_in, conv_dw_w, conv_dw_b, conv_ln_g, conv_ln_b, w_out, ln1_g, ln1_b, w_router_group, b_router_group, w_router_expert, b_router_expert, w_expert_gate, w_expert_up, w_expert_down, ln2_g, ln2_b):
    b, s, d = x.shape
    depth = w_in.shape[0]
    assert depth == 1
    alpha = float((2.0 * depth) ** 0.25)
    n = b * s
    dc = conv_dw_w.shape[2]
    d_attn = N_HEADS * HEAD_DIM
    d_kv = N_KV_HEADS * HEAD_DIM
    d_qi = IDX_HEADS * IDX_DIM
    n_small = IDX_DIM + IDX_HEADS
    assert w_in.shape[2] == 2 * dc + d_attn + 2 * d_kv + d_qi + n_small

    x2 = x.reshape(n, d)
    w = w_in[0].astype(BF16)
    o_qkvi = 2 * dc
    o_small = o_qkvi + d_attn + 2 * d_kv + d_qi
    w_small = jnp.pad(w[:, o_small:], ((0, 0), (0, LANES - n_small)))
    qkvi_scale = jnp.concatenate([jnp.full((d_attn,), HEAD_DIM ** -0.5 * LOG2E, F32), jnp.ones((2 * d_kv,), F32),
                                  jnp.full((d_qi,), IDX_DIM ** -0.5, F32)])[None, :]

    xb, small = _proj_idx_and_cast(x2, w_small)
    glu = _proj_glu(xb, w, dc)
    qkvi = _proj_scale(xb, w, o_qkvi, o_small - o_qkvi, qkvi_scale, BF16, "proj_qkvi")

    conv_out = _conformer_conv(glu.reshape(b, s, dc), conv_dw_w[0], conv_dw_b, conv_ln_g, conv_ln_b)
    attn_out = _dsa_attention(qkvi.reshape(b, s, -1), small.reshape(b, s, LANES), b, s)

    n_route = N_GROUPS + N_EXPERTS
    w_router = jnp.pad(jnp.concatenate([w_router_group[0], w_router_expert[0]], axis=1),
                       ((0, 0), (0, LANES - n_route))).astype(BF16)
    b_router = jnp.pad(jnp.concatenate([b_router_group[0], b_router_expert[0]]), (0, LANES - n_route))[None, :]
    h1, hp, logits = _out_proj_ln(conv_out.reshape(n, dc), attn_out.reshape(n, d_attn),
                                  w_out[0, :dc].astype(BF16), w_out[0, dc:].astype(BF16),
                                  x2, ln1_g, ln1_b, w_router, b_router, alpha)

    out = _hier_moe_ln(h1, hp, logits, w_expert_gate[0], w_expert_up[0], w_expert_down[0], ln2_g, ln2_b, alpha)
    return out.reshape(b, s, d)
```

```python
import functools
import math

import jax
import jax.numpy as jnp
from jax import lax
from jax.experimental import pallas as pl
from jax.experimental.pallas import tpu as pltpu

F32 = jnp.float32
BF16 = jnp.bfloat16
I32 = jnp.int32

CONV_WIDTH = 31
N_HEADS = 16
HEAD_DIM = 128
N_KV_HEADS = 4
IDX_HEADS = 16
IDX_DIM = 64
TOPK_MAX = 256
N_GROUPS = 4
EXPERTS_PER_GROUP = 8
N_EXPERTS = N_GROUPS * EXPERTS_PER_GROUP
LN_EPS = 1e-5

LANES = 128
SUBLANES = 8
BF16_ROWS = 16
MIB = 1024 * 1024

Q_TILE = 128
KEY_CHUNK = 512
CONV_HALO = 32
MOE_ROWS = 256
COMBINE_ROWS = 128
LN_GROUPS = 4
INT_MIN = -(2 ** 31)
INT_MAX = 2 ** 31 - 1
LOG2E = math.log2(math.e)
NEG_BIG = -1e30
M_INIT = -1e20


def _cparams(semantics, vmem_mib):
    return pltpu.CompilerParams(dimension_semantics=semantics, vmem_limit_bytes=vmem_mib * MIB)


def _layer_norm_rows(z, g, b):
    mu = jnp.mean(z, axis=-1, keepdims=True)
    zc = z - mu
    var = jnp.mean(zc * zc, axis=-1, keepdims=True)
    return zc * lax.rsqrt(var + LN_EPS) * g + b


def _mm_glu_kernel(x_ref, wa_ref, wg_ref, o_ref):
    x = x_ref[...]
    a = jnp.dot(x, wa_ref[...], preferred_element_type=F32)
    g = jnp.dot(x, wg_ref[...], preferred_element_type=F32)
    o_ref[...] = (a * jax.nn.sigmoid(g)).astype(o_ref.dtype)


def _mm_scale_kernel(x_ref, w_ref, s_ref, o_ref):
    acc = jnp.dot(x_ref[...], w_ref[...], preferred_element_type=F32)
    o_ref[...] = (acc * s_ref[...]).astype(o_ref.dtype)


def _mm_tiles(n, k, cols):
    tm = min(1024, n)
    tn = min(512, cols)
    assert n % tm == 0 and cols % tn == 0
    return tm, tn


def _proj_glu(xb, w, cols):
    n, k = xb.shape
    tm, tn = _mm_tiles(n, k, cols)
    gate_block = cols // tn
    return pl.pallas_call(
        _mm_glu_kernel,
        grid=(n // tm, cols // tn),
        in_specs=[pl.BlockSpec((tm, k), lambda i, j: (i, 0)),
                  pl.BlockSpec((k, tn), lambda i, j: (0, j)),
                  pl.BlockSpec((k, tn), lambda i, j: (0, j + gate_block))],
        out_specs=pl.BlockSpec((tm, tn), lambda i, j: (i, j)),
        out_shape=jax.ShapeDtypeStruct((n, cols), F32),
        compiler_params=_cparams(("parallel", "arbitrary"), 48),
        name="proj_glu",
    )(xb, w, w)


def _proj_scale(xb, w, first_col, cols, scale, out_dtype, name):
    n, k = xb.shape
    tm, tn = _mm_tiles(n, k, cols)
    assert first_col % tn == 0
    first_block = first_col // tn
    return pl.pallas_call(
        _mm_scale_kernel,
        grid=(n // tm, cols // tn),
        in_specs=[pl.BlockSpec((tm, k), lambda i, j: (i, 0)),
                  pl.BlockSpec((k, tn), lambda i, j: (0, j + first_block)),
                  pl.BlockSpec((1, tn), lambda i, j: (0, j))],
        out_specs=pl.BlockSpec((tm, tn), lambda i, j: (i, j)),
        out_shape=jax.ShapeDtypeStruct((n, cols), out_dtype),
        compiler_params=_cparams(("parallel", "arbitrary"), 48),
        name=name,
    )(xb, w, scale)


def _idx_cast_kernel(x_ref, w_ref, xb_ref, o_ref):
    xb = x_ref[...].astype(BF16)
    xb_ref[...] = xb
    o_ref[...] = jnp.dot(xb, w_ref[...], preferred_element_type=F32)


def _proj_idx_and_cast(x2, w_small):
    n, k = x2.shape
    tm = min(256, n)
    return pl.pallas_call(
        _idx_cast_kernel,
        grid=(n // tm,),
        in_specs=[pl.BlockSpec((tm, k), lambda i: (i, 0)),
                  pl.BlockSpec((k, LANES), lambda i: (0, 0))],
        out_specs=[pl.BlockSpec((tm, k), lambda i: (i, 0)),
                   pl.BlockSpec((tm, LANES), lambda i: (i, 0))],
        out_shape=[jax.ShapeDtypeStruct((n, k), BF16), jax.ShapeDtypeStruct((n, LANES), F32)],
        compiler_params=_cparams(("parallel",), 32),
        name="proj_idx",
    )(x2, w_small)


def _conv_kernel(cur_ref, prev_ref, w_ref, b_ref, g_ref, beta_ref, o_ref, sh_ref, y_ref, *, rows, lane_chunk):
    t = pl.program_id(1)
    dc = cur_ref.shape[2]
    row_sub = 32
    sh_ref[0, 0:CONV_HALO, :] = jnp.where(t > 0, prev_ref[0], 0.0)
    sh_ref[0, CONV_HALO:CONV_HALO + rows, :] = cur_ref[0]
    first = CONV_HALO - (CONV_WIDTH - 1)
    shifted_rows = rows + CONV_HALO - SUBLANES
    for lc in range(dc // lane_chunk):
        cols = slice(lc * lane_chunk, (lc + 1) * lane_chunk)
        for r in range(1, SUBLANES):
            sh_ref[r, 0:shifted_rows, cols] = sh_ref[0, r:r + shifted_rows, cols]

    for r0 in range(0, rows, row_sub):
        for lc in range(dc // lane_chunk):
            cols = slice(lc * lane_chunk, (lc + 1) * lane_chunk)
            acc = jnp.zeros((row_sub, lane_chunk), F32)
            for j in range(CONV_WIDTH):
                shift, base = (first + j) % SUBLANES, (first + j) // SUBLANES * SUBLANES
                acc = acc + sh_ref[shift, r0 + base:r0 + base + row_sub, cols] * w_ref[j:j + 1, cols]
            y_ref[r0:r0 + row_sub, cols] = acc + b_ref[:, cols]

    def norm_rows(i, carry):
        r0 = pl.multiple_of(i * BF16_ROWS, BF16_ROWS)
        yn = _layer_norm_rows(y_ref[pl.ds(r0, BF16_ROWS), :], g_ref[...], beta_ref[...])
        o_ref[0, pl.ds(r0, BF16_ROWS), :] = (yn * jax.nn.sigmoid(yn)).astype(o_ref.dtype)
        return carry

    lax.fori_loop(0, rows // BF16_ROWS, norm_rows, 0, unroll=4)


def _conformer_conv(glu, w_dw, b_dw, g_ln, b_ln):
    b, s, dc = glu.shape
    rows = min(128, s)
    halo_blocks = rows // CONV_HALO
    lane_chunk = min(512, dc)
    kern = functools.partial(_conv_kernel, rows=rows, lane_chunk=lane_chunk)
    return pl.pallas_call(
        kern,
        grid=(b, s // rows),
        in_specs=[pl.BlockSpec((1, rows, dc), lambda bi, t: (bi, t, 0)),
                  pl.BlockSpec((1, CONV_HALO, dc), lambda bi, t: (bi, jnp.maximum(t * halo_blocks - 1, 0), 0)),
                  pl.BlockSpec((CONV_WIDTH, dc), lambda bi, t: (0, 0)),
                  pl.BlockSpec((1, dc), lambda bi, t: (0, 0)),
                  pl.BlockSpec((1, dc), lambda bi, t: (0, 0)),
                  pl.BlockSpec((1, dc), lambda bi, t: (0, 0))],
        out_specs=pl.BlockSpec((1, rows, dc), lambda bi, t: (bi, t, 0)),
        out_shape=jax.ShapeDtypeStruct((b, s, dc), BF16),
        scratch_shapes=[pltpu.VMEM((SUBLANES, CONV_HALO + rows, dc), F32), pltpu.VMEM((rows, dc), F32)],
        compiler_params=_cparams(("parallel", "arbitrary"), 32),
        name="conformer_conv",
    )(glu, glu, w_dw, b_dw, g_ln, b_ln)


def _dsa_kernel(q_ref, qi_ref, k_ref, v_ref, kidx_ref, w_ref, o_ref,
                ke_ref, ko_ref, vt_ref, key_ref, cut_ref, nd_ref, qs_ref, qis_ref, m_ref, l_ref, acc_ref,
                *, seq, chunk, topk):
    tb = pl.program_id(1)
    t0 = tb * Q_TILE
    n_chunks = (t0 + Q_TILE + chunk - 1) // chunk
    rep = N_HEADS // N_KV_HEADS
    nt = (((1,), (1,)), ((), ()))

    @pl.when(tb == 0)
    def _():
        def build(c, carry):
            r0 = pl.multiple_of(c * chunk, chunk)
            kx = kidx_ref[0, pl.ds(r0, chunk), :]
            lane = lax.broadcasted_iota(I32, kx.shape, 1)
            ke_ref[pl.ds(r0, chunk), :] = jnp.where(lane < IDX_DIM, kx, 0.0).astype(BF16)
            ko_ref[pl.ds(r0, chunk), :] = jnp.where(lane >= IDX_DIM, pltpu.roll(kx, IDX_DIM, 1), 0.0).astype(BF16)
            for g in range(N_KV_HEADS):
                vg = v_ref[0, pl.ds(r0, chunk), g * HEAD_DIM:(g + 1) * HEAD_DIM]
                vt_ref[c, g * HEAD_DIM:(g + 1) * HEAD_DIM, :] = vg.astype(F32).T.astype(BF16)
            return carry
        lax.fori_loop(0, seq // chunk, build, 0)

    for g in range(N_KV_HEADS):
        for r in range(rep):
            h = g * rep + r
            qs_ref[g, r * Q_TILE:(r + 1) * Q_TILE, :] = q_ref[0, :, h * HEAD_DIM:(h + 1) * HEAD_DIM]
    for jj in range(IDX_HEADS // 4):
        for half in range(2):
            pair = 2 * jj + half
            qis_ref[jj, half * Q_TILE:(half + 1) * Q_TILE, :] = qi_ref[0, :, pair * LANES:(pair + 1) * LANES]
    w_t = w_ref[0].T * (IDX_HEADS ** -0.5)

    key_row = lax.broadcasted_iota(I32, (chunk, Q_TILE), 0)
    q_pos = t0 + lax.broadcasted_iota(I32, (chunk, Q_TILE), 1)

    def score_chunk(c, carry):
        r0 = pl.multiple_of(c * chunk, chunk)
        ke = ke_ref[pl.ds(r0, chunk), :]
        ko = ko_ref[pl.ds(r0, chunk), :]
        acc = jnp.zeros((chunk, Q_TILE), F32)
        for jj in range(IDX_HEADS // 4):
            rhs = qis_ref[jj]
            de = lax.dot_general(ke, rhs, nt, preferred_element_type=F32)
            do = lax.dot_general(ko, rhs, nt, preferred_element_type=F32)
            for half in range(2):
                h_even = 2 * (2 * jj + half)
                cols = slice(half * Q_TILE, (half + 1) * Q_TILE)
                acc = acc + w_t[IDX_DIM + h_even:IDX_DIM + h_even + 1, :] * jnp.maximum(de[:, cols], 0.0)
                acc = acc + w_t[IDX_DIM + h_even + 1:IDX_DIM + h_even + 2, :] * jnp.maximum(do[:, cols], 0.0)
        bits = lax.bitcast_convert_type(acc, I32)
        key = bits ^ ((bits >> 31) & 0x7FFFFFFF)
        key_ref[c] = jnp.where(r0 + key_row <= q_pos, key, INT_MIN)
        return carry

    lax.fori_loop(0, n_chunks, score_chunk, 0)

    def count_ge(cand):
        def count_chunk(c, cnts):
            cnts = list(cnts)
            for r in range(chunk // SUBLANES):
                slab = key_ref[c, r * SUBLANES:(r + 1) * SUBLANES, :]
                cnts[r % len(cnts)] = cnts[r % len(cnts)] + jnp.where(slab >= cand, 1.0, 0.0)
            return tuple(cnts)

        zero = jnp.zeros((SUBLANES, Q_TILE), F32)
        cnts = lax.fori_loop(0, n_chunks, count_chunk, (zero, zero, zero, zero))
        return jnp.sum((cnts[0] + cnts[1]) + (cnts[2] + cnts[3]), axis=0, keepdims=True)

    def bit_step(i, t_u):
        cand_u = t_u | (jnp.int32(1) << (31 - i))
        return jnp.where(count_ge(cand_u ^ INT_MIN) >= float(topk), cand_u, t_u)

    t_u = lax.fori_loop(0, 32, bit_step, jnp.zeros((1, Q_TILE), I32))
    thr = jnp.maximum(t_u ^ INT_MIN, INT_MIN + 1)

    n_ge = count_ge(thr)
    cut_ref[...] = jnp.full(cut_ref.shape, INT_MAX, I32)

    @pl.when(jnp.max(n_ge) > topk)
    def _():
        def count_where(pred):
            def count_chunk(c, cnt):
                r0 = pl.multiple_of(c * chunk, chunk)
                hit = jnp.where(pred(key_ref[c], r0 + key_row), 1, 0)
                return cnt + jnp.sum(hit, axis=0, keepdims=True)
            return lax.fori_loop(0, n_chunks, count_chunk, jnp.zeros((1, Q_TILE), I32))

        keep = topk - count_where(lambda kc, pos: kc > thr)
        pos_bits = seq.bit_length()

        def bit_step(i, cut):
            cand = cut | (jnp.int32(1) << (pos_bits - 1 - i))
            taken = count_where(lambda kc, pos: jnp.where(kc == thr, pos, INT_MAX) < cand)
            return jnp.where(taken <= keep, cand, cut)
        cut_ref[...] = lax.fori_loop(0, pos_bits, bit_step, jnp.zeros((1, Q_TILE), I32))

    cut = cut_ref[...]

    m_ref[...] = jnp.full(m_ref.shape, M_INIT, F32)
    l_ref[...] = jnp.zeros(l_ref.shape, F32)
    acc_ref[...] = jnp.zeros(acc_ref.shape, F32)

    def attend_chunk(c, carry):
        r0 = pl.multiple_of(c * chunk, chunk)
        pos = r0 + key_row
        kc = key_ref[c]
        rank_pos = jnp.where(kc > thr, -1, jnp.where(kc == thr, pos, INT_MAX))
        nd_ref[...] = jnp.where(rank_pos < cut, (pos - q_pos).astype(F32), NEG_BIG)
        def logits(g):
            kg = k_ref[0, pl.ds(r0, chunk), g * HEAD_DIM:(g + 1) * HEAD_DIM]
            return lax.dot_general(kg, qs_ref[g], nt, preferred_element_type=F32)

        s_next = logits(0)
        for g in range(N_KV_HEADS):
            s_all = s_next
            if g + 1 < N_KV_HEADS:
                s_next = logits(g + 1)
            probs = []
            alphas = []
            for r in range(rep):
                h = g * rep + r
                slope = float(2.0 ** (-8.0 * (h + 1) / N_HEADS)) * LOG2E
                sr = s_all[:, r * Q_TILE:(r + 1) * Q_TILE] + slope * nd_ref[...]
                m_old = m_ref[h]
                m_new = jnp.maximum(m_old, jnp.max(sr, axis=0, keepdims=True))
                alpha = jnp.exp2(m_old - m_new)
                p = jnp.exp2(sr - m_new)
                l_ref[h] = alpha * l_ref[h] + jnp.sum(p, axis=0, keepdims=True)
                m_ref[h] = m_new
                probs.append(p.astype(BF16))
                alphas.append(alpha)
            vt = vt_ref[c, g * HEAD_DIM:(g + 1) * HEAD_DIM, :]
            pv = jnp.dot(vt, jnp.concatenate(probs, axis=1), preferred_element_type=F32)
            acc_ref[g] = jnp.concatenate(alphas, axis=1) * acc_ref[g] + pv
        return carry

    lax.fori_loop(0, n_chunks, attend_chunk, 0)

    for g in range(N_KV_HEADS):
        for r in range(rep):
            h = g * rep + r
            o_t = acc_ref[g, :, r * Q_TILE:(r + 1) * Q_TILE] * (1.0 / l_ref[h])
            o_ref[0, :, h * HEAD_DIM:(h + 1) * HEAD_DIM] = o_t.T.astype(o_ref.dtype)


def _dsa_attention(qkvi, small, b, s):
    d_attn = N_HEADS * HEAD_DIM
    d_kv = N_KV_HEADS * HEAD_DIM
    d_qi = IDX_HEADS * IDX_DIM
    rep = N_HEADS // N_KV_HEADS
    chunk = min(KEY_CHUNK, s)
    topk = min(TOPK_MAX, s // 4)
    assert s % chunk == 0 and chunk % Q_TILE == 0 and d_attn % d_qi == 0 and d_attn % d_kv == 0
    assert Q_TILE == LANES and HEAD_DIM == LANES and 2 * IDX_DIM == LANES
    kern = functools.partial(_dsa_kernel, seq=s, chunk=chunk, topk=topk)
    return pl.pallas_call(
        kern,
        grid=(b, s // Q_TILE),
        in_specs=[pl.BlockSpec((1, Q_TILE, d_attn), lambda bi, t: (bi, t, 0)),
                  pl.BlockSpec((1, Q_TILE, d_qi), lambda bi, t: (bi, t, (d_attn + 2 * d_kv) // d_qi)),
                  pl.BlockSpec((1, s, d_kv), lambda bi, t: (bi, 0, d_attn // d_kv)),
                  pl.BlockSpec((1, s, d_kv), lambda bi, t: (bi, 0, d_attn // d_kv + 1)),
                  pl.BlockSpec((1, s, LANES), lambda bi, t: (bi, 0, 0)),
                  pl.BlockSpec((1, Q_TILE, LANES), lambda bi, t: (bi, t, 0))],
        out_specs=pl.BlockSpec((1, Q_TILE, d_attn), lambda bi, t: (bi, t, 0)),
        out_shape=jax.ShapeDtypeStruct((b, s, d_attn), BF16),
        scratch_shapes=[pltpu.VMEM((s, LANES), BF16),
                        pltpu.VMEM((s, LANES), BF16),
                        pltpu.VMEM((s // chunk, d_kv, chunk), BF16),
                        pltpu.VMEM((s // chunk, chunk, Q_TILE), I32),
                        pltpu.VMEM((1, Q_TILE), I32),
                        pltpu.VMEM((chunk, Q_TILE), F32),
                        pltpu.VMEM((N_KV_HEADS, rep * Q_TILE, HEAD_DIM), BF16),
                        pltpu.VMEM((IDX_HEADS // 4, 2 * Q_TILE, LANES), BF16),
                        pltpu.VMEM((N_HEADS, 1, Q_TILE), F32),
                        pltpu.VMEM((N_HEADS, 1, Q_TILE), F32),
                        pltpu.VMEM((N_KV_HEADS, HEAD_DIM, rep * Q_TILE), F32)],
        compiler_params=_cparams(("parallel", "arbitrary"), 48),
        name="dsa_attention",
    )(qkvi, qkvi, qkvi, qkvi, small, small)


def _pack_bf16_halves(hb):
    half = hb.shape[1] // 2
    bits = lax.bitcast_convert_type(hb.astype(F32), I32)
    return lax.shift_right_logical(bits[:, :half], 16) | (bits[:, half:] & -65536)


def _unpack_bf16_halves(words):
    lo = lax.bitcast_convert_type(words << 16, F32).astype(BF16)
    hi = lax.bitcast_convert_type(words & -65536, F32).astype(BF16)
    return lo, hi


def _outproj_kernel(conv_ref, attn_ref, wc_ref, wa_ref, x_ref, g_ref, b_ref, wr_ref, br_ref,
                    h_ref, hp_ref, lg_ref, hb_ref, *, alpha):
    j = pl.program_id(1)
    tm, d = h_ref.shape
    tn = x_ref.shape[1]
    n_tiles = d // tn
    mix = jnp.dot(conv_ref[...], wc_ref[...], preferred_element_type=F32)
    mix = mix + jnp.dot(attn_ref[...], wa_ref[...], preferred_element_type=F32)
    z = alpha * x_ref[...] + mix
    for jj in range(n_tiles):
        @pl.when(j == jj)
        def _(jj=jj):
            h_ref[:, jj * tn:(jj + 1) * tn] = z

    @pl.when(j == n_tiles - 1)
    def _():
        def rows(i, carry):
            starts = [pl.multiple_of((LN_GROUPS * i + k) * BF16_ROWS, BF16_ROWS) for k in range(LN_GROUPS)]
            zs = [h_ref[pl.ds(r0, BF16_ROWS), :] for r0 in starts]
            hs = [_layer_norm_rows(z, g_ref[...], b_ref[...]) for z in zs]
            for r0, h in zip(starts, hs):
                h_ref[pl.ds(r0, BF16_ROWS), :] = h
                hb_ref[pl.ds(r0, BF16_ROWS), :] = h.astype(BF16)
            return carry
        lax.fori_loop(0, tm // (LN_GROUPS * BF16_ROWS), rows, 0)

        def pack_rows(i, carry):
            r0 = pl.multiple_of(i * BF16_ROWS, BF16_ROWS)
            hp_ref[pl.ds(r0, BF16_ROWS), :] = _pack_bf16_halves(hb_ref[pl.ds(r0, BF16_ROWS), :])
            return carry
        lax.fori_loop(0, tm // BF16_ROWS, pack_rows, 0, unroll=2)
        lg_ref[...] = jnp.dot(hb_ref[...], wr_ref[...], preferred_element_type=F32) + br_ref[...]


def _out_proj_ln(conv_out, attn_out, w_conv, w_attn, x2, ln_g, ln_b, w_router, b_router, alpha):
    n, d = x2.shape
    dc = conv_out.shape[1]
    da = attn_out.shape[1]
    tm = min(512, n)
    tn = min(512, d)
    assert n % tm == 0 and d % tn == 0
    kern = functools.partial(_outproj_kernel, alpha=alpha)
    return pl.pallas_call(
        kern,
        grid=(n // tm, d // tn),
        in_specs=[pl.BlockSpec((tm, dc), lambda i, j: (i, 0)),
                  pl.BlockSpec((tm, da), lambda i, j: (i, 0)),
                  pl.BlockSpec((dc, tn), lambda i, j: (0, j)),
                  pl.BlockSpec((da, tn), lambda i, j: (0, j)),
                  pl.BlockSpec((tm, tn), lambda i, j: (i, j)),
                  pl.BlockSpec((1, d), lambda i, j: (0, 0)),
                  pl.BlockSpec((1, d), lambda i, j: (0, 0)),
                  pl.BlockSpec((d, LANES), lambda i, j: (0, 0)),
                  pl.BlockSpec((1, LANES), lambda i, j: (0, 0))],
        out_specs=[pl.BlockSpec((tm, d), lambda i, j: (i, 0)),
                   pl.BlockSpec((tm, d // 2), lambda i, j: (i, 0)),
                   pl.BlockSpec((tm, LANES), lambda i, j: (i, 0))],
        out_shape=[jax.ShapeDtypeStruct((n, d), F32), jax.ShapeDtypeStruct((n, d // 2), I32),
                   jax.ShapeDtypeStruct((n, LANES), F32)],
        scratch_shapes=[pltpu.VMEM((tm, d), BF16)],
        compiler_params=_cparams(("parallel", "arbitrary"), 56),
        name="out_proj_ln1",
    )(conv_out, attn_out, w_conv, w_attn, x2, ln_g, ln_b, w_router, b_router)


def _first_argmax(vals, lane, valid):
    masked = jnp.where(valid, vals, -jnp.inf)
    mx = jnp.max(masked, axis=1, keepdims=True)
    idx = jnp.min(jnp.where(valid & (masked == mx), lane, LANES), axis=1, keepdims=True)
    return mx, idx


def _router_kernel(lg_ref, tri_ref, e_ref, gate_ref, cnt_ref):
    @pl.when(pl.program_id(0) == 0)
    def _():
        cnt_ref[...] = jnp.zeros(cnt_ref.shape, F32)

    lg = lg_ref[...]
    lane = lax.broadcasted_iota(I32, lg.shape, 1)
    is_group = lane < N_GROUPS
    g_max, g_sel = _first_argmax(lg, lane, is_group)
    g_den = jnp.sum(jnp.where(is_group, jnp.exp(lg - g_max), 0.0), axis=1, keepdims=True)
    p_group = 1.0 / g_den
    lo = N_GROUPS + g_sel * EXPERTS_PER_GROUP
    in_group = (lane >= lo) & (lane < lo + EXPERTS_PER_GROUP)
    e_max, _ = _first_argmax(lg, lane, in_group)
    e_exp = jnp.where(in_group, jnp.exp(lg - e_max), 0.0)
    prob = e_exp / jnp.sum(e_exp, axis=1, keepdims=True)
    p1, i1 = _first_argmax(prob, lane, in_group)
    p2, i2 = _first_argmax(prob, lane, in_group & (lane != i1))
    norm = p_group / (p1 + p2)
    gate_ref[...] = jnp.where(lane == 0, p1 * norm, jnp.where(lane == 1, p2 * norm, 0.0))
    chosen = jnp.where((lane == i1) | (lane == i2), 1.0, 0.0)
    before = jnp.dot(tri_ref[...], chosen.astype(BF16), preferred_element_type=F32) + cnt_ref[...]
    rank1 = jnp.sum(jnp.where(lane == i1, before, 0.0), axis=1, keepdims=True).astype(I32)
    rank2 = jnp.sum(jnp.where(lane == i2, before, 0.0), axis=1, keepdims=True).astype(I32)
    cnt_ref[...] += jnp.sum(chosen, axis=0, keepdims=True)
    e_ref[...] = jnp.where(lane == 0, i1 - N_GROUPS,
                           jnp.where(lane == 1, i2 - N_GROUPS,
                                     jnp.where(lane == 2, rank1, jnp.where(lane == 3, rank2, 0))))


def _route(logits):
    n = logits.shape[0]
    tm = min(512, n)
    tri = jnp.tril(jnp.ones((tm, tm), BF16), -1)
    spec = pl.BlockSpec((tm, LANES), lambda i: (i, 0))
    return pl.pallas_call(
        _router_kernel,
        grid=(n // tm,),
        in_specs=[spec, pl.BlockSpec((tm, tm), lambda i: (0, 0))],
        out_specs=[spec, spec, pl.BlockSpec((1, LANES), lambda i: (0, 0))],
        out_shape=[jax.ShapeDtypeStruct((n, LANES), I32), jax.ShapeDtypeStruct((n, LANES), F32),
                   jax.ShapeDtypeStruct((1, LANES), F32)],
        compiler_params=_cparams(("arbitrary",), 32),
        name="router",
    )(logits, tri)


def _start_row_copies(src_hbm, idx_ref, idx_row, dst_ref, slot, row0, sem, count):
    group = 8

    def issue(i, carry):
        for k in range(group):
            r = i * group + k
            pltpu.make_async_copy(src_hbm.at[pl.ds(idx_ref[0, idx_row, r], 1), :],
                                  dst_ref.at[slot, pl.ds(row0 + r, 1), :], sem.at[slot]).start(priority=k % 2)
        return carry
    lax.fori_loop(0, count // group, issue, 0)


def _wait_slot(src_hbm, dst_ref, slot, sem):
    rows = dst_ref.shape[1]
    pltpu.make_async_copy(src_hbm.at[pl.ds(0, rows), :], dst_ref.at[slot], sem.at[slot]).wait()


def _combine_kernel(pos_ref, pos_next_ref, y_hbm, h_ref, gate_ref, g_ref, b_ref, o_ref, buf_ref, sem, *, alpha):
    i = pl.program_id(0)
    n = pl.num_programs(0)
    slot = i % 2
    rows = h_ref.shape[0]

    def start(idx_ref, into):
        for k in range(2):
            _start_row_copies(y_hbm, idx_ref, k, buf_ref, into, k * rows, sem, rows)

    @pl.when(i == 0)
    def _():
        start(pos_ref, 0)

    @pl.when(i + 1 < n)
    def _():
        start(pos_next_ref, 1 - slot)

    _wait_slot(y_hbm, buf_ref, slot, sem)

    def norm_rows(j, carry):
        r0 = pl.multiple_of(j * SUBLANES, SUBLANES)
        gates = gate_ref[pl.ds(r0, SUBLANES), :]
        ffn = (buf_ref[slot, pl.ds(r0, SUBLANES), :] * gates[:, 0:1]
               + buf_ref[slot, pl.ds(rows + r0, SUBLANES), :] * gates[:, 1:2])
        z = alpha * h_ref[pl.ds(r0, SUBLANES), :] + ffn
        o_ref[pl.ds(r0, SUBLANES), :] = _layer_norm_rows(z, g_ref[...], b_ref[...])
        return carry
    lax.fori_loop(0, rows // SUBLANES, norm_rows, 0, unroll=8)


def _combine_ln(y, pos2, h1, gate_lanes, ln_g, ln_b, alpha):
    n, d = h1.shape
    tb = min(COMBINE_ROWS, n)
    nblk = n // tb
    pos_blocks = pos2.reshape(nblk, tb, 2).transpose(0, 2, 1)
    kern = functools.partial(_combine_kernel, alpha=alpha)
    return pl.pallas_call(
        kern,
        grid=(nblk,),
        in_specs=[pl.BlockSpec((1, 2, tb), lambda i: (i, 0, 0), memory_space=pltpu.SMEM),
                  pl.BlockSpec((1, 2, tb), lambda i: (jnp.minimum(i + 1, nblk - 1), 0, 0), memory_space=pltpu.SMEM),
                  pl.BlockSpec(memory_space=pl.ANY),
                  pl.BlockSpec((tb, d), lambda i: (i, 0)),
                  pl.BlockSpec((tb, LANES), lambda i: (i, 0)),
                  pl.BlockSpec((1, d), lambda i: (0, 0)),
                  pl.BlockSpec((1, d), lambda i: (0, 0))],
        out_specs=pl.BlockSpec((tb, d), lambda i: (i, 0)),
        out_shape=jax.ShapeDtypeStruct((n, d), F32),
        scratch_shapes=[pltpu.VMEM((2, 2 * tb, d), F32), pltpu.SemaphoreType.DMA((2,))],
        compiler_params=_cparams(("arbitrary",), 32),
        name="moe_combine_ln2",
    )(pos_blocks, pos_blocks, y, h1, gate_lanes, ln_g, ln_b)


def _cast_weight(dst_ref, src_ref):
    rows = src_ref.shape[1]
    step = min(256, rows)

    def body(i, carry):
        r0 = pl.multiple_of(i * step, step)
        dst_ref[pl.ds(r0, step), :] = src_ref[0, pl.ds(r0, step), :].astype(BF16)
        return carry
    lax.fori_loop(0, rows // step, body, 0)


ITEM_COMPUTE = 1
ITEM_NEW_WEIGHTS = 2
ITEM_ZERO_FILL = 4


def _expert_hidden(words, w1b_ref, w3b_ref):
    x_lo, x_hi = _unpack_bf16_halves(words)
    half = x_lo.shape[1]

    def project(w_ref):
        return (jnp.dot(x_lo, w_ref[0:half, :], preferred_element_type=F32)
                + jnp.dot(x_hi, w_ref[half:2 * half, :], preferred_element_type=F32))

    a = project(w1b_ref)
    b = project(w3b_ref)
    return (a * jax.nn.sigmoid(a) * b).astype(BF16)


def _moe_up_gather_kernel(e_ref, cw_ref, bi_ref, bo_ref, co_ref, flag_ref, tok_ref, tok1_ref, tok2_ref, hp_hbm,
                          w1_ref, w3_ref, o_ref, xs_ref, buf_ref, w1b_ref, w3b_ref, sem):
    it = pl.program_id(0)
    last = pl.num_programs(0) - 1
    flag = flag_ref[it]
    n_slots, rows = buf_ref.shape[0], buf_ref.shape[1]
    slot = it % n_slots

    def computes(item):
        return (item <= last) & ((flag_ref[jnp.minimum(item, last)] & ITEM_COMPUTE) != 0)

    @pl.when(it == 0)
    def _():
        _start_row_copies(hp_hbm, tok_ref, 0, buf_ref, 0, 0, sem, rows)

        @pl.when(computes(1))
        def _():
            _start_row_copies(hp_hbm, tok1_ref, 0, buf_ref, 1, 0, sem, rows)

    @pl.when(computes(it + 2))
    def _():
        _start_row_copies(hp_hbm, tok2_ref, 0, buf_ref, (it + 2) % n_slots, 0, sem, rows)

    @pl.when((flag & ITEM_NEW_WEIGHTS) != 0)
    def _():
        _cast_weight(w1b_ref, w1_ref)
        _cast_weight(w3b_ref, w3_ref)

    @pl.when((flag & ITEM_COMPUTE) != 0)
    def _():
        _wait_slot(hp_hbm, buf_ref, slot, sem)
        words = buf_ref[slot]
        xs_ref[...] = words
        o_ref[...] = _expert_hidden(words, w1b_ref, w3b_ref)

    @pl.when((flag & ITEM_ZERO_FILL) != 0)
    def _():
        o_ref[...] = jnp.zeros(o_ref.shape, o_ref.dtype)
        xs_ref[...] = jnp.zeros(xs_ref.shape, xs_ref.dtype)


def _moe_up_kernel(e_ref, cw_ref, bi_ref, bo_ref, co_ref, flag_ref, x_ref, w1_ref, w3_ref, o_ref, w1b_ref, w3b_ref):
    flag = flag_ref[pl.program_id(0)]

    @pl.when((flag & ITEM_NEW_WEIGHTS) != 0)
    def _():
        _cast_weight(w1b_ref, w1_ref)
        _cast_weight(w3b_ref, w3_ref)

    @pl.when((flag & ITEM_COMPUTE) != 0)
    def _():
        o_ref[...] = _expert_hidden(x_ref[...], w1b_ref, w3b_ref)

    @pl.when((flag & ITEM_ZERO_FILL) != 0)
    def _():
        o_ref[...] = jnp.zeros(o_ref.shape, o_ref.dtype)


def _moe_down_kernel(e_ref, cw_ref, bi_ref, bo_ref, co_ref, flag_ref, h0_ref, h1_ref, w2_ref, o_ref, w2b_ref):
    flag = flag_ref[pl.program_id(0)]

    @pl.when((flag & ITEM_NEW_WEIGHTS) != 0)
    def _():
        _cast_weight(w2b_ref, w2_ref)

    @pl.when((flag & ITEM_COMPUTE) != 0)
    def _():
        fc = h0_ref.shape[1]
        o_ref[...] = (jnp.dot(h0_ref[...], w2b_ref[0:fc, :], preferred_element_type=F32)
                      + jnp.dot(h1_ref[...], w2b_ref[fc:2 * fc, :], preferred_element_type=F32))

    @pl.when((flag & ITEM_ZERO_FILL) != 0)
    def _():
        o_ref[...] = jnp.zeros(o_ref.shape, o_ref.dtype)


def _work_items(blocks_per_expert, block_start, n_chunks, n_blocks):
    n_items = n_chunks * n_blocks
    per_e = n_chunks * blocks_per_expert
    end = jnp.cumsum(per_e)
    start = end - per_e
    it = jnp.arange(n_items, dtype=I32)
    total = end[-1]
    used_blocks = total // n_chunks
    itc = jnp.minimum(it, total - 1)
    e = jnp.sum(itc[:, None] >= end[None, :], axis=1).astype(I32)
    local = itc - start[e]
    nb = jnp.maximum(blocks_per_expert[e], 1)
    c = local // nb
    r = local - c * nb
    b_in = block_start[e] + r
    active = it < total
    spare = jnp.maximum(it - total, 0)
    b_out = jnp.where(active, b_in, used_blocks + spare // n_chunks)
    c_out = jnp.where(active, c, spare % n_chunks)
    first = active & (r == 0)
    flag = jnp.where(active, ITEM_COMPUTE + ITEM_NEW_WEIGHTS * first.astype(I32), ITEM_ZERO_FILL)
    later_first = lax.cummin(jnp.where(first, it, n_items)[::-1])[::-1]
    next_start = jnp.concatenate([later_first[1:], jnp.full((1,), n_items, I32)])
    ahead = jnp.minimum(next_start, n_items - 1)
    use_next = active & ~first & (next_start < n_items)
    e_w = jnp.where(use_next, e[ahead], e)
    c_w = jnp.where(use_next, c[ahead], c)
    return tuple(v.astype(I32) for v in (e_w, c_w, b_in, b_out, c_out, flag))


def _moe_up_gather(hp, buf_tok, w1, w3, items):
    p = buf_tok.shape[0]
    half = hp.shape[1]
    d, f = w1.shape[1], w1.shape[2]
    fc = f // 2
    n_items = items[0].shape[0]
    nblk = p // MOE_ROWS
    tok_blocks = buf_tok.reshape(nblk, 1, MOE_ROWS)
    wspec = pl.BlockSpec((1, d, fc), lambda it, e, cw, bi, bo, co, fl: (e[it], 0, 0))
    lookahead = 2

    def tok_spec(ahead):
        return pl.BlockSpec((1, 1, MOE_ROWS),
                            lambda it, e, cw, bi, bo, co, fl: (bi[jnp.minimum(it + ahead, n_items - 1)], 0, 0),
                            memory_space=pltpu.SMEM)

    return pl.pallas_call(
        _moe_up_gather_kernel,
        grid_spec=pltpu.PrefetchScalarGridSpec(
            num_scalar_prefetch=6,
            grid=(n_items,),
            in_specs=[tok_spec(k) for k in range(lookahead + 1)] + [pl.BlockSpec(memory_space=pl.ANY), wspec, wspec],
            out_specs=[pl.BlockSpec((MOE_ROWS, fc), lambda it, e, cw, bi, bo, co, fl: (bo[it], 0)),
                       pl.BlockSpec((MOE_ROWS, half), lambda it, e, cw, bi, bo, co, fl: (bo[it], 0))],
            scratch_shapes=[pltpu.VMEM((lookahead + 1, MOE_ROWS, half), I32), pltpu.VMEM((d, fc), BF16),
                            pltpu.VMEM((d, fc), BF16), pltpu.SemaphoreType.DMA((lookahead + 1,))]),
        out_shape=[jax.ShapeDtypeStruct((p, fc), BF16), jax.ShapeDtypeStruct((p, half), I32)],
        compiler_params=_cparams(("arbitrary",), 56),
        name="moe_up_gather",
    )(*items, *([tok_blocks] * (lookahead + 1)), hp, w1, w3)


def _moe_up(xs, w1, w3, items):
    p, half = xs.shape
    d, f = w1.shape[1], w1.shape[2]
    fc = f // 2
    assert d == 2 * half
    wspec = pl.BlockSpec((1, d, fc), lambda it, e, cw, bi, bo, co, fl: (e[it], 0, 1))
    return pl.pallas_call(
        _moe_up_kernel,
        grid_spec=pltpu.PrefetchScalarGridSpec(
            num_scalar_prefetch=6,
            grid=(items[0].shape[0],),
            in_specs=[pl.BlockSpec((MOE_ROWS, half), lambda it, e, cw, bi, bo, co, fl: (bi[it], 0)), wspec, wspec],
            out_specs=pl.BlockSpec((MOE_ROWS, fc), lambda it, e, cw, bi, bo, co, fl: (bo[it], 0)),
            scratch_shapes=[pltpu.VMEM((d, fc), BF16), pltpu.VMEM((d, fc), BF16)]),
        out_shape=jax.ShapeDtypeStruct((p, fc), BF16),
        compiler_params=_cparams(("arbitrary",), 56),
        name="moe_up",
    )(*items, xs, w1, w3)


def _moe_down(h0, h1, w2, items, d_chunk):
    p, fc = h0.shape
    f = 2 * fc
    d = w2.shape[2]
    hspec = pl.BlockSpec((MOE_ROWS, fc), lambda it, e, cw, bi, bo, co, fl: (bi[it], 0))
    return pl.pallas_call(
        _moe_down_kernel,
        grid_spec=pltpu.PrefetchScalarGridSpec(
            num_scalar_prefetch=6,
            grid=(items[0].shape[0],),
            in_specs=[hspec, hspec,
                      pl.BlockSpec((1, f, d_chunk), lambda it, e, cw, bi, bo, co, fl: (e[it], 0, cw[it]))],
            out_specs=pl.BlockSpec((MOE_ROWS, d_chunk), lambda it, e, cw, bi, bo, co, fl: (bo[it], co[it])),
            scratch_shapes=[pltpu.VMEM((f, d_chunk), BF16)]),
        out_shape=jax.ShapeDtypeStruct((p, d), F32),
        compiler_params=_cparams(("arbitrary",), 48),
        name="moe_down",
    )(*items, h0, h1, w2)


def _hier_moe_ln(h1, hp, logits, w1, w3, w2, ln_g, ln_b, alpha):
    n, d = h1.shape
    e_lanes, gate_lanes, lane_counts = _route(logits)
    a = 2 * n
    e_flat = e_lanes[:, 0:2].reshape(a)
    rank = e_lanes[:, 2:4].reshape(a)
    counts = lane_counts[0, N_GROUPS:N_GROUPS + N_EXPERTS].astype(I32)
    blocks_per_expert = (counts + MOE_ROWS - 1) // MOE_ROWS
    block_start = jnp.cumsum(blocks_per_expert) - blocks_per_expert
    onehot = e_flat[:, None] == jnp.arange(N_EXPERTS, dtype=I32)[None, :]
    pos = jnp.sum(jnp.where(onehot, block_start[None, :], 0), axis=1) * MOE_ROWS + rank
    n_blocks = (a + MOE_ROWS - 1) // MOE_ROWS + N_EXPERTS
    buf_tok = jnp.zeros((n_blocks * MOE_ROWS,), I32).at[pos].set(jnp.arange(a, dtype=I32) // 2)

    d_chunk = min(2048, d)
    up_items = _work_items(blocks_per_expert, block_start, 1, n_blocks)
    h0, xs = _moe_up_gather(hp, buf_tok, w1, w3, up_items)
    h1_mid = _moe_up(xs, w1, w3, up_items)
    y = _moe_down(h0, h1_mid, w2, _work_items(blocks_per_expert, block_start, d // d_chunk, n_blocks), d_chunk)
    return _combine_ln(y, pos.reshape(n, 2), h1, gate_lanes, ln_g, ln_b, alpha)


def kernel(x, w_in, conv_dw_w, conv_dw_b, conv_ln_g, conv_ln_b, w_out, ln1_g, ln1_b, w_router_group, b_router_group, w_router_expert, b_router_expert, w_expert_gate, w_expert_up, w_expert_down, ln2_g, ln2_b):
    b, s, d = x.shape
    depth = w_in.shape[0]
    assert depth == 1
    alpha = float((2.0 * depth) ** 0.25)
    n = b * s
    dc = conv_dw_w.shape[2]
    d_attn = N_HEADS * HEAD_DIM
    d_kv = N_KV_HEADS * HEAD_DIM
    d_qi = IDX_HEADS * IDX_DIM
    n_small = IDX_DIM + IDX_HEADS
    assert w_in.shape[2] == 2 * dc + d_attn + 2 * d_kv + d_qi + n_small

    x2 = x.reshape(n, d)
    w = w_in[0].astype(BF16)
    o_qkvi = 2 * dc
    o_small = o_qkvi + d_attn + 2 * d_kv + d_qi
    w_small = jnp.pad(w[:, o_small:], ((0, 0), (0, LANES - n_small)))
    qkvi_scale = jnp.concatenate([jnp.full((d_attn,), HEAD_DIM ** -0.5 * LOG2E, F32), jnp.ones((2 * d_kv,), F32),
                                  jnp.full((d_qi,), IDX_DIM ** -0.5, F32)])[None, :]

    xb, small = _proj_idx_and_cast(x2, w_small)
    glu = _proj_glu(xb, w, dc)
    qkvi = _proj_scale(xb, w, o_qkvi, o_small - o_qkvi, qkvi_scale, BF16, "proj_qkvi")

    conv_out = _conformer_conv(glu.reshape(b, s, dc), conv_dw_w[0], conv_dw_b, conv_ln_g, conv_ln_b)
    attn_out = _dsa_attention(qkvi.reshape(b, s, -1), small.reshape(b, s, LANES), b, s)

    n_route = N_GROUPS + N_EXPERTS
    w_router = jnp.pad(jnp.concatenate([w_router_group[0], w_router_expert[0]], axis=1),
                       ((0, 0), (0, LANES - n_route))).astype(BF16)
    b_router = jnp.pad(jnp.concatenate([b_router_group[0], b_router_expert[0]]), (0, LANES - n_route))[None, :]
    h1, hp, logits = _out_proj_ln(conv_out.reshape(n, dc), attn_out.reshape(n, d_attn),
                                  w_out[0, :dc].astype(BF16), w_out[0, dc:].astype(BF16),
                                  x2, ln1_g, ln1_b, w_router, b_router, alpha)

    out = _hier_moe_ln(h1, hp, logits, w_expert_gate[0], w_expert_up[0], w_expert_down[0], ln2_g, ln2_b, alpha)
    return out.reshape(b, s, d)
```

```python
import functools
import math

import jax
import jax.numpy as jnp
from jax import lax
from jax.experimental import pallas as pl
from jax.experimental.pallas import tpu as pltpu

F32 = jnp.float32
BF16 = jnp.bfloat16
I32 = jnp.int32

CONV_WIDTH = 31
N_HEADS = 16
HEAD_DIM = 128
N_KV_HEADS = 4
IDX_HEADS = 16
IDX_DIM = 64
TOPK_MAX = 256
N_GROUPS = 4
EXPERTS_PER_GROUP = 8
N_EXPERTS = N_GROUPS * EXPERTS_PER_GROUP
LN_EPS = 1e-5

LANES = 128
SUBLANES = 8
BF16_ROWS = 16
MIB = 1024 * 1024

Q_TILE = 128
KEY_CHUNK = 512
CONV_HALO = 32
MOE_ROWS = 256
COMBINE_ROWS = 128
LN_GROUPS = 4
INT_MIN = -(2 ** 31)
INT_MAX = 2 ** 31 - 1
LOG2E = math.log2(math.e)
NEG_BIG = -1e30
M_INIT = -1e20


def _cparams(semantics, vmem_mib):
    return pltpu.CompilerParams(dimension_semantics=semantics, vmem_limit_bytes=vmem_mib * MIB)


def _layer_norm_rows(z, g, b):
    mu = jnp.mean(z, axis=-1, keepdims=True)
    zc = z - mu
    var = jnp.mean(zc * zc, axis=-1, keepdims=True)
    return zc * lax.rsqrt(var + LN_EPS) * g + b


def _mm_glu_kernel(x_ref, wa_ref, wg_ref, o_ref):
    x = x_ref[...]
    a = jnp.dot(x, wa_ref[...], preferred_element_type=F32)
    g = jnp.dot(x, wg_ref[...], preferred_element_type=F32)
    o_ref[...] = (a * jax.nn.sigmoid(g)).astype(o_ref.dtype)


def _mm_scale_kernel(x_ref, w_ref, s_ref, o_ref):
    acc = jnp.dot(x_ref[...], w_ref[...], preferred_element_type=F32)
    o_ref[...] = (acc * s_ref[...]).astype(o_ref.dtype)


def _mm_tiles(n, k, cols):
    tm = min(1024, n)
    tn = min(512, cols)
    assert n % tm == 0 and cols % tn == 0
    return tm, tn


def _proj_glu(xb, w, cols):
    n, k = xb.shape
    tm, tn = _mm_tiles(n, k, cols)
    gate_block = cols // tn
    return pl.pallas_call(
        _mm_glu_kernel,
        grid=(n // tm, cols // tn),
        in_specs=[pl.BlockSpec((tm, k), lambda i, j: (i, 0)),
                  pl.BlockSpec((k, tn), lambda i, j: (0, j)),
                  pl.BlockSpec((k, tn), lambda i, j: (0, j + gate_block))],
        out_specs=pl.BlockSpec((tm, tn), lambda i, j: (i, j)),
        out_shape=jax.ShapeDtypeStruct((n, cols), F32),
        compiler_params=_cparams(("parallel", "arbitrary"), 48),
        name="proj_glu",
    )(xb, w, w)


def _proj_scale(xb, w, first_col, cols, scale, out_dtype, name):
    n, k = xb.shape
    tm, tn = _mm_tiles(n, k, cols)
    assert first_col % tn == 0
    first_block = first_col // tn
    return pl.pallas_call(
        _mm_scale_kernel,
        grid=(n // tm, cols // tn),
        in_specs=[pl.BlockSpec((tm, k), lambda i, j: (i, 0)),
                  pl.BlockSpec((k, tn), lambda i, j: (0, j + first_block)),
                  pl.BlockSpec((1, tn), lambda i, j: (0, j))],
        out_specs=pl.BlockSpec((tm, tn), lambda i, j: (i, j)),
        out_shape=jax.ShapeDtypeStruct((n, cols), out_dtype),
        compiler_params=_cparams(("parallel", "arbitrary"), 48),
        name=name,
    )(xb, w, scale)


def _idx_cast_kernel(x_ref, w_ref, xb_ref, o_ref):
    xb = x_ref[...].astype(BF16)
    xb_ref[...] = xb
    o_ref[...] = jnp.dot(xb, w_ref[...], preferred_element_type=F32)


def _proj_idx_and_cast(x2, w_small):
    n, k = x2.shape
    tm = min(256, n)
    return pl.pallas_call(
        _idx_cast_kernel,
        grid=(n // tm,),
        in_specs=[pl.BlockSpec((tm, k), lambda i: (i, 0)),
                  pl.BlockSpec((k, LANES), lambda i: (0, 0))],
        out_specs=[pl.BlockSpec((tm, k), lambda i: (i, 0)),
                   pl.BlockSpec((tm, LANES), lambda i: (i, 0))],
        out_shape=[jax.ShapeDtypeStruct((n, k), BF16), jax.ShapeDtypeStruct((n, LANES), F32)],
        compiler_params=_cparams(("parallel",), 32),
        name="proj_idx",
    )(x2, w_small)


def _conv_kernel(cur_ref, prev_ref, w_ref, b_ref, g_ref, beta_ref, o_ref, sh_ref, y_ref, *, rows, lane_chunk):
    t = pl.program_id(1)
    dc = cur_ref.shape[2]
    row_sub = 32
    sh_ref[0, 0:CONV_HALO, :] = jnp.where(t > 0, prev_ref[0], 0.0)
    sh_ref[0, CONV_HALO:CONV_HALO + rows, :] = cur_ref[0]
    first = CONV_HALO - (CONV_WIDTH - 1)
    shifted_rows = rows + CONV_HALO - SUBLANES
    for lc in range(dc // lane_chunk):
        cols = slice(lc * lane_chunk, (lc + 1) * lane_chunk)
        for r in range(1, SUBLANES):
            sh_ref[r, 0:shifted_rows, cols] = sh_ref[0, r:r + shifted_rows, cols]

    for r0 in range(0, rows, row_sub):
        for lc in range(dc // lane_chunk):
            cols = slice(lc * lane_chunk, (lc + 1) * lane_chunk)
            acc = jnp.zeros((row_sub, lane_chunk), F32)
            for j in range(CONV_WIDTH):
                shift, base = (first + j) % SUBLANES, (first + j) // SUBLANES * SUBLANES
                acc = acc + sh_ref[shift, r0 + base:r0 + base + row_sub, cols] * w_ref[j:j + 1, cols]
            y_ref[r0:r0 + row_sub, cols] = acc + b_ref[:, cols]

    def norm_rows(i, carry):
        r0 = pl.multiple_of(i * BF16_ROWS, BF16_ROWS)
        yn = _layer_norm_rows(y_ref[pl.ds(r0, BF16_ROWS), :], g_ref[...], beta_ref[...])
        o_ref[0, pl.ds(r0, BF16_ROWS), :] = (yn * jax.nn.sigmoid(yn)).astype(o_ref.dtype)
        return carry

    lax.fori_loop(0, rows // BF16_ROWS, norm_rows, 0, unroll=4)


def _conformer_conv(glu, w_dw, b_dw, g_ln, b_ln):
    b, s, dc = glu.shape
    rows = min(128, s)
    halo_blocks = rows // CONV_HALO
    lane_chunk = min(512, dc)
    kern = functools.partial(_conv_kernel, rows=rows, lane_chunk=lane_chunk)
    return pl.pallas_call(
        kern,
        grid=(b, s // rows),
        in_specs=[pl.BlockSpec((1, rows, dc), lambda bi, t: (bi, t, 0)),
                  pl.BlockSpec((1, CONV_HALO, dc), lambda bi, t: (bi, jnp.maximum(t * halo_blocks - 1, 0), 0)),
                  pl.BlockSpec((CONV_WIDTH, dc), lambda bi, t: (0, 0)),
                  pl.BlockSpec((1, dc), lambda bi, t: (0, 0)),
                  pl.BlockSpec((1, dc), lambda bi, t: (0, 0)),
                  pl.BlockSpec((1, dc), lambda bi, t: (0, 0))],
        out_specs=pl.BlockSpec((1, rows, dc), lambda bi, t: (bi, t, 0)),
        out_shape=jax.ShapeDtypeStruct((b, s, dc), BF16),
        scratch_shapes=[pltpu.VMEM((SUBLANES, CONV_HALO + rows, dc), F32), pltpu.VMEM((rows, dc), F32)],
        compiler_params=_cparams(("parallel", "arbitrary"), 32),
        name="conformer_conv",
    )(glu, glu, w_dw, b_dw, g_ln, b_ln)


def _dsa_kernel(q_ref, qi_ref, k_ref, v_ref, kidx_ref, w_ref, o_ref,
                ke_ref, ko_ref, vt_ref, key_ref, cut_ref, nd_ref, qs_ref, qis_ref, m_ref, l_ref, acc_ref,
                *, seq, chunk, topk):
    tb = pl.program_id(1)
    t0 = tb * Q_TILE
    n_chunks = (t0 + Q_TILE + chunk - 1) // chunk
    rep = N_HEADS // N_KV_HEADS
    nt = (((1,), (1,)), ((), ()))

    @pl.when(tb == 0)
    def _():
        def build(c, carry):
            r0 = pl.multiple_of(c * chunk, chunk)
            kx = kidx_ref[0, pl.ds(r0, chunk), :]
            lane = lax.broadcasted_iota(I32, kx.shape, 1)
            ke_ref[pl.ds(r0, chunk), :] = jnp.where(lane < IDX_DIM, kx, 0.0).astype(BF16)
            ko_ref[pl.ds(r0, chunk), :] = jnp.where(lane >= IDX_DIM, pltpu.roll(kx, IDX_DIM, 1), 0.0).astype(BF16)
            for g in range(N_KV_HEADS):
                vg = v_ref[0, pl.ds(r0, chunk), g * HEAD_DIM:(g + 1) * HEAD_DIM]
                vt_ref[c, g * HEAD_DIM:(g + 1) * HEAD_DIM, :] = vg.astype(F32).T.astype(BF16)
            return carry
        lax.fori_loop(0, seq // chunk, build, 0)

    for g in range(N_KV_HEADS):
        for r in range(rep):
            h = g * rep + r
            qs_ref[g, r * Q_TILE:(r + 1) * Q_TILE, :] = q_ref[0, :, h * HEAD_DIM:(h + 1) * HEAD_DIM]
    for jj in range(IDX_HEADS // 4):
        for half in range(2):
            pair = 2 * jj + half
            qis_ref[jj, half * Q_TILE:(half + 1) * Q_TILE, :] = qi_ref[0, :, pair * LANES:(pair + 1) * LANES]
    w_t = w_ref[0].T * (IDX_HEADS ** -0.5)

    key_row = lax.broadcasted_iota(I32, (chunk, Q_TILE), 0)
    q_pos = t0 + lax.broadcasted_iota(I32, (chunk, Q_TILE), 1)

    def score_chunk(c, carry):
        r0 = pl.multiple_of(c * chunk, chunk)
        ke = ke_ref[pl.ds(r0, chunk), :]
        ko = ko_ref[pl.ds(r0, chunk), :]
        acc = jnp.zeros((chunk, Q_TILE), F32)
        for jj in range(IDX_HEADS // 4):
            rhs = qis_ref[jj]
            de = lax.dot_general(ke, rhs, nt, preferred_element_type=F32)
            do = lax.dot_general(ko, rhs, nt, preferred_element_type=F32)
            for half in range(2):
                h_even = 2 * (2 * jj + half)
                cols = slice(half * Q_TILE, (half + 1) * Q_TILE)
                acc = acc + w_t[IDX_DIM + h_even:IDX_DIM + h_even + 1, :] * jnp.maximum(de[:, cols], 0.0)
                acc = acc + w_t[IDX_DIM + h_even + 1:IDX_DIM + h_even + 2, :] * jnp.maximum(do[:, cols], 0.0)
        bits = lax.bitcast_convert_type(acc, I32)
        key = bits ^ ((bits >> 31) & 0x7FFFFFFF)
        key_ref[c] = jnp.where(r0 + key_row <= q_pos, key, INT_MIN)
        return carry

    lax.fori_loop(0, n_chunks, score_chunk, 0)

    def count_ge(cand):
        def count_chunk(c, cnts):
            cnts = list(cnts)
            for r in range(chunk // SUBLANES):
                slab = key_ref[c, r * SUBLANES:(r + 1) * SUBLANES, :]
                cnts[r % len(cnts)] = cnts[r % len(cnts)] + jnp.where(slab >= cand, 1.0, 0.0)
            return tuple(cnts)

        zero = jnp.zeros((SUBLANES, Q_TILE), F32)
        cnts = lax.fori_loop(0, n_chunks, count_chunk, (zero, zero, zero, zero))
        return jnp.sum((cnts[0] + cnts[1]) + (cnts[2] + cnts[3]), axis=0, keepdims=True)

    def bit_step(i, t_u):
        cand_u = t_u | (jnp.int32(1) << (31 - i))
        return jnp.where(count_ge(cand_u ^ INT_MIN) >= float(topk), cand_u, t_u)

    t_u = lax.fori_loop(0, 32, bit_step, jnp.zeros((1, Q_TILE), I32))
    thr = jnp.maximum(t_u ^ INT_MIN, INT_MIN + 1)

    n_ge = count_ge(thr)
    cut_ref[...] = jnp.full(cut_ref.shape, INT_MAX, I32)

    @pl.when(jnp.max(n_ge) > topk)
    def _():
        def count_where(pred):
            def count_chunk(c, cnt):
                r0 = pl.multiple_of(c * chunk, chunk)
                hit = jnp.where(pred(key_ref[c], r0 + key_row), 1, 0)
                return cnt + jnp.sum(hit, axis=0, keepdims=True)
            return lax.fori_loop(0, n_chunks, count_chunk, jnp.zeros((1, Q_TILE), I32))

        keep = topk - count_where(lambda kc, pos: kc > thr)
        pos_bits = seq.bit_length()

        def bit_step(i, cut):
            cand = cut | (jnp.int32(1) << (pos_bits - 1 - i))
            taken = count_where(lambda kc, pos: jnp.where(kc == thr, pos, INT_MAX) < cand)
            return jnp.where(taken <= keep, cand, cut)
        cut_ref[...] = lax.fori_loop(0, pos_bits, bit_step, jnp.zeros((1, Q_TILE), I32))

    cut = cut_ref[...]

    m_ref[...] = jnp.full(m_ref.shape, M_INIT, F32)
    l_ref[...] = jnp.zeros(l_ref.shape, F32)
    acc_ref[...] = jnp.zeros(acc_ref.shape, F32)

    def attend_chunk(c, carry):
        r0 = pl.multiple_of(c * chunk, chunk)
        pos = r0 + key_row
        kc = key_ref[c]
        rank_pos = jnp.where(kc > thr, -1, jnp.where(kc == thr, pos, INT_MAX))
        nd_ref[...] = jnp.where(rank_pos < cut, (pos - q_pos).astype(F32), NEG_BIG)
        def logits(g):
            kg = k_ref[0, pl.ds(r0, chunk), g * HEAD_DIM:(g + 1) * HEAD_DIM]
            return lax.dot_general(kg, qs_ref[g], nt, preferred_element_type=F32)

        s_next = logits(0)
        for g in range(N_KV_HEADS):
            s_all = s_next
            if g + 1 < N_KV_HEADS:
                s_next = logits(g + 1)
            probs = []
            alphas = []
            for r in range(rep):
                h = g * rep + r
                slope = float(2.0 ** (-8.0 * (h + 1) / N_HEADS)) * LOG2E
                sr = s_all[:, r * Q_TILE:(r + 1) * Q_TILE] + slope * nd_ref[...]
                m_old = m_ref[h]
                m_new = jnp.maximum(m_old, jnp.max(sr, axis=0, keepdims=True))
                alpha = jnp.exp2(m_old - m_new)
                p = jnp.exp2(sr - m_new)
                l_ref[h] = alpha * l_ref[h] + jnp.sum(p, axis=0, keepdims=True)
                m_ref[h] = m_new
                probs.append(p.astype(BF16))
                alphas.append(alpha)
            vt = vt_ref[c, g * HEAD_DIM:(g + 1) * HEAD_DIM, :]
            pv = jnp.dot(vt, jnp.concatenate(probs, axis=1), preferred_element_type=F32)
            acc_ref[g] = jnp.concatenate(alphas, axis=1) * acc_ref[g] + pv
        return carry

    lax.fori_loop(0, n_chunks, attend_chunk, 0)

    for g in range(N_KV_HEADS):
        for r in range(rep):
            h = g * rep + r
            o_t = acc_ref[g, :, r * Q_TILE:(r + 1) * Q_TILE] * (1.0 / l_ref[h])
            o_ref[0, :, h * HEAD_DIM:(h + 1) * HEAD_DIM] = o_t.T.astype(o_ref.dtype)


def _dsa_attention(qkvi, small, b, s):
    d_attn = N_HEADS * HEAD_DIM
    d_kv = N_KV_HEADS * HEAD_DIM
    d_qi = IDX_HEADS * IDX_DIM
    rep = N_HEADS // N_KV_HEADS
    chunk = min(KEY_CHUNK, s)
    topk = min(TOPK_MAX, s // 4)
    assert s % chunk == 0 and chunk % Q_TILE == 0 and d_attn % d_qi == 0 and d_attn % d_kv == 0
    assert Q_TILE == LANES and HEAD_DIM == LANES and 2 * IDX_DIM == LANES
    kern = functools.partial(_dsa_kernel, seq=s, chunk=chunk, topk=topk)
    return pl.pallas_call(
        kern,
        grid=(b, s // Q_TILE),
        in_specs=[pl.BlockSpec((1, Q_TILE, d_attn), lambda bi, t: (bi, t, 0)),
                  pl.BlockSpec((1, Q_TILE, d_qi), lambda bi, t: (bi, t, (d_attn + 2 * d_kv) // d_qi)),
                  pl.BlockSpec((1, s, d_kv), lambda bi, t: (bi, 0, d_attn // d_kv)),
                  pl.BlockSpec((1, s, d_kv), lambda bi, t: (bi, 0, d_attn // d_kv + 1)),
                  pl.BlockSpec((1, s, LANES), lambda bi, t: (bi, 0, 0)),
                  pl.BlockSpec((1, Q_TILE, LANES), lambda bi, t: (bi, t, 0))],
        out_specs=pl.BlockSpec((1, Q_TILE, d_attn), lambda bi, t: (bi, t, 0)),
        out_shape=jax.ShapeDtypeStruct((b, s, d_attn), BF16),
        scratch_shapes=[pltpu.VMEM((s, LANES), BF16),
                        pltpu.VMEM((s, LANES), BF16),
                        pltpu.VMEM((s // chunk, d_kv, chunk), BF16),
                        pltpu.VMEM((s // chunk, chunk, Q_TILE), I32),
                        pltpu.VMEM((1, Q_TILE), I32),
                        pltpu.VMEM((chunk, Q_TILE), F32),
                        pltpu.VMEM((N_KV_HEADS, rep * Q_TILE, HEAD_DIM), BF16),
                        pltpu.VMEM((IDX_HEADS // 4, 2 * Q_TILE, LANES), BF16),
                        pltpu.VMEM((N_HEADS, 1, Q_TILE), F32),
                        pltpu.VMEM((N_HEADS, 1, Q_TILE), F32),
                        pltpu.VMEM((N_KV_HEADS, HEAD_DIM, rep * Q_TILE), F32)],
        compiler_params=_cparams(("parallel", "arbitrary"), 48),
        name="dsa_attention",
    )(qkvi, qkvi, qkvi, qkvi, small, small)


def _pack_bf16_halves(hb):
    half = hb.shape[1] // 2
    bits = lax.bitcast_convert_type(hb.astype(F32), I32)
    return lax.shift_right_logical(bits[:, :half], 16) | (bits[:, half:] & -65536)


def _unpack_bf16_halves(words):
    lo = lax.bitcast_convert_type(words << 16, F32).astype(BF16)
    hi = lax.bitcast_convert_type(words & -65536, F32).astype(BF16)
    return lo, hi


def _outproj_kernel(conv_ref, attn_ref, wc_ref, wa_ref, x_ref, g_ref, b_ref, wr_ref, br_ref,
                    h_ref, hp_ref, lg_ref, hb_ref, *, alpha):
    j = pl.program_id(1)
    tm, d = h_ref.shape
    tn = x_ref.shape[1]
    n_tiles = d // tn
    mix = jnp.dot(conv_ref[...], wc_ref[...], preferred_element_type=F32)
    mix = mix + jnp.dot(attn_ref[...], wa_ref[...], preferred_element_type=F32)
    z = alpha * x_ref[...] + mix
    for jj in range(n_tiles):
        @pl.when(j == jj)
        def _(jj=jj):
            h_ref[:, jj * tn:(jj + 1) * tn] = z

    @pl.when(j == n_tiles - 1)
    def _():
        def rows(i, carry):
            starts = [pl.multiple_of((LN_GROUPS * i + k) * BF16_ROWS, BF16_ROWS) for k in range(LN_GROUPS)]
            zs = [h_ref[pl.ds(r0, BF16_ROWS), :] for r0 in starts]
            hs = [_layer_norm_rows(z, g_ref[...], b_ref[...]) for z in zs]
            for r0, h in zip(starts, hs):
                h_ref[pl.ds(r0, BF16_ROWS), :] = h
                hb_ref[pl.ds(r0, BF16_ROWS), :] = h.astype(BF16)
            return carry
        lax.fori_loop(0, tm // (LN_GROUPS * BF16_ROWS), rows, 0)

        def pack_rows(i, carry):
            r0 = pl.multiple_of(i * BF16_ROWS, BF16_ROWS)
            hp_ref[pl.ds(r0, BF16_ROWS), :] = _pack_bf16_halves(hb_ref[pl.ds(r0, BF16_ROWS), :])
            return carry
        lax.fori_loop(0, tm // BF16_ROWS, pack_rows, 0, unroll=2)
        lg_ref[...] = jnp.dot(hb_ref[...], wr_ref[...], preferred_element_type=F32) + br_ref[...]


def _out_proj_ln(conv_out, attn_out, w_conv, w_attn, x2, ln_g, ln_b, w_router, b_router, alpha):
    n, d = x2.shape
    dc = conv_out.shape[1]
    da = attn_out.shape[1]
    tm = min(512, n)
    tn = min(512, d)
    assert n % tm == 0 and d % tn == 0
    kern = functools.partial(_outproj_kernel, alpha=alpha)
    return pl.pallas_call(
        kern,
        grid=(n // tm, d // tn),
        in_specs=[pl.BlockSpec((tm, dc), lambda i, j: (i, 0)),
                  pl.BlockSpec((tm, da), lambda i, j: (i, 0)),
                  pl.BlockSpec((dc, tn), lambda i, j: (0, j)),
                  pl.BlockSpec((da, tn), lambda i, j: (0, j)),
                  pl.BlockSpec((tm, tn), lambda i, j: (i, j)),
                  pl.BlockSpec((1, d), lambda i, j: (0, 0)),
                  pl.BlockSpec((1, d), lambda i, j: (0, 0)),
                  pl.BlockSpec((d, LANES), lambda i, j: (0, 0)),
                  pl.BlockSpec((1, LANES), lambda i, j: (0, 0))],
        out_specs=[pl.BlockSpec((tm, d), lambda i, j: (i, 0)),
                   pl.BlockSpec((tm, d // 2), lambda i, j: (i, 0)),
                   pl.BlockSpec((tm, LANES), lambda i, j: (i, 0))],
        out_shape=[jax.ShapeDtypeStruct((n, d), F32), jax.ShapeDtypeStruct((n, d // 2), I32),
                   jax.ShapeDtypeStruct((n, LANES), F32)],
        scratch_shapes=[pltpu.VMEM((tm, d), BF16)],
        compiler_params=_cparams(("parallel", "arbitrary"), 56),
        name="out_proj_ln1",
    )(conv_out, attn_out, w_conv, w_attn, x2, ln_g, ln_b, w_router, b_router)


def _first_argmax(vals, lane, valid):
    masked = jnp.where(valid, vals, -jnp.inf)
    mx = jnp.max(masked, axis=1, keepdims=True)
    idx = jnp.min(jnp.where(valid & (masked == mx), lane, LANES), axis=1, keepdims=True)
    return mx, idx


def _router_kernel(lg_ref, tri_ref, e_ref, gate_ref, cnt_ref):
    @pl.when(pl.program_id(0) == 0)
    def _():
        cnt_ref[...] = jnp.zeros(cnt_ref.shape, F32)

    lg = lg_ref[...]
    lane = lax.broadcasted_iota(I32, lg.shape, 1)
    is_group = lane < N_GROUPS
    g_max, g_sel = _first_argmax(lg, lane, is_group)
    g_den = jnp.sum(jnp.where(is_group, jnp.exp(lg - g_max), 0.0), axis=1, keepdims=True)
    p_group = 1.0 / g_den
    lo = N_GROUPS + g_sel * EXPERTS_PER_GROUP
    in_group = (lane >= lo) & (lane < lo + EXPERTS_PER_GROUP)
    e_max, _ = _first_argmax(lg, lane, in_group)
    e_exp = jnp.where(in_group, jnp.exp(lg - e_max), 0.0)
    prob = e_exp / jnp.sum(e_exp, axis=1, keepdims=True)
    p1, i1 = _first_argmax(prob, lane, in_group)
    p2, i2 = _first_argmax(prob, lane, in_group & (lane != i1))
    norm = p_group / (p1 + p2)
    gate_ref[...] = jnp.where(lane == 0, p1 * norm, jnp.where(lane == 1, p2 * norm, 0.0))
    chosen = jnp.where((lane == i1) | (lane == i2), 1.0, 0.0)
    before = jnp.dot(tri_ref[...], chosen.astype(BF16), preferred_element_type=F32) + cnt_ref[...]
    rank1 = jnp.sum(jnp.where(lane == i1, before, 0.0), axis=1, keepdims=True).astype(I32)
    rank2 = jnp.sum(jnp.where(lane == i2, before, 0.0), axis=1, keepdims=True).astype(I32)
    cnt_ref[...] += jnp.sum(chosen, axis=0, keepdims=True)
    e_ref[...] = jnp.where(lane == 0, i1 - N_GROUPS,
                           jnp.where(lane == 1, i2 - N_GROUPS,
                                     jnp.where(lane == 2, rank1, jnp.where(lane == 3, rank2, 0))))


def _route(logits):
    n = logits.shape[0]
    tm = min(512, n)
    tri = jnp.tril(jnp.ones((tm, tm), BF16), -1)
    spec = pl.BlockSpec((tm, LANES), lambda i: (i, 0))
    return pl.pallas_call(
        _router_kernel,
        grid=(n // tm,),
        in_specs=[spec, pl.BlockSpec((tm, tm), lambda i: (0, 0))],
        out_specs=[spec, spec, pl.BlockSpec((1, LANES), lambda i: (0, 0))],
        out_shape=[jax.ShapeDtypeStruct((n, LANES), I32), jax.ShapeDtypeStruct((n, LANES), F32),
                   jax.ShapeDtypeStruct((1, LANES), F32)],
        compiler_params=_cparams(("arbitrary",), 32),
        name="router",
    )(logits, tri)


def _start_row_copies(src_hbm, idx_ref, idx_row, dst_ref, slot, row0, sem, count):
    group = 8

    def issue(i, carry):
        for k in range(group):
            r = i * group + k
            pltpu.make_async_copy(src_hbm.at[pl.ds(idx_ref[0, idx_row, r], 1), :],
                                  dst_ref.at[slot, pl.ds(row0 + r, 1), :], sem.at[slot]).start(priority=k % 2)
        return carry
    lax.fori_loop(0, count // group, issue, 0)


def _wait_slot(src_hbm, dst_ref, slot, sem):
    rows = dst_ref.shape[1]
    pltpu.make_async_copy(src_hbm.at[pl.ds(0, rows), :], dst_ref.at[slot], sem.at[slot]).wait()


def _combine_kernel(pos_ref, pos_next_ref, y_hbm, h_ref, gate_ref, g_ref, b_ref, o_ref, buf_ref, sem, *, alpha):
    i = pl.program_id(0)
    n = pl.num_programs(0)
    slot = i % 2
    rows = h_ref.shape[0]

    def start(idx_ref, into):
        for k in range(2):
            _start_row_copies(y_hbm, idx_ref, k, buf_ref, into, k * rows, sem, rows)

    @pl.when(i == 0)
    def _():
        start(pos_ref, 0)

    @pl.when(i + 1 < n)
    def _():
        start(pos_next_ref, 1 - slot)

    _wait_slot(y_hbm, buf_ref, slot, sem)

    def norm_rows(j, carry):
        r0 = pl.multiple_of(j * SUBLANES, SUBLANES)
        gates = gate_ref[pl.ds(r0, SUBLANES), :]
        ffn = (buf_ref[slot, pl.ds(r0, SUBLANES), :] * gates[:, 0:1]
               + buf_ref[slot, pl.ds(rows + r0, SUBLANES), :] * gates[:, 1:2])
        z = alpha * h_ref[pl.ds(r0, SUBLANES), :] + ffn
        o_ref[pl.ds(r0, SUBLANES), :] = _layer_norm_rows(z, g_ref[...], b_ref[...])
        return carry
    lax.fori_loop(0, rows // SUBLANES, norm_rows, 0, unroll=8)


def _combine_ln(y, pos2, h1, gate_lanes, ln_g, ln_b, alpha):
    n, d = h1.shape
    tb = min(COMBINE_ROWS, n)
    nblk = n // tb
    pos_blocks = pos2.reshape(nblk, tb, 2).transpose(0, 2, 1)
    kern = functools.partial(_combine_kernel, alpha=alpha)
    return pl.pallas_call(
        kern,
        grid=(nblk,),
        in_specs=[pl.BlockSpec((1, 2, tb), lambda i: (i, 0, 0), memory_space=pltpu.SMEM),
                  pl.BlockSpec((1, 2, tb), lambda i: (jnp.minimum(i + 1, nblk - 1), 0, 0), memory_space=pltpu.SMEM),
                  pl.BlockSpec(memory_space=pl.ANY),
                  pl.BlockSpec((tb, d), lambda i: (i, 0)),
                  pl.BlockSpec((tb, LANES), lambda i: (i, 0)),
                  pl.BlockSpec((1, d), lambda i: (0, 0)),
                  pl.BlockSpec((1, d), lambda i: (0, 0))],
        out_specs=pl.BlockSpec((tb, d), lambda i: (i, 0)),
        out_shape=jax.ShapeDtypeStruct((n, d), F32),
        scratch_shapes=[pltpu.VMEM((2, 2 * tb, d), F32), pltpu.SemaphoreType.DMA((2,))],
        compiler_params=_cparams(("arbitrary",), 32),
        name="moe_combine_ln2",
    )(pos_blocks, pos_blocks, y, h1, gate_lanes, ln_g, ln_b)


def _cast_weight(dst_ref, src_ref):
    rows = src_ref.shape[1]
    step = min(256, rows)

    def body(i, carry):
        r0 = pl.multiple_of(i * step, step)
        dst_ref[pl.ds(r0, step), :] = src_ref[0, pl.ds(r0, step), :].astype(BF16)
        return carry
    lax.fori_loop(0, rows // step, body, 0)


ITEM_COMPUTE = 1
ITEM_NEW_WEIGHTS = 2
ITEM_ZERO_FILL = 4


def _expert_hidden(words, w1b_ref, w3b_ref):
    x_lo, x_hi = _unpack_bf16_halves(words)
    half = x_lo.shape[1]

    def project(w_ref):
        return (jnp.dot(x_lo, w_ref[0:half, :], preferred_element_type=F32)
                + jnp.dot(x_hi, w_ref[half:2 * half, :], preferred_element_type=F32))

    a = project(w1b_ref)
    b = project(w3b_ref)
    return (a * jax.nn.sigmoid(a) * b).astype(BF16)


def _moe_up_gather_kernel(e_ref, cw_ref, bi_ref, bo_ref, co_ref, flag_ref, tok_ref, tok1_ref, tok2_ref, hp_hbm,
                          w1_ref, w3_ref, o_ref, xs_ref, buf_ref, w1b_ref, w3b_ref, sem):
    it = pl.program_id(0)
    last = pl.num_programs(0) - 1
    flag = flag_ref[it]
    n_slots, rows = buf_ref.shape[0], buf_ref.shape[1]
    slot = it % n_slots

    def computes(item):
        return (item <= last) & ((flag_ref[jnp.minimum(item, last)] & ITEM_COMPUTE) != 0)

    @pl.when(it == 0)
    def _():
        _start_row_copies(hp_hbm, tok_ref, 0, buf_ref, 0, 0, sem, rows)

        @pl.when(computes(1))
        def _():
            _start_row_copies(hp_hbm, tok1_ref, 0, buf_ref, 1, 0, sem, rows)

    @pl.when(computes(it + 2))
    def _():
        _start_row_copies(hp_hbm, tok2_ref, 0, buf_ref, (it + 2) % n_slots, 0, sem, rows)

    @pl.when((flag & ITEM_NEW_WEIGHTS) != 0)
    def _():
        _cast_weight(w1b_ref, w1_ref)
        _cast_weight(w3b_ref, w3_ref)

    @pl.when((flag & ITEM_COMPUTE) != 0)
    def _():
        _wait_slot(hp_hbm, buf_ref, slot, sem)
        words = buf_ref[slot]
        xs_ref[...] = words
        o_ref[...] = _expert_hidden(words, w1b_ref, w3b_ref)

    @pl.when((flag & ITEM_ZERO_FILL) != 0)
    def _():
        o_ref[...] = jnp.zeros(o_ref.shape, o_ref.dtype)
        xs_ref[...] = jnp.zeros(xs_ref.shape, xs_ref.dtype)


def _moe_up_kernel(e_ref, cw_ref, bi_ref, bo_ref, co_ref, flag_ref, x_ref, w1_ref, w3_ref, o_ref, w1b_ref, w3b_ref):
    flag = flag_ref[pl.program_id(0)]

    @pl.when((flag & ITEM_NEW_WEIGHTS) != 0)
    def _():
        _cast_weight(w1b_ref, w1_ref)
        _cast_weight(w3b_ref, w3_ref)

    @pl.when((flag & ITEM_COMPUTE) != 0)
    def _():
        o_ref[...] = _expert_hidden(x_ref[...], w1b_ref, w3b_ref)

    @pl.when((flag & ITEM_ZERO_FILL) != 0)
    def _():
        o_ref[...] = jnp.zeros(o_ref.shape, o_ref.dtype)


def _moe_down_kernel(e_ref, cw_ref, bi_ref, bo_ref, co_ref, flag_ref, h0_ref, h1_ref, w2_ref, o_ref, w2b_ref):
    flag = flag_ref[pl.program_id(0)]

    @pl.when((flag & ITEM_NEW_WEIGHTS) != 0)
    def _():
        _cast_weight(w2b_ref, w2_ref)

    @pl.when((flag & ITEM_COMPUTE) != 0)
    def _():
        fc = h0_ref.shape[1]
        o_ref[...] = (jnp.dot(h0_ref[...], w2b_ref[0:fc, :], preferred_element_type=F32)
                      + jnp.dot(h1_ref[...], w2b_ref[fc:2 * fc, :], preferred_element_type=F32))

    @pl.when((flag & ITEM_ZERO_FILL) != 0)
    def _():
        o_ref[...] = jnp.zeros(o_ref.shape, o_ref.dtype)


def _work_items(blocks_per_expert, block_start, n_chunks, n_blocks):
    n_items = n_chunks * n_blocks
    per_e = n_chunks * blocks_per_expert
    end = jnp.cumsum(per_e)
    start = end - per_e
    it = jnp.arange(n_items, dtype=I32)
    total = end[-1]
    used_blocks = total // n_chunks
    itc = jnp.minimum(it, total - 1)
    e = jnp.sum(itc[:, None] >= end[None, :], axis=1).astype(I32)
    local = itc - start[e]
    nb = jnp.maximum(blocks_per_expert[e], 1)
    c = local // nb
    r = local - c * nb
    b_in = block_start[e] + r
    active = it < total
    spare = jnp.maximum(it - total, 0)
    b_out = jnp.where(active, b_in, used_blocks + spare // n_chunks)
    c_out = jnp.where(active, c, spare % n_chunks)
    first = active & (r == 0)
    flag = jnp.where(active, ITEM_COMPUTE + ITEM_NEW_WEIGHTS * first.astype(I32), ITEM_ZERO_FILL)
    later_first = lax.cummin(jnp.where(first, it, n_items)[::-1])[::-1]
    next_start = jnp.concatenate([later_first[1:], jnp.full((1,), n_items, I32)])
    ahead = jnp.minimum(next_start, n_items - 1)
    use_next = active & ~first & (next_start < n_items)
    e_w = jnp.where(use_next, e[ahead], e)
    c_w = jnp.where(use_next, c[ahead], c)
    return tuple(v.astype(I32) for v in (e_w, c_w, b_in, b_out, c_out, flag))


def _moe_up_gather(hp, buf_tok, w1, w3, items):
    p = buf_tok.shape[0]
    half = hp.shape[1]
    d, f = w1.shape[1], w1.shape[2]
    fc = f // 2
    n_items = items[0].shape[0]
    nblk = p // MOE_ROWS
    tok_blocks = buf_tok.reshape(nblk, 1, MOE_ROWS)
    wspec = pl.BlockSpec((1, d, fc), lambda it, e, cw, bi, bo, co, fl: (e[it], 0, 0))
    lookahead = 2

    def tok_spec(ahead):
        return pl.BlockSpec((1, 1, MOE_ROWS),
                            lambda it, e, cw, bi, bo, co, fl: (bi[jnp.minimum(it + ahead, n_items - 1)], 0, 0),
                            memory_space=pltpu.SMEM)

    return pl.pallas_call(
        _moe_up_gather_kernel,
        grid_spec=pltpu.PrefetchScalarGridSpec(
            num_scalar_prefetch=6,
            grid=(n_items,),
            in_specs=[tok_spec(k) for k in range(lookahead + 1)] + [pl.BlockSpec(memory_space=pl.ANY), wspec, wspec],
            out_specs=[pl.BlockSpec((MOE_ROWS, fc), lambda it, e, cw, bi, bo, co, fl: (bo[it], 0)),
                       pl.BlockSpec((MOE_ROWS, half), lambda it, e, cw, bi, bo, co, fl: (bo[it], 0))],
            scratch_shapes=[pltpu.VMEM((lookahead + 1, MOE_ROWS, half), I32), pltpu.VMEM((d, fc), BF16),
                            pltpu.VMEM((d, fc), BF16), pltpu.SemaphoreType.DMA((lookahead + 1,))]),
        out_shape=[jax.ShapeDtypeStruct((p, fc), BF16), jax.ShapeDtypeStruct((p, half), I32)],
        compiler_params=_cparams(("arbitrary",), 56),
        name="moe_up_gather",
    )(*items, *([tok_blocks] * (lookahead + 1)), hp, w1, w3)


def _moe_up(xs, w1, w3, items):
    p, half = xs.shape
    d, f = w1.shape[1], w1.shape[2]
    fc = f // 2
    assert d == 2 * half
    wspec = pl.BlockSpec((1, d, fc), lambda it, e, cw, bi, bo, co, fl: (e[it], 0, 1))
    return pl.pallas_call(
        _moe_up_kernel,
        grid_spec=pltpu.PrefetchScalarGridSpec(
            num_scalar_prefetch=6,
            grid=(items[0].shape[0],),
            in_specs=[pl.BlockSpec((MOE_ROWS, half), lambda it, e, cw, bi, bo, co, fl: (bi[it], 0)), wspec, wspec],
            out_specs=pl.BlockSpec((MOE_ROWS, fc), lambda it, e, cw, bi, bo, co, fl: (bo[it], 0)),
            scratch_shapes=[pltpu.VMEM((d, fc), BF16), pltpu.VMEM((d, fc), BF16)]),
        out_shape=jax.ShapeDtypeStruct((p, fc), BF16),
        compiler_params=_cparams(("arbitrary",), 56),
        name="moe_up",
    )(*items, xs, w1, w3)


def _moe_down(h0, h1, w2, items, d_chunk):
    p, fc = h0.shape
    f = 2 * fc
    d = w2.shape[2]
    hspec = pl.BlockSpec((MOE_ROWS, fc), lambda it, e, cw, bi, bo, co, fl: (bi[it], 0))
    return pl.pallas_call(
        _moe_down_kernel,
        grid_spec=pltpu.PrefetchScalarGridSpec(
            num_scalar_prefetch=6,
            grid=(items[0].shape[0],),
            in_specs=[hspec, hspec,
                      pl.BlockSpec((1, f, d_chunk), lambda it, e, cw, bi, bo, co, fl: (e[it], 0, cw[it]))],
            out_specs=pl.BlockSpec((MOE_ROWS, d_chunk), lambda it, e, cw, bi, bo, co, fl: (bo[it], co[it])),
            scratch_shapes=[pltpu.VMEM((f, d_chunk), BF16)]),
        out_shape=jax.ShapeDtypeStruct((p, d), F32),
        compiler_params=_cparams(("arbitrary",), 56),
        name="moe_down",
    )(*items, h0, h1, w2)


def _hier_moe_ln(h1, hp, logits, w1, w3, w2, ln_g, ln_b, alpha):
    n, d = h1.shape
    e_lanes, gate_lanes, lane_counts = _route(logits)
    a = 2 * n
    e_flat = e_lanes[:, 0:2].reshape(a)
    rank = e_lanes[:, 2:4].reshape(a)
    counts = lane_counts[0, N_GROUPS:N_GROUPS + N_EXPERTS].astype(I32)
    blocks_per_expert = (counts + MOE_ROWS - 1) // MOE_ROWS
    block_start = jnp.cumsum(blocks_per_expert) - blocks_per_expert
    onehot = e_flat[:, None] == jnp.arange(N_EXPERTS, dtype=I32)[None, :]
    pos = jnp.sum(jnp.where(onehot, block_start[None, :], 0), axis=1) * MOE_ROWS + rank
    n_blocks = (a + MOE_ROWS - 1) // MOE_ROWS + N_EXPERTS
    buf_tok = jnp.zeros((n_blocks * MOE_ROWS,), I32).at[pos].set(jnp.arange(a, dtype=I32) // 2)

    d_chunk = min(4096, d)
    up_items = _work_items(blocks_per_expert, block_start, 1, n_blocks)
    h0, xs = _moe_up_gather(hp, buf_tok, w1, w3, up_items)
    h1_mid = _moe_up(xs, w1, w3, up_items)
    y = _moe_down(h0, h1_mid, w2, _work_items(blocks_per_expert, block_start, d // d_chunk, n_blocks), d_chunk)
    return _combine_ln(y, pos.reshape(n, 2), h1, gate_lanes, ln_g, ln_b, alpha)


def kernel(x, w_in, conv_dw_w, conv_dw_b, conv_ln_g, conv_ln_b, w_out, ln1_g, ln1_b, w_router_group, b_router_group, w_router_expert, b_router_expert, w_expert_gate, w_expert_up, w_expert_down, ln2_g, ln2_b):
    b, s, d = x.shape
    depth = w_in.shape[0]
    assert depth == 1
    alpha = float((2.0 * depth) ** 0.25)
    n = b * s
    dc = conv_dw_w.shape[2]
    d_attn = N_HEADS * HEAD_DIM
    d_kv = N_KV_HEADS * HEAD_DIM
    d_qi = IDX_HEADS * IDX_DIM
    n_small = IDX_DIM + IDX_HEADS
    assert w_in.shape[2] == 2 * dc + d_attn + 2 * d_kv + d_qi + n_small

    x2 = x.reshape(n, d)
    w = w_in[0].astype(BF16)
    o_qkvi = 2 * dc
    o_small = o_qkvi + d_attn + 2 * d_kv + d_qi
    w_small = jnp.pad(w[:, o_small:], ((0, 0), (0, LANES - n_small)))
    qkvi_scale = jnp.concatenate([jnp.full((d_attn,), HEAD_DIM ** -0.5 * LOG2E, F32), jnp.ones((2 * d_kv,), F32),
                                  jnp.full((d_qi,), IDX_DIM ** -0.5, F32)])[None, :]

    xb, small = _proj_idx_and_cast(x2, w_small)
    glu = _proj_glu(xb, w, dc)
    qkvi = _proj_scale(xb, w, o_qkvi, o_small - o_qkvi, qkvi_scale, BF16, "proj_qkvi")

    conv_out = _conformer_conv(glu.reshape(b, s, dc), conv_dw_w[0], conv_dw_b, conv_ln_g, conv_ln_b)
    attn_out = _dsa_attention(qkvi.reshape(b, s, -1), small.reshape(b, s, LANES), b, s)

    n_route = N_GROUPS + N_EXPERTS
    w_router = jnp.pad(jnp.concatenate([w_router_group[0], w_router_expert[0]], axis=1),
                       ((0, 0), (0, LANES - n_route))).astype(BF16)
    b_router = jnp.pad(jnp.concatenate([b_router_group[0], b_router_expert[0]]), (0, LANES - n_route))[None, :]
    h1, hp, logits = _out_proj_ln(conv_out.reshape(n, dc), attn_out.reshape(n, d_attn),
                                  w_out[0, :dc].astype(BF16), w_out[0, dc:].astype(BF16),
                                  x2, ln1_g, ln1_b, w_router, b_router, alpha)

    out = _hier_moe_ln(h1, hp, logits, w_expert_gate[0], w_expert_up[0], w_expert_down[0], ln2_g, ln2_b, alpha)
    return out.reshape(b, s, d)
```

```python
import functools
import math

import jax
import jax.numpy as jnp
from jax import lax
from jax.experimental import pallas as pl
from jax.experimental.pallas import tpu as pltpu

F32 = jnp.float32
BF16 = jnp.bfloat16
I32 = jnp.int32

CONV_WIDTH = 31
N_HEADS = 16
HEAD_DIM = 128
N_KV_HEADS = 4
IDX_HEADS = 16
IDX_DIM = 64
TOPK_MAX = 256
N_GROUPS = 4
EXPERTS_PER_GROUP = 8
N_EXPERTS = N_GROUPS * EXPERTS_PER_GROUP
LN_EPS = 1e-5

LANES = 128
SUBLANES = 8
BF16_ROWS = 16
MIB = 1024 * 1024

Q_TILE = 128
KEY_CHUNK = 512
CONV_HALO = 32
MOE_ROWS = 256
COMBINE_ROWS = 128
LN_GROUPS = 4
INT_MIN = -(2 ** 31)
INT_MAX = 2 ** 31 - 1
LOG2E = math.log2(math.e)
NEG_BIG = -1e30
M_INIT = -1e20


def _cparams(semantics, vmem_mib):
    return pltpu.CompilerParams(dimension_semantics=semantics, vmem_limit_bytes=vmem_mib * MIB)


def _layer_norm_rows(z, g, b):
    mu = jnp.mean(z, axis=-1, keepdims=True)
    zc = z - mu
    var = jnp.mean(zc * zc, axis=-1, keepdims=True)
    return zc * lax.rsqrt(var + LN_EPS) * g + b


def _mm_glu_kernel(x_ref, wa_ref, wg_ref, o_ref):
    x = x_ref[...]
    a = jnp.dot(x, wa_ref[...], preferred_element_type=F32)
    g = jnp.dot(x, wg_ref[...], preferred_element_type=F32)
    o_ref[...] = (a * jax.nn.sigmoid(g)).astype(o_ref.dtype)


def _mm_scale_kernel(x_ref, w_ref, s_ref, o_ref):
    acc = jnp.dot(x_ref[...], w_ref[...], preferred_element_type=F32)
    o_ref[...] = (acc * s_ref[...]).astype(o_ref.dtype)


def _mm_tiles(n, k, cols):
    tm = min(1024, n)
    tn = min(512, cols)
    assert n % tm == 0 and cols % tn == 0
    return tm, tn


def _proj_glu(xb, w, cols):
    n, k = xb.shape
    tm, tn = _mm_tiles(n, k, cols)
    gate_block = cols // tn
    return pl.pallas_call(
        _mm_glu_kernel,
        grid=(n // tm, cols // tn),
        in_specs=[pl.BlockSpec((tm, k), lambda i, j: (i, 0)),
                  pl.BlockSpec((k, tn), lambda i, j: (0, j)),
                  pl.BlockSpec((k, tn), lambda i, j: (0, j + gate_block))],
        out_specs=pl.BlockSpec((tm, tn), lambda i, j: (i, j)),
        out_shape=jax.ShapeDtypeStruct((n, cols), F32),
        compiler_params=_cparams(("parallel", "arbitrary"), 48),
        name="proj_glu",
    )(xb, w, w)


def _proj_scale(xb, w, first_col, cols, scale, out_dtype, name):
    n, k = xb.shape
    tm, tn = _mm_tiles(n, k, cols)
    assert first_col % tn == 0
    first_block = first_col // tn
    return pl.pallas_call(
        _mm_scale_kernel,
        grid=(n // tm, cols // tn),
        in_specs=[pl.BlockSpec((tm, k), lambda i, j: (i, 0)),
                  pl.BlockSpec((k, tn), lambda i, j: (0, j + first_block)),
                  pl.BlockSpec((1, tn), lambda i, j: (0, j))],
        out_specs=pl.BlockSpec((tm, tn), lambda i, j: (i, j)),
        out_shape=jax.ShapeDtypeStruct((n, cols), out_dtype),
        compiler_params=_cparams(("parallel", "arbitrary"), 48),
        name=name,
    )(xb, w, scale)


def _idx_cast_kernel(x_ref, w_ref, xb_ref, o_ref):
    xb = x_ref[...].astype(BF16)
    xb_ref[...] = xb
    o_ref[...] = jnp.dot(xb, w_ref[...], preferred_element_type=F32)


def _proj_idx_and_cast(x2, w_small):
    n, k = x2.shape
    tm = min(256, n)
    return pl.pallas_call(
        _idx_cast_kernel,
        grid=(n // tm,),
        in_specs=[pl.BlockSpec((tm, k), lambda i: (i, 0)),
                  pl.BlockSpec((k, LANES), lambda i: (0, 0))],
        out_specs=[pl.BlockSpec((tm, k), lambda i: (i, 0)),
                   pl.BlockSpec((tm, LANES), lambda i: (i, 0))],
        out_shape=[jax.ShapeDtypeStruct((n, k), BF16), jax.ShapeDtypeStruct((n, LANES), F32)],
        compiler_params=_cparams(("parallel",), 32),
        name="proj_idx",
    )(x2, w_small)


def _conv_kernel(cur_ref, prev_ref, w_ref, b_ref, g_ref, beta_ref, o_ref, sh_ref, y_ref, *, rows, lane_chunk):
    t = pl.program_id(1)
    dc = cur_ref.shape[2]
    row_sub = 32
    sh_ref[0, 0:CONV_HALO, :] = jnp.where(t > 0, prev_ref[0], 0.0)
    sh_ref[0, CONV_HALO:CONV_HALO + rows, :] = cur_ref[0]
    first = CONV_HALO - (CONV_WIDTH - 1)
    shifted_rows = rows + CONV_HALO - SUBLANES
    for lc in range(dc // lane_chunk):
        cols = slice(lc * lane_chunk, (lc + 1) * lane_chunk)
        for r in range(1, SUBLANES):
            sh_ref[r, 0:shifted_rows, cols] = sh_ref[0, r:r + shifted_rows, cols]

    for r0 in range(0, rows, row_sub):
        for lc in range(dc // lane_chunk):
            cols = slice(lc * lane_chunk, (lc + 1) * lane_chunk)
            acc = jnp.zeros((row_sub, lane_chunk), F32)
            for j in range(CONV_WIDTH):
                shift, base = (first + j) % SUBLANES, (first + j) // SUBLANES * SUBLANES
                acc = acc + sh_ref[shift, r0 + base:r0 + base + row_sub, cols] * w_ref[j:j + 1, cols]
            y_ref[r0:r0 + row_sub, cols] = acc + b_ref[:, cols]

    def norm_rows(i, carry):
        r0 = pl.multiple_of(i * BF16_ROWS, BF16_ROWS)
        yn = _layer_norm_rows(y_ref[pl.ds(r0, BF16_ROWS), :], g_ref[...], beta_ref[...])
        o_ref[0, pl.ds(r0, BF16_ROWS), :] = (yn * jax.nn.sigmoid(yn)).astype(o_ref.dtype)
        return carry

    lax.fori_loop(0, rows // BF16_ROWS, norm_rows, 0, unroll=4)


def _conformer_conv(glu, w_dw, b_dw, g_ln, b_ln):
    b, s, dc = glu.shape
    rows = min(256, s)
    halo_blocks = rows // CONV_HALO
    lane_chunk = min(512, dc)
    kern = functools.partial(_conv_kernel, rows=rows, lane_chunk=lane_chunk)
    return pl.pallas_call(
        kern,
        grid=(b, s // rows),
        in_specs=[pl.BlockSpec((1, rows, dc), lambda bi, t: (bi, t, 0)),
                  pl.BlockSpec((1, CONV_HALO, dc), lambda bi, t: (bi, jnp.maximum(t * halo_blocks - 1, 0), 0)),
                  pl.BlockSpec((CONV_WIDTH, dc), lambda bi, t: (0, 0)),
                  pl.BlockSpec((1, dc), lambda bi, t: (0, 0)),
                  pl.BlockSpec((1, dc), lambda bi, t: (0, 0)),
                  pl.BlockSpec((1, dc), lambda bi, t: (0, 0))],
        out_specs=pl.BlockSpec((1, rows, dc), lambda bi, t: (bi, t, 0)),
        out_shape=jax.ShapeDtypeStruct((b, s, dc), BF16),
        scratch_shapes=[pltpu.VMEM((SUBLANES, CONV_HALO + rows, dc), F32), pltpu.VMEM((rows, dc), F32)],
        compiler_params=_cparams(("parallel", "arbitrary"), 40),
        name="conformer_conv",
    )(glu, glu, w_dw, b_dw, g_ln, b_ln)


def _dsa_kernel(q_ref, qi_ref, k_ref, v_ref, kidx_ref, w_ref, o_ref,
                ke_ref, ko_ref, vt_ref, key_ref, cut_ref, nd_ref, qs_ref, qis_ref, m_ref, l_ref, acc_ref,
                *, seq, chunk, topk):
    tb = pl.program_id(1)
    t0 = tb * Q_TILE
    n_chunks = (t0 + Q_TILE + chunk - 1) // chunk
    rep = N_HEADS // N_KV_HEADS
    nt = (((1,), (1,)), ((), ()))

    @pl.when(tb == 0)
    def _():
        def build(c, carry):
            r0 = pl.multiple_of(c * chunk, chunk)
            kx = kidx_ref[0, pl.ds(r0, chunk), :]
            lane = lax.broadcasted_iota(I32, kx.shape, 1)
            ke_ref[pl.ds(r0, chunk), :] = jnp.where(lane < IDX_DIM, kx, 0.0).astype(BF16)
            ko_ref[pl.ds(r0, chunk), :] = jnp.where(lane >= IDX_DIM, pltpu.roll(kx, IDX_DIM, 1), 0.0).astype(BF16)
            for g in range(N_KV_HEADS):
                vg = v_ref[0, pl.ds(r0, chunk), g * HEAD_DIM:(g + 1) * HEAD_DIM]
                vt_ref[c, g * HEAD_DIM:(g + 1) * HEAD_DIM, :] = vg.astype(F32).T.astype(BF16)
            return carry
        lax.fori_loop(0, seq // chunk, build, 0)

    for g in range(N_KV_HEADS):
        for r in range(rep):
            h = g * rep + r
            qs_ref[g, r * Q_TILE:(r + 1) * Q_TILE, :] = q_ref[0, :, h * HEAD_DIM:(h + 1) * HEAD_DIM]
    for jj in range(IDX_HEADS // 4):
        for half in range(2):
            pair = 2 * jj + half
            qis_ref[jj, half * Q_TILE:(half + 1) * Q_TILE, :] = qi_ref[0, :, pair * LANES:(pair + 1) * LANES]
    w_t = w_ref[0].T * (IDX_HEADS ** -0.5)

    key_row = lax.broadcasted_iota(I32, (chunk, Q_TILE), 0)
    q_pos = t0 + lax.broadcasted_iota(I32, (chunk, Q_TILE), 1)

    def score_chunk(c, carry):
        r0 = pl.multiple_of(c * chunk, chunk)
        ke = ke_ref[pl.ds(r0, chunk), :]
        ko = ko_ref[pl.ds(r0, chunk), :]
        acc = jnp.zeros((chunk, Q_TILE), F32)
        for jj in range(IDX_HEADS // 4):
            rhs = qis_ref[jj]
            de = lax.dot_general(ke, rhs, nt, preferred_element_type=F32)
            do = lax.dot_general(ko, rhs, nt, preferred_element_type=F32)
            for half in range(2):
                h_even = 2 * (2 * jj + half)
                cols = slice(half * Q_TILE, (half + 1) * Q_TILE)
                acc = acc + w_t[IDX_DIM + h_even:IDX_DIM + h_even + 1, :] * jnp.maximum(de[:, cols], 0.0)
                acc = acc + w_t[IDX_DIM + h_even + 1:IDX_DIM + h_even + 2, :] * jnp.maximum(do[:, cols], 0.0)
        bits = lax.bitcast_convert_type(acc, I32)
        key = bits ^ ((bits >> 31) & 0x7FFFFFFF)
        key_ref[c] = jnp.where(r0 + key_row <= q_pos, key, INT_MIN)
        return carry

    lax.fori_loop(0, n_chunks, score_chunk, 0)

    def count_ge(cand):
        def count_chunk(c, cnts):
            cnts = list(cnts)
            for r in range(chunk // SUBLANES):
                slab = key_ref[c, r * SUBLANES:(r + 1) * SUBLANES, :]
                cnts[r % len(cnts)] = cnts[r % len(cnts)] + jnp.where(slab >= cand, 1.0, 0.0)
            return tuple(cnts)

        zero = jnp.zeros((SUBLANES, Q_TILE), F32)
        cnts = lax.fori_loop(0, n_chunks, count_chunk, (zero, zero, zero, zero))
        return jnp.sum((cnts[0] + cnts[1]) + (cnts[2] + cnts[3]), axis=0, keepdims=True)

    def bit_step(i, t_u):
        cand_u = t_u | (jnp.int32(1) << (31 - i))
        return jnp.where(count_ge(cand_u ^ INT_MIN) >= float(topk), cand_u, t_u)

    t_u = lax.fori_loop(0, 32, bit_step, jnp.zeros((1, Q_TILE), I32))
    thr = jnp.maximum(t_u ^ INT_MIN, INT_MIN + 1)

    n_ge = count_ge(thr)
    cut_ref[...] = jnp.full(cut_ref.shape, INT_MAX, I32)

    @pl.when(jnp.max(n_ge) > topk)
    def _():
        def count_where(pred):
            def count_chunk(c, cnt):
                r0 = pl.multiple_of(c * chunk, chunk)
                hit = jnp.where(pred(key_ref[c], r0 + key_row), 1, 0)
                return cnt + jnp.sum(hit, axis=0, keepdims=True)
            return lax.fori_loop(0, n_chunks, count_chunk, jnp.zeros((1, Q_TILE), I32))

        keep = topk - count_where(lambda kc, pos: kc > thr)
        pos_bits = seq.bit_length()

        def bit_step(i, cut):
            cand = cut | (jnp.int32(1) << (pos_bits - 1 - i))
            taken = count_where(lambda kc, pos: jnp.where(kc == thr, pos, INT_MAX) < cand)
            return jnp.where(taken <= keep, cand, cut)
        cut_ref[...] = lax.fori_loop(0, pos_bits, bit_step, jnp.zeros((1, Q_TILE), I32))

    cut = cut_ref[...]

    m_ref[...] = jnp.full(m_ref.shape, M_INIT, F32)
    l_ref[...] = jnp.zeros(l_ref.shape, F32)
    acc_ref[...] = jnp.zeros(acc_ref.shape, F32)

    def attend_chunk(c, carry):
        r0 = pl.multiple_of(c * chunk, chunk)
        pos = r0 + key_row
        kc = key_ref[c]
        rank_pos = jnp.where(kc > thr, -1, jnp.where(kc == thr, pos, INT_MAX))
        nd_ref[...] = jnp.where(rank_pos < cut, (pos - q_pos).astype(F32), NEG_BIG)
        def logits(g):
            kg = k_ref[0, pl.ds(r0, chunk), g * HEAD_DIM:(g + 1) * HEAD_DIM]
            return lax.dot_general(kg, qs_ref[g], nt, preferred_element_type=F32)

        s_next = logits(0)
        for g in range(N_KV_HEADS):
            s_all = s_next
            if g + 1 < N_KV_HEADS:
                s_next = logits(g + 1)
            probs = []
            alphas = []
            for r in range(rep):
                h = g * rep + r
                slope = float(2.0 ** (-8.0 * (h + 1) / N_HEADS)) * LOG2E
                sr = s_all[:, r * Q_TILE:(r + 1) * Q_TILE] + slope * nd_ref[...]
                m_old = m_ref[h]
                m_new = jnp.maximum(m_old, jnp.max(sr, axis=0, keepdims=True))
                alpha = jnp.exp2(m_old - m_new)
                p = jnp.exp2(sr - m_new)
                l_ref[h] = alpha * l_ref[h] + jnp.sum(p, axis=0, keepdims=True)
                m_ref[h] = m_new
                probs.append(p.astype(BF16))
                alphas.append(alpha)
            vt = vt_ref[c, g * HEAD_DIM:(g + 1) * HEAD_DIM, :]
            pv = jnp.dot(vt, jnp.concatenate(probs, axis=1), preferred_element_type=F32)
            acc_ref[g] = jnp.concatenate(alphas, axis=1) * acc_ref[g] + pv
        return carry

    lax.fori_loop(0, n_chunks, attend_chunk, 0)

    for g in range(N_KV_HEADS):
        for r in range(rep):
            h = g * rep + r
            o_t = acc_ref[g, :, r * Q_TILE:(r + 1) * Q_TILE] * (1.0 / l_ref[h])
            o_ref[0, :, h * HEAD_DIM:(h + 1) * HEAD_DIM] = o_t.T.astype(o_ref.dtype)


def _dsa_attention(qkvi, small, b, s):
    d_attn = N_HEADS * HEAD_DIM
    d_kv = N_KV_HEADS * HEAD_DIM
    d_qi = IDX_HEADS * IDX_DIM
    rep = N_HEADS // N_KV_HEADS
    chunk = min(KEY_CHUNK, s)
    topk = min(TOPK_MAX, s // 4)
    assert s % chunk == 0 and chunk % Q_TILE == 0 and d_attn % d_qi == 0 and d_attn % d_kv == 0
    assert Q_TILE == LANES and HEAD_DIM == LANES and 2 * IDX_DIM == LANES
    kern = functools.partial(_dsa_kernel, seq=s, chunk=chunk, topk=topk)
    return pl.pallas_call(
        kern,
        grid=(b, s // Q_TILE),
        in_specs=[pl.BlockSpec((1, Q_TILE, d_attn), lambda bi, t: (bi, t, 0)),
                  pl.BlockSpec((1, Q_TILE, d_qi), lambda bi, t: (bi, t, (d_attn + 2 * d_kv) // d_qi)),
                  pl.BlockSpec((1, s, d_kv), lambda bi, t: (bi, 0, d_attn // d_kv)),
                  pl.BlockSpec((1, s, d_kv), lambda bi, t: (bi, 0, d_attn // d_kv + 1)),
                  pl.BlockSpec((1, s, LANES), lambda bi, t: (bi, 0, 0)),
                  pl.BlockSpec((1, Q_TILE, LANES), lambda bi, t: (bi, t, 0))],
        out_specs=pl.BlockSpec((1, Q_TILE, d_attn), lambda bi, t: (bi, t, 0)),
        out_shape=jax.ShapeDtypeStruct((b, s, d_attn), BF16),
        scratch_shapes=[pltpu.VMEM((s, LANES), BF16),
                        pltpu.VMEM((s, LANES), BF16),
                        pltpu.VMEM((s // chunk, d_kv, chunk), BF16),
                        pltpu.VMEM((s // chunk, chunk, Q_TILE), I32),
                        pltpu.VMEM((1, Q_TILE), I32),
                        pltpu.VMEM((chunk, Q_TILE), F32),
                        pltpu.VMEM((N_KV_HEADS, rep * Q_TILE, HEAD_DIM), BF16),
                        pltpu.VMEM((IDX_HEADS // 4, 2 * Q_TILE, LANES), BF16),
                        pltpu.VMEM((N_HEADS, 1, Q_TILE), F32),
                        pltpu.VMEM((N_HEADS, 1, Q_TILE), F32),
                        pltpu.VMEM((N_KV_HEADS, HEAD_DIM, rep * Q_TILE), F32)],
        compiler_params=_cparams(("parallel", "arbitrary"), 48),
        name="dsa_attention",
    )(qkvi, qkvi, qkvi, qkvi, small, small)


def _pack_bf16_halves(hb):
    half = hb.shape[1] // 2
    bits = lax.bitcast_convert_type(hb.astype(F32), I32)
    return lax.shift_right_logical(bits[:, :half], 16) | (bits[:, half:] & -65536)


def _unpack_bf16_halves(words):
    lo = lax.bitcast_convert_type(words << 16, F32).astype(BF16)
    hi = lax.bitcast_convert_type(words & -65536, F32).astype(BF16)
    return lo, hi


def _outproj_kernel(conv_ref, attn_ref, wc_ref, wa_ref, x_ref, g_ref, b_ref, wr_ref, br_ref,
                    h_ref, hp_ref, lg_ref, hb_ref, *, alpha):
    j = pl.program_id(1)
    tm, d = h_ref.shape
    tn = x_ref.shape[1]
    n_tiles = d // tn
    mix = jnp.dot(conv_ref[...], wc_ref[...], preferred_element_type=F32)
    mix = mix + jnp.dot(attn_ref[...], wa_ref[...], preferred_element_type=F32)
    z = alpha * x_ref[...] + mix
    for jj in range(n_tiles):
        @pl.when(j == jj)
        def _(jj=jj):
            h_ref[:, jj * tn:(jj + 1) * tn] = z

    @pl.when(j == n_tiles - 1)
    def _():
        def rows(i, carry):
            starts = [pl.multiple_of((LN_GROUPS * i + k) * BF16_ROWS, BF16_ROWS) for k in range(LN_GROUPS)]
            zs = [h_ref[pl.ds(r0, BF16_ROWS), :] for r0 in starts]
            hs = [_layer_norm_rows(z, g_ref[...], b_ref[...]) for z in zs]
            for r0, h in zip(starts, hs):
                h_ref[pl.ds(r0, BF16_ROWS), :] = h
                hb_ref[pl.ds(r0, BF16_ROWS), :] = h.astype(BF16)
            return carry
        lax.fori_loop(0, tm // (LN_GROUPS * BF16_ROWS), rows, 0)

        def pack_rows(i, carry):
            r0 = pl.multiple_of(i * BF16_ROWS, BF16_ROWS)
            hp_ref[pl.ds(r0, BF16_ROWS), :] = _pack_bf16_halves(hb_ref[pl.ds(r0, BF16_ROWS), :])
            return carry
        lax.fori_loop(0, tm // BF16_ROWS, pack_rows, 0, unroll=2)
        lg_ref[...] = jnp.dot(hb_ref[...], wr_ref[...], preferred_element_type=F32) + br_ref[...]


def _out_proj_ln(conv_out, attn_out, w_conv, w_attn, x2, ln_g, ln_b, w_router, b_router, alpha):
    n, d = x2.shape
    dc = conv_out.shape[1]
    da = attn_out.shape[1]
    tm = min(512, n)
    tn = min(512, d)
    assert n % tm == 0 and d % tn == 0
    kern = functools.partial(_outproj_kernel, alpha=alpha)
    return pl.pallas_call(
        kern,
        grid=(n // tm, d // tn),
        in_specs=[pl.BlockSpec((tm, dc), lambda i, j: (i, 0)),
                  pl.BlockSpec((tm, da), lambda i, j: (i, 0)),
                  pl.BlockSpec((dc, tn), lambda i, j: (0, j)),
                  pl.BlockSpec((da, tn), lambda i, j: (0, j)),
                  pl.BlockSpec((tm, tn), lambda i, j: (i, j)),
                  pl.BlockSpec((1, d), lambda i, j: (0, 0)),
                  pl.BlockSpec((1, d), lambda i, j: (0, 0)),
                  pl.BlockSpec((d, LANES), lambda i, j: (0, 0)),
                  pl.BlockSpec((1, LANES), lambda i, j: (0, 0))],
        out_specs=[pl.BlockSpec((tm, d), lambda i, j: (i, 0)),
                   pl.BlockSpec((tm, d // 2), lambda i, j: (i, 0)),
                   pl.BlockSpec((tm, LANES), lambda i, j: (i, 0))],
        out_shape=[jax.ShapeDtypeStruct((n, d), F32), jax.ShapeDtypeStruct((n, d // 2), I32),
                   jax.ShapeDtypeStruct((n, LANES), F32)],
        scratch_shapes=[pltpu.VMEM((tm, d), BF16)],
        compiler_params=_cparams(("parallel", "arbitrary"), 56),
        name="out_proj_ln1",
    )(conv_out, attn_out, w_conv, w_attn, x2, ln_g, ln_b, w_router, b_router)


def _first_argmax(vals, lane, valid):
    masked = jnp.where(valid, vals, -jnp.inf)
    mx = jnp.max(masked, axis=1, keepdims=True)
    idx = jnp.min(jnp.where(valid & (masked == mx), lane, LANES), axis=1, keepdims=True)
    return mx, idx


def _router_kernel(lg_ref, tri_ref, e_ref, gate_ref, cnt_ref):
    @pl.when(pl.program_id(0) == 0)
    def _():
        cnt_ref[...] = jnp.zeros(cnt_ref.shape, F32)

    lg = lg_ref[...]
    lane = lax.broadcasted_iota(I32, lg.shape, 1)
    is_group = lane < N_GROUPS
    g_max, g_sel = _first_argmax(lg, lane, is_group)
    g_den = jnp.sum(jnp.where(is_group, jnp.exp(lg - g_max), 0.0), axis=1, keepdims=True)
    p_group = 1.0 / g_den
    lo = N_GROUPS + g_sel * EXPERTS_PER_GROUP
    in_group = (lane >= lo) & (lane < lo + EXPERTS_PER_GROUP)
    e_max, _ = _first_argmax(lg, lane, in_group)
    e_exp = jnp.where(in_group, jnp.exp(lg - e_max), 0.0)
    prob = e_exp / jnp.sum(e_exp, axis=1, keepdims=True)
    p1, i1 = _first_argmax(prob, lane, in_group)
    p2, i2 = _first_argmax(prob, lane, in_group & (lane != i1))
    norm = p_group / (p1 + p2)
    gate_ref[...] = jnp.where(lane == 0, p1 * norm, jnp.where(lane == 1, p2 * norm, 0.0))
    chosen = jnp.where((lane == i1) | (lane == i2), 1.0, 0.0)
    before = jnp.dot(tri_ref[...], chosen.astype(BF16), preferred_element_type=F32) + cnt_ref[...]
    rank1 = jnp.sum(jnp.where(lane == i1, before, 0.0), axis=1, keepdims=True).astype(I32)
    rank2 = jnp.sum(jnp.where(lane == i2, before, 0.0), axis=1, keepdims=True).astype(I32)
    cnt_ref[...] += jnp.sum(chosen, axis=0, keepdims=True)
    e_ref[...] = jnp.where(lane == 0, i1 - N_GROUPS,
                           jnp.where(lane == 1, i2 - N_GROUPS,
                                     jnp.where(lane == 2, rank1, jnp.where(lane == 3, rank2, 0))))


def _route(logits):
    n = logits.shape[0]
    tm = min(512, n)
    tri = jnp.tril(jnp.ones((tm, tm), BF16), -1)
    spec = pl.BlockSpec((tm, LANES), lambda i: (i, 0))
    return pl.pallas_call(
        _router_kernel,
        grid=(n // tm,),
        in_specs=[spec, pl.BlockSpec((tm, tm), lambda i: (0, 0))],
        out_specs=[spec, spec, pl.BlockSpec((1, LANES), lambda i: (0, 0))],
        out_shape=[jax.ShapeDtypeStruct((n, LANES), I32), jax.ShapeDtypeStruct((n, LANES), F32),
                   jax.ShapeDtypeStruct((1, LANES), F32)],
        compiler_params=_cparams(("arbitrary",), 32),
        name="router",
    )(logits, tri)


def _start_row_copies(src_hbm, idx_ref, idx_row, dst_ref, slot, row0, sem, count):
    group = 8

    def issue(i, carry):
        for k in range(group):
            r = i * group + k
            pltpu.make_async_copy(src_hbm.at[pl.ds(idx_ref[0, idx_row, r], 1), :],
                                  dst_ref.at[slot, pl.ds(row0 + r, 1), :], sem.at[slot]).start(priority=k % 2)
        return carry
    lax.fori_loop(0, count // group, issue, 0)


def _wait_slot(src_hbm, dst_ref, slot, sem):
    rows = dst_ref.shape[1]
    pltpu.make_async_copy(src_hbm.at[pl.ds(0, rows), :], dst_ref.at[slot], sem.at[slot]).wait()


def _combine_kernel(pos_ref, pos_next_ref, y_hbm, h_ref, gate_ref, g_ref, b_ref, o_ref, buf_ref, sem, *, alpha):
    i = pl.program_id(0)
    n = pl.num_programs(0)
    slot = i % 2
    rows = h_ref.shape[0]

    def start(idx_ref, into):
        for k in range(2):
            _start_row_copies(y_hbm, idx_ref, k, buf_ref, into, k * rows, sem, rows)

    @pl.when(i == 0)
    def _():
        start(pos_ref, 0)

    @pl.when(i + 1 < n)
    def _():
        start(pos_next_ref, 1 - slot)

    _wait_slot(y_hbm, buf_ref, slot, sem)

    def norm_rows(j, carry):
        r0 = pl.multiple_of(j * SUBLANES, SUBLANES)
        gates = gate_ref[pl.ds(r0, SUBLANES), :]
        ffn = (buf_ref[slot, pl.ds(r0, SUBLANES), :] * gates[:, 0:1]
               + buf_ref[slot, pl.ds(rows + r0, SUBLANES), :] * gates[:, 1:2])
        z = alpha * h_ref[pl.ds(r0, SUBLANES), :] + ffn
        o_ref[pl.ds(r0, SUBLANES), :] = _layer_norm_rows(z, g_ref[...], b_ref[...])
        return carry
    lax.fori_loop(0, rows // SUBLANES, norm_rows, 0, unroll=8)


def _combine_ln(y, pos2, h1, gate_lanes, ln_g, ln_b, alpha):
    n, d = h1.shape
    tb = min(COMBINE_ROWS, n)
    nblk = n // tb
    pos_blocks = pos2.reshape(nblk, tb, 2).transpose(0, 2, 1)
    kern = functools.partial(_combine_kernel, alpha=alpha)
    return pl.pallas_call(
        kern,
        grid=(nblk,),
        in_specs=[pl.BlockSpec((1, 2, tb), lambda i: (i, 0, 0), memory_space=pltpu.SMEM),
                  pl.BlockSpec((1, 2, tb), lambda i: (jnp.minimum(i + 1, nblk - 1), 0, 0), memory_space=pltpu.SMEM),
                  pl.BlockSpec(memory_space=pl.ANY),
                  pl.BlockSpec((tb, d), lambda i: (i, 0)),
                  pl.BlockSpec((tb, LANES), lambda i: (i, 0)),
                  pl.BlockSpec((1, d), lambda i: (0, 0)),
                  pl.BlockSpec((1, d), lambda i: (0, 0))],
        out_specs=pl.BlockSpec((tb, d), lambda i: (i, 0)),
        out_shape=jax.ShapeDtypeStruct((n, d), F32),
        scratch_shapes=[pltpu.VMEM((2, 2 * tb, d), F32), pltpu.SemaphoreType.DMA((2,))],
        compiler_params=_cparams(("arbitrary",), 32),
        name="moe_combine_ln2",
    )(pos_blocks, pos_blocks, y, h1, gate_lanes, ln_g, ln_b)


def _cast_weight(dst_ref, src_ref):
    rows = src_ref.shape[1]
    step = min(256, rows)

    def body(i, carry):
        r0 = pl.multiple_of(i * step, step)
        dst_ref[pl.ds(r0, step), :] = src_ref[0, pl.ds(r0, step), :].astype(BF16)
        return carry
    lax.fori_loop(0, rows // step, body, 0)


ITEM_COMPUTE = 1
ITEM_NEW_WEIGHTS = 2
ITEM_ZERO_FILL = 4


def _expert_hidden(words, w1b_ref, w3b_ref):
    x_lo, x_hi = _unpack_bf16_halves(words)
    half = x_lo.shape[1]

    def project(w_ref):
        return (jnp.dot(x_lo, w_ref[0:half, :], preferred_element_type=F32)
                + jnp.dot(x_hi, w_ref[half:2 * half, :], preferred_element_type=F32))

    a = project(w1b_ref)
    b = project(w3b_ref)
    return (a * jax.nn.sigmoid(a) * b).astype(BF16)


def _moe_up_gather_kernel(e_ref, cw_ref, bi_ref, bo_ref, co_ref, flag_ref, tok_ref, tok1_ref, tok2_ref, hp_hbm,
                          w1_ref, w3_ref, o_ref, xs_ref, buf_ref, w1b_ref, w3b_ref, sem):
    it = pl.program_id(0)
    last = pl.num_programs(0) - 1
    flag = flag_ref[it]
    n_slots, rows = buf_ref.shape[0], buf_ref.shape[1]
    slot = it % n_slots

    def computes(item):
        return (item <= last) & ((flag_ref[jnp.minimum(item, last)] & ITEM_COMPUTE) != 0)

    @pl.when(it == 0)
    def _():
        _start_row_copies(hp_hbm, tok_ref, 0, buf_ref, 0, 0, sem, rows)

        @pl.when(computes(1))
        def _():
            _start_row_copies(hp_hbm, tok1_ref, 0, buf_ref, 1, 0, sem, rows)

    @pl.when(computes(it + 2))
    def _():
        _start_row_copies(hp_hbm, tok2_ref, 0, buf_ref, (it + 2) % n_slots, 0, sem, rows)

    @pl.when((flag & ITEM_NEW_WEIGHTS) != 0)
    def _():
        _cast_weight(w1b_ref, w1_ref)
        _cast_weight(w3b_ref, w3_ref)

    @pl.when((flag & ITEM_COMPUTE) != 0)
    def _():
        _wait_slot(hp_hbm, buf_ref, slot, sem)
        words = buf_ref[slot]
        xs_ref[...] = words
        o_ref[...] = _expert_hidden(words, w1b_ref, w3b_ref)

    @pl.when((flag & ITEM_ZERO_FILL) != 0)
    def _():
        o_ref[...] = jnp.zeros(o_ref.shape, o_ref.dtype)
        xs_ref[...] = jnp.zeros(xs_ref.shape, xs_ref.dtype)


def _moe_up_kernel(e_ref, cw_ref, bi_ref, bo_ref, co_ref, flag_ref, x_ref, w1_ref, w3_ref, o_ref, w1b_ref, w3b_ref):
    flag = flag_ref[pl.program_id(0)]

    @pl.when((flag & ITEM_NEW_WEIGHTS) != 0)
    def _():
        _cast_weight(w1b_ref, w1_ref)
        _cast_weight(w3b_ref, w3_ref)

    @pl.when((flag & ITEM_COMPUTE) != 0)
    def _():
        o_ref[...] = _expert_hidden(x_ref[...], w1b_ref, w3b_ref)

    @pl.when((flag & ITEM_ZERO_FILL) != 0)
    def _():
        o_ref[...] = jnp.zeros(o_ref.shape, o_ref.dtype)


def _moe_down_kernel(e_ref, cw_ref, bi_ref, bo_ref, co_ref, flag_ref, h0_ref, h1_ref, w2_ref, o_ref, w2b_ref):
    flag = flag_ref[pl.program_id(0)]

    @pl.when((flag & ITEM_NEW_WEIGHTS) != 0)
    def _():
        _cast_weight(w2b_ref, w2_ref)

    @pl.when((flag & ITEM_COMPUTE) != 0)
    def _():
        fc = h0_ref.shape[1]
        o_ref[...] = (jnp.dot(h0_ref[...], w2b_ref[0:fc, :], preferred_element_type=F32)
                      + jnp.dot(h1_ref[...], w2b_ref[fc:2 * fc, :], preferred_element_type=F32))

    @pl.when((flag & ITEM_ZERO_FILL) != 0)
    def _():
        o_ref[...] = jnp.zeros(o_ref.shape, o_ref.dtype)


def _work_items(blocks_per_expert, block_start, n_chunks, n_blocks):
    n_items = n_chunks * n_blocks
    per_e = n_chunks * blocks_per_expert
    end = jnp.cumsum(per_e)
    start = end - per_e
    it = jnp.arange(n_items, dtype=I32)
    total = end[-1]
    used_blocks = total // n_chunks
    itc = jnp.minimum(it, total - 1)
    e = jnp.sum(itc[:, None] >= end[None, :], axis=1).astype(I32)
    local = itc - start[e]
    nb = jnp.maximum(blocks_per_expert[e], 1)
    c = local // nb
    r = local - c * nb
    b_in = block_start[e] + r
    active = it < total
    spare = jnp.maximum(it - total, 0)
    b_out = jnp.where(active, b_in, used_blocks + spare // n_chunks)
    c_out = jnp.where(active, c, spare % n_chunks)
    first = active & (r == 0)
    flag = jnp.where(active, ITEM_COMPUTE + ITEM_NEW_WEIGHTS * first.astype(I32), ITEM_ZERO_FILL)
    later_first = lax.cummin(jnp.where(first, it, n_items)[::-1])[::-1]
    next_start = jnp.concatenate([later_first[1:], jnp.full((1,), n_items, I32)])
    ahead = jnp.minimum(next_start, n_items - 1)
    use_next = active & ~first & (next_start < n_items)
    e_w = jnp.where(use_next, e[ahead], e)
    c_w = jnp.where(use_next, c[ahead], c)
    return tuple(v.astype(I32) for v in (e_w, c_w, b_in, b_out, c_out, flag))


def _moe_up_gather(hp, buf_tok, w1, w3, items):
    p = buf_tok.shape[0]
    half = hp.shape[1]
    d, f = w1.shape[1], w1.shape[2]
    fc = f // 2
    n_items = items[0].shape[0]
    nblk = p // MOE_ROWS
    tok_blocks = buf_tok.reshape(nblk, 1, MOE_ROWS)
    wspec = pl.BlockSpec((1, d, fc), lambda it, e, cw, bi, bo, co, fl: (e[it], 0, 0))
    lookahead = 2

    def tok_spec(ahead):
        return pl.BlockSpec((1, 1, MOE_ROWS),
                            lambda it, e, cw, bi, bo, co, fl: (bi[jnp.minimum(it + ahead, n_items - 1)], 0, 0),
                            memory_space=pltpu.SMEM)

    return pl.pallas_call(
        _moe_up_gather_kernel,
        grid_spec=pltpu.PrefetchScalarGridSpec(
            num_scalar_prefetch=6,
            grid=(n_items,),
            in_specs=[tok_spec(k) for k in range(lookahead + 1)] + [pl.BlockSpec(memory_space=pl.ANY), wspec, wspec],
            out_specs=[pl.BlockSpec((MOE_ROWS, fc), lambda it, e, cw, bi, bo, co, fl: (bo[it], 0)),
                       pl.BlockSpec((MOE_ROWS, half), lambda it, e, cw, bi, bo, co, fl: (bo[it], 0))],
            scratch_shapes=[pltpu.VMEM((lookahead + 1, MOE_ROWS, half), I32), pltpu.VMEM((d, fc), BF16),
                            pltpu.VMEM((d, fc), BF16), pltpu.SemaphoreType.DMA((lookahead + 1,))]),
        out_shape=[jax.ShapeDtypeStruct((p, fc), BF16), jax.ShapeDtypeStruct((p, half), I32)],
        compiler_params=_cparams(("arbitrary",), 56),
        name="moe_up_gather",
    )(*items, *([tok_blocks] * (lookahead + 1)), hp, w1, w3)


def _moe_up(xs, w1, w3, items):
    p, half = xs.shape
    d, f = w1.shape[1], w1.shape[2]
    fc = f // 2
    assert d == 2 * half
    wspec = pl.BlockSpec((1, d, fc), lambda it, e, cw, bi, bo, co, fl: (e[it], 0, 1))
    return pl.pallas_call(
        _moe_up_kernel,
        grid_spec=pltpu.PrefetchScalarGridSpec(
            num_scalar_prefetch=6,
            grid=(items[0].shape[0],),
            in_specs=[pl.BlockSpec((MOE_ROWS, half), lambda it, e, cw, bi, bo, co, fl: (bi[it], 0)), wspec, wspec],
            out_specs=pl.BlockSpec((MOE_ROWS, fc), lambda it, e, cw, bi, bo, co, fl: (bo[it], 0)),
            scratch_shapes=[pltpu.VMEM((d, fc), BF16), pltpu.VMEM((d, fc), BF16)]),
        out_shape=jax.ShapeDtypeStruct((p, fc), BF16),
        compiler_params=_cparams(("arbitrary",), 56),
        name="moe_up",
    )(*items, xs, w1, w3)


def _moe_down(h0, h1, w2, items, d_chunk):
    p, fc = h0.shape
    f = 2 * fc
    d = w2.shape[2]
    hspec = pl.BlockSpec((MOE_ROWS, fc), lambda it, e, cw, bi, bo, co, fl: (bi[it], 0))
    return pl.pallas_call(
        _moe_down_kernel,
        grid_spec=pltpu.PrefetchScalarGridSpec(
            num_scalar_prefetch=6,
            grid=(items[0].shape[0],),
            in_specs=[hspec, hspec,
                      pl.BlockSpec((1, f, d_chunk), lambda it, e, cw, bi, bo, co, fl: (e[it], 0, cw[it]))],
            out_specs=pl.BlockSpec((MOE_ROWS, d_chunk), lambda it, e, cw, bi, bo, co, fl: (bo[it], co[it])),
            scratch_shapes=[pltpu.VMEM((f, d_chunk), BF16)]),
        out_shape=jax.ShapeDtypeStruct((p, d), F32),
        compiler_params=_cparams(("arbitrary",), 56),
        name="moe_down",
    )(*items, h0, h1, w2)


def _hier_moe_ln(h1, hp, logits, w1, w3, w2, ln_g, ln_b, alpha):
    n, d = h1.shape
    e_lanes, gate_lanes, lane_counts = _route(logits)
    a = 2 * n
    e_flat = e_lanes[:, 0:2].reshape(a)
    rank = e_lanes[:, 2:4].reshape(a)
    counts = lane_counts[0, N_GROUPS:N_GROUPS + N_EXPERTS].astype(I32)
    blocks_per_expert = (counts + MOE_ROWS - 1) // MOE_ROWS
    block_start = jnp.cumsum(blocks_per_expert) - blocks_per_expert
    onehot = e_flat[:, None] == jnp.arange(N_EXPERTS, dtype=I32)[None, :]
    pos = jnp.sum(jnp.where(onehot, block_start[None, :], 0), axis=1) * MOE_ROWS + rank
    n_blocks = (a + MOE_ROWS - 1) // MOE_ROWS + N_EXPERTS
    buf_tok = jnp.zeros((n_blocks * MOE_ROWS,), I32).at[pos].set(jnp.arange(a, dtype=I32) // 2)

    d_chunk = min(4096, d)
    up_items = _work_items(blocks_per_expert, block_start, 1, n_blocks)
    h0, xs = _moe_up_gather(hp, buf_tok, w1, w3, up_items)
    h1_mid = _moe_up(xs, w1, w3, up_items)
    y = _moe_down(h0, h1_mid, w2, _work_items(blocks_per_expert, block_start, d // d_chunk, n_blocks), d_chunk)
    return _combine_ln(y, pos.reshape(n, 2), h1, gate_lanes, ln_g, ln_b, alpha)


def kernel(x, w_in, conv_dw_w, conv_dw_b, conv_ln_g, conv_ln_b, w_out, ln1_g, ln1_b, w_router_group, b_router_group, w_router_expert, b_router_expert, w_expert_gate, w_expert_up, w_expert_down, ln2_g, ln2_b):
    b, s, d = x.shape
    depth = w_in.shape[0]
    assert depth == 1
    alpha = float((2.0 * depth) ** 0.25)
    n = b * s
    dc = conv_dw_w.shape[2]
    d_attn = N_HEADS * HEAD_DIM
    d_kv = N_KV_HEADS * HEAD_DIM
    d_qi = IDX_HEADS * IDX_DIM
    n_small = IDX_DIM + IDX_HEADS
    assert w_in.shape[2] == 2 * dc + d_attn + 2 * d_kv + d_qi + n_small

    x2 = x.reshape(n, d)
    w = w_in[0].astype(BF16)
    o_qkvi = 2 * dc
    o_small = o_qkvi + d_attn + 2 * d_kv + d_qi
    w_small = jnp.pad(w[:, o_small:], ((0, 0), (0, LANES - n_small)))
    qkvi_scale = jnp.concatenate([jnp.full((d_attn,), HEAD_DIM ** -0.5 * LOG2E, F32), jnp.ones((2 * d_kv,), F32),
                                  jnp.full((d_qi,), IDX_DIM ** -0.5, F32)])[None, :]

    xb, small = _proj_idx_and_cast(x2, w_small)
    glu = _proj_glu(xb, w, dc)
    qkvi = _proj_scale(xb, w, o_qkvi, o_small - o_qkvi, qkvi_scale, BF16, "proj_qkvi")

    conv_out = _conformer_conv(glu.reshape(b, s, dc), conv_dw_w[0], conv_dw_b, conv_ln_g, conv_ln_b)
    attn_out = _dsa_attention(qkvi.reshape(b, s, -1), small.reshape(b, s, LANES), b, s)

    n_route = N_GROUPS + N_EXPERTS
    w_router = jnp.pad(jnp.concatenate([w_router_group[0], w_router_expert[0]], axis=1),
                       ((0, 0), (0, LANES - n_route))).astype(BF16)
    b_router = jnp.pad(jnp.concatenate([b_router_group[0], b_router_expert[0]]), (0, LANES - n_route))[None, :]
    h1, hp, logits = _out_proj_ln(conv_out.reshape(n, dc), attn_out.reshape(n, d_attn),
                                  w_out[0, :dc].astype(BF16), w_out[0, dc:].astype(BF16),
                                  x2, ln1_g, ln1_b, w_router, b_router, alpha)

    out = _hier_moe_ln(h1, hp, logits, w_expert_gate[0], w_expert_up[0], w_expert_down[0], ln2_g, ln2_b, alpha)
    return out.reshape(b, s, d)
```

```python
import functools
import math

import jax
import jax.numpy as jnp
from jax import lax
from jax.experimental import pallas as pl
from jax.experimental.pallas import tpu as pltpu

F32 = jnp.float32
BF16 = jnp.bfloat16
I32 = jnp.int32

CONV_WIDTH = 31
N_HEADS = 16
HEAD_DIM = 128
N_KV_HEADS = 4
IDX_HEADS = 16
IDX_DIM = 64
TOPK_MAX = 256
N_GROUPS = 4
EXPERTS_PER_GROUP = 8
N_EXPERTS = N_GROUPS * EXPERTS_PER_GROUP
LN_EPS = 1e-5

LANES = 128
SUBLANES = 8
BF16_ROWS = 16
MIB = 1024 * 1024

Q_TILE = 128
KEY_CHUNK = 512
CONV_HALO = 32
MOE_ROWS = 256
COMBINE_ROWS = 128
LN_GROUPS = 4
INT_MIN = -(2 ** 31)
INT_MAX = 2 ** 31 - 1
LOG2E = math.log2(math.e)
NEG_BIG = -1e30
M_INIT = -1e20


def _cparams(semantics, vmem_mib):
    return pltpu.CompilerParams(dimension_semantics=semantics, vmem_limit_bytes=vmem_mib * MIB)


def _layer_norm_rows(z, g, b):
    mu = jnp.mean(z, axis=-1, keepdims=True)
    zc = z - mu
    var = jnp.mean(zc * zc, axis=-1, keepdims=True)
    return zc * lax.rsqrt(var + LN_EPS) * g + b


def _mm_glu_kernel(x_ref, wa_ref, wg_ref, o_ref):
    x = x_ref[...]
    a = jnp.dot(x, wa_ref[...], preferred_element_type=F32)
    g = jnp.dot(x, wg_ref[...], preferred_element_type=F32)
    o_ref[...] = (a * jax.nn.sigmoid(g)).astype(o_ref.dtype)


def _mm_scale_kernel(x_ref, w_ref, s_ref, o_ref):
    acc = jnp.dot(x_ref[...], w_ref[...], preferred_element_type=F32)
    o_ref[...] = (acc * s_ref[...]).astype(o_ref.dtype)


def _mm_tiles(n, k, cols):
    tm = min(1024, n)
    tn = min(512, cols)
    assert n % tm == 0 and cols % tn == 0
    return tm, tn


def _proj_glu(xb, w, cols):
    n, k = xb.shape
    tm, tn = _mm_tiles(n, k, cols)
    gate_block = cols // tn
    return pl.pallas_call(
        _mm_glu_kernel,
        grid=(n // tm, cols // tn),
        in_specs=[pl.BlockSpec((tm, k), lambda i, j: (i, 0)),
                  pl.BlockSpec((k, tn), lambda i, j: (0, j)),
                  pl.BlockSpec((k, tn), lambda i, j: (0, j + gate_block))],
        out_specs=pl.BlockSpec((tm, tn), lambda i, j: (i, j)),
        out_shape=jax.ShapeDtypeStruct((n, cols), F32),
        compiler_params=_cparams(("parallel", "arbitrary"), 48),
        name="proj_glu",
    )(xb, w, w)


def _proj_scale(xb, w, first_col, cols, scale, out_dtype, name):
    n, k = xb.shape
    tm, tn = _mm_tiles(n, k, cols)
    assert first_col % tn == 0
    first_block = first_col // tn
    return pl.pallas_call(
        _mm_scale_kernel,
        grid=(n // tm, cols // tn),
        in_specs=[pl.BlockSpec((tm, k), lambda i, j: (i, 0)),
                  pl.BlockSpec((k, tn), lambda i, j: (0, j + first_block)),
                  pl.BlockSpec((1, tn), lambda i, j: (0, j))],
        out_specs=pl.BlockSpec((tm, tn), lambda i, j: (i, j)),
        out_shape=jax.ShapeDtypeStruct((n, cols), out_dtype),
        compiler_params=_cparams(("parallel", "arbitrary"), 48),
        name=name,
    )(xb, w, scale)


def _idx_cast_kernel(x_ref, w_ref, xb_ref, o_ref):
    xb = x_ref[...].astype(BF16)
    xb_ref[...] = xb
    o_ref[...] = jnp.dot(xb, w_ref[...], preferred_element_type=F32)


def _proj_idx_and_cast(x2, w_small):
    n, k = x2.shape
    tm = min(256, n)
    return pl.pallas_call(
        _idx_cast_kernel,
        grid=(n // tm,),
        in_specs=[pl.BlockSpec((tm, k), lambda i: (i, 0)),
                  pl.BlockSpec((k, LANES), lambda i: (0, 0))],
        out_specs=[pl.BlockSpec((tm, k), lambda i: (i, 0)),
                   pl.BlockSpec((tm, LANES), lambda i: (i, 0))],
        out_shape=[jax.ShapeDtypeStruct((n, k), BF16), jax.ShapeDtypeStruct((n, LANES), F32)],
        compiler_params=_cparams(("parallel",), 32),
        name="proj_idx",
    )(x2, w_small)


def _conv_kernel(cur_ref, prev_ref, w_ref, b_ref, g_ref, beta_ref, o_ref, sh_ref, y_ref, *, rows, lane_chunk):
    t = pl.program_id(1)
    dc = cur_ref.shape[2]
    row_sub = 32
    sh_ref[0, 0:CONV_HALO, :] = jnp.where(t > 0, prev_ref[0], 0.0)
    sh_ref[0, CONV_HALO:CONV_HALO + rows, :] = cur_ref[0]
    first = CONV_HALO - (CONV_WIDTH - 1)
    shifted_rows = rows + CONV_HALO - SUBLANES
    for lc in range(dc // lane_chunk):
        cols = slice(lc * lane_chunk, (lc + 1) * lane_chunk)
        for r in range(1, SUBLANES):
            sh_ref[r, 0:shifted_rows, cols] = sh_ref[0, r:r + shifted_rows, cols]

    for r0 in range(0, rows, row_sub):
        for lc in range(dc // lane_chunk):
            cols = slice(lc * lane_chunk, (lc + 1) * lane_chunk)
            acc = jnp.zeros((row_sub, lane_chunk), F32)
            for j in range(CONV_WIDTH):
                shift, base = (first + j) % SUBLANES, (first + j) // SUBLANES * SUBLANES
                acc = acc + sh_ref[shift, r0 + base:r0 + base + row_sub, cols] * w_ref[j:j + 1, cols]
            y_ref[r0:r0 + row_sub, cols] = acc + b_ref[:, cols]

    def norm_rows(i, carry):
        r0 = pl.multiple_of(i * BF16_ROWS, BF16_ROWS)
        yn = _layer_norm_rows(y_ref[pl.ds(r0, BF16_ROWS), :], g_ref[...], beta_ref[...])
        o_ref[0, pl.ds(r0, BF16_ROWS), :] = (yn * jax.nn.sigmoid(yn)).astype(o_ref.dtype)
        return carry

    lax.fori_loop(0, rows // BF16_ROWS, norm_rows, 0, unroll=4)


def _conformer_conv(glu, w_dw, b_dw, g_ln, b_ln):
    b, s, dc = glu.shape
    rows = min(256, s)
    halo_blocks = rows // CONV_HALO
    lane_chunk = min(512, dc)
    kern = functools.partial(_conv_kernel, rows=rows, lane_chunk=lane_chunk)
    return pl.pallas_call(
        kern,
        grid=(b, s // rows),
        in_specs=[pl.BlockSpec((1, rows, dc), lambda bi, t: (bi, t, 0)),
                  pl.BlockSpec((1, CONV_HALO, dc), lambda bi, t: (bi, jnp.maximum(t * halo_blocks - 1, 0), 0)),
                  pl.BlockSpec((CONV_WIDTH, dc), lambda bi, t: (0, 0)),
                  pl.BlockSpec((1, dc), lambda bi, t: (0, 0)),
                  pl.BlockSpec((1, dc), lambda bi, t: (0, 0)),
                  pl.BlockSpec((1, dc), lambda bi, t: (0, 0))],
        out_specs=pl.BlockSpec((1, rows, dc), lambda bi, t: (bi, t, 0)),
        out_shape=jax.ShapeDtypeStruct((b, s, dc), BF16),
        scratch_shapes=[pltpu.VMEM((SUBLANES, CONV_HALO + rows, dc), F32), pltpu.VMEM((rows, dc), F32)],
        compiler_params=_cparams(("parallel", "arbitrary"), 40),
        name="conformer_conv",
    )(glu, glu, w_dw, b_dw, g_ln, b_ln)


def _dsa_kernel(q_ref, qi_ref, k_ref, v_ref, kidx_ref, w_ref, o_ref,
                ke_ref, ko_ref, vt_ref, key_ref, cut_ref, nd_ref, qs_ref, qis_ref, m_ref, l_ref, acc_ref,
                *, seq, chunk, topk):
    tb = pl.program_id(1)
    t0 = tb * Q_TILE
    n_chunks = (t0 + Q_TILE + chunk - 1) // chunk
    rep = N_HEADS // N_KV_HEADS
    nt = (((1,), (1,)), ((), ()))

    @pl.when(tb == 0)
    def _():
        def build(c, carry):
            r0 = pl.multiple_of(c * chunk, chunk)
            kx = kidx_ref[0, pl.ds(r0, chunk), :]
            lane = lax.broadcasted_iota(I32, kx.shape, 1)
            ke_ref[pl.ds(r0, chunk), :] = jnp.where(lane < IDX_DIM, kx, 0.0).astype(BF16)
            ko_ref[pl.ds(r0, chunk), :] = jnp.where(lane >= IDX_DIM, pltpu.roll(kx, IDX_DIM, 1), 0.0).astype(BF16)
            for g in range(N_KV_HEADS):
                vg = v_ref[0, pl.ds(r0, chunk), g * HEAD_DIM:(g + 1) * HEAD_DIM]
                vt_ref[c, g * HEAD_DIM:(g + 1) * HEAD_DIM, :] = vg.astype(F32).T.astype(BF16)
            return carry
        lax.fori_loop(0, seq // chunk, build, 0)

    for g in range(N_KV_HEADS):
        for r in range(rep):
            h = g * rep + r
            qs_ref[g, r * Q_TILE:(r + 1) * Q_TILE, :] = q_ref[0, :, h * HEAD_DIM:(h + 1) * HEAD_DIM]
    for jj in range(IDX_HEADS // 4):
        for half in range(2):
            pair = 2 * jj + half
            qis_ref[jj, half * Q_TILE:(half + 1) * Q_TILE, :] = qi_ref[0, :, pair * LANES:(pair + 1) * LANES]
    w_t = w_ref[0].T * (IDX_HEADS ** -0.5)

    key_row = lax.broadcasted_iota(I32, (chunk, Q_TILE), 0)
    q_pos = t0 + lax.broadcasted_iota(I32, (chunk, Q_TILE), 1)

    def score_chunk(c, carry):
        r0 = pl.multiple_of(c * chunk, chunk)
        ke = ke_ref[pl.ds(r0, chunk), :]
        ko = ko_ref[pl.ds(r0, chunk), :]
        acc = jnp.zeros((chunk, Q_TILE), F32)
        for jj in range(IDX_HEADS // 4):
            rhs = qis_ref[jj]
            de = lax.dot_general(ke, rhs, nt, preferred_element_type=F32)
            do = lax.dot_general(ko, rhs, nt, preferred_element_type=F32)
            for half in range(2):
                h_even = 2 * (2 * jj + half)
                cols = slice(half * Q_TILE, (half + 1) * Q_TILE)
                acc = acc + w_t[IDX_DIM + h_even:IDX_DIM + h_even + 1, :] * jnp.maximum(de[:, cols], 0.0)
                acc = acc + w_t[IDX_DIM + h_even + 1:IDX_DIM + h_even + 2, :] * jnp.maximum(do[:, cols], 0.0)
        bits = lax.bitcast_convert_type(acc, I32)
        key = bits ^ ((bits >> 31) & 0x7FFFFFFF)
        key_ref[c] = jnp.where(r0 + key_row <= q_pos, key, INT_MIN)
        return carry

    lax.fori_loop(0, n_chunks, score_chunk, 0)

    def count_ge(cand):
        def count_chunk(c, cnts):
            cnts = list(cnts)
            for r in range(chunk // SUBLANES):
                slab = key_ref[c, r * SUBLANES:(r + 1) * SUBLANES, :]
                cnts[r % len(cnts)] = cnts[r % len(cnts)] + jnp.where(slab >= cand, 1.0, 0.0)
            return tuple(cnts)

        zero = jnp.zeros((SUBLANES, Q_TILE), F32)
        cnts = lax.fori_loop(0, n_chunks, count_chunk, (zero, zero, zero, zero))
        return jnp.sum((cnts[0] + cnts[1]) + (cnts[2] + cnts[3]), axis=0, keepdims=True)

    def bit_step(i, carry):
        t_u, n_ge = carry
        cand_u = t_u | (jnp.int32(1) << (31 - i))
        tot = count_ge(cand_u ^ INT_MIN)
        keep = tot >= float(topk)
        return jnp.where(keep, cand_u, t_u), jnp.where(keep, tot, n_ge)

    t_u, n_ge = lax.fori_loop(0, 32, bit_step, (jnp.zeros((1, Q_TILE), I32), jnp.zeros((1, Q_TILE), F32)))
    thr = jnp.maximum(t_u ^ INT_MIN, INT_MIN + 1)

    cut_ref[...] = jnp.full(cut_ref.shape, INT_MAX, I32)

    @pl.when(jnp.max(n_ge) > topk)
    def _():
        def count_where(pred):
            def count_chunk(c, cnt):
                r0 = pl.multiple_of(c * chunk, chunk)
                hit = jnp.where(pred(key_ref[c], r0 + key_row), 1, 0)
                return cnt + jnp.sum(hit, axis=0, keepdims=True)
            return lax.fori_loop(0, n_chunks, count_chunk, jnp.zeros((1, Q_TILE), I32))

        keep = topk - count_where(lambda kc, pos: kc > thr)
        pos_bits = seq.bit_length()

        def bit_step(i, cut):
            cand = cut | (jnp.int32(1) << (pos_bits - 1 - i))
            taken = count_where(lambda kc, pos: jnp.where(kc == thr, pos, INT_MAX) < cand)
            return jnp.where(taken <= keep, cand, cut)
        cut_ref[...] = lax.fori_loop(0, pos_bits, bit_step, jnp.zeros((1, Q_TILE), I32))

    cut = cut_ref[...]

    m_ref[...] = jnp.full(m_ref.shape, M_INIT, F32)
    l_ref[...] = jnp.zeros(l_ref.shape, F32)
    acc_ref[...] = jnp.zeros(acc_ref.shape, F32)

    def attend_chunk(c, carry):
        r0 = pl.multiple_of(c * chunk, chunk)
        pos = r0 + key_row
        kc = key_ref[c]
        rank_pos = jnp.where(kc > thr, -1, jnp.where(kc == thr, pos, INT_MAX))
        nd_ref[...] = jnp.where(rank_pos < cut, (pos - q_pos).astype(F32), NEG_BIG)
        def logits(g):
            kg = k_ref[0, pl.ds(r0, chunk), g * HEAD_DIM:(g + 1) * HEAD_DIM]
            return lax.dot_general(kg, qs_ref[g], nt, preferred_element_type=F32)

        s_next = logits(0)
        for g in range(N_KV_HEADS):
            s_all = s_next
            if g + 1 < N_KV_HEADS:
                s_next = logits(g + 1)
            probs = []
            alphas = []
            for r in range(rep):
                h = g * rep + r
                slope = float(2.0 ** (-8.0 * (h + 1) / N_HEADS)) * LOG2E
                sr = s_all[:, r * Q_TILE:(r + 1) * Q_TILE] + slope * nd_ref[...]
                m_old = m_ref[h]
                m_new = jnp.maximum(m_old, jnp.max(sr, axis=0, keepdims=True))
                alpha = jnp.exp2(m_old - m_new)
                p = jnp.exp2(sr - m_new)
                l_ref[h] = alpha * l_ref[h] + jnp.sum(p, axis=0, keepdims=True)
                m_ref[h] = m_new
                probs.append(p.astype(BF16))
                alphas.append(alpha)
            vt = vt_ref[c, g * HEAD_DIM:(g + 1) * HEAD_DIM, :]
            pv = jnp.dot(vt, jnp.concatenate(probs, axis=1), preferred_element_type=F32)
            acc_ref[g] = jnp.concatenate(alphas, axis=1) * acc_ref[g] + pv
        return carry

    lax.fori_loop(0, n_chunks, attend_chunk, 0)

    for g in range(N_KV_HEADS):
        for r in range(rep):
            h = g * rep + r
            o_t = acc_ref[g, :, r * Q_TILE:(r + 1) * Q_TILE] * (1.0 / l_ref[h])
            o_ref[0, :, h * HEAD_DIM:(h + 1) * HEAD_DIM] = o_t.T.astype(o_ref.dtype)


def _dsa_attention(qkvi, small, b, s):
    d_attn = N_HEADS * HEAD_DIM
    d_kv = N_KV_HEADS * HEAD_DIM
    d_qi = IDX_HEADS * IDX_DIM
    rep = N_HEADS // N_KV_HEADS
    chunk = min(KEY_CHUNK, s)
    topk = min(TOPK_MAX, s // 4)
    assert s % chunk == 0 and chunk % Q_TILE == 0 and d_attn % d_qi == 0 and d_attn % d_kv == 0
    assert Q_TILE == LANES and HEAD_DIM == LANES and 2 * IDX_DIM == LANES
    kern = functools.partial(_dsa_kernel, seq=s, chunk=chunk, topk=topk)
    return pl.pallas_call(
        kern,
        grid=(b, s // Q_TILE),
        in_specs=[pl.BlockSpec((1, Q_TILE, d_attn), lambda bi, t: (bi, t, 0)),
                  pl.BlockSpec((1, Q_TILE, d_qi), lambda bi, t: (bi, t, (d_attn + 2 * d_kv) // d_qi)),
                  pl.BlockSpec((1, s, d_kv), lambda bi, t: (bi, 0, d_attn // d_kv)),
                  pl.BlockSpec((1, s, d_kv), lambda bi, t: (bi, 0, d_attn // d_kv + 1)),
                  pl.BlockSpec((1, s, LANES), lambda bi, t: (bi, 0, 0)),
                  pl.BlockSpec((1, Q_TILE, LANES), lambda bi, t: (bi, t, 0))],
        out_specs=pl.BlockSpec((1, Q_TILE, d_attn), lambda bi, t: (bi, t, 0)),
        out_shape=jax.ShapeDtypeStruct((b, s, d_attn), BF16),
        scratch_shapes=[pltpu.VMEM((s, LANES), BF16),
                        pltpu.VMEM((s, LANES), BF16),
                        pltpu.VMEM((s // chunk, d_kv, chunk), BF16),
                        pltpu.VMEM((s // chunk, chunk, Q_TILE), I32),
                        pltpu.VMEM((1, Q_TILE), I32),
                        pltpu.VMEM((chunk, Q_TILE), F32),
                        pltpu.VMEM((N_KV_HEADS, rep * Q_TILE, HEAD_DIM), BF16),
                        pltpu.VMEM((IDX_HEADS // 4, 2 * Q_TILE, LANES), BF16),
                        pltpu.VMEM((N_HEADS, 1, Q_TILE), F32),
                        pltpu.VMEM((N_HEADS, 1, Q_TILE), F32),
                        pltpu.VMEM((N_KV_HEADS, HEAD_DIM, rep * Q_TILE), F32)],
        compiler_params=_cparams(("parallel", "arbitrary"), 48),
        name="dsa_attention",
    )(qkvi, qkvi, qkvi, qkvi, small, small)


def _pack_bf16_halves(hb):
    half = hb.shape[1] // 2
    bits = lax.bitcast_convert_type(hb.astype(F32), I32)
    return lax.shift_right_logical(bits[:, :half], 16) | (bits[:, half:] & -65536)


def _unpack_bf16_halves(words):
    lo = lax.bitcast_convert_type(words << 16, F32).astype(BF16)
    hi = lax.bitcast_convert_type(words & -65536, F32).astype(BF16)
    return lo, hi


def _outproj_kernel(conv_ref, attn_ref, wc_ref, wa_ref, x_ref, g_ref, b_ref, wr_ref, br_ref,
                    h_ref, hp_ref, lg_ref, hb_ref, *, alpha):
    j = pl.program_id(1)
    tm, d = h_ref.shape
    tn = x_ref.shape[1]
    n_tiles = d // tn
    mix = jnp.dot(conv_ref[...], wc_ref[...], preferred_element_type=F32)
    mix = mix + jnp.dot(attn_ref[...], wa_ref[...], preferred_element_type=F32)
    z = alpha * x_ref[...] + mix
    for jj in range(n_tiles):
        @pl.when(j == jj)
        def _(jj=jj):
            h_ref[:, jj * tn:(jj + 1) * tn] = z

    @pl.when(j == n_tiles - 1)
    def _():
        def rows(i, carry):
            starts = [pl.multiple_of((LN_GROUPS * i + k) * BF16_ROWS, BF16_ROWS) for k in range(LN_GROUPS)]
            zs = [h_ref[pl.ds(r0, BF16_ROWS), :] for r0 in starts]
            hs = [_layer_norm_rows(z, g_ref[...], b_ref[...]) for z in zs]
            for r0, h in zip(starts, hs):
                h_ref[pl.ds(r0, BF16_ROWS), :] = h
                hb_ref[pl.ds(r0, BF16_ROWS), :] = h.astype(BF16)
            return carry
        lax.fori_loop(0, tm // (LN_GROUPS * BF16_ROWS), rows, 0)

        def pack_rows(i, carry):
            r0 = pl.multiple_of(i * BF16_ROWS, BF16_ROWS)
            hp_ref[pl.ds(r0, BF16_ROWS), :] = _pack_bf16_halves(hb_ref[pl.ds(r0, BF16_ROWS), :])
            return carry
        lax.fori_loop(0, tm // BF16_ROWS, pack_rows, 0, unroll=2)
        lg_ref[...] = jnp.dot(hb_ref[...], wr_ref[...], preferred_element_type=F32) + br_ref[...]


def _out_proj_ln(conv_out, attn_out, w_out_b, x2, ln_g, ln_b, w_router, b_router, alpha):
    n, d = x2.shape
    dc = conv_out.shape[1]
    da = attn_out.shape[1]
    tm = min(512, n)
    tn = min(512, d)
    assert n % tm == 0 and d % tn == 0
    if dc % da == 0:
        w_conv = w_attn = w_out_b
        attn_block = dc // da
    else:
        w_conv, w_attn, attn_block = w_out_b[:dc], w_out_b[dc:], 0
    kern = functools.partial(_outproj_kernel, alpha=alpha)
    return pl.pallas_call(
        kern,
        grid=(n // tm, d // tn),
        in_specs=[pl.BlockSpec((tm, dc), lambda i, j: (i, 0)),
                  pl.BlockSpec((tm, da), lambda i, j: (i, 0)),
                  pl.BlockSpec((dc, tn), lambda i, j: (0, j)),
                  pl.BlockSpec((da, tn), lambda i, j: (attn_block, j)),
                  pl.BlockSpec((tm, tn), lambda i, j: (i, j)),
                  pl.BlockSpec((1, d), lambda i, j: (0, 0)),
                  pl.BlockSpec((1, d), lambda i, j: (0, 0)),
                  pl.BlockSpec((d, LANES), lambda i, j: (0, 0)),
                  pl.BlockSpec((1, LANES), lambda i, j: (0, 0))],
        out_specs=[pl.BlockSpec((tm, d), lambda i, j: (i, 0)),
                   pl.BlockSpec((tm, d // 2), lambda i, j: (i, 0)),
                   pl.BlockSpec((tm, LANES), lambda i, j: (i, 0))],
        out_shape=[jax.ShapeDtypeStruct((n, d), F32), jax.ShapeDtypeStruct((n, d // 2), I32),
                   jax.ShapeDtypeStruct((n, LANES), F32)],
        scratch_shapes=[pltpu.VMEM((tm, d), BF16)],
        compiler_params=_cparams(("parallel", "arbitrary"), 56),
        name="out_proj_ln1",
    )(conv_out, attn_out, w_conv, w_attn, x2, ln_g, ln_b, w_router, b_router)


def _first_argmax(vals, lane, valid):
    masked = jnp.where(valid, vals, -jnp.inf)
    mx = jnp.max(masked, axis=1, keepdims=True)
    idx = jnp.min(jnp.where(valid & (masked == mx), lane, LANES), axis=1, keepdims=True)
    return mx, idx


def _router_kernel(lg_ref, tri_ref, e_ref, gate_ref, cnt_ref):
    @pl.when(pl.program_id(0) == 0)
    def _():
        cnt_ref[...] = jnp.zeros(cnt_ref.shape, F32)

    lg = lg_ref[...]
    lane = lax.broadcasted_iota(I32, lg.shape, 1)
    is_group = lane < N_GROUPS
    g_max, g_sel = _first_argmax(lg, lane, is_group)
    g_den = jnp.sum(jnp.where(is_group, jnp.exp(lg - g_max), 0.0), axis=1, keepdims=True)
    p_group = 1.0 / g_den
    lo = N_GROUPS + g_sel * EXPERTS_PER_GROUP
    in_group = (lane >= lo) & (lane < lo + EXPERTS_PER_GROUP)
    e_max, _ = _first_argmax(lg, lane, in_group)
    e_exp = jnp.where(in_group, jnp.exp(lg - e_max), 0.0)
    prob = e_exp / jnp.sum(e_exp, axis=1, keepdims=True)
    p1, i1 = _first_argmax(prob, lane, in_group)
    p2, i2 = _first_argmax(prob, lane, in_group & (lane != i1))
    norm = p_group / (p1 + p2)
    gate_ref[...] = jnp.where(lane == 0, p1 * norm, jnp.where(lane == 1, p2 * norm, 0.0))
    chosen = jnp.where((lane == i1) | (lane == i2), 1.0, 0.0)
    before = jnp.dot(tri_ref[...], chosen.astype(BF16), preferred_element_type=F32) + cnt_ref[...]
    rank1 = jnp.sum(jnp.where(lane == i1, before, 0.0), axis=1, keepdims=True).astype(I32)
    rank2 = jnp.sum(jnp.where(lane == i2, before, 0.0), axis=1, keepdims=True).astype(I32)
    cnt_ref[...] += jnp.sum(chosen, axis=0, keepdims=True)
    e_ref[...] = jnp.where(lane == 0, i1 - N_GROUPS,
                           jnp.where(lane == 1, i2 - N_GROUPS,
                                     jnp.where(lane == 2, rank1, jnp.where(lane == 3, rank2, 0))))


def _route(logits):
    n = logits.shape[0]
    tm = min(512, n)
    tri = jnp.tril(jnp.ones((tm, tm), BF16), -1)
    spec = pl.BlockSpec((tm, LANES), lambda i: (i, 0))
    return pl.pallas_call(
        _router_kernel,
        grid=(n // tm,),
        in_specs=[spec, pl.BlockSpec((tm, tm), lambda i: (0, 0))],
        out_specs=[spec, spec, pl.BlockSpec((1, LANES), lambda i: (0, 0))],
        out_shape=[jax.ShapeDtypeStruct((n, LANES), I32), jax.ShapeDtypeStruct((n, LANES), F32),
                   jax.ShapeDtypeStruct((1, LANES), F32)],
        compiler_params=_cparams(("arbitrary",), 32),
        name="router",
    )(logits, tri)


def _start_row_copies(src_hbm, idx_ref, idx_row, dst_ref, slot, row0, sem, count, alternate=True):
    group = 8

    def issue(i, carry):
        for k in range(group):
            r = i * group + k
            pltpu.make_async_copy(src_hbm.at[pl.ds(idx_ref[0, idx_row, r], 1), :],
                                  dst_ref.at[slot, pl.ds(row0 + r, 1), :],
                                  sem.at[slot]).start(priority=k % 2 if alternate else 0)
        return carry
    lax.fori_loop(0, count // group, issue, 0)


def _wait_slot(src_hbm, dst_ref, slot, sem):
    rows = dst_ref.shape[1]
    pltpu.make_async_copy(src_hbm.at[pl.ds(0, rows), :], dst_ref.at[slot], sem.at[slot]).wait()


def _combine_kernel(pos_ref, pos_next_ref, y_hbm, h_ref, gate_ref, g_ref, b_ref, o_ref, buf_ref, sem, *, alpha):
    i = pl.program_id(0)
    n = pl.num_programs(0)
    slot = i % 2
    rows = h_ref.shape[0]

    def start(idx_ref, into):
        for k in range(2):
            _start_row_copies(y_hbm, idx_ref, k, buf_ref, into, k * rows, sem, rows)

    @pl.when(i == 0)
    def _():
        start(pos_ref, 0)

    @pl.when(i + 1 < n)
    def _():
        start(pos_next_ref, 1 - slot)

    _wait_slot(y_hbm, buf_ref, slot, sem)

    def norm_rows(j, carry):
        r0 = pl.multiple_of(j * SUBLANES, SUBLANES)
        gates = gate_ref[pl.ds(r0, SUBLANES), :]
        ffn = (buf_ref[slot, pl.ds(r0, SUBLANES), :] * gates[:, 0:1]
               + buf_ref[slot, pl.ds(rows + r0, SUBLANES), :] * gates[:, 1:2])
        z = alpha * h_ref[pl.ds(r0, SUBLANES), :] + ffn
        o_ref[pl.ds(r0, SUBLANES), :] = _layer_norm_rows(z, g_ref[...], b_ref[...])
        return carry
    lax.fori_loop(0, rows // SUBLANES, norm_rows, 0, unroll=8)


def _combine_ln(y, pos2, h1, gate_lanes, ln_g, ln_b, alpha):
    n, d = h1.shape
    tb = min(COMBINE_ROWS, n)
    nblk = n // tb
    pos_blocks = pos2.reshape(nblk, tb, 2).transpose(0, 2, 1)
    kern = functools.partial(_combine_kernel, alpha=alpha)
    return pl.pallas_call(
        kern,
        grid=(nblk,),
        in_specs=[pl.BlockSpec((1, 2, tb), lambda i: (i, 0, 0), memory_space=pltpu.SMEM),
                  pl.BlockSpec((1, 2, tb), lambda i: (jnp.minimum(i + 1, nblk - 1), 0, 0), memory_space=pltpu.SMEM),
                  pl.BlockSpec(memory_space=pl.ANY),
                  pl.BlockSpec((tb, d), lambda i: (i, 0)),
                  pl.BlockSpec((tb, LANES), lambda i: (i, 0)),
                  pl.BlockSpec((1, d), lambda i: (0, 0)),
                  pl.BlockSpec((1, d), lambda i: (0, 0))],
        out_specs=pl.BlockSpec((tb, d), lambda i: (i, 0)),
        out_shape=jax.ShapeDtypeStruct((n, d), F32),
        scratch_shapes=[pltpu.VMEM((2, 2 * tb, d), F32), pltpu.SemaphoreType.DMA((2,))],
        compiler_params=_cparams(("arbitrary",), 32),
        name="moe_combine_ln2",
    )(pos_blocks, pos_blocks, y, h1, gate_lanes, ln_g, ln_b)


def _cast_weight(dst_ref, src_ref):
    rows = src_ref.shape[1]
    step = min(256, rows)

    def body(i, carry):
        r0 = pl.multiple_of(i * step, step)
        dst_ref[pl.ds(r0, step), :] = src_ref[0, pl.ds(r0, step), :].astype(BF16)
        return carry
    lax.fori_loop(0, rows // step, body, 0)


ITEM_COMPUTE = 1
ITEM_NEW_WEIGHTS = 2
ITEM_ZERO_FILL = 4


def _expert_hidden(words, w1b_ref, w3b_ref):
    x_lo, x_hi = _unpack_bf16_halves(words)
    half = x_lo.shape[1]

    def project(w_ref):
        return (jnp.dot(x_lo, w_ref[0:half, :], preferred_element_type=F32)
                + jnp.dot(x_hi, w_ref[half:2 * half, :], preferred_element_type=F32))

    a = project(w1b_ref)
    b = project(w3b_ref)
    return (a * jax.nn.sigmoid(a) * b).astype(BF16)


def _moe_up_gather_kernel(e_ref, cw_ref, bi_ref, bo_ref, co_ref, flag_ref, tok_ref, tok1_ref, tok2_ref, hp_hbm,
                          w1_ref, w3_ref, o_ref, xs_ref, buf_ref, w1b_ref, w3b_ref, sem):
    it = pl.program_id(0)
    last = pl.num_programs(0) - 1
    flag = flag_ref[it]
    n_slots, rows = buf_ref.shape[0], buf_ref.shape[1]
    slot = it % n_slots

    def computes(item):
        return (item <= last) & ((flag_ref[jnp.minimum(item, last)] & ITEM_COMPUTE) != 0)

    @pl.when(it == 0)
    def _():
        _start_row_copies(hp_hbm, tok_ref, 0, buf_ref, 0, 0, sem, rows, alternate=False)

        @pl.when(computes(1))
        def _():
            _start_row_copies(hp_hbm, tok1_ref, 0, buf_ref, 1, 0, sem, rows, alternate=False)

    @pl.when(computes(it + 2))
    def _():
        _start_row_copies(hp_hbm, tok2_ref, 0, buf_ref, (it + 2) % n_slots, 0, sem, rows, alternate=False)

    @pl.when((flag & ITEM_NEW_WEIGHTS) != 0)
    def _():
        _cast_weight(w1b_ref, w1_ref)
        _cast_weight(w3b_ref, w3_ref)

    @pl.when((flag & ITEM_COMPUTE) != 0)
    def _():
        _wait_slot(hp_hbm, buf_ref, slot, sem)
        words = buf_ref[slot]
        xs_ref[...] = words
        o_ref[...] = _expert_hidden(words, w1b_ref, w3b_ref)

    @pl.when((flag & ITEM_ZERO_FILL) != 0)
    def _():
        o_ref[...] = jnp.zeros(o_ref.shape, o_ref.dtype)
        xs_ref[...] = jnp.zeros(xs_ref.shape, xs_ref.dtype)


def _moe_up_kernel(e_ref, cw_ref, bi_ref, bo_ref, co_ref, flag_ref, x_ref, w1_ref, w3_ref, o_ref, w1b_ref, w3b_ref):
    flag = flag_ref[pl.program_id(0)]

    @pl.when((flag & ITEM_NEW_WEIGHTS) != 0)
    def _():
        _cast_weight(w1b_ref, w1_ref)
        _cast_weight(w3b_ref, w3_ref)

    @pl.when((flag & ITEM_COMPUTE) != 0)
    def _():
        o_ref[...] = _expert_hidden(x_ref[...], w1b_ref, w3b_ref)

    @pl.when((flag & ITEM_ZERO_FILL) != 0)
    def _():
        o_ref[...] = jnp.zeros(o_ref.shape, o_ref.dtype)


def _moe_down_kernel(e_ref, cw_ref, bi_ref, bo_ref, co_ref, flag_ref, h0_ref, h1_ref, w2_ref, o_ref, w2b_ref):
    flag = flag_ref[pl.program_id(0)]

    @pl.when((flag & ITEM_NEW_WEIGHTS) != 0)
    def _():
        _cast_weight(w2b_ref, w2_ref)

    @pl.when((flag & ITEM_COMPUTE) != 0)
    def _():
        fc = h0_ref.shape[1]
        o_ref[...] = (jnp.dot(h0_ref[...], w2b_ref[0:fc, :], preferred_element_type=F32)
                      + jnp.dot(h1_ref[...], w2b_ref[fc:2 * fc, :], preferred_element_type=F32))

    @pl.when((flag & ITEM_ZERO_FILL) != 0)
    def _():
        o_ref[...] = jnp.zeros(o_ref.shape, o_ref.dtype)


def _work_items(blocks_per_expert, block_start, n_chunks, n_blocks):
    n_items = n_chunks * n_blocks
    per_e = n_chunks * blocks_per_expert
    end = jnp.cumsum(per_e)
    start = end - per_e
    it = jnp.arange(n_items, dtype=I32)
    total = end[-1]
    used_blocks = total // n_chunks
    itc = jnp.minimum(it, total - 1)
    e = jnp.sum(itc[:, None] >= end[None, :], axis=1).astype(I32)
    local = itc - start[e]
    nb = jnp.maximum(blocks_per_expert[e], 1)
    c = local // nb
    r = local - c * nb
    b_in = block_start[e] + r
    active = it < total
    spare = jnp.maximum(it - total, 0)
    b_out = jnp.where(active, b_in, used_blocks + spare // n_chunks)
    c_out = jnp.where(active, c, spare % n_chunks)
    first = active & (r == 0)
    flag = jnp.where(active, ITEM_COMPUTE + ITEM_NEW_WEIGHTS * first.astype(I32), ITEM_ZERO_FILL)
    later_first = lax.cummin(jnp.where(first, it, n_items)[::-1])[::-1]
    next_start = jnp.concatenate([later_first[1:], jnp.full((1,), n_items, I32)])
    ahead = jnp.minimum(next_start, n_items - 1)
    use_next = active & ~first & (next_start < n_items)
    e_w = jnp.where(use_next, e[ahead], e)
    c_w = jnp.where(use_next, c[ahead], c)
    return tuple(v.astype(I32) for v in (e_w, c_w, b_in, b_out, c_out, flag))


def _moe_up_gather(hp, buf_tok, w1, w3, items):
    p = buf_tok.shape[0]
    half = hp.shape[1]
    d, f = w1.shape[1], w1.shape[2]
    fc = f // 2
    n_items = items[0].shape[0]
    nblk = p // MOE_ROWS
    tok_blocks = buf_tok.reshape(nblk, 1, MOE_ROWS)
    wspec = pl.BlockSpec((1, d, fc), lambda it, e, cw, bi, bo, co, fl: (e[it], 0, 0))
    lookahead = 2

    def tok_spec(ahead):
        return pl.BlockSpec((1, 1, MOE_ROWS),
                            lambda it, e, cw, bi, bo, co, fl: (bi[jnp.minimum(it + ahead, n_items - 1)], 0, 0),
                            memory_space=pltpu.SMEM)

    return pl.pallas_call(
        _moe_up_gather_kernel,
        grid_spec=pltpu.PrefetchScalarGridSpec(
            num_scalar_prefetch=6,
            grid=(n_items,),
            in_specs=[tok_spec(k) for k in range(lookahead + 1)] + [pl.BlockSpec(memory_space=pl.ANY), wspec, wspec],
            out_specs=[pl.BlockSpec((MOE_ROWS, fc), lambda it, e, cw, bi, bo, co, fl: (bo[it], 0)),
                       pl.BlockSpec((MOE_ROWS, half), lambda it, e, cw, bi, bo, co, fl: (bo[it], 0))],
            scratch_shapes=[pltpu.VMEM((lookahead + 1, MOE_ROWS, half), I32), pltpu.VMEM((d, fc), BF16),
                            pltpu.VMEM((d, fc), BF16), pltpu.SemaphoreType.DMA((lookahead + 1,))]),
        out_shape=[jax.ShapeDtypeStruct((p, fc), BF16), jax.ShapeDtypeStruct((p, half), I32)],
        compiler_params=_cparams(("arbitrary",), 56),
        name="moe_up_gather",
    )(*items, *([tok_blocks] * (lookahead + 1)), hp, w1, w3)


def _moe_up(xs, w1, w3, items):
    p, half = xs.shape
    d, f = w1.shape[1], w1.shape[2]
    fc = f // 2
    assert d == 2 * half
    wspec = pl.BlockSpec((1, d, fc), lambda it, e, cw, bi, bo, co, fl: (e[it], 0, 1))
    return pl.pallas_call(
        _moe_up_kernel,
        grid_spec=pltpu.PrefetchScalarGridSpec(
            num_scalar_prefetch=6,
            grid=(items[0].shape[0],),
            in_specs=[pl.BlockSpec((MOE_ROWS, half), lambda it, e, cw, bi, bo, co, fl: (bi[it], 0)), wspec, wspec],
            out_specs=pl.BlockSpec((MOE_ROWS, fc), lambda it, e, cw, bi, bo, co, fl: (bo[it], 0)),
            scratch_shapes=[pltpu.VMEM((d, fc), BF16), pltpu.VMEM((d, fc), BF16)]),
        out_shape=jax.ShapeDtypeStruct((p, fc), BF16),
        compiler_params=_cparams(("arbitrary",), 56),
        name="moe_up",
    )(*items, xs, w1, w3)


def _moe_down(h0, h1, w2, items, d_chunk):
    p, fc = h0.shape
    f = 2 * fc
    d = w2.shape[2]
    hspec = pl.BlockSpec((MOE_ROWS, fc), lambda it, e, cw, bi, bo, co, fl: (bi[it], 0))
    return pl.pallas_call(
        _moe_down_kernel,
        grid_spec=pltpu.PrefetchScalarGridSpec(
            num_scalar_prefetch=6,
            grid=(items[0].shape[0],),
            in_specs=[hspec, hspec,
                      pl.BlockSpec((1, f, d_chunk), lambda it, e, cw, bi, bo, co, fl: (e[it], 0, cw[it]))],
            out_specs=pl.BlockSpec((MOE_ROWS, d_chunk), lambda it, e, cw, bi, bo, co, fl: (bo[it], co[it])),
            scratch_shapes=[pltpu.VMEM((f, d_chunk), BF16)]),
        out_shape=jax.ShapeDtypeStruct((p, d), F32),
        compiler_params=_cparams(("arbitrary",), 56),
        name="moe_down",
    )(*items, h0, h1, w2)


def _hier_moe_ln(h1, hp, logits, w1, w3, w2, ln_g, ln_b, alpha):
    n, d = h1.shape
    e_lanes, gate_lanes, lane_counts = _route(logits)
    a = 2 * n
    e_flat = e_lanes[:, 0:2].reshape(a)
    rank = e_lanes[:, 2:4].reshape(a)
    counts = lane_counts[0, N_GROUPS:N_GROUPS + N_EXPERTS].astype(I32)
    blocks_per_expert = (counts + MOE_ROWS - 1) // MOE_ROWS
    block_start = jnp.cumsum(blocks_per_expert) - blocks_per_expert
    onehot = e_flat[:, None] == jnp.arange(N_EXPERTS, dtype=I32)[None, :]
    pos = jnp.sum(jnp.where(onehot, block_start[None, :], 0), axis=1) * MOE_ROWS + rank
    n_blocks = (a + MOE_ROWS - 1) // MOE_ROWS + N_EXPERTS
    buf_tok = jnp.zeros((n_blocks * MOE_ROWS,), I32).at[pos].set(jnp.arange(a, dtype=I32) // 2)

    d_chunk = min(4096, d)
    up_items = _work_items(blocks_per_expert, block_start, 1, n_blocks)
    h0, xs = _moe_up_gather(hp, buf_tok, w1, w3, up_items)
    h1_mid = _moe_up(xs, w1, w3, up_items)
    y = _moe_down(h0, h1_mid, w2, _work_items(blocks_per_expert, block_start, d // d_chunk, n_blocks), d_chunk)
    return _combine_ln(y, pos.reshape(n, 2), h1, gate_lanes, ln_g, ln_b, alpha)


def kernel(x, w_in, conv_dw_w, conv_dw_b, conv_ln_g, conv_ln_b, w_out, ln1_g, ln1_b, w_router_group, b_router_group, w_router_expert, b_router_expert, w_expert_gate, w_expert_up, w_expert_down, ln2_g, ln2_b):
    b, s, d = x.shape
    depth = w_in.shape[0]
    assert depth == 1
    alpha = float((2.0 * depth) ** 0.25)
    n = b * s
    dc = conv_dw_w.shape[2]
    d_attn = N_HEADS * HEAD_DIM
    d_kv = N_KV_HEADS * HEAD_DIM
    d_qi = IDX_HEADS * IDX_DIM
    n_small = IDX_DIM + IDX_HEADS
    assert w_in.shape[2] == 2 * dc + d_attn + 2 * d_kv + d_qi + n_small

    x2 = x.reshape(n, d)
    w = w_in[0].astype(BF16)
    o_qkvi = 2 * dc
    o_small = o_qkvi + d_attn + 2 * d_kv + d_qi
    w_small = jnp.pad(w[:, o_small:], ((0, 0), (0, LANES - n_small)))
    qkvi_scale = jnp.concatenate([jnp.full((d_attn,), HEAD_DIM ** -0.5 * LOG2E, F32), jnp.ones((2 * d_kv,), F32),
                                  jnp.full((d_qi,), IDX_DIM ** -0.5, F32)])[None, :]

    xb, small = _proj_idx_and_cast(x2, w_small)
    glu = _proj_glu(xb, w, dc)
    qkvi = _proj_scale(xb, w, o_qkvi, o_small - o_qkvi, qkvi_scale, BF16, "proj_qkvi")

    conv_out = _conformer_conv(glu.reshape(b, s, dc), conv_dw_w[0], conv_dw_b, conv_ln_g, conv_ln_b)
    attn_out = _dsa_attention(qkvi.reshape(b, s, -1), small.reshape(b, s, LANES), b, s)

    n_route = N_GROUPS + N_EXPERTS
    w_router = jnp.pad(jnp.concatenate([w_router_group[0], w_router_expert[0]], axis=1),
                       ((0, 0), (0, LANES - n_route))).astype(BF16)
    b_router = jnp.pad(jnp.concatenate([b_router_group[0], b_router_expert[0]]), (0, LANES - n_route))[None, :]
    h1, hp, logits = _out_proj_ln(conv_out.reshape(n, dc), attn_out.reshape(n, d_attn),
                                  w_out[0].astype(BF16),
                                  x2, ln1_g, ln1_b, w_router, b_router, alpha)

    out = _hier_moe_ln(h1, hp, logits, w_expert_gate[0], w_expert_up[0], w_expert_down[0], ln2_g, ln2_b, alpha)
    return out.reshape(b, s, d)
```

```python
import functools
import math

import jax
import jax.numpy as jnp
from jax import lax
from jax.experimental import pallas as pl
from jax.experimental.pallas import tpu as pltpu

F32 = jnp.float32
BF16 = jnp.bfloat16
I32 = jnp.int32

CONV_WIDTH = 31
N_HEADS = 16
HEAD_DIM = 128
N_KV_HEADS = 4
IDX_HEADS = 16
IDX_DIM = 64
TOPK_MAX = 256
N_GROUPS = 4
EXPERTS_PER_GROUP = 8
N_EXPERTS = N_GROUPS * EXPERTS_PER_GROUP
LN_EPS = 1e-5

LANES = 128
SUBLANES = 8
BF16_ROWS = 16
MIB = 1024 * 1024

Q_TILE = 128
KEY_CHUNK = 512
CONV_HALO = 32
MOE_ROWS = 256
COMBINE_ROWS = 128
LN_GROUPS = 4
INT_MIN = -(2 ** 31)
INT_MAX = 2 ** 31 - 1
LOG2E = math.log2(math.e)
NEG_BIG = -1e30
M_INIT = -1e20


def _cparams(semantics, vmem_mib):
    return pltpu.CompilerParams(dimension_semantics=semantics, vmem_limit_bytes=vmem_mib * MIB)


def _layer_norm_rows(z, g, b):
    mu = jnp.mean(z, axis=-1, keepdims=True)
    zc = z - mu
    var = jnp.mean(zc * zc, axis=-1, keepdims=True)
    return zc * lax.rsqrt(var + LN_EPS) * g + b


def _mm_glu_kernel(x_ref, wa_ref, wg_ref, o_ref):
    x = x_ref[...]
    a = jnp.dot(x, wa_ref[...], preferred_element_type=F32)
    g = jnp.dot(x, wg_ref[...], preferred_element_type=F32)
    o_ref[...] = (a * jax.nn.sigmoid(g)).astype(o_ref.dtype)


def _mm_scale_kernel(x_ref, w_ref, s_ref, o_ref):
    acc = jnp.dot(x_ref[...], w_ref[...], preferred_element_type=F32)
    o_ref[...] = (acc * s_ref[...]).astype(o_ref.dtype)


def _mm_tiles(n, k, cols):
    tm = min(1024, n)
    tn = min(512, cols)
    assert n % tm == 0 and cols % tn == 0
    return tm, tn


def _proj_glu(xb, w, cols):
    n, k = xb.shape
    tm, tn = _mm_tiles(n, k, cols)
    gate_block = cols // tn
    return pl.pallas_call(
        _mm_glu_kernel,
        grid=(n // tm, cols // tn),
        in_specs=[pl.BlockSpec((tm, k), lambda i, j: (i, 0)),
                  pl.BlockSpec((k, tn), lambda i, j: (0, j)),
                  pl.BlockSpec((k, tn), lambda i, j: (0, j + gate_block))],
        out_specs=pl.BlockSpec((tm, tn), lambda i, j: (i, j)),
        out_shape=jax.ShapeDtypeStruct((n, cols), F32),
        compiler_params=_cparams(("parallel", "arbitrary"), 48),
        name="proj_glu",
    )(xb, w, w)


def _proj_scale(xb, w, first_col, cols, scale, out_dtype, name):
    n, k = xb.shape
    tm, tn = _mm_tiles(n, k, cols)
    assert first_col % tn == 0
    first_block = first_col // tn
    return pl.pallas_call(
        _mm_scale_kernel,
        grid=(n // tm, cols // tn),
        in_specs=[pl.BlockSpec((tm, k), lambda i, j: (i, 0)),
                  pl.BlockSpec((k, tn), lambda i, j: (0, j + first_block)),
                  pl.BlockSpec((1, tn), lambda i, j: (0, j))],
        out_specs=pl.BlockSpec((tm, tn), lambda i, j: (i, j)),
        out_shape=jax.ShapeDtypeStruct((n, cols), out_dtype),
        compiler_params=_cparams(("parallel", "arbitrary"), 48),
        name=name,
    )(xb, w, scale)


def _idx_cast_kernel(x_ref, w_ref, xb_ref, o_ref):
    xb = x_ref[...].astype(BF16)
    xb_ref[...] = xb
    o_ref[...] = jnp.dot(xb, w_ref[...], preferred_element_type=F32)


def _proj_idx_and_cast(x2, w_small):
    n, k = x2.shape
    tm = min(256, n)
    return pl.pallas_call(
        _idx_cast_kernel,
        grid=(n // tm,),
        in_specs=[pl.BlockSpec((tm, k), lambda i: (i, 0)),
                  pl.BlockSpec((k, LANES), lambda i: (0, 0))],
        out_specs=[pl.BlockSpec((tm, k), lambda i: (i, 0)),
                   pl.BlockSpec((tm, LANES), lambda i: (i, 0))],
        out_shape=[jax.ShapeDtypeStruct((n, k), BF16), jax.ShapeDtypeStruct((n, LANES), F32)],
        compiler_params=_cparams(("parallel",), 32),
        name="proj_idx",
    )(x2, w_small)


def _conv_kernel(cur_ref, prev_ref, w_ref, b_ref, g_ref, beta_ref, o_ref, sh_ref, y_ref, *, rows, lane_chunk):
    t = pl.program_id(1)
    dc = cur_ref.shape[2]
    row_sub = 32
    sh_ref[0, 0:CONV_HALO, :] = jnp.where(t > 0, prev_ref[0], 0.0)
    sh_ref[0, CONV_HALO:CONV_HALO + rows, :] = cur_ref[0]
    first = CONV_HALO - (CONV_WIDTH - 1)
    shifted_rows = rows + CONV_HALO - SUBLANES
    for lc in range(dc // lane_chunk):
        cols = slice(lc * lane_chunk, (lc + 1) * lane_chunk)
        for r in range(1, SUBLANES):
            sh_ref[r, 0:shifted_rows, cols] = sh_ref[0, r:r + shifted_rows, cols]

    for r0 in range(0, rows, row_sub):
        for lc in range(dc // lane_chunk):
            cols = slice(lc * lane_chunk, (lc + 1) * lane_chunk)
            acc = jnp.zeros((row_sub, lane_chunk), F32)
            for j in range(CONV_WIDTH):
                shift, base = (first + j) % SUBLANES, (first + j) // SUBLANES * SUBLANES
                acc = acc + sh_ref[shift, r0 + base:r0 + base + row_sub, cols] * w_ref[j:j + 1, cols]
            y_ref[r0:r0 + row_sub, cols] = acc + b_ref[:, cols]

    def norm_rows(i, carry):
        r0 = pl.multiple_of(i * BF16_ROWS, BF16_ROWS)
        yn = _layer_norm_rows(y_ref[pl.ds(r0, BF16_ROWS), :], g_ref[...], beta_ref[...])
        o_ref[0, pl.ds(r0, BF16_ROWS), :] = (yn * jax.nn.sigmoid(yn)).astype(o_ref.dtype)
        return carry

    lax.fori_loop(0, rows // BF16_ROWS, norm_rows, 0, unroll=4)


def _conformer_conv(glu, w_dw, b_dw, g_ln, b_ln):
    b, s, dc = glu.shape
    rows = min(128, s)
    halo_blocks = rows // CONV_HALO
    lane_chunk = min(512, dc)
    kern = functools.partial(_conv_kernel, rows=rows, lane_chunk=lane_chunk)
    return pl.pallas_call(
        kern,
        grid=(b, s // rows),
        in_specs=[pl.BlockSpec((1, rows, dc), lambda bi, t: (bi, t, 0)),
                  pl.BlockSpec((1, CONV_HALO, dc), lambda bi, t: (bi, jnp.maximum(t * halo_blocks - 1, 0), 0)),
                  pl.BlockSpec((CONV_WIDTH, dc), lambda bi, t: (0, 0)),
                  pl.BlockSpec((1, dc), lambda bi, t: (0, 0)),
                  pl.BlockSpec((1, dc), lambda bi, t: (0, 0)),
                  pl.BlockSpec((1, dc), lambda bi, t: (0, 0))],
        out_specs=pl.BlockSpec((1, rows, dc), lambda bi, t: (bi, t, 0)),
        out_shape=jax.ShapeDtypeStruct((b, s, dc), BF16),
        scratch_shapes=[pltpu.VMEM((SUBLANES, CONV_HALO + rows, dc), F32), pltpu.VMEM((rows, dc), F32)],
        compiler_params=_cparams(("parallel", "arbitrary"), 32),
        name="conformer_conv",
    )(glu, glu, w_dw, b_dw, g_ln, b_ln)


def _dsa_kernel(q_ref, qi_ref, k_ref, v_ref, kidx_ref, w_ref, o_ref,
                ke_ref, ko_ref, vt_ref, key_ref, cut_ref, nd_ref, qs_ref, qis_ref, m_ref, l_ref, acc_ref,
                *, seq, chunk, topk):
    tb = pl.program_id(1)
    t0 = tb * Q_TILE
    n_chunks = (t0 + Q_TILE + chunk - 1) // chunk
    rep = N_HEADS // N_KV_HEADS
    nt = (((1,), (1,)), ((), ()))

    @pl.when(tb == 0)
    def _():
        def build(c, carry):
            r0 = pl.multiple_of(c * chunk, chunk)
            kx = kidx_ref[0, pl.ds(r0, chunk), :]
            lane = lax.broadcasted_iota(I32, kx.shape, 1)
            ke_ref[pl.ds(r0, chunk), :] = jnp.where(lane < IDX_DIM, kx, 0.0).astype(BF16)
            ko_ref[pl.ds(r0, chunk), :] = jnp.where(lane >= IDX_DIM, pltpu.roll(kx, IDX_DIM, 1), 0.0).astype(BF16)
            for g in range(N_KV_HEADS):
                vg = v_ref[0, pl.ds(r0, chunk), g * HEAD_DIM:(g + 1) * HEAD_DIM]
                vt_ref[c, g * HEAD_DIM:(g + 1) * HEAD_DIM, :] = vg.astype(F32).T.astype(BF16)
            return carry
        lax.fori_loop(0, seq // chunk, build, 0)

    for g in range(N_KV_HEADS):
        for r in range(rep):
            h = g * rep + r
            qs_ref[g, r * Q_TILE:(r + 1) * Q_TILE, :] = q_ref[0, :, h * HEAD_DIM:(h + 1) * HEAD_DIM]
    for jj in range(IDX_HEADS // 4):
        for half in range(2):
            pair = 2 * jj + half
            qis_ref[jj, half * Q_TILE:(half + 1) * Q_TILE, :] = qi_ref[0, :, pair * LANES:(pair + 1) * LANES]
    w_t = w_ref[0].T * (IDX_HEADS ** -0.5)

    key_row = lax.broadcasted_iota(I32, (chunk, Q_TILE), 0)
    q_pos = t0 + lax.broadcasted_iota(I32, (chunk, Q_TILE), 1)

    def score_chunk(c, carry):
        r0 = pl.multiple_of(c * chunk, chunk)
        ke = ke_ref[pl.ds(r0, chunk), :]
        ko = ko_ref[pl.ds(r0, chunk), :]
        acc = jnp.zeros((chunk, Q_TILE), F32)
        for jj in range(IDX_HEADS // 4):
            rhs = qis_ref[jj]
            de = lax.dot_general(ke, rhs, nt, preferred_element_type=F32)
            do = lax.dot_general(ko, rhs, nt, preferred_element_type=F32)
            for half in range(2):
                h_even = 2 * (2 * jj + half)
                cols = slice(half * Q_TILE, (half + 1) * Q_TILE)
                acc = acc + w_t[IDX_DIM + h_even:IDX_DIM + h_even + 1, :] * jnp.maximum(de[:, cols], 0.0)
                acc = acc + w_t[IDX_DIM + h_even + 1:IDX_DIM + h_even + 2, :] * jnp.maximum(do[:, cols], 0.0)
        bits = lax.bitcast_convert_type(acc, I32)
        key = bits ^ ((bits >> 31) & 0x7FFFFFFF)
        key_ref[c] = jnp.where(r0 + key_row <= q_pos, key, INT_MIN)
        return carry

    lax.fori_loop(0, n_chunks, score_chunk, 0)

    def count_ge(cand):
        def count_chunk(c, cnts):
            cnts = list(cnts)
            for r in range(chunk // SUBLANES):
                slab = key_ref[c, r * SUBLANES:(r + 1) * SUBLANES, :]
                cnts[r % len(cnts)] = cnts[r % len(cnts)] + jnp.where(slab >= cand, 1.0, 0.0)
            return tuple(cnts)

        zero = jnp.zeros((SUBLANES, Q_TILE), F32)
        cnts = lax.fori_loop(0, n_chunks, count_chunk, (zero, zero, zero, zero))
        return jnp.sum((cnts[0] + cnts[1]) + (cnts[2] + cnts[3]), axis=0, keepdims=True)

    def bit_step(i, t_u):
        cand_u = t_u | (jnp.int32(1) << (31 - i))
        return jnp.where(count_ge(cand_u ^ INT_MIN) >= float(topk), cand_u, t_u)

    t_u = lax.fori_loop(0, 32, bit_step, jnp.zeros((1, Q_TILE), I32))
    thr = jnp.maximum(t_u ^ INT_MIN, INT_MIN + 1)

    n_ge = count_ge(thr)
    cut_ref[...] = jnp.full(cut_ref.shape, INT_MAX, I32)

    @pl.when(jnp.max(n_ge) > topk)
    def _():
        def count_where(pred):
            def count_chunk(c, cnt):
                r0 = pl.multiple_of(c * chunk, chunk)
                hit = jnp.where(pred(key_ref[c], r0 + key_row), 1, 0)
                return cnt + jnp.sum(hit, axis=0, keepdims=True)
            return lax.fori_loop(0, n_chunks, count_chunk, jnp.zeros((1, Q_TILE), I32))

        keep = topk - count_where(lambda kc, pos: kc > thr)
        pos_bits = seq.bit_length()

        def bit_step(i, cut):
            cand = cut | (jnp.int32(1) << (pos_bits - 1 - i))
            taken = count_where(lambda kc, pos: jnp.where(kc == thr, pos, INT_MAX) < cand)
            return jnp.where(taken <= keep, cand, cut)
        cut_ref[...] = lax.fori_loop(0, pos_bits, bit_step, jnp.zeros((1, Q_TILE), I32))

    cut = cut_ref[...]

    m_ref[...] = jnp.full(m_ref.shape, M_INIT, F32)
    l_ref[...] = jnp.zeros(l_ref.shape, F32)
    acc_ref[...] = jnp.zeros(acc_ref.shape, F32)

    def attend_chunk(c, carry):
        r0 = pl.multiple_of(c * chunk, chunk)
        pos = r0 + key_row
        kc = key_ref[c]
        rank_pos = jnp.where(kc > thr, -1, jnp.where(kc == thr, pos, INT_MAX))
        nd_ref[...] = jnp.where(rank_pos < cut, (pos - q_pos).astype(F32), NEG_BIG)
        def logits(g):
            kg = k_ref[0, pl.ds(r0, chunk), g * HEAD_DIM:(g + 1) * HEAD_DIM]
            return lax.dot_general(kg, qs_ref[g], nt, preferred_element_type=F32)

        s_next = logits(0)
        for g in range(N_KV_HEADS):
            s_all = s_next
            if g + 1 < N_KV_HEADS:
                s_next = logits(g + 1)
            probs = []
            alphas = []
            for r in range(rep):
                h = g * rep + r
                slope = float(2.0 ** (-8.0 * (h + 1) / N_HEADS)) * LOG2E
                sr = s_all[:, r * Q_TILE:(r + 1) * Q_TILE] + slope * nd_ref[...]
                m_old = m_ref[h]
                m_new = jnp.maximum(m_old, jnp.max(sr, axis=0, keepdims=True))
                alpha = jnp.exp2(m_old - m_new)
                p = jnp.exp2(sr - m_new)
                l_ref[h] = alpha * l_ref[h] + jnp.sum(p, axis=0, keepdims=True)
                m_ref[h] = m_new
                probs.append(p.astype(BF16))
                alphas.append(alpha)
            vt = vt_ref[c, g * HEAD_DIM:(g + 1) * HEAD_DIM, :]
            pv = jnp.dot(vt, jnp.concatenate(probs, axis=1), preferred_element_type=F32)
            acc_ref[g] = jnp.concatenate(alphas, axis=1) * acc_ref[g] + pv
        return carry

    lax.fori_loop(0, n_chunks, attend_chunk, 0)

    for g in range(N_KV_HEADS):
        for r in range(rep):
            h = g * rep + r
            o_t = acc_ref[g, :, r * Q_TILE:(r + 1) * Q_TILE] * (1.0 / l_ref[h])
            o_ref[0, :, h * HEAD_DIM:(h + 1) * HEAD_DIM] = o_t.T.astype(o_ref.dtype)


def _dsa_attention(qkvi, small, b, s):
    d_attn = N_HEADS * HEAD_DIM
    d_kv = N_KV_HEADS * HEAD_DIM
    d_qi = IDX_HEADS * IDX_DIM
    rep = N_HEADS // N_KV_HEADS
    chunk = min(KEY_CHUNK, s)
    topk = min(TOPK_MAX, s // 4)
    assert s % chunk == 0 and chunk % Q_TILE == 0 and d_attn % d_qi == 0 and d_attn % d_kv == 0
    assert Q_TILE == LANES and HEAD_DIM == LANES and 2 * IDX_DIM == LANES
    kern = functools.partial(_dsa_kernel, seq=s, chunk=chunk, topk=topk)
    return pl.pallas_call(
        kern,
        grid=(b, s // Q_TILE),
        in_specs=[pl.BlockSpec((1, Q_TILE, d_attn), lambda bi, t: (bi, t, 0)),
                  pl.BlockSpec((1, Q_TILE, d_qi), lambda bi, t: (bi, t, (d_attn + 2 * d_kv) // d_qi)),
                  pl.BlockSpec((1, s, d_kv), lambda bi, t: (bi, 0, d_attn // d_kv)),
                  pl.BlockSpec((1, s, d_kv), lambda bi, t: (bi, 0, d_attn // d_kv + 1)),
                  pl.BlockSpec((1, s, LANES), lambda bi, t: (bi, 0, 0)),
                  pl.BlockSpec((1, Q_TILE, LANES), lambda bi, t: (bi, t, 0))],
        out_specs=pl.BlockSpec((1, Q_TILE, d_attn), lambda bi, t: (bi, t, 0)),
        out_shape=jax.ShapeDtypeStruct((b, s, d_attn), BF16),
        scratch_shapes=[pltpu.VMEM((s, LANES), BF16),
                        pltpu.VMEM((s, LANES), BF16),
                        pltpu.VMEM((s // chunk, d_kv, chunk), BF16),
                        pltpu.VMEM((s // chunk, chunk, Q_TILE), I32),
                        pltpu.VMEM((1, Q_TILE), I32),
                        pltpu.VMEM((chunk, Q_TILE), F32),
                        pltpu.VMEM((N_KV_HEADS, rep * Q_TILE, HEAD_DIM), BF16),
                        pltpu.VMEM((IDX_HEADS // 4, 2 * Q_TILE, LANES), BF16),
                        pltpu.VMEM((N_HEADS, 1, Q_TILE), F32),
                        pltpu.VMEM((N_HEADS, 1, Q_TILE), F32),
                        pltpu.VMEM((N_KV_HEADS, HEAD_DIM, rep * Q_TILE), F32)],
        compiler_params=_cparams(("parallel", "arbitrary"), 48),
        name="dsa_attention",
    )(qkvi, qkvi, qkvi, qkvi, small, small)


def _pack_bf16_halves(hb):
    half = hb.shape[1] // 2
    bits = lax.bitcast_convert_type(hb.astype(F32), I32)
    return lax.shift_right_logical(bits[:, :half], 16) | (bits[:, half:] & -65536)


def _unpack_bf16_halves(words):
    lo = lax.bitcast_convert_type(words << 16, F32).astype(BF16)
    hi = lax.bitcast_convert_type(words & -65536, F32).astype(BF16)
    return lo, hi


def _outproj_kernel(conv_ref, attn_ref, wc_ref, wa_ref, x_ref, g_ref, b_ref, wr_ref, br_ref,
                    h_ref, hp_ref, lg_ref, hb_ref, *, alpha):
    j = pl.program_id(1)
    tm, d = h_ref.shape
    tn = x_ref.shape[1]
    n_tiles = d // tn
    mix = jnp.dot(conv_ref[...], wc_ref[...], preferred_element_type=F32)
    mix = mix + jnp.dot(attn_ref[...], wa_ref[...], preferred_element_type=F32)
    z = alpha * x_ref[...] + mix
    for jj in range(n_tiles):
        @pl.when(j == jj)
        def _(jj=jj):
            h_ref[:, jj * tn:(jj + 1) * tn] = z

    @pl.when(j == n_tiles - 1)
    def _():
        def rows(i, carry):
            starts = [pl.multiple_of((LN_GROUPS * i + k) * BF16_ROWS, BF16_ROWS) for k in range(LN_GROUPS)]
            zs = [h_ref[pl.ds(r0, BF16_ROWS), :] for r0 in starts]
            hs = [_layer_norm_rows(z, g_ref[...], b_ref[...]) for z in zs]
            for r0, h in zip(starts, hs):
                h_ref[pl.ds(r0, BF16_ROWS), :] = h
                hb_ref[pl.ds(r0, BF16_ROWS), :] = h.astype(BF16)
            return carry
        lax.fori_loop(0, tm // (LN_GROUPS * BF16_ROWS), rows, 0)

        def pack_rows(i, carry):
            r0 = pl.multiple_of(i * BF16_ROWS, BF16_ROWS)
            hp_ref[pl.ds(r0, BF16_ROWS), :] = _pack_bf16_halves(hb_ref[pl.ds(r0, BF16_ROWS), :])
            return carry
        lax.fori_loop(0, tm // BF16_ROWS, pack_rows, 0, unroll=2)
        lg_ref[...] = jnp.dot(hb_ref[...], wr_ref[...], preferred_element_type=F32) + br_ref[...]


def _out_proj_ln(conv_out, attn_out, w_conv, w_attn, x2, ln_g, ln_b, w_router, b_router, alpha):
    n, d = x2.shape
    dc = conv_out.shape[1]
    da = attn_out.shape[1]
    tm = min(512, n)
    tn = min(512, d)
    assert n % tm == 0 and d % tn == 0
    kern = functools.partial(_outproj_kernel, alpha=alpha)
    return pl.pallas_call(
        kern,
        grid=(n // tm, d // tn),
        in_specs=[pl.BlockSpec((tm, dc), lambda i, j: (i, 0)),
                  pl.BlockSpec((tm, da), lambda i, j: (i, 0)),
                  pl.BlockSpec((dc, tn), lambda i, j: (0, j)),
                  pl.BlockSpec((da, tn), lambda i, j: (0, j)),
                  pl.BlockSpec((tm, tn), lambda i, j: (i, j)),
                  pl.BlockSpec((1, d), lambda i, j: (0, 0)),
                  pl.BlockSpec((1, d), lambda i, j: (0, 0)),
                  pl.BlockSpec((d, LANES), lambda i, j: (0, 0)),
                  pl.BlockSpec((1, LANES), lambda i, j: (0, 0))],
        out_specs=[pl.BlockSpec((tm, d), lambda i, j: (i, 0)),
                   pl.BlockSpec((tm, d // 2), lambda i, j: (i, 0)),
                   pl.BlockSpec((tm, LANES), lambda i, j: (i, 0))],
        out_shape=[jax.ShapeDtypeStruct((n, d), F32), jax.ShapeDtypeStruct((n, d // 2), I32),
                   jax.ShapeDtypeStruct((n, LANES), F32)],
        scratch_shapes=[pltpu.VMEM((tm, d), BF16)],
        compiler_params=_cparams(("parallel", "arbitrary"), 56),
        name="out_proj_ln1",
    )(conv_out, attn_out, w_conv, w_attn, x2, ln_g, ln_b, w_router, b_router)


def _first_argmax(vals, lane, valid):
    masked = jnp.where(valid, vals, -jnp.inf)
    mx = jnp.max(masked, axis=1, keepdims=True)
    idx = jnp.min(jnp.where(valid & (masked == mx), lane, LANES), axis=1, keepdims=True)
    return mx, idx


def _router_kernel(lg_ref, tri_ref, e_ref, gate_ref, cnt_ref):
    @pl.when(pl.program_id(0) == 0)
    def _():
        cnt_ref[...] = jnp.zeros(cnt_ref.shape, F32)

    lg = lg_ref[...]
    lane = lax.broadcasted_iota(I32, lg.shape, 1)
    is_group = lane < N_GROUPS
    g_max, g_sel = _first_argmax(lg, lane, is_group)
    g_den = jnp.sum(jnp.where(is_group, jnp.exp(lg - g_max), 0.0), axis=1, keepdims=True)
    p_group = 1.0 / g_den
    lo = N_GROUPS + g_sel * EXPERTS_PER_GROUP
    in_group = (lane >= lo) & (lane < lo + EXPERTS_PER_GROUP)
    e_max, _ = _first_argmax(lg, lane, in_group)
    e_exp = jnp.where(in_group, jnp.exp(lg - e_max), 0.0)
    prob = e_exp / jnp.sum(e_exp, axis=1, keepdims=True)
    p1, i1 = _first_argmax(prob, lane, in_group)
    p2, i2 = _first_argmax(prob, lane, in_group & (lane != i1))
    norm = p_group / (p1 + p2)
    gate_ref[...] = jnp.where(lane == 0, p1 * norm, jnp.where(lane == 1, p2 * norm, 0.0))
    chosen = jnp.where((lane == i1) | (lane == i2), 1.0, 0.0)
    before = jnp.dot(tri_ref[...], chosen.astype(BF16), preferred_element_type=F32) + cnt_ref[...]
    rank1 = jnp.sum(jnp.where(lane == i1, before, 0.0), axis=1, keepdims=True).astype(I32)
    rank2 = jnp.sum(jnp.where(lane == i2, before, 0.0), axis=1, keepdims=True).astype(I32)
    cnt_ref[...] += jnp.sum(chosen, axis=0, keepdims=True)
    e_ref[...] = jnp.where(lane == 0, i1 - N_GROUPS,
                           jnp.where(lane == 1, i2 - N_GROUPS,
                                     jnp.where(lane == 2, rank1, jnp.where(lane == 3, rank2, 0))))


def _route(logits):
    n = logits.shape[0]
    tm = min(512, n)
    tri = jnp.tril(jnp.ones((tm, tm), BF16), -1)
    spec = pl.BlockSpec((tm, LANES), lambda i: (i, 0))
    return pl.pallas_call(
        _router_kernel,
        grid=(n // tm,),
        in_specs=[spec, pl.BlockSpec((tm, tm), lambda i: (0, 0))],
        out_specs=[spec, spec, pl.BlockSpec((1, LANES), lambda i: (0, 0))],
        out_shape=[jax.ShapeDtypeStruct((n, LANES), I32), jax.ShapeDtypeStruct((n, LANES), F32),
                   jax.ShapeDtypeStruct((1, LANES), F32)],
        compiler_params=_cparams(("arbitrary",), 32),
        name="router",
    )(logits, tri)


def _start_row_copies(src_hbm, idx_ref, idx_row, dst_ref, slot, row0, sem, count):
    group = 8

    def issue(i, carry):
        for k in range(group):
            r = i * group + k
            pltpu.make_async_copy(src_hbm.at[pl.ds(idx_ref[0, idx_row, r], 1), :],
                                  dst_ref.at[slot, pl.ds(row0 + r, 1), :], sem.at[slot]).start(priority=k % 2)
        return carry
    lax.fori_loop(0, count // group, issue, 0)


def _wait_slot(src_hbm, dst_ref, slot, sem):
    rows = dst_ref.shape[1]
    pltpu.make_async_copy(src_hbm.at[pl.ds(0, rows), :], dst_ref.at[slot], sem.at[slot]).wait()


def _combine_kernel(pos_ref, pos_next_ref, y_hbm, h_ref, gate_ref, g_ref, b_ref, o_ref, buf_ref, sem, *, alpha):
    i = pl.program_id(0)
    n = pl.num_programs(0)
    slot = i % 2
    rows = h_ref.shape[0]

    def start(idx_ref, into):
        for k in range(2):
            _start_row_copies(y_hbm, idx_ref, k, buf_ref, into, k * rows, sem, rows)

    @pl.when(i == 0)
    def _():
        start(pos_ref, 0)

    @pl.when(i + 1 < n)
    def _():
        start(pos_next_ref, 1 - slot)

    _wait_slot(y_hbm, buf_ref, slot, sem)

    def norm_rows(j, carry):
        r0 = pl.multiple_of(j * SUBLANES, SUBLANES)
        gates = gate_ref[pl.ds(r0, SUBLANES), :]
        ffn = (buf_ref[slot, pl.ds(r0, SUBLANES), :] * gates[:, 0:1]
               + buf_ref[slot, pl.ds(rows + r0, SUBLANES), :] * gates[:, 1:2])
        z = alpha * h_ref[pl.ds(r0, SUBLANES), :] + ffn
        o_ref[pl.ds(r0, SUBLANES), :] = _layer_norm_rows(z, g_ref[...], b_ref[...])
        return carry
    lax.fori_loop(0, rows // SUBLANES, norm_rows, 0, unroll=8)


def _combine_ln(y, pos2, h1, gate_lanes, ln_g, ln_b, alpha):
    n, d = h1.shape
    tb = min(COMBINE_ROWS, n)
    nblk = n // tb
    pos_blocks = pos2.reshape(nblk, tb, 2).transpose(0, 2, 1)
    kern = functools.partial(_combine_kernel, alpha=alpha)
    return pl.pallas_call(
        kern,
        grid=(nblk,),
        in_specs=[pl.BlockSpec((1, 2, tb), lambda i: (i, 0, 0), memory_space=pltpu.SMEM),
                  pl.BlockSpec((1, 2, tb), lambda i: (jnp.minimum(i + 1, nblk - 1), 0, 0), memory_space=pltpu.SMEM),
                  pl.BlockSpec(memory_space=pl.ANY),
                  pl.BlockSpec((tb, d), lambda i: (i, 0)),
                  pl.BlockSpec((tb, LANES), lambda i: (i, 0)),
                  pl.BlockSpec((1, d), lambda i: (0, 0)),
                  pl.BlockSpec((1, d), lambda i: (0, 0))],
        out_specs=pl.BlockSpec((tb, d), lambda i: (i, 0)),
        out_shape=jax.ShapeDtypeStruct((n, d), F32),
        scratch_shapes=[pltpu.VMEM((2, 2 * tb, d), F32), pltpu.SemaphoreType.DMA((2,))],
        compiler_params=_cparams(("arbitrary",), 32),
        name="moe_combine_ln2",
    )(pos_blocks, pos_blocks, y, h1, gate_lanes, ln_g, ln_b)


def _cast_weight(dst_ref, src_ref):
    rows = src_ref.shape[1]
    step = min(256, rows)

    def body(i, carry):
        r0 = pl.multiple_of(i * step, step)
        dst_ref[pl.ds(r0, step), :] = src_ref[0, pl.ds(r0, step), :].astype(BF16)
        return carry
    lax.fori_loop(0, rows // step, body, 0)


ITEM_COMPUTE = 1
ITEM_NEW_WEIGHTS = 2
ITEM_ZERO_FILL = 4


def _expert_hidden(words, w1b_ref, w3b_ref):
    x_lo, x_hi = _unpack_bf16_halves(words)
    half = x_lo.shape[1]

    def project(w_ref):
        return (jnp.dot(x_lo, w_ref[0:half, :], preferred_element_type=F32)
                + jnp.dot(x_hi, w_ref[half:2 * half, :], preferred_element_type=F32))

    a = project(w1b_ref)
    b = project(w3b_ref)
    return (a * jax.nn.sigmoid(a) * b).astype(BF16)


def _moe_up_gather_kernel(e_ref, cw_ref, bi_ref, bo_ref, co_ref, flag_ref, tok_ref, tok1_ref, tok2_ref, hp_hbm,
                          w1_ref, w3_ref, o_ref, xs_ref, buf_ref, w1b_ref, w3b_ref, sem):
    it = pl.program_id(0)
    last = pl.num_programs(0) - 1
    flag = flag_ref[it]
    n_slots, n_tiles = buf_ref.shape[0], buf_ref.shape[1]
    slot = it % n_slots

    def computes(item):
        return (item <= last) & ((flag_ref[jnp.minimum(item, last)] & ITEM_COMPUTE) != 0)

    def start_rows(idx_ref, into):
        def issue(i, carry):
            for k in range(SUBLANES):
                tok = idx_ref[0, 0, i * SUBLANES + k]
                pltpu.make_async_copy(hp_hbm.at[lax.shift_right_logical(tok, 3), pl.ds(tok & (SUBLANES - 1), 1), :],
                                      buf_ref.at[into, i, pl.ds(k, 1), :], sem.at[into]).start(priority=k % 2)
            return carry
        lax.fori_loop(0, n_tiles, issue, 0)

    @pl.when(it == 0)
    def _():
        start_rows(tok_ref, 0)

        @pl.when(computes(1))
        def _():
            start_rows(tok1_ref, 1)

    @pl.when(computes(it + 2))
    def _():
        start_rows(tok2_ref, (it + 2) % n_slots)

    @pl.when((flag & ITEM_NEW_WEIGHTS) != 0)
    def _():
        _cast_weight(w1b_ref, w1_ref)
        _cast_weight(w3b_ref, w3_ref)

    @pl.when((flag & ITEM_COMPUTE) != 0)
    def _():
        pltpu.make_async_copy(hp_hbm.at[pl.ds(0, n_tiles)], buf_ref.at[slot], sem.at[slot]).wait()
        words = buf_ref[slot].reshape(n_tiles * SUBLANES, buf_ref.shape[3])
        xs_ref[...] = words
        o_ref[...] = _expert_hidden(words, w1b_ref, w3b_ref)

    @pl.when((flag & ITEM_ZERO_FILL) != 0)
    def _():
        o_ref[...] = jnp.zeros(o_ref.shape, o_ref.dtype)
        xs_ref[...] = jnp.zeros(xs_ref.shape, xs_ref.dtype)


def _moe_up_kernel(e_ref, cw_ref, bi_ref, bo_ref, co_ref, flag_ref, x_ref, w1_ref, w3_ref, o_ref, w1b_ref, w3b_ref):
    flag = flag_ref[pl.program_id(0)]

    @pl.when((flag & ITEM_NEW_WEIGHTS) != 0)
    def _():
        _cast_weight(w1b_ref, w1_ref)
        _cast_weight(w3b_ref, w3_ref)

    @pl.when((flag & ITEM_COMPUTE) != 0)
    def _():
        o_ref[...] = _expert_hidden(x_ref[...], w1b_ref, w3b_ref)

    @pl.when((flag & ITEM_ZERO_FILL) != 0)
    def _():
        o_ref[...] = jnp.zeros(o_ref.shape, o_ref.dtype)


def _moe_down_kernel(e_ref, cw_ref, bi_ref, bo_ref, co_ref, flag_ref, h0_ref, h1_ref, w2_ref, o_ref, w2b_ref):
    flag = flag_ref[pl.program_id(0)]

    @pl.when((flag & ITEM_NEW_WEIGHTS) != 0)
    def _():
        _cast_weight(w2b_ref, w2_ref)

    @pl.when((flag & ITEM_COMPUTE) != 0)
    def _():
        fc = h0_ref.shape[1]
        o_ref[...] = (jnp.dot(h0_ref[...], w2b_ref[0:fc, :], preferred_element_type=F32)
                      + jnp.dot(h1_ref[...], w2b_ref[fc:2 * fc, :], preferred_element_type=F32))

    @pl.when((flag & ITEM_ZERO_FILL) != 0)
    def _():
        o_ref[...] = jnp.zeros(o_ref.shape, o_ref.dtype)


def _work_items(blocks_per_expert, block_start, n_chunks, n_blocks):
    n_items = n_chunks * n_blocks
    per_e = n_chunks * blocks_per_expert
    end = jnp.cumsum(per_e)
    start = end - per_e
    it = jnp.arange(n_items, dtype=I32)
    total = end[-1]
    used_blocks = total // n_chunks
    itc = jnp.minimum(it, total - 1)
    e = jnp.sum(itc[:, None] >= end[None, :], axis=1).astype(I32)
    local = itc - start[e]
    nb = jnp.maximum(blocks_per_expert[e], 1)
    c = local // nb
    r = local - c * nb
    b_in = block_start[e] + r
    active = it < total
    spare = jnp.maximum(it - total, 0)
    b_out = jnp.where(active, b_in, used_blocks + spare // n_chunks)
    c_out = jnp.where(active, c, spare % n_chunks)
    first = active & (r == 0)
    flag = jnp.where(active, ITEM_COMPUTE + ITEM_NEW_WEIGHTS * first.astype(I32), ITEM_ZERO_FILL)
    later_first = lax.cummin(jnp.where(first, it, n_items)[::-1])[::-1]
    next_start = jnp.concatenate([later_first[1:], jnp.full((1,), n_items, I32)])
    ahead = jnp.minimum(next_start, n_items - 1)
    use_next = active & ~first & (next_start < n_items)
    e_w = jnp.where(use_next, e[ahead], e)
    c_w = jnp.where(use_next, c[ahead], c)
    return tuple(v.astype(I32) for v in (e_w, c_w, b_in, b_out, c_out, flag))


def _moe_up_gather(hp, buf_tok, w1, w3, items):
    p = buf_tok.shape[0]
    half = hp.shape[1]
    d, f = w1.shape[1], w1.shape[2]
    fc = f // 2
    n_items = items[0].shape[0]
    nblk = p // MOE_ROWS
    tok_blocks = buf_tok.reshape(nblk, 1, MOE_ROWS)
    wspec = pl.BlockSpec((1, d, fc), lambda it, e, cw, bi, bo, co, fl: (e[it], 0, 0))
    lookahead = 2

    def tok_spec(ahead):
        return pl.BlockSpec((1, 1, MOE_ROWS),
                            lambda it, e, cw, bi, bo, co, fl: (bi[jnp.minimum(it + ahead, n_items - 1)], 0, 0),
                            memory_space=pltpu.SMEM)

    return pl.pallas_call(
        _moe_up_gather_kernel,
        grid_spec=pltpu.PrefetchScalarGridSpec(
            num_scalar_prefetch=6,
            grid=(n_items,),
            in_specs=[tok_spec(k) for k in range(lookahead + 1)] + [pl.BlockSpec(memory_space=pl.ANY), wspec, wspec],
            out_specs=[pl.BlockSpec((MOE_ROWS, fc), lambda it, e, cw, bi, bo, co, fl: (bo[it], 0)),
                       pl.BlockSpec((MOE_ROWS, half), lambda it, e, cw, bi, bo, co, fl: (bo[it], 0))],
            scratch_shapes=[pltpu.VMEM((lookahead + 1, MOE_ROWS // SUBLANES, SUBLANES, half), I32),
                            pltpu.VMEM((d, fc), BF16),
                            pltpu.VMEM((d, fc), BF16), pltpu.SemaphoreType.DMA((lookahead + 1,))]),
        out_shape=[jax.ShapeDtypeStruct((p, fc), BF16), jax.ShapeDtypeStruct((p, half), I32)],
        compiler_params=_cparams(("arbitrary",), 56),
        name="moe_up_gather",
    )(*items, *([tok_blocks] * (lookahead + 1)), hp.reshape(hp.shape[0] // SUBLANES, SUBLANES, half), w1, w3)


def _moe_up(xs, w1, w3, items):
    p, half = xs.shape
    d, f = w1.shape[1], w1.shape[2]
    fc = f // 2
    assert d == 2 * half
    wspec = pl.BlockSpec((1, d, fc), lambda it, e, cw, bi, bo, co, fl: (e[it], 0, 1))
    return pl.pallas_call(
        _moe_up_kernel,
        grid_spec=pltpu.PrefetchScalarGridSpec(
            num_scalar_prefetch=6,
            grid=(items[0].shape[0],),
            in_specs=[pl.BlockSpec((MOE_ROWS, half), lambda it, e, cw, bi, bo, co, fl: (bi[it], 0)), wspec, wspec],
            out_specs=pl.BlockSpec((MOE_ROWS, fc), lambda it, e, cw, bi, bo, co, fl: (bo[it], 0)),
            scratch_shapes=[pltpu.VMEM((d, fc), BF16), pltpu.VMEM((d, fc), BF16)]),
        out_shape=jax.ShapeDtypeStruct((p, fc), BF16),
        compiler_params=_cparams(("arbitrary",), 56),
        name="moe_up",
    )(*items, xs, w1, w3)


def _moe_down(h0, h1, w2, items, d_chunk):
    p, fc = h0.shape
    f = 2 * fc
    d = w2.shape[2]
    hspec = pl.BlockSpec((MOE_ROWS, fc), lambda it, e, cw, bi, bo, co, fl: (bi[it], 0))
    return pl.pallas_call(
        _moe_down_kernel,
        grid_spec=pltpu.PrefetchScalarGridSpec(
            num_scalar_prefetch=6,
            grid=(items[0].shape[0],),
            in_specs=[hspec, hspec,
                      pl.BlockSpec((1, f, d_chunk), lambda it, e, cw, bi, bo, co, fl: (e[it], 0, cw[it]))],
            out_specs=pl.BlockSpec((MOE_ROWS, d_chunk), lambda it, e, cw, bi, bo, co, fl: (bo[it], co[it])),
            scratch_shapes=[pltpu.VMEM((f, d_chunk), BF16)]),
        out_shape=jax.ShapeDtypeStruct((p, d), F32),
        compiler_params=_cparams(("arbitrary",), 56),
        name="moe_down",
    )(*items, h0, h1, w2)


def _hier_moe_ln(h1, hp, logits, w1, w3, w2, ln_g, ln_b, alpha):
    n, d = h1.shape
    e_lanes, gate_lanes, lane_counts = _route(logits)
    a = 2 * n
    e_flat = e_lanes[:, 0:2].reshape(a)
    rank = e_lanes[:, 2:4].reshape(a)
    counts = lane_counts[0, N_GROUPS:N_GROUPS + N_EXPERTS].astype(I32)
    blocks_per_expert = (counts + MOE_ROWS - 1) // MOE_ROWS
    block_start = jnp.cumsum(blocks_per_expert) - blocks_per_expert
    onehot = e_flat[:, None] == jnp.arange(N_EXPERTS, dtype=I32)[None, :]
    pos = jnp.sum(jnp.where(onehot, block_start[None, :], 0), axis=1) * MOE_ROWS + rank
    n_blocks = (a + MOE_ROWS - 1) // MOE_ROWS + N_EXPERTS
    buf_tok = jnp.zeros((n_blocks * MOE_ROWS,), I32).at[pos].set(jnp.arange(a, dtype=I32) // 2)

    d_chunk = min(4096, d)
    up_items = _work_items(blocks_per_expert, block_start, 1, n_blocks)
    h0, xs = _moe_up_gather(hp, buf_tok, w1, w3, up_items)
    h1_mid = _moe_up(xs, w1, w3, up_items)
    y = _moe_down(h0, h1_mid, w2, _work_items(blocks_per_expert, block_start, d // d_chunk, n_blocks), d_chunk)
    return _combine_ln(y, pos.reshape(n, 2), h1, gate_lanes, ln_g, ln_b, alpha)


def kernel(x, w_in, conv_dw_w, conv_dw_b, conv_ln_g, conv_ln_b, w_out, ln1_g, ln1_b, w_router_group, b_router_group, w_router_expert, b_router_expert, w_expert_gate, w_expert_up, w_expert_down, ln2_g, ln2_b):
    b, s, d = x.shape
    depth = w_in.shape[0]
    assert depth == 1
    alpha = float((2.0 * depth) ** 0.25)
    n = b * s
    dc = conv_dw_w.shape[2]
    d_attn = N_HEADS * HEAD_DIM
    d_kv = N_KV_HEADS * HEAD_DIM
    d_qi = IDX_HEADS * IDX_DIM
    n_small = IDX_DIM + IDX_HEADS
    assert w_in.shape[2] == 2 * dc + d_attn + 2 * d_kv + d_qi + n_small

    x2 = x.reshape(n, d)
    w = w_in[0].astype(BF16)
    o_qkvi = 2 * dc
    o_small = o_qkvi + d_attn + 2 * d_kv + d_qi
    w_small = jnp.pad(w[:, o_small:], ((0, 0), (0, LANES - n_small)))
    qkvi_scale = jnp.concatenate([jnp.full((d_attn,), HEAD_DIM ** -0.5 * LOG2E, F32), jnp.ones((2 * d_kv,), F32),
                                  jnp.full((d_qi,), IDX_DIM ** -0.5, F32)])[None, :]

    xb, small = _proj_idx_and_cast(x2, w_small)
    glu = _proj_glu(xb, w, dc)
    qkvi = _proj_scale(xb, w, o_qkvi, o_small - o_qkvi, qkvi_scale, BF16, "proj_qkvi")

    conv_out = _conformer_conv(glu.reshape(b, s, dc), conv_dw_w[0], conv_dw_b, conv_ln_g, conv_ln_b)
    attn_out = _dsa_attention(qkvi.reshape(b, s, -1), small.reshape(b, s, LANES), b, s)

    n_route = N_GROUPS + N_EXPERTS
    w_router = jnp.pad(jnp.concatenate([w_router_group[0], w_router_expert[0]], axis=1),
                       ((0, 0), (0, LANES - n_route))).astype(BF16)
    b_router = jnp.pad(jnp.concatenate([b_router_group[0], b_router_expert[0]]), (0, LANES - n_route))[None, :]
    h1, hp, logits = _out_proj_ln(conv_out.reshape(n, dc), attn_out.reshape(n, d_attn),
                                  w_out[0, :dc].astype(BF16), w_out[0, dc:].astype(BF16),
                                  x2, ln1_g, ln1_b, w_router, b_router, alpha)

    out = _hier_moe_ln(h1, hp, logits, w_expert_gate[0], w_expert_up[0], w_expert_down[0], ln2_g, ln2_b, alpha)
    return out.reshape(b, s, d)
```

```python
import functools
import math

import jax
import jax.numpy as jnp
from jax import lax
from jax.experimental import pallas as pl
from jax.experimental.pallas import tpu as pltpu

F32 = jnp.float32
BF16 = jnp.bfloat16
I32 = jnp.int32

CONV_WIDTH = 31
N_HEADS = 16
HEAD_DIM = 128
N_KV_HEADS = 4
IDX_HEADS = 16
IDX_DIM = 64
TOPK_MAX = 256
N_GROUPS = 4
EXPERTS_PER_GROUP = 8
N_EXPERTS = N_GROUPS * EXPERTS_PER_GROUP
LN_EPS = 1e-5

LANES = 128
SUBLANES = 8
BF16_ROWS = 16
MIB = 1024 * 1024

Q_TILE = 128
KEY_CHUNK = 512
CONV_HALO = 32
MOE_ROWS = 256
COMBINE_ROWS = 128
LN_GROUPS = 4
INT_MIN = -(2 ** 31)
INT_MAX = 2 ** 31 - 1
LOG2E = math.log2(math.e)
NEG_BIG = -1e30
M_INIT = -1e20


def _cparams(semantics, vmem_mib):
    return pltpu.CompilerParams(dimension_semantics=semantics, vmem_limit_bytes=vmem_mib * MIB)


def _layer_norm_rows(z, g, b):
    mu = jnp.mean(z, axis=-1, keepdims=True)
    zc = z - mu
    var = jnp.mean(zc * zc, axis=-1, keepdims=True)
    return zc * lax.rsqrt(var + LN_EPS) * g + b


def _mm_glu_kernel(x_ref, wa_ref, wg_ref, o_ref):
    x = x_ref[...]
    a = jnp.dot(x, wa_ref[...], preferred_element_type=F32)
    g = jnp.dot(x, wg_ref[...], preferred_element_type=F32)
    o_ref[...] = (a * jax.nn.sigmoid(g)).astype(o_ref.dtype)


def _mm_scale_kernel(x_ref, w_ref, s_ref, o_ref):
    acc = jnp.dot(x_ref[...], w_ref[...], preferred_element_type=F32)
    o_ref[...] = (acc * s_ref[...]).astype(o_ref.dtype)


def _mm_tiles(n, k, cols):
    tm = min(1024, n)
    tn = min(512, cols)
    assert n % tm == 0 and cols % tn == 0
    return tm, tn


def _proj_glu(xb, w, cols):
    n, k = xb.shape
    tm, tn = _mm_tiles(n, k, cols)
    gate_block = cols // tn
    return pl.pallas_call(
        _mm_glu_kernel,
        grid=(n // tm, cols // tn),
        in_specs=[pl.BlockSpec((tm, k), lambda i, j: (i, 0)),
                  pl.BlockSpec((k, tn), lambda i, j: (0, j)),
                  pl.BlockSpec((k, tn), lambda i, j: (0, j + gate_block))],
        out_specs=pl.BlockSpec((tm, tn), lambda i, j: (i, j)),
        out_shape=jax.ShapeDtypeStruct((n, cols), F32),
        compiler_params=_cparams(("parallel", "arbitrary"), 48),
        name="proj_glu",
    )(xb, w, w)


def _proj_scale(xb, w, first_col, cols, scale, out_dtype, name):
    n, k = xb.shape
    tm, tn = _mm_tiles(n, k, cols)
    assert first_col % tn == 0
    first_block = first_col // tn
    return pl.pallas_call(
        _mm_scale_kernel,
        grid=(n // tm, cols // tn),
        in_specs=[pl.BlockSpec((tm, k), lambda i, j: (i, 0)),
                  pl.BlockSpec((k, tn), lambda i, j: (0, j + first_block)),
                  pl.BlockSpec((1, tn), lambda i, j: (0, j))],
        out_specs=pl.BlockSpec((tm, tn), lambda i, j: (i, j)),
        out_shape=jax.ShapeDtypeStruct((n, cols), out_dtype),
        compiler_params=_cparams(("parallel", "arbitrary"), 48),
        name=name,
    )(xb, w, scale)


def _idx_cast_kernel(x_ref, w_ref, xb_ref, o_ref):
    xb = x_ref[...].astype(BF16)
    xb_ref[...] = xb
    o_ref[...] = jnp.dot(xb, w_ref[...], preferred_element_type=F32)


def _proj_idx_and_cast(x2, w_small):
    n, k = x2.shape
    tm = min(256, n)
    return pl.pallas_call(
        _idx_cast_kernel,
        grid=(n // tm,),
        in_specs=[pl.BlockSpec((tm, k), lambda i: (i, 0)),
                  pl.BlockSpec((k, LANES), lambda i: (0, 0))],
        out_specs=[pl.BlockSpec((tm, k), lambda i: (i, 0)),
                   pl.BlockSpec((tm, LANES), lambda i: (i, 0))],
        out_shape=[jax.ShapeDtypeStruct((n, k), BF16), jax.ShapeDtypeStruct((n, LANES), F32)],
        compiler_params=_cparams(("parallel",), 32),
        name="proj_idx",
    )(x2, w_small)


def _conv_kernel(cur_ref, prev_ref, w_ref, b_ref, g_ref, beta_ref, o_ref, sh_ref, y_ref, *, rows, lane_chunk):
    t = pl.program_id(1)
    dc = cur_ref.shape[2]
    row_sub = 32
    sh_ref[0, 0:CONV_HALO, :] = jnp.where(t > 0, prev_ref[0], 0.0)
    sh_ref[0, CONV_HALO:CONV_HALO + rows, :] = cur_ref[0]
    first = CONV_HALO - (CONV_WIDTH - 1)
    shifted_rows = rows + CONV_HALO - SUBLANES
    for lc in range(dc // lane_chunk):
        cols = slice(lc * lane_chunk, (lc + 1) * lane_chunk)
        for r in range(1, SUBLANES):
            sh_ref[r, 0:shifted_rows, cols] = sh_ref[0, r:r + shifted_rows, cols]

    for r0 in range(0, rows, row_sub):
        for lc in range(dc // lane_chunk):
            cols = slice(lc * lane_chunk, (lc + 1) * lane_chunk)
            acc = jnp.zeros((row_sub, lane_chunk), F32)
            for j in range(CONV_WIDTH):
                shift, base = (first + j) % SUBLANES, (first + j) // SUBLANES * SUBLANES
                acc = acc + sh_ref[shift, r0 + base:r0 + base + row_sub, cols] * w_ref[j:j + 1, cols]
            y_ref[r0:r0 + row_sub, cols] = acc + b_ref[:, cols]

    def norm_rows(i, carry):
        r0 = pl.multiple_of(i * BF16_ROWS, BF16_ROWS)
        yn = _layer_norm_rows(y_ref[pl.ds(r0, BF16_ROWS), :], g_ref[...], beta_ref[...])
        o_ref[0, pl.ds(r0, BF16_ROWS), :] = (yn * jax.nn.sigmoid(yn)).astype(o_ref.dtype)
        return carry

    lax.fori_loop(0, rows // BF16_ROWS, norm_rows, 0, unroll=4)


def _conformer_conv(glu, w_dw, b_dw, g_ln, b_ln):
    b, s, dc = glu.shape
    rows = min(128, s)
    halo_blocks = rows // CONV_HALO
    lane_chunk = min(512, dc)
    kern = functools.partial(_conv_kernel, rows=rows, lane_chunk=lane_chunk)
    return pl.pallas_call(
        kern,
        grid=(b, s // rows),
        in_specs=[pl.BlockSpec((1, rows, dc), lambda bi, t: (bi, t, 0)),
                  pl.BlockSpec((1, CONV_HALO, dc), lambda bi, t: (bi, jnp.maximum(t * halo_blocks - 1, 0), 0)),
                  pl.BlockSpec((CONV_WIDTH, dc), lambda bi, t: (0, 0)),
                  pl.BlockSpec((1, dc), lambda bi, t: (0, 0)),
                  pl.BlockSpec((1, dc), lambda bi, t: (0, 0)),
                  pl.BlockSpec((1, dc), lambda bi, t: (0, 0))],
        out_specs=pl.BlockSpec((1, rows, dc), lambda bi, t: (bi, t, 0)),
        out_shape=jax.ShapeDtypeStruct((b, s, dc), BF16),
        scratch_shapes=[pltpu.VMEM((SUBLANES, CONV_HALO + rows, dc), F32), pltpu.VMEM((rows, dc), F32)],
        compiler_params=_cparams(("parallel", "arbitrary"), 32),
        name="conformer_conv",
    )(glu, glu, w_dw, b_dw, g_ln, b_ln)


def _dsa_kernel(q_ref, qi_ref, k_ref, v_ref, kidx_ref, w_ref, o_ref,
                ke_ref, ko_ref, vt_ref, key_ref, cut_ref, nd_ref, qs_ref, qis_ref, m_ref, l_ref, acc_ref,
                *, seq, chunk, topk):
    tb = pl.program_id(1)
    t0 = tb * Q_TILE
    n_chunks = (t0 + Q_TILE + chunk - 1) // chunk
    rep = N_HEADS // N_KV_HEADS
    nt = (((1,), (1,)), ((), ()))

    @pl.when(tb == 0)
    def _():
        def build(c, carry):
            r0 = pl.multiple_of(c * chunk, chunk)
            kx = kidx_ref[0, pl.ds(r0, chunk), :]
            lane = lax.broadcasted_iota(I32, kx.shape, 1)
            ke_ref[pl.ds(r0, chunk), :] = jnp.where(lane < IDX_DIM, kx, 0.0).astype(BF16)
            ko_ref[pl.ds(r0, chunk), :] = jnp.where(lane >= IDX_DIM, pltpu.roll(kx, IDX_DIM, 1), 0.0).astype(BF16)
            for g in range(N_KV_HEADS):
                vg = v_ref[0, pl.ds(r0, chunk), g * HEAD_DIM:(g + 1) * HEAD_DIM]
                vt_ref[c, g * HEAD_DIM:(g + 1) * HEAD_DIM, :] = vg.astype(F32).T.astype(BF16)
            return carry
        lax.fori_loop(0, seq // chunk, build, 0)

    for g in range(N_KV_HEADS):
        for r in range(rep):
            h = g * rep + r
            qs_ref[g, r * Q_TILE:(r + 1) * Q_TILE, :] = q_ref[0, :, h * HEAD_DIM:(h + 1) * HEAD_DIM]
    for jj in range(IDX_HEADS // 4):
        for half in range(2):
            pair = 2 * jj + half
            qis_ref[jj, half * Q_TILE:(half + 1) * Q_TILE, :] = qi_ref[0, :, pair * LANES:(pair + 1) * LANES]
    w_t = w_ref[0].T * (IDX_HEADS ** -0.5)

    key_row = lax.broadcasted_iota(I32, (chunk, Q_TILE), 0)
    q_pos = t0 + lax.broadcasted_iota(I32, (chunk, Q_TILE), 1)

    def score_chunk(c, carry):
        r0 = pl.multiple_of(c * chunk, chunk)
        ke = ke_ref[pl.ds(r0, chunk), :]
        ko = ko_ref[pl.ds(r0, chunk), :]
        acc = jnp.zeros((chunk, Q_TILE), F32)
        for jj in range(IDX_HEADS // 4):
            rhs = qis_ref[jj]
            de = lax.dot_general(ke, rhs, nt, preferred_element_type=F32)
            do = lax.dot_general(ko, rhs, nt, preferred_element_type=F32)
            for half in range(2):
                h_even = 2 * (2 * jj + half)
                cols = slice(half * Q_TILE, (half + 1) * Q_TILE)
                acc = acc + w_t[IDX_DIM + h_even:IDX_DIM + h_even + 1, :] * jnp.maximum(de[:, cols], 0.0)
                acc = acc + w_t[IDX_DIM + h_even + 1:IDX_DIM + h_even + 2, :] * jnp.maximum(do[:, cols], 0.0)
        bits = lax.bitcast_convert_type(acc, I32)
        key = bits ^ ((bits >> 31) & 0x7FFFFFFF)
        key_ref[c] = jnp.where(r0 + key_row <= q_pos, key, INT_MIN)
        return carry

    lax.fori_loop(0, n_chunks, score_chunk, 0)

    def count_ge(cand):
        def count_chunk(c, cnts):
            cnts = list(cnts)
            for r in range(chunk // SUBLANES):
                slab = key_ref[c, r * SUBLANES:(r + 1) * SUBLANES, :]
                cnts[r % len(cnts)] = cnts[r % len(cnts)] + jnp.where(slab >= cand, 1.0, 0.0)
            return tuple(cnts)

        zero = jnp.zeros((SUBLANES, Q_TILE), F32)
        cnts = lax.fori_loop(0, n_chunks, count_chunk, (zero, zero, zero, zero))
        return jnp.sum((cnts[0] + cnts[1]) + (cnts[2] + cnts[3]), axis=0, keepdims=True)

    def bit_step(i, t_u):
        cand_u = t_u | (jnp.int32(1) << (31 - i))
        return jnp.where(count_ge(cand_u ^ INT_MIN) >= float(topk), cand_u, t_u)

    t_u = lax.fori_loop(0, 32, bit_step, jnp.zeros((1, Q_TILE), I32))
    thr = jnp.maximum(t_u ^ INT_MIN, INT_MIN + 1)

    n_ge = count_ge(thr)
    cut_ref[...] = jnp.full(cut_ref.shape, INT_MAX, I32)

    @pl.when(jnp.max(n_ge) > topk)
    def _():
        def count_where(pred):
            def count_chunk(c, cnt):
                r0 = pl.multiple_of(c * chunk, chunk)
                hit = jnp.where(pred(key_ref[c], r0 + key_row), 1, 0)
                return cnt + jnp.sum(hit, axis=0, keepdims=True)
            return lax.fori_loop(0, n_chunks, count_chunk, jnp.zeros((1, Q_TILE), I32))

        keep = topk - count_where(lambda kc, pos: kc > thr)
        pos_bits = seq.bit_length()

        def bit_step(i, cut):
            cand = cut | (jnp.int32(1) << (pos_bits - 1 - i))
            taken = count_where(lambda kc, pos: jnp.where(kc == thr, pos, INT_MAX) < cand)
            return jnp.where(taken <= keep, cand, cut)
        cut_ref[...] = lax.fori_loop(0, pos_bits, bit_step, jnp.zeros((1, Q_TILE), I32))

    cut = cut_ref[...]

    m_ref[...] = jnp.full(m_ref.shape, M_INIT, F32)
    l_ref[...] = jnp.zeros(l_ref.shape, F32)
    acc_ref[...] = jnp.zeros(acc_ref.shape, F32)

    def attend_chunk(c, carry):
        r0 = pl.multiple_of(c * chunk, chunk)
        pos = r0 + key_row
        kc = key_ref[c]
        rank_pos = jnp.where(kc > thr, -1, jnp.where(kc == thr, pos, INT_MAX))
        nd_ref[...] = jnp.where(rank_pos < cut, (pos - q_pos).astype(F32), NEG_BIG)
        def logits(g):
            kg = k_ref[0, pl.ds(r0, chunk), g * HEAD_DIM:(g + 1) * HEAD_DIM]
            return lax.dot_general(kg, qs_ref[g], nt, preferred_element_type=F32)

        s_next = logits(0)
        for g in range(N_KV_HEADS):
            s_all = s_next
            if g + 1 < N_KV_HEADS:
                s_next = logits(g + 1)
            probs = []
            alphas = []
            for r in range(rep):
                h = g * rep + r
                slope = float(2.0 ** (-8.0 * (h + 1) / N_HEADS)) * LOG2E
                sr = s_all[:, r * Q_TILE:(r + 1) * Q_TILE] + slope * nd_ref[...]
                m_old = m_ref[h]
                m_new = jnp.maximum(m_old, jnp.max(sr, axis=0, keepdims=True))
                alpha = jnp.exp2(m_old - m_new)
                p = jnp.exp2(sr - m_new)
                l_ref[h] = alpha * l_ref[h] + jnp.sum(p, axis=0, keepdims=True)
                m_ref[h] = m_new
                probs.append(p.astype(BF16))
                alphas.append(alpha)
            vt = vt_ref[c, g * HEAD_DIM:(g + 1) * HEAD_DIM, :]
            pv = jnp.dot(vt, jnp.concatenate(probs, axis=1), preferred_element_type=F32)
            acc_ref[g] = jnp.concatenate(alphas, axis=1) * acc_ref[g] + pv
        return carry

    lax.fori_loop(0, n_chunks, attend_chunk, 0)

    for g in range(N_KV_HEADS):
        for r in range(rep):
            h = g * rep + r
            o_t = acc_ref[g, :, r * Q_TILE:(r + 1) * Q_TILE] * (1.0 / l_ref[h])
            o_ref[0, :, h * HEAD_DIM:(h + 1) * HEAD_DIM] = o_t.T.astype(o_ref.dtype)


def _dsa_attention(qkvi, small, b, s):
    d_attn = N_HEADS * HEAD_DIM
    d_kv = N_KV_HEADS * HEAD_DIM
    d_qi = IDX_HEADS * IDX_DIM
    rep = N_HEADS // N_KV_HEADS
    chunk = min(KEY_CHUNK, s)
    topk = min(TOPK_MAX, s // 4)
    assert s % chunk == 0 and chunk % Q_TILE == 0 and d_attn % d_qi == 0 and d_attn % d_kv == 0
    assert Q_TILE == LANES and HEAD_DIM == LANES and 2 * IDX_DIM == LANES
    kern = functools.partial(_dsa_kernel, seq=s, chunk=chunk, topk=topk)
    return pl.pallas_call(
        kern,
        grid=(b, s // Q_TILE),
        in_specs=[pl.BlockSpec((1, Q_TILE, d_attn), lambda bi, t: (bi, t, 0)),
                  pl.BlockSpec((1, Q_TILE, d_qi), lambda bi, t: (bi, t, (d_attn + 2 * d_kv) // d_qi)),
                  pl.BlockSpec((1, s, d_kv), lambda bi, t: (bi, 0, d_attn // d_kv)),
                  pl.BlockSpec((1, s, d_kv), lambda bi, t: (bi, 0, d_attn // d_kv + 1)),
                  pl.BlockSpec((1, s, LANES), lambda bi, t: (bi, 0, 0)),
                  pl.BlockSpec((1, Q_TILE, LANES), lambda bi, t: (bi, t, 0))],
        out_specs=pl.BlockSpec((1, Q_TILE, d_attn), lambda bi, t: (bi, t, 0)),
        out_shape=jax.ShapeDtypeStruct((b, s, d_attn), BF16),
        scratch_shapes=[pltpu.VMEM((s, LANES), BF16),
                        pltpu.VMEM((s, LANES), BF16),
                        pltpu.VMEM((s // chunk, d_kv, chunk), BF16),
                        pltpu.VMEM((s // chunk, chunk, Q_TILE), I32),
                        pltpu.VMEM((1, Q_TILE), I32),
                        pltpu.VMEM((chunk, Q_TILE), F32),
                        pltpu.VMEM((N_KV_HEADS, rep * Q_TILE, HEAD_DIM), BF16),
                        pltpu.VMEM((IDX_HEADS // 4, 2 * Q_TILE, LANES), BF16),
                        pltpu.VMEM((N_HEADS, 1, Q_TILE), F32),
                        pltpu.VMEM((N_HEADS, 1, Q_TILE), F32),
                        pltpu.VMEM((N_KV_HEADS, HEAD_DIM, rep * Q_TILE), F32)],
        compiler_params=_cparams(("parallel", "arbitrary"), 48),
        name="dsa_attention",
    )(qkvi, qkvi, qkvi, qkvi, small, small)


def _pack_bf16_halves(hb):
    half = hb.shape[1] // 2
    bits = lax.bitcast_convert_type(hb.astype(F32), I32)
    return lax.shift_right_logical(bits[:, :half], 16) | (bits[:, half:] & -65536)


def _unpack_bf16_halves(words):
    lo = lax.bitcast_convert_type(words << 16, F32).astype(BF16)
    hi = lax.bitcast_convert_type(words & -65536, F32).astype(BF16)
    return lo, hi


def _outproj_kernel(conv_ref, attn_ref, wc_ref, wa_ref, x_ref, g_ref, b_ref, wr_ref, br_ref,
                    h_ref, hp_ref, lg_ref, hb_ref, *, alpha):
    j = pl.program_id(1)
    tm, d = h_ref.shape
    tn = x_ref.shape[1]
    n_tiles = d // tn
    mix = jnp.dot(conv_ref[...], wc_ref[...], preferred_element_type=F32)
    mix = mix + jnp.dot(attn_ref[...], wa_ref[...], preferred_element_type=F32)
    z = alpha * x_ref[...] + mix
    for jj in range(n_tiles):
        @pl.when(j == jj)
        def _(jj=jj):
            h_ref[:, jj * tn:(jj + 1) * tn] = z

    @pl.when(j == n_tiles - 1)
    def _():
        def rows(i, carry):
            starts = [pl.multiple_of((LN_GROUPS * i + k) * BF16_ROWS, BF16_ROWS) for k in range(LN_GROUPS)]
            zs = [h_ref[pl.ds(r0, BF16_ROWS), :] for r0 in starts]
            hs = [_layer_norm_rows(z, g_ref[...], b_ref[...]) for z in zs]
            for r0, h in zip(starts, hs):
                h_ref[pl.ds(r0, BF16_ROWS), :] = h
                hb_ref[pl.ds(r0, BF16_ROWS), :] = h.astype(BF16)
            return carry
        lax.fori_loop(0, tm // (LN_GROUPS * BF16_ROWS), rows, 0)

        def pack_rows(i, carry):
            r0 = pl.multiple_of(i * BF16_ROWS, BF16_ROWS)
            hp_ref[pl.ds(r0, BF16_ROWS), :] = _pack_bf16_halves(hb_ref[pl.ds(r0, BF16_ROWS), :])
            return carry
        lax.fori_loop(0, tm // BF16_ROWS, pack_rows, 0, unroll=2)
        lg_ref[...] = jnp.dot(hb_ref[...], wr_ref[...], preferred_element_type=F32) + br_ref[...]


def _out_proj_ln(conv_out, attn_out, w_conv, w_attn, x2, ln_g, ln_b, w_router, b_router, alpha):
    n, d = x2.shape
    dc = conv_out.shape[1]
    da = attn_out.shape[1]
    tm = min(512, n)
    tn = min(512, d)
    assert n % tm == 0 and d % tn == 0
    kern = functools.partial(_outproj_kernel, alpha=alpha)
    return pl.pallas_call(
        kern,
        grid=(n // tm, d // tn),
        in_specs=[pl.BlockSpec((tm, dc), lambda i, j: (i, 0)),
                  pl.BlockSpec((tm, da), lambda i, j: (i, 0)),
                  pl.BlockSpec((dc, tn), lambda i, j: (0, j)),
                  pl.BlockSpec((da, tn), lambda i, j: (0, j)),
                  pl.BlockSpec((tm, tn), lambda i, j: (i, j)),
                  pl.BlockSpec((1, d), lambda i, j: (0, 0)),
                  pl.BlockSpec((1, d), lambda i, j: (0, 0)),
                  pl.BlockSpec((d, LANES), lambda i, j: (0, 0)),
                  pl.BlockSpec((1, LANES), lambda i, j: (0, 0))],
        out_specs=[pl.BlockSpec((tm, d), lambda i, j: (i, 0)),
                   pl.BlockSpec((tm, d // 2), lambda i, j: (i, 0)),
                   pl.BlockSpec((tm, LANES), lambda i, j: (i, 0))],
        out_shape=[jax.ShapeDtypeStruct((n, d), F32), jax.ShapeDtypeStruct((n, d // 2), I32),
                   jax.ShapeDtypeStruct((n, LANES), F32)],
        scratch_shapes=[pltpu.VMEM((tm, d), BF16)],
        compiler_params=_cparams(("parallel", "arbitrary"), 56),
        name="out_proj_ln1",
    )(conv_out, attn_out, w_conv, w_attn, x2, ln_g, ln_b, w_router, b_router)


def _first_argmax(vals, lane, valid):
    masked = jnp.where(valid, vals, -jnp.inf)
    mx = jnp.max(masked, axis=1, keepdims=True)
    idx = jnp.min(jnp.where(valid & (masked == mx), lane, LANES), axis=1, keepdims=True)
    return mx, idx


def _router_kernel(lg_ref, tri_ref, e_ref, gate_ref, cnt_ref):
    @pl.when(pl.program_id(0) == 0)
    def _():
        cnt_ref[...] = jnp.zeros(cnt_ref.shape, F32)

    lg = lg_ref[...]
    lane = lax.broadcasted_iota(I32, lg.shape, 1)
    is_group = lane < N_GROUPS
    g_max, g_sel = _first_argmax(lg, lane, is_group)
    g_den = jnp.sum(jnp.where(is_group, jnp.exp(lg - g_max), 0.0), axis=1, keepdims=True)
    p_group = 1.0 / g_den
    lo = N_GROUPS + g_sel * EXPERTS_PER_GROUP
    in_group = (lane >= lo) & (lane < lo + EXPERTS_PER_GROUP)
    e_max, _ = _first_argmax(lg, lane, in_group)
    e_exp = jnp.where(in_group, jnp.exp(lg - e_max), 0.0)
    prob = e_exp / jnp.sum(e_exp, axis=1, keepdims=True)
    p1, i1 = _first_argmax(prob, lane, in_group)
    p2, i2 = _first_argmax(prob, lane, in_group & (lane != i1))
    norm = p_group / (p1 + p2)
    gate_ref[...] = jnp.where(lane == 0, p1 * norm, jnp.where(lane == 1, p2 * norm, 0.0))
    chosen = jnp.where((lane == i1) | (lane == i2), 1.0, 0.0)
    before = jnp.dot(tri_ref[...], chosen.astype(BF16), preferred_element_type=F32) + cnt_ref[...]
    rank1 = jnp.sum(jnp.where(lane == i1, before, 0.0), axis=1, keepdims=True).astype(I32)
    rank2 = jnp.sum(jnp.where(lane == i2, before, 0.0), axis=1, keepdims=True).astype(I32)
    cnt_ref[...] += jnp.sum(chosen, axis=0, keepdims=True)
    e_ref[...] = jnp.where(lane == 0, i1 - N_GROUPS,
                           jnp.where(lane == 1, i2 - N_GROUPS,
                                     jnp.where(lane == 2, rank1, jnp.where(lane == 3, rank2, 0))))


def _route(logits):
    n = logits.shape[0]
    tm = min(512, n)
    tri = jnp.tril(jnp.ones((tm, tm), BF16), -1)
    spec = pl.BlockSpec((tm, LANES), lambda i: (i, 0))
    return pl.pallas_call(
        _router_kernel,
        grid=(n // tm,),
        in_specs=[spec, pl.BlockSpec((tm, tm), lambda i: (0, 0))],
        out_specs=[spec, spec, pl.BlockSpec((1, LANES), lambda i: (0, 0))],
        out_shape=[jax.ShapeDtypeStruct((n, LANES), I32), jax.ShapeDtypeStruct((n, LANES), F32),
                   jax.ShapeDtypeStruct((1, LANES), F32)],
        compiler_params=_cparams(("arbitrary",), 32),
        name="router",
    )(logits, tri)


def _start_row_copies(src_hbm, idx_ref, idx_row, dst_ref, slot, row0, sem, count):
    group = 8

    def issue(i, carry):
        for k in range(group):
            r = i * group + k
            pltpu.make_async_copy(src_hbm.at[pl.ds(idx_ref[0, idx_row, r], 1), :],
                                  dst_ref.at[slot, pl.ds(row0 + r, 1), :], sem.at[slot]).start(priority=k % 2)
        return carry
    lax.fori_loop(0, count // group, issue, 0)


def _wait_slot(src_hbm, dst_ref, slot, sem):
    rows = dst_ref.shape[1]
    pltpu.make_async_copy(src_hbm.at[pl.ds(0, rows), :], dst_ref.at[slot], sem.at[slot]).wait()


def _combine_kernel(pos_ref, pos_next_ref, y_hbm, h_ref, gate_ref, g_ref, b_ref, o_ref, buf_ref, sem, *, alpha):
    i = pl.program_id(0)
    n = pl.num_programs(0)
    slot = i % 2
    rows = h_ref.shape[0]

    def start(idx_ref, into):
        for k in range(2):
            _start_row_copies(y_hbm, idx_ref, k, buf_ref, into, k * rows, sem, rows)

    @pl.when(i == 0)
    def _():
        start(pos_ref, 0)

    @pl.when(i + 1 < n)
    def _():
        start(pos_next_ref, 1 - slot)

    _wait_slot(y_hbm, buf_ref, slot, sem)

    def norm_rows(j, carry):
        r0 = pl.multiple_of(j * SUBLANES, SUBLANES)
        gates = gate_ref[pl.ds(r0, SUBLANES), :]
        ffn = (buf_ref[slot, pl.ds(r0, SUBLANES), :] * gates[:, 0:1]
               + buf_ref[slot, pl.ds(rows + r0, SUBLANES), :] * gates[:, 1:2])
        z = alpha * h_ref[pl.ds(r0, SUBLANES), :] + ffn
        o_ref[pl.ds(r0, SUBLANES), :] = _layer_norm_rows(z, g_ref[...], b_ref[...])
        return carry
    lax.fori_loop(0, rows // SUBLANES, norm_rows, 0, unroll=8)


def _combine_ln(y, pos2, h1, gate_lanes, ln_g, ln_b, alpha):
    n, d = h1.shape
    tb = min(COMBINE_ROWS, n)
    nblk = n // tb
    pos_blocks = pos2.reshape(nblk, tb, 2).transpose(0, 2, 1)
    kern = functools.partial(_combine_kernel, alpha=alpha)
    return pl.pallas_call(
        kern,
        grid=(nblk,),
        in_specs=[pl.BlockSpec((1, 2, tb), lambda i: (i, 0, 0), memory_space=pltpu.SMEM),
                  pl.BlockSpec((1, 2, tb), lambda i: (jnp.minimum(i + 1, nblk - 1), 0, 0), memory_space=pltpu.SMEM),
                  pl.BlockSpec(memory_space=pl.ANY),
                  pl.BlockSpec((tb, d), lambda i: (i, 0)),
                  pl.BlockSpec((tb, LANES), lambda i: (i, 0)),
                  pl.BlockSpec((1, d), lambda i: (0, 0)),
                  pl.BlockSpec((1, d), lambda i: (0, 0))],
        out_specs=pl.BlockSpec((tb, d), lambda i: (i, 0)),
        out_shape=jax.ShapeDtypeStruct((n, d), F32),
        scratch_shapes=[pltpu.VMEM((2, 2 * tb, d), F32), pltpu.SemaphoreType.DMA((2,))],
        compiler_params=_cparams(("arbitrary",), 32),
        name="moe_combine_ln2",
    )(pos_blocks, pos_blocks, y, h1, gate_lanes, ln_g, ln_b)


def _cast_weight(dst_ref, src_ref):
    rows = src_ref.shape[1]
    step = min(256, rows)

    def body(i, carry):
        r0 = pl.multiple_of(i * step, step)
        dst_ref[pl.ds(r0, step), :] = src_ref[0, pl.ds(r0, step), :].astype(BF16)
        return carry
    lax.fori_loop(0, rows // step, body, 0)


ITEM_COMPUTE = 1
ITEM_NEW_WEIGHTS = 2
ITEM_ZERO_FILL = 4


def _expert_hidden(words, w1b_ref, w3b_ref):
    x_lo, x_hi = _unpack_bf16_halves(words)
    half = x_lo.shape[1]

    def project(w_ref):
        return (jnp.dot(x_lo, w_ref[0:half, :], preferred_element_type=F32)
                + jnp.dot(x_hi, w_ref[half:2 * half, :], preferred_element_type=F32))

    a = project(w1b_ref)
    b = project(w3b_ref)
    return (a * jax.nn.sigmoid(a) * b).astype(BF16)


def _moe_up_gather_kernel(e_ref, cw_ref, bi_ref, bo_ref, tiles_ref, flag_ref, tok_ref, tok1_ref, tok2_ref, hp_hbm,
                          w1_ref, w3_ref, o_ref, xs_ref, buf_ref, w1b_ref, w3b_ref, sem):
    it = pl.program_id(0)
    last = pl.num_programs(0) - 1
    flag = flag_ref[it]
    n_slots, n_tiles = buf_ref.shape[0], buf_ref.shape[1]
    slot = it % n_slots

    def computes(item):
        return (item <= last) & ((flag_ref[jnp.minimum(item, last)] & ITEM_COMPUTE) != 0)

    def tiles_of(item):
        return tiles_ref[jnp.minimum(item, last)]

    def start_rows(idx_ref, into, n_issue):
        def issue(i, carry):
            for k in range(SUBLANES):
                tok = idx_ref[0, 0, i * SUBLANES + k]
                pltpu.make_async_copy(hp_hbm.at[lax.shift_right_logical(tok, 3), pl.ds(tok & (SUBLANES - 1), 1), :],
                                      buf_ref.at[into, i, pl.ds(k, 1), :], sem.at[into]).start(priority=k % 2)
            return carry
        lax.fori_loop(0, n_issue, issue, 0)

    @pl.when(it == 0)
    def _():
        buf_ref[...] = jnp.zeros(buf_ref.shape, buf_ref.dtype)
        start_rows(tok_ref, 0, tiles_of(0))

        @pl.when(computes(1))
        def _():
            start_rows(tok1_ref, 1, tiles_of(1))

    @pl.when(computes(it + 2))
    def _():
        start_rows(tok2_ref, (it + 2) % n_slots, tiles_of(it + 2))

    @pl.when((flag & ITEM_NEW_WEIGHTS) != 0)
    def _():
        _cast_weight(w1b_ref, w1_ref)
        _cast_weight(w3b_ref, w3_ref)

    @pl.when((flag & ITEM_COMPUTE) != 0)
    def _():
        def wait_tile(i, carry):
            pltpu.make_async_copy(hp_hbm.at[pl.ds(0, 1)], buf_ref.at[slot, pl.ds(i, 1)], sem.at[slot]).wait()
            return carry
        lax.fori_loop(0, tiles_of(it), wait_tile, 0)
        words = buf_ref[slot].reshape(n_tiles * SUBLANES, buf_ref.shape[3])
        xs_ref[...] = words
        o_ref[...] = _expert_hidden(words, w1b_ref, w3b_ref)

    @pl.when((flag & ITEM_ZERO_FILL) != 0)
    def _():
        o_ref[...] = jnp.zeros(o_ref.shape, o_ref.dtype)
        xs_ref[...] = jnp.zeros(xs_ref.shape, xs_ref.dtype)


def _moe_up_kernel(e_ref, cw_ref, bi_ref, bo_ref, co_ref, flag_ref, x_ref, w1_ref, w3_ref, o_ref, w1b_ref, w3b_ref):
    flag = flag_ref[pl.program_id(0)]

    @pl.when((flag & ITEM_NEW_WEIGHTS) != 0)
    def _():
        _cast_weight(w1b_ref, w1_ref)
        _cast_weight(w3b_ref, w3_ref)

    @pl.when((flag & ITEM_COMPUTE) != 0)
    def _():
        o_ref[...] = _expert_hidden(x_ref[...], w1b_ref, w3b_ref)

    @pl.when((flag & ITEM_ZERO_FILL) != 0)
    def _():
        o_ref[...] = jnp.zeros(o_ref.shape, o_ref.dtype)


def _moe_down_kernel(e_ref, cw_ref, bi_ref, bo_ref, co_ref, flag_ref, h0_ref, h1_ref, w2_ref, o_ref, w2b_ref):
    flag = flag_ref[pl.program_id(0)]

    @pl.when((flag & ITEM_NEW_WEIGHTS) != 0)
    def _():
        _cast_weight(w2b_ref, w2_ref)

    @pl.when((flag & ITEM_COMPUTE) != 0)
    def _():
        fc = h0_ref.shape[1]
        o_ref[...] = (jnp.dot(h0_ref[...], w2b_ref[0:fc, :], preferred_element_type=F32)
                      + jnp.dot(h1_ref[...], w2b_ref[fc:2 * fc, :], preferred_element_type=F32))

    @pl.when((flag & ITEM_ZERO_FILL) != 0)
    def _():
        o_ref[...] = jnp.zeros(o_ref.shape, o_ref.dtype)


def _work_items(blocks_per_expert, block_start, n_chunks, n_blocks):
    n_items = n_chunks * n_blocks
    per_e = n_chunks * blocks_per_expert
    end = jnp.cumsum(per_e)
    start = end - per_e
    it = jnp.arange(n_items, dtype=I32)
    total = end[-1]
    used_blocks = total // n_chunks
    itc = jnp.minimum(it, total - 1)
    e = jnp.sum(itc[:, None] >= end[None, :], axis=1).astype(I32)
    local = itc - start[e]
    nb = jnp.maximum(blocks_per_expert[e], 1)
    c = local // nb
    r = local - c * nb
    b_in = block_start[e] + r
    active = it < total
    spare = jnp.maximum(it - total, 0)
    b_out = jnp.where(active, b_in, used_blocks + spare // n_chunks)
    c_out = jnp.where(active, c, spare % n_chunks)
    first = active & (r == 0)
    flag = jnp.where(active, ITEM_COMPUTE + ITEM_NEW_WEIGHTS * first.astype(I32), ITEM_ZERO_FILL)
    later_first = lax.cummin(jnp.where(first, it, n_items)[::-1])[::-1]
    next_start = jnp.concatenate([later_first[1:], jnp.full((1,), n_items, I32)])
    ahead = jnp.minimum(next_start, n_items - 1)
    use_next = active & ~first & (next_start < n_items)
    e_w = jnp.where(use_next, e[ahead], e)
    c_w = jnp.where(use_next, c[ahead], c)
    return tuple(v.astype(I32) for v in (e_w, c_w, b_in, b_out, c_out, flag))


def _moe_up_gather(hp, buf_tok, w1, w3, items):
    p = buf_tok.shape[0]
    half = hp.shape[1]
    d, f = w1.shape[1], w1.shape[2]
    fc = f // 2
    n_items = items[0].shape[0]
    nblk = p // MOE_ROWS
    tok_blocks = buf_tok.reshape(nblk, 1, MOE_ROWS)
    wspec = pl.BlockSpec((1, d, fc), lambda it, e, cw, bi, bo, co, fl: (e[it], 0, 0))
    lookahead = 2

    def tok_spec(ahead):
        return pl.BlockSpec((1, 1, MOE_ROWS),
                            lambda it, e, cw, bi, bo, co, fl: (bi[jnp.minimum(it + ahead, n_items - 1)], 0, 0),
                            memory_space=pltpu.SMEM)

    return pl.pallas_call(
        _moe_up_gather_kernel,
        grid_spec=pltpu.PrefetchScalarGridSpec(
            num_scalar_prefetch=6,
            grid=(n_items,),
            in_specs=[tok_spec(k) for k in range(lookahead + 1)] + [pl.BlockSpec(memory_space=pl.ANY), wspec, wspec],
            out_specs=[pl.BlockSpec((MOE_ROWS, fc), lambda it, e, cw, bi, bo, co, fl: (bo[it], 0)),
                       pl.BlockSpec((MOE_ROWS, half), lambda it, e, cw, bi, bo, co, fl: (bo[it], 0))],
            scratch_shapes=[pltpu.VMEM((lookahead + 1, MOE_ROWS // SUBLANES, SUBLANES, half), I32),
                            pltpu.VMEM((d, fc), BF16),
                            pltpu.VMEM((d, fc), BF16), pltpu.SemaphoreType.DMA((lookahead + 1,))]),
        out_shape=[jax.ShapeDtypeStruct((p, fc), BF16), jax.ShapeDtypeStruct((p, half), I32)],
        compiler_params=_cparams(("arbitrary",), 56),
        name="moe_up_gather",
    )(*items, *([tok_blocks] * (lookahead + 1)), hp.reshape(hp.shape[0] // SUBLANES, SUBLANES, half), w1, w3)


def _moe_up(xs, w1, w3, items):
    p, half = xs.shape
    d, f = w1.shape[1], w1.shape[2]
    fc = f // 2
    assert d == 2 * half
    wspec = pl.BlockSpec((1, d, fc), lambda it, e, cw, bi, bo, co, fl: (e[it], 0, 1))
    return pl.pallas_call(
        _moe_up_kernel,
        grid_spec=pltpu.PrefetchScalarGridSpec(
            num_scalar_prefetch=6,
            grid=(items[0].shape[0],),
            in_specs=[pl.BlockSpec((MOE_ROWS, half), lambda it, e, cw, bi, bo, co, fl: (bi[it], 0)), wspec, wspec],
            out_specs=pl.BlockSpec((MOE_ROWS, fc), lambda it, e, cw, bi, bo, co, fl: (bo[it], 0)),
            scratch_shapes=[pltpu.VMEM((d, fc), BF16), pltpu.VMEM((d, fc), BF16)]),
        out_shape=jax.ShapeDtypeStruct((p, fc), BF16),
        compiler_params=_cparams(("arbitrary",), 56),
        name="moe_up",
    )(*items, xs, w1, w3)


def _moe_down(h0, h1, w2, items, d_chunk):
    p, fc = h0.shape
    f = 2 * fc
    d = w2.shape[2]
    hspec = pl.BlockSpec((MOE_ROWS, fc), lambda it, e, cw, bi, bo, co, fl: (bi[it], 0))
    return pl.pallas_call(
        _moe_down_kernel,
        grid_spec=pltpu.PrefetchScalarGridSpec(
            num_scalar_prefetch=6,
            grid=(items[0].shape[0],),
            in_specs=[hspec, hspec,
                      pl.BlockSpec((1, f, d_chunk), lambda it, e, cw, bi, bo, co, fl: (e[it], 0, cw[it]))],
            out_specs=pl.BlockSpec((MOE_ROWS, d_chunk), lambda it, e, cw, bi, bo, co, fl: (bo[it], co[it])),
            scratch_shapes=[pltpu.VMEM((f, d_chunk), BF16)]),
        out_shape=jax.ShapeDtypeStruct((p, d), F32),
        compiler_params=_cparams(("arbitrary",), 56),
        name="moe_down",
    )(*items, h0, h1, w2)


def _hier_moe_ln(h1, hp, logits, w1, w3, w2, ln_g, ln_b, alpha):
    n, d = h1.shape
    e_lanes, gate_lanes, lane_counts = _route(logits)
    a = 2 * n
    e_flat = e_lanes[:, 0:2].reshape(a)
    rank = e_lanes[:, 2:4].reshape(a)
    counts = lane_counts[0, N_GROUPS:N_GROUPS + N_EXPERTS].astype(I32)
    blocks_per_expert = (counts + MOE_ROWS - 1) // MOE_ROWS
    block_start = jnp.cumsum(blocks_per_expert) - blocks_per_expert
    onehot = e_flat[:, None] == jnp.arange(N_EXPERTS, dtype=I32)[None, :]
    pos = jnp.sum(jnp.where(onehot, block_start[None, :], 0), axis=1) * MOE_ROWS + rank
    n_blocks = (a + MOE_ROWS - 1) // MOE_ROWS + N_EXPERTS
    buf_tok = jnp.zeros((n_blocks * MOE_ROWS,), I32).at[pos].set(jnp.arange(a, dtype=I32) // 2)

    d_chunk = min(4096, d)
    up_items = _work_items(blocks_per_expert, block_start, 1, n_blocks)
    blk = jnp.arange(n_blocks, dtype=I32)
    e_of = jnp.minimum(jnp.sum(blk[:, None] >= (block_start + blocks_per_expert)[None, :], axis=1), N_EXPERTS - 1)
    real_rows = jnp.clip(counts[e_of] - (blk - block_start[e_of]) * MOE_ROWS, 0, MOE_ROWS)
    item_tiles = ((real_rows + SUBLANES - 1) // SUBLANES)[up_items[2]].astype(I32)
    h0, xs = _moe_up_gather(hp, buf_tok, w1, w3, up_items[:4] + (item_tiles,) + up_items[5:])
    h1_mid = _moe_up(xs, w1, w3, up_items)
    y = _moe_down(h0, h1_mid, w2, _work_items(blocks_per_expert, block_start, d // d_chunk, n_blocks), d_chunk)
    return _combine_ln(y, pos.reshape(n, 2), h1, gate_lanes, ln_g, ln_b, alpha)


def kernel(x, w_in, conv_dw_w, conv_dw_b, conv_ln_g, conv_ln_b, w_out, ln1_g, ln1_b, w_router_group, b_router_group, w_router_expert, b_router_expert, w_expert_gate, w_expert_up, w_expert_down, ln2_g, ln2_b):
    b, s, d = x.shape
    depth = w_in.shape[0]
    assert depth == 1
    alpha = float((2.0 * depth) ** 0.25)
    n = b * s
    dc = conv_dw_w.shape[2]
    d_attn = N_HEADS * HEAD_DIM
    d_kv = N_KV_HEADS * HEAD_DIM
    d_qi = IDX_HEADS * IDX_DIM
    n_small = IDX_DIM + IDX_HEADS
    assert w_in.shape[2] == 2 * dc + d_attn + 2 * d_kv + d_qi + n_small

    x2 = x.reshape(n, d)
    w = w_in[0].astype(BF16)
    o_qkvi = 2 * dc
    o_small = o_qkvi + d_attn + 2 * d_kv + d_qi
    w_small = jnp.pad(w[:, o_small:], ((0, 0), (0, LANES - n_small)))
    qkvi_scale = jnp.concatenate([jnp.full((d_attn,), HEAD_DIM ** -0.5 * LOG2E, F32), jnp.ones((2 * d_kv,), F32),
                                  jnp.full((d_qi,), IDX_DIM ** -0.5, F32)])[None, :]

    xb, small = _proj_idx_and_cast(x2, w_small)
    glu = _proj_glu(xb, w, dc)
    qkvi = _proj_scale(xb, w, o_qkvi, o_small - o_qkvi, qkvi_scale, BF16, "proj_qkvi")

    conv_out = _conformer_conv(glu.reshape(b, s, dc), conv_dw_w[0], conv_dw_b, conv_ln_g, conv_ln_b)
    attn_out = _dsa_attention(qkvi.reshape(b, s, -1), small.reshape(b, s, LANES), b, s)

    n_route = N_GROUPS + N_EXPERTS
    w_router = jnp.pad(jnp.concatenate([w_router_group[0], w_router_expert[0]], axis=1),
                       ((0, 0), (0, LANES - n_route))).astype(BF16)
    b_router = jnp.pad(jnp.concatenate([b_router_group[0], b_router_expert[0]]), (0, LANES - n_route))[None, :]
    h1, hp, logits = _out_proj_ln(conv_out.reshape(n, dc), attn_out.reshape(n, d_attn),
                                  w_out[0, :dc].astype(BF16), w_out[0, dc:].astype(BF16),
                                  x2, ln1_g, ln1_b, w_router, b_router, alpha)

    out = _hier_moe_ln(h1, hp, logits, w_expert_gate[0], w_expert_up[0], w_expert_down[0], ln2_g, ln2_b, alpha)
    return out.reshape(b, s, d)
```

```python
import functools
import math

import jax
import jax.numpy as jnp
from jax import lax
from jax.experimental import pallas as pl
from jax.experimental.pallas import tpu as pltpu

F32 = jnp.float32
BF16 = jnp.bfloat16
I32 = jnp.int32

CONV_WIDTH = 31
N_HEADS = 16
HEAD_DIM = 128
N_KV_HEADS = 4
IDX_HEADS = 16
IDX_DIM = 64
TOPK_MAX = 256
N_GROUPS = 4
EXPERTS_PER_GROUP = 8
N_EXPERTS = N_GROUPS * EXPERTS_PER_GROUP
LN_EPS = 1e-5

LANES = 128
SUBLANES = 8
BF16_ROWS = 16
MIB = 1024 * 1024

Q_TILE = 128
KEY_CHUNK = 512
CONV_HALO = 32
MOE_ROWS = 256
COMBINE_ROWS = 128
LN_GROUPS = 4
INT_MIN = -(2 ** 31)
INT_MAX = 2 ** 31 - 1
LOG2E = math.log2(math.e)
NEG_BIG = -1e30
M_INIT = -1e20


def _cparams(semantics, vmem_mib):
    return pltpu.CompilerParams(dimension_semantics=semantics, vmem_limit_bytes=vmem_mib * MIB)


def _layer_norm_rows(z, g, b):
    mu = jnp.mean(z, axis=-1, keepdims=True)
    zc = z - mu
    var = jnp.mean(zc * zc, axis=-1, keepdims=True)
    return zc * lax.rsqrt(var + LN_EPS) * g + b


def _mm_glu_kernel(x_ref, wa_ref, wg_ref, o_ref):
    x = x_ref[...]
    a = jnp.dot(x, wa_ref[...], preferred_element_type=F32)
    g = jnp.dot(x, wg_ref[...], preferred_element_type=F32)
    o_ref[...] = (a * jax.nn.sigmoid(g)).astype(o_ref.dtype)


def _mm_scale_kernel(x_ref, w_ref, s_ref, o_ref):
    acc = jnp.dot(x_ref[...], w_ref[...], preferred_element_type=F32)
    o_ref[...] = (acc * s_ref[...]).astype(o_ref.dtype)


def _mm_tiles(n, k, cols):
    tm = min(1024, n)
    tn = min(512, cols)
    assert n % tm == 0 and cols % tn == 0
    return tm, tn


def _proj_glu(xb, w, cols):
    n, k = xb.shape
    tm, tn = _mm_tiles(n, k, cols)
    gate_block = cols // tn
    return pl.pallas_call(
        _mm_glu_kernel,
        grid=(n // tm, cols // tn),
        in_specs=[pl.BlockSpec((tm, k), lambda i, j: (i, 0)),
                  pl.BlockSpec((k, tn), lambda i, j: (0, j)),
                  pl.BlockSpec((k, tn), lambda i, j: (0, j + gate_block))],
        out_specs=pl.BlockSpec((tm, tn), lambda i, j: (i, j)),
        out_shape=jax.ShapeDtypeStruct((n, cols), F32),
        compiler_params=_cparams(("parallel", "arbitrary"), 48),
        name="proj_glu",
    )(xb, w, w)


def _proj_scale(xb, w, first_col, cols, scale, out_dtype, name):
    n, k = xb.shape
    tm, tn = _mm_tiles(n, k, cols)
    assert first_col % tn == 0
    first_block = first_col // tn
    return pl.pallas_call(
        _mm_scale_kernel,
        grid=(n // tm, cols // tn),
        in_specs=[pl.BlockSpec((tm, k), lambda i, j: (i, 0)),
                  pl.BlockSpec((k, tn), lambda i, j: (0, j + first_block)),
                  pl.BlockSpec((1, tn), lambda i, j: (0, j))],
        out_specs=pl.BlockSpec((tm, tn), lambda i, j: (i, j)),
        out_shape=jax.ShapeDtypeStruct((n, cols), out_dtype),
        compiler_params=_cparams(("parallel", "arbitrary"), 48),
        name=name,
    )(xb, w, scale)


def _idx_cast_kernel(x_ref, w_ref, xb_ref, o_ref):
    xb = x_ref[...].astype(BF16)
    xb_ref[...] = xb
    o_ref[...] = jnp.dot(xb, w_ref[...], preferred_element_type=F32)


def _proj_idx_and_cast(x2, w_small):
    n, k = x2.shape
    tm = min(256, n)
    return pl.pallas_call(
        _idx_cast_kernel,
        grid=(n // tm,),
        in_specs=[pl.BlockSpec((tm, k), lambda i: (i, 0)),
                  pl.BlockSpec((k, LANES), lambda i: (0, 0))],
        out_specs=[pl.BlockSpec((tm, k), lambda i: (i, 0)),
                   pl.BlockSpec((tm, LANES), lambda i: (i, 0))],
        out_shape=[jax.ShapeDtypeStruct((n, k), BF16), jax.ShapeDtypeStruct((n, LANES), F32)],
        compiler_params=_cparams(("parallel",), 32),
        name="proj_idx",
    )(x2, w_small)


def _conv_kernel(cur_ref, prev_ref, w_ref, b_ref, g_ref, beta_ref, o_ref, sh_ref, y_ref, *, rows, lane_chunk):
    t = pl.program_id(1)
    dc = cur_ref.shape[2]
    row_sub = 32
    sh_ref[0, 0:CONV_HALO, :] = jnp.where(t > 0, prev_ref[0], 0.0)
    sh_ref[0, CONV_HALO:CONV_HALO + rows, :] = cur_ref[0]
    first = CONV_HALO - (CONV_WIDTH - 1)
    shifted_rows = rows + CONV_HALO - SUBLANES
    for lc in range(dc // lane_chunk):
        cols = slice(lc * lane_chunk, (lc + 1) * lane_chunk)
        for r in range(1, SUBLANES):
            sh_ref[r, 0:shifted_rows, cols] = sh_ref[0, r:r + shifted_rows, cols]

    for r0 in range(0, rows, row_sub):
        for lc in range(dc // lane_chunk):
            cols = slice(lc * lane_chunk, (lc + 1) * lane_chunk)
            acc = jnp.zeros((row_sub, lane_chunk), F32)
            for j in range(CONV_WIDTH):
                shift, base = (first + j) % SUBLANES, (first + j) // SUBLANES * SUBLANES
                acc = acc + sh_ref[shift, r0 + base:r0 + base + row_sub, cols] * w_ref[j:j + 1, cols]
            y_ref[r0:r0 + row_sub, cols] = acc + b_ref[:, cols]

    def norm_rows(i, carry):
        r0 = pl.multiple_of(i * BF16_ROWS, BF16_ROWS)
        yn = _layer_norm_rows(y_ref[pl.ds(r0, BF16_ROWS), :], g_ref[...], beta_ref[...])
        o_ref[0, pl.ds(r0, BF16_ROWS), :] = (yn * jax.nn.sigmoid(yn)).astype(o_ref.dtype)
        return carry

    lax.fori_loop(0, rows // BF16_ROWS, norm_rows, 0, unroll=4)


def _conformer_conv(glu, w_dw, b_dw, g_ln, b_ln):
    b, s, dc = glu.shape
    rows = min(128, s)
    halo_blocks = rows // CONV_HALO
    lane_chunk = min(512, dc)
    kern = functools.partial(_conv_kernel, rows=rows, lane_chunk=lane_chunk)
    return pl.pallas_call(
        kern,
        grid=(b, s // rows),
        in_specs=[pl.BlockSpec((1, rows, dc), lambda bi, t: (bi, t, 0)),
                  pl.BlockSpec((1, CONV_HALO, dc), lambda bi, t: (bi, jnp.maximum(t * halo_blocks - 1, 0), 0)),
                  pl.BlockSpec((CONV_WIDTH, dc), lambda bi, t: (0, 0)),
                  pl.BlockSpec((1, dc), lambda bi, t: (0, 0)),
                  pl.BlockSpec((1, dc), lambda bi, t: (0, 0)),
                  pl.BlockSpec((1, dc), lambda bi, t: (0, 0))],
        out_specs=pl.BlockSpec((1, rows, dc), lambda bi, t: (bi, t, 0)),
        out_shape=jax.ShapeDtypeStruct((b, s, dc), BF16),
        scratch_shapes=[pltpu.VMEM((SUBLANES, CONV_HALO + rows, dc), F32), pltpu.VMEM((rows, dc), F32)],
        compiler_params=_cparams(("parallel", "arbitrary"), 32),
        name="conformer_conv",
    )(glu, glu, w_dw, b_dw, g_ln, b_ln)


def _dsa_kernel(q_ref, qi_ref, k_ref, v_ref, kidx_ref, w_ref, o_ref,
                ke_ref, ko_ref, vt_ref, key_ref, cut_ref, nd_ref, qs_ref, qis_ref, m_ref, l_ref, acc_ref,
                *, seq, chunk, topk):
    tb = pl.program_id(1)
    t0 = tb * Q_TILE
    n_chunks = (t0 + Q_TILE + chunk - 1) // chunk
    rep = N_HEADS // N_KV_HEADS
    nt = (((1,), (1,)), ((), ()))

    @pl.when(tb == 0)
    def _():
        def build(c, carry):
            r0 = pl.multiple_of(c * chunk, chunk)
            kx = kidx_ref[0, pl.ds(r0, chunk), :]
            lane = lax.broadcasted_iota(I32, kx.shape, 1)
            ke_ref[pl.ds(r0, chunk), :] = jnp.where(lane < IDX_DIM, kx, 0.0).astype(BF16)
            ko_ref[pl.ds(r0, chunk), :] = jnp.where(lane >= IDX_DIM, pltpu.roll(kx, IDX_DIM, 1), 0.0).astype(BF16)
            for g in range(N_KV_HEADS):
                vg = v_ref[0, pl.ds(r0, chunk), g * HEAD_DIM:(g + 1) * HEAD_DIM]
                vt_ref[c, g * HEAD_DIM:(g + 1) * HEAD_DIM, :] = vg.astype(F32).T.astype(BF16)
            return carry
        lax.fori_loop(0, seq // chunk, build, 0)

    for g in range(N_KV_HEADS):
        for r in range(rep):
            h = g * rep + r
            qs_ref[g, r * Q_TILE:(r + 1) * Q_TILE, :] = q_ref[0, :, h * HEAD_DIM:(h + 1) * HEAD_DIM]
    for jj in range(IDX_HEADS // 4):
        for half in range(2):
            pair = 2 * jj + half
            qis_ref[jj, half * Q_TILE:(half + 1) * Q_TILE, :] = qi_ref[0, :, pair * LANES:(pair + 1) * LANES]
    w_t = w_ref[0].T * (IDX_HEADS ** -0.5)

    key_row = lax.broadcasted_iota(I32, (chunk, Q_TILE), 0)
    q_pos = t0 + lax.broadcasted_iota(I32, (chunk, Q_TILE), 1)

    def score_chunk(c, carry):
        r0 = pl.multiple_of(c * chunk, chunk)
        ke = ke_ref[pl.ds(r0, chunk), :]
        ko = ko_ref[pl.ds(r0, chunk), :]
        acc = jnp.zeros((chunk, Q_TILE), F32)
        for jj in range(IDX_HEADS // 4):
            rhs = qis_ref[jj]
            de = lax.dot_general(ke, rhs, nt, preferred_element_type=F32)
            do = lax.dot_general(ko, rhs, nt, preferred_element_type=F32)
            for half in range(2):
                h_even = 2 * (2 * jj + half)
                cols = slice(half * Q_TILE, (half + 1) * Q_TILE)
                acc = acc + w_t[IDX_DIM + h_even:IDX_DIM + h_even + 1, :] * jnp.maximum(de[:, cols], 0.0)
                acc = acc + w_t[IDX_DIM + h_even + 1:IDX_DIM + h_even + 2, :] * jnp.maximum(do[:, cols], 0.0)
        bits = lax.bitcast_convert_type(acc, I32)
        key = bits ^ ((bits >> 31) & 0x7FFFFFFF)
        key_ref[c] = jnp.where(r0 + key_row <= q_pos, key, INT_MIN)
        return carry

    lax.fori_loop(0, n_chunks, score_chunk, 0)

    def count_ge(cand):
        def count_chunk(c, cnts):
            cnts = list(cnts)
            for r in range(chunk // SUBLANES):
                slab = key_ref[c, r * SUBLANES:(r + 1) * SUBLANES, :]
                cnts[r % len(cnts)] = cnts[r % len(cnts)] + jnp.where(slab >= cand, 1.0, 0.0)
            return tuple(cnts)

        zero = jnp.zeros((SUBLANES, Q_TILE), F32)
        cnts = lax.fori_loop(0, n_chunks, count_chunk, (zero, zero, zero, zero))
        return jnp.sum((cnts[0] + cnts[1]) + (cnts[2] + cnts[3]), axis=0, keepdims=True)

    def bit_step(i, t_u):
        cand_u = t_u | (jnp.int32(1) << (31 - i))
        return jnp.where(count_ge(cand_u ^ INT_MIN) >= float(topk), cand_u, t_u)

    t_u = lax.fori_loop(0, 32, bit_step, jnp.zeros((1, Q_TILE), I32))
    thr = jnp.maximum(t_u ^ INT_MIN, INT_MIN + 1)

    n_ge = count_ge(thr)
    cut_ref[...] = jnp.full(cut_ref.shape, INT_MAX, I32)

    @pl.when(jnp.max(n_ge) > topk)
    def _():
        def count_where(pred):
            def count_chunk(c, cnt):
                r0 = pl.multiple_of(c * chunk, chunk)
                hit = jnp.where(pred(key_ref[c], r0 + key_row), 1, 0)
                return cnt + jnp.sum(hit, axis=0, keepdims=True)
            return lax.fori_loop(0, n_chunks, count_chunk, jnp.zeros((1, Q_TILE), I32))

        keep = topk - count_where(lambda kc, pos: kc > thr)
        pos_bits = seq.bit_length()

        def bit_step(i, cut):
            cand = cut | (jnp.int32(1) << (pos_bits - 1 - i))
            taken = count_where(lambda kc, pos: jnp.where(kc == thr, pos, INT_MAX) < cand)
            return jnp.where(taken <= keep, cand, cut)
        cut_ref[...] = lax.fori_loop(0, pos_bits, bit_step, jnp.zeros((1, Q_TILE), I32))

    cut = cut_ref[...]

    m_ref[...] = jnp.full(m_ref.shape, M_INIT, F32)
    l_ref[...] = jnp.zeros(l_ref.shape, F32)
    acc_ref[...] = jnp.zeros(acc_ref.shape, F32)

    def attend_chunk(c, carry):
        r0 = pl.multiple_of(c * chunk, chunk)
        pos = r0 + key_row
        kc = key_ref[c]
        rank_pos = jnp.where(kc > thr, -1, jnp.where(kc == thr, pos, INT_MAX))
        nd_ref[...] = jnp.where(rank_pos < cut, (pos - q_pos).astype(F32), NEG_BIG)
        def logits(g):
            kg = k_ref[0, pl.ds(r0, chunk), g * HEAD_DIM:(g + 1) * HEAD_DIM]
            return lax.dot_general(kg, qs_ref[g], nt, preferred_element_type=F32)

        s_next = logits(0)
        for g in range(N_KV_HEADS):
            s_all = s_next
            if g + 1 < N_KV_HEADS:
                s_next = logits(g + 1)
            probs = []
            alphas = []
            for r in range(rep):
                h = g * rep + r
                slope = float(2.0 ** (-8.0 * (h + 1) / N_HEADS)) * LOG2E
                sr = s_all[:, r * Q_TILE:(r + 1) * Q_TILE] + slope * nd_ref[...]
                m_old = m_ref[h]
                m_new = jnp.maximum(m_old, jnp.max(sr, axis=0, keepdims=True))
                alpha = jnp.exp2(m_old - m_new)
                p = jnp.exp2(sr - m_new)
                l_ref[h] = alpha * l_ref[h] + jnp.sum(p, axis=0, keepdims=True)
                m_ref[h] = m_new
                probs.append(p.astype(BF16))
                alphas.append(alpha)
            vt = vt_ref[c, g * HEAD_DIM:(g + 1) * HEAD_DIM, :]
            pv = jnp.dot(vt, jnp.concatenate(probs, axis=1), preferred_element_type=F32)
            acc_ref[g] = jnp.concatenate(alphas, axis=1) * acc_ref[g] + pv
        return carry

    lax.fori_loop(0, n_chunks, attend_chunk, 0)

    for g in range(N_KV_HEADS):
        for r in range(rep):
            h = g * rep + r
            o_t = acc_ref[g, :, r * Q_TILE:(r + 1) * Q_TILE] * (1.0 / l_ref[h])
            o_ref[0, :, h * HEAD_DIM:(h + 1) * HEAD_DIM] = o_t.T.astype(o_ref.dtype)


def _dsa_attention(qkvi, small, b, s):
    d_attn = N_HEADS * HEAD_DIM
    d_kv = N_KV_HEADS * HEAD_DIM
    d_qi = IDX_HEADS * IDX_DIM
    rep = N_HEADS // N_KV_HEADS
    chunk = min(KEY_CHUNK, s)
    topk = min(TOPK_MAX, s // 4)
    assert s % chunk == 0 and chunk % Q_TILE == 0 and d_attn % d_qi == 0 and d_attn % d_kv == 0
    assert Q_TILE == LANES and HEAD_DIM == LANES and 2 * IDX_DIM == LANES
    kern = functools.partial(_dsa_kernel, seq=s, chunk=chunk, topk=topk)
    return pl.pallas_call(
        kern,
        grid=(b, s // Q_TILE),
        in_specs=[pl.BlockSpec((1, Q_TILE, d_attn), lambda bi, t: (bi, t, 0)),
                  pl.BlockSpec((1, Q_TILE, d_qi), lambda bi, t: (bi, t, (d_attn + 2 * d_kv) // d_qi)),
                  pl.BlockSpec((1, s, d_kv), lambda bi, t: (bi, 0, d_attn // d_kv)),
                  pl.BlockSpec((1, s, d_kv), lambda bi, t: (bi, 0, d_attn // d_kv + 1)),
                  pl.BlockSpec((1, s, LANES), lambda bi, t: (bi, 0, 0)),
                  pl.BlockSpec((1, Q_TILE, LANES), lambda bi, t: (bi, t, 0))],
        out_specs=pl.BlockSpec((1, Q_TILE, d_attn), lambda bi, t: (bi, t, 0)),
        out_shape=jax.ShapeDtypeStruct((b, s, d_attn), BF16),
        scratch_shapes=[pltpu.VMEM((s, LANES), BF16),
                        pltpu.VMEM((s, LANES), BF16),
                        pltpu.VMEM((s // chunk, d_kv, chunk), BF16),
                        pltpu.VMEM((s // chunk, chunk, Q_TILE), I32),
                        pltpu.VMEM((1, Q_TILE), I32),
                        pltpu.VMEM((chunk, Q_TILE), F32),
                        pltpu.VMEM((N_KV_HEADS, rep * Q_TILE, HEAD_DIM), BF16),
                        pltpu.VMEM((IDX_HEADS // 4, 2 * Q_TILE, LANES), BF16),
                        pltpu.VMEM((N_HEADS, 1, Q_TILE), F32),
                        pltpu.VMEM((N_HEADS, 1, Q_TILE), F32),
                        pltpu.VMEM((N_KV_HEADS, HEAD_DIM, rep * Q_TILE), F32)],
        compiler_params=_cparams(("parallel", "arbitrary"), 48),
        name="dsa_attention",
    )(qkvi, qkvi, qkvi, qkvi, small, small)


def _pack_bf16_halves(hb):
    half = hb.shape[1] // 2
    bits = lax.bitcast_convert_type(hb.astype(F32), I32)
    return lax.shift_right_logical(bits[:, :half], 16) | (bits[:, half:] & -65536)


def _unpack_bf16_halves(words):
    lo = lax.bitcast_convert_type(words << 16, F32).astype(BF16)
    hi = lax.bitcast_convert_type(words & -65536, F32).astype(BF16)
    return lo, hi


def _outproj_kernel(conv_ref, attn_ref, wc_ref, wa_ref, x_ref, g_ref, b_ref, wr_ref, br_ref,
                    h_ref, hp_ref, lg_ref, hb_ref, *, alpha):
    j = pl.program_id(1)
    tm, d = h_ref.shape
    tn = x_ref.shape[1]
    n_tiles = d // tn
    mix = jnp.dot(conv_ref[...], wc_ref[...], preferred_element_type=F32)
    mix = mix + jnp.dot(attn_ref[...], wa_ref[...], preferred_element_type=F32)
    z = alpha * x_ref[...] + mix
    for jj in range(n_tiles):
        @pl.when(j == jj)
        def _(jj=jj):
            h_ref[:, jj * tn:(jj + 1) * tn] = z

    @pl.when(j == n_tiles - 1)
    def _():
        def rows(i, carry):
            starts = [pl.multiple_of((LN_GROUPS * i + k) * BF16_ROWS, BF16_ROWS) for k in range(LN_GROUPS)]
            zs = [h_ref[pl.ds(r0, BF16_ROWS), :] for r0 in starts]
            hs = [_layer_norm_rows(z, g_ref[...], b_ref[...]) for z in zs]
            for r0, h in zip(starts, hs):
                h_ref[pl.ds(r0, BF16_ROWS), :] = h
                hb_ref[pl.ds(r0, BF16_ROWS), :] = h.astype(BF16)
            return carry
        lax.fori_loop(0, tm // (LN_GROUPS * BF16_ROWS), rows, 0)

        def pack_rows(i, carry):
            r0 = pl.multiple_of(i * BF16_ROWS, BF16_ROWS)
            hp_ref[pl.ds(r0, BF16_ROWS), :] = _pack_bf16_halves(hb_ref[pl.ds(r0, BF16_ROWS), :])
            return carry
        lax.fori_loop(0, tm // BF16_ROWS, pack_rows, 0, unroll=2)
        lg_ref[...] = jnp.dot(hb_ref[...], wr_ref[...], preferred_element_type=F32) + br_ref[...]


def _out_proj_ln(conv_out, attn_out, w_conv, w_attn, x2, ln_g, ln_b, w_router, b_router, alpha):
    n, d = x2.shape
    dc = conv_out.shape[1]
    da = attn_out.shape[1]
    tm = min(512, n)
    tn = min(512, d)
    assert n % tm == 0 and d % tn == 0
    kern = functools.partial(_outproj_kernel, alpha=alpha)
    return pl.pallas_call(
        kern,
        grid=(n // tm, d // tn),
        in_specs=[pl.BlockSpec((tm, dc), lambda i, j: (i, 0)),
                  pl.BlockSpec((tm, da), lambda i, j: (i, 0)),
                  pl.BlockSpec((dc, tn), lambda i, j: (0, j)),
                  pl.BlockSpec((da, tn), lambda i, j: (0, j)),
                  pl.BlockSpec((tm, tn), lambda i, j: (i, j)),
                  pl.BlockSpec((1, d), lambda i, j: (0, 0)),
                  pl.BlockSpec((1, d), lambda i, j: (0, 0)),
                  pl.BlockSpec((d, LANES), lambda i, j: (0, 0)),
                  pl.BlockSpec((1, LANES), lambda i, j: (0, 0))],
        out_specs=[pl.BlockSpec((tm, d), lambda i, j: (i, 0)),
                   pl.BlockSpec((tm, d // 2), lambda i, j: (i, 0)),
                   pl.BlockSpec((tm, LANES), lambda i, j: (i, 0))],
        out_shape=[jax.ShapeDtypeStruct((n, d), F32), jax.ShapeDtypeStruct((n, d // 2), I32),
                   jax.ShapeDtypeStruct((n, LANES), F32)],
        scratch_shapes=[pltpu.VMEM((tm, d), BF16)],
        compiler_params=_cparams(("parallel", "arbitrary"), 56),
        name="out_proj_ln1",
    )(conv_out, attn_out, w_conv, w_attn, x2, ln_g, ln_b, w_router, b_router)


def _first_argmax(vals, lane, valid):
    masked = jnp.where(valid, vals, -jnp.inf)
    mx = jnp.max(masked, axis=1, keepdims=True)
    idx = jnp.min(jnp.where(valid & (masked == mx), lane, LANES), axis=1, keepdims=True)
    return mx, idx


def _router_kernel(lg_ref, tri_ref, e_ref, gate_ref, cnt_ref):
    @pl.when(pl.program_id(0) == 0)
    def _():
        cnt_ref[...] = jnp.zeros(cnt_ref.shape, F32)

    lg = lg_ref[...]
    lane = lax.broadcasted_iota(I32, lg.shape, 1)
    is_group = lane < N_GROUPS
    g_max, g_sel = _first_argmax(lg, lane, is_group)
    g_den = jnp.sum(jnp.where(is_group, jnp.exp(lg - g_max), 0.0), axis=1, keepdims=True)
    p_group = 1.0 / g_den
    lo = N_GROUPS + g_sel * EXPERTS_PER_GROUP
    in_group = (lane >= lo) & (lane < lo + EXPERTS_PER_GROUP)
    e_max, _ = _first_argmax(lg, lane, in_group)
    e_exp = jnp.where(in_group, jnp.exp(lg - e_max), 0.0)
    prob = e_exp / jnp.sum(e_exp, axis=1, keepdims=True)
    p1, i1 = _first_argmax(prob, lane, in_group)
    p2, i2 = _first_argmax(prob, lane, in_group & (lane != i1))
    norm = p_group / (p1 + p2)
    gate_ref[...] = jnp.where(lane == 0, p1 * norm, jnp.where(lane == 1, p2 * norm, 0.0))
    chosen = jnp.where((lane == i1) | (lane == i2), 1.0, 0.0)
    before = jnp.dot(tri_ref[...], chosen.astype(BF16), preferred_element_type=F32) + cnt_ref[...]
    rank1 = jnp.sum(jnp.where(lane == i1, before, 0.0), axis=1, keepdims=True).astype(I32)
    rank2 = jnp.sum(jnp.where(lane == i2, before, 0.0), axis=1, keepdims=True).astype(I32)
    cnt_ref[...] += jnp.sum(chosen, axis=0, keepdims=True)
    e_ref[...] = jnp.where(lane == 0, i1 - N_GROUPS,
                           jnp.where(lane == 1, i2 - N_GROUPS,
                                     jnp.where(lane == 2, rank1, jnp.where(lane == 3, rank2, 0))))


def _route(logits):
    n = logits.shape[0]
    tm = min(512, n)
    tri = jnp.tril(jnp.ones((tm, tm), BF16), -1)
    spec = pl.BlockSpec((tm, LANES), lambda i: (i, 0))
    return pl.pallas_call(
        _router_kernel,
        grid=(n // tm,),
        in_specs=[spec, pl.BlockSpec((tm, tm), lambda i: (0, 0))],
        out_specs=[spec, spec, pl.BlockSpec((1, LANES), lambda i: (0, 0))],
        out_shape=[jax.ShapeDtypeStruct((n, LANES), I32), jax.ShapeDtypeStruct((n, LANES), F32),
                   jax.ShapeDtypeStruct((1, LANES), F32)],
        compiler_params=_cparams(("arbitrary",), 32),
        name="router",
    )(logits, tri)


def _start_row_copies(src_hbm, idx_ref, idx_row, dst_ref, slot, row0, sem, count):
    group = 8

    def issue(i, carry):
        for k in range(group):
            r = i * group + k
            pltpu.make_async_copy(src_hbm.at[pl.ds(idx_ref[0, idx_row, r], 1), :],
                                  dst_ref.at[slot, pl.ds(row0 + r, 1), :], sem.at[slot]).start(priority=k % 2)
        return carry
    lax.fori_loop(0, count // group, issue, 0)


def _wait_slot(src_hbm, dst_ref, slot, sem):
    rows = dst_ref.shape[1]
    pltpu.make_async_copy(src_hbm.at[pl.ds(0, rows), :], dst_ref.at[slot], sem.at[slot]).wait()


def _combine_kernel(pos_ref, pos_next_ref, y_hbm, h_ref, gate_ref, g_ref, b_ref, o_ref, buf_ref, sem, *, alpha):
    i = pl.program_id(0)
    n = pl.num_programs(0)
    slot = i % 2
    rows = h_ref.shape[0]

    def start(idx_ref, into):
        for k in range(2):
            _start_row_copies(y_hbm, idx_ref, k, buf_ref, into, k * rows, sem, rows)

    @pl.when(i == 0)
    def _():
        start(pos_ref, 0)

    @pl.when(i + 1 < n)
    def _():
        start(pos_next_ref, 1 - slot)

    _wait_slot(y_hbm, buf_ref, slot, sem)

    def norm_rows(j, carry):
        r0 = pl.multiple_of(j * SUBLANES, SUBLANES)
        gates = gate_ref[pl.ds(r0, SUBLANES), :]
        ffn = (buf_ref[slot, pl.ds(r0, SUBLANES), :] * gates[:, 0:1]
               + buf_ref[slot, pl.ds(rows + r0, SUBLANES), :] * gates[:, 1:2])
        z = alpha * h_ref[pl.ds(r0, SUBLANES), :] + ffn
        o_ref[pl.ds(r0, SUBLANES), :] = _layer_norm_rows(z, g_ref[...], b_ref[...])
        return carry
    lax.fori_loop(0, rows // SUBLANES, norm_rows, 0, unroll=8)


def _combine_ln(y, pos2, h1, gate_lanes, ln_g, ln_b, alpha):
    n, d = h1.shape
    tb = min(COMBINE_ROWS, n)
    nblk = n // tb
    pos_blocks = pos2.reshape(nblk, tb, 2).transpose(0, 2, 1)
    kern = functools.partial(_combine_kernel, alpha=alpha)
    return pl.pallas_call(
        kern,
        grid=(nblk,),
        in_specs=[pl.BlockSpec((1, 2, tb), lambda i: (i, 0, 0), memory_space=pltpu.SMEM),
                  pl.BlockSpec((1, 2, tb), lambda i: (jnp.minimum(i + 1, nblk - 1), 0, 0), memory_space=pltpu.SMEM),
                  pl.BlockSpec(memory_space=pl.ANY),
                  pl.BlockSpec((tb, d), lambda i: (i, 0)),
                  pl.BlockSpec((tb, LANES), lambda i: (i, 0)),
                  pl.BlockSpec((1, d), lambda i: (0, 0)),
                  pl.BlockSpec((1, d), lambda i: (0, 0))],
        out_specs=pl.BlockSpec((tb, d), lambda i: (i, 0)),
        out_shape=jax.ShapeDtypeStruct((n, d), F32),
        scratch_shapes=[pltpu.VMEM((2, 2 * tb, d), F32), pltpu.SemaphoreType.DMA((2,))],
        compiler_params=_cparams(("arbitrary",), 32),
        name="moe_combine_ln2",
    )(pos_blocks, pos_blocks, y, h1, gate_lanes, ln_g, ln_b)


def _cast_weight(dst_ref, src_ref):
    rows = src_ref.shape[1]
    step = min(256, rows)

    def body(i, carry):
        r0 = pl.multiple_of(i * step, step)
        dst_ref[pl.ds(r0, step), :] = src_ref[0, pl.ds(r0, step), :].astype(BF16)
        return carry
    lax.fori_loop(0, rows // step, body, 0)


ITEM_COMPUTE = 1
ITEM_NEW_WEIGHTS = 2
ITEM_ZERO_FILL = 4


def _expert_hidden(words, w1b_ref, w3b_ref):
    x_lo, x_hi = _unpack_bf16_halves(words)
    half = x_lo.shape[1]

    def project(w_ref):
        return (jnp.dot(x_lo, w_ref[0:half, :], preferred_element_type=F32)
                + jnp.dot(x_hi, w_ref[half:2 * half, :], preferred_element_type=F32))

    a = project(w1b_ref)
    b = project(w3b_ref)
    return (a * jax.nn.sigmoid(a) * b).astype(BF16)


def _moe_up_gather_kernel(e_ref, cw_ref, bi_ref, bo_ref, tiles_ref, flag_ref, tok_ref, tok1_ref, tok2_ref, hp_hbm,
                          w1_ref, w3_ref, o_ref, xs_ref, buf_ref, w1b_ref, w3b_ref, sem):
    it = pl.program_id(0)
    last = pl.num_programs(0) - 1
    flag = flag_ref[it]
    n_slots, n_tiles = buf_ref.shape[0], buf_ref.shape[1]
    slot = it % n_slots

    def computes(item):
        return (item <= last) & ((flag_ref[jnp.minimum(item, last)] & ITEM_COMPUTE) != 0)

    def tiles_of(item):
        return tiles_ref[jnp.minimum(item, last)]

    def start_rows(idx_ref, into, n_issue):
        def issue(i, carry):
            for k in range(SUBLANES):
                tok = idx_ref[0, 0, i * SUBLANES + k]
                pltpu.make_async_copy(hp_hbm.at[lax.shift_right_logical(tok, 3), pl.ds(tok & (SUBLANES - 1), 1), :],
                                      buf_ref.at[into, i, pl.ds(k, 1), :], sem.at[into]).start(priority=k % 2)
            return carry
        lax.fori_loop(0, n_issue, issue, 0)

    @pl.when(it == 0)
    def _():
        buf_ref[...] = jnp.zeros(buf_ref.shape, buf_ref.dtype)
        start_rows(tok_ref, 0, tiles_of(0))

        @pl.when(computes(1))
        def _():
            start_rows(tok1_ref, 1, tiles_of(1))

    @pl.when(computes(it + 2))
    def _():
        start_rows(tok2_ref, (it + 2) % n_slots, tiles_of(it + 2))

    @pl.when((flag & ITEM_NEW_WEIGHTS) != 0)
    def _():
        _cast_weight(w1b_ref, w1_ref)
        _cast_weight(w3b_ref, w3_ref)

    @pl.when((flag & ITEM_COMPUTE) != 0)
    def _():
        def wait_tile(i, carry):
            pltpu.make_async_copy(hp_hbm.at[pl.ds(0, 1)], buf_ref.at[slot, pl.ds(i, 1)], sem.at[slot]).wait()
            return carry
        lax.fori_loop(0, tiles_of(it), wait_tile, 0)
        words = buf_ref[slot].reshape(n_tiles * SUBLANES, buf_ref.shape[3])
        xs_ref[...] = words
        o_ref[...] = _expert_hidden(words, w1b_ref, w3b_ref)

    @pl.when((flag & ITEM_ZERO_FILL) != 0)
    def _():
        o_ref[...] = jnp.zeros(o_ref.shape, o_ref.dtype)
        xs_ref[...] = jnp.zeros(xs_ref.shape, xs_ref.dtype)


def _moe_up_kernel(e_ref, cw_ref, bi_ref, bo_ref, co_ref, flag_ref, x_ref, w1_ref, w3_ref, o_ref, w1b_ref, w3b_ref):
    flag = flag_ref[pl.program_id(0)]

    @pl.when((flag & ITEM_NEW_WEIGHTS) != 0)
    def _():
        _cast_weight(w1b_ref, w1_ref)
        _cast_weight(w3b_ref, w3_ref)

    @pl.when((flag & ITEM_COMPUTE) != 0)
    def _():
        o_ref[...] = _expert_hidden(x_ref[...], w1b_ref, w3b_ref)

    @pl.when((flag & ITEM_ZERO_FILL) != 0)
    def _():
        o_ref[...] = jnp.zeros(o_ref.shape, o_ref.dtype)


def _moe_down_kernel(e_ref, cw_ref, bi_ref, bo_ref, co_ref, flag_ref, h0_ref, h1_ref, w2_ref, o_ref, w2b_ref):
    flag = flag_ref[pl.program_id(0)]

    @pl.when((flag & ITEM_NEW_WEIGHTS) != 0)
    def _():
        _cast_weight(w2b_ref, w2_ref)

    @pl.when((flag & ITEM_COMPUTE) != 0)
    def _():
        fc = h0_ref.shape[1]
        o_ref[...] = (jnp.dot(h0_ref[...], w2b_ref[0:fc, :], preferred_element_type=F32)
                      + jnp.dot(h1_ref[...], w2b_ref[fc:2 * fc, :], preferred_element_type=F32))

    @pl.when((flag & ITEM_ZERO_FILL) != 0)
    def _():
        o_ref[...] = jnp.zeros(o_ref.shape, o_ref.dtype)


def _work_items(blocks_per_expert, block_start, n_chunks, n_blocks):
    n_items = n_chunks * n_blocks
    per_e = n_chunks * blocks_per_expert
    end = jnp.cumsum(per_e)
    start = end - per_e
    it = jnp.arange(n_items, dtype=I32)
    total = end[-1]
    used_blocks = total // n_chunks
    itc = jnp.minimum(it, total - 1)
    e = jnp.sum(itc[:, None] >= end[None, :], axis=1).astype(I32)
    local = itc - start[e]
    nb = jnp.maximum(blocks_per_expert[e], 1)
    c = local // nb
    r = local - c * nb
    b_in = block_start[e] + r
    active = it < total
    spare = jnp.maximum(it - total, 0)
    b_out = jnp.where(active, b_in, used_blocks + spare // n_chunks)
    c_out = jnp.where(active, c, spare % n_chunks)
    first = active & (r == 0)
    flag = jnp.where(active, ITEM_COMPUTE + ITEM_NEW_WEIGHTS * first.astype(I32), ITEM_ZERO_FILL)
    later_first = lax.cummin(jnp.where(first, it, n_items)[::-1])[::-1]
    next_start = jnp.concatenate([later_first[1:], jnp.full((1,), n_items, I32)])
    ahead = jnp.minimum(next_start, n_items - 1)
    use_next = active & ~first & (next_start < n_items)
    e_w = jnp.where(use_next, e[ahead], e)
    c_w = jnp.where(use_next, c[ahead], c)
    return tuple(v.astype(I32) for v in (e_w, c_w, b_in, b_out, c_out, flag))


def _moe_up_gather(hp, buf_tok, w1, w3, items):
    p = buf_tok.shape[0]
    half = hp.shape[1]
    d, f = w1.shape[1], w1.shape[2]
    fc = f // 2
    n_items = items[0].shape[0]
    nblk = p // MOE_ROWS
    tok_blocks = buf_tok.reshape(nblk, 1, MOE_ROWS)
    wspec = pl.BlockSpec((1, d, fc), lambda it, e, cw, bi, bo, co, fl: (e[it], 0, 0))
    lookahead = 2

    def tok_spec(ahead):
        return pl.BlockSpec((1, 1, MOE_ROWS),
                            lambda it, e, cw, bi, bo, co, fl: (bi[jnp.minimum(it + ahead, n_items - 1)], 0, 0),
                            memory_space=pltpu.SMEM)

    return pl.pallas_call(
        _moe_up_gather_kernel,
        grid_spec=pltpu.PrefetchScalarGridSpec(
            num_scalar_prefetch=6,
            grid=(n_items,),
            in_specs=[tok_spec(k) for k in range(lookahead + 1)] + [pl.BlockSpec(memory_space=pl.ANY), wspec, wspec],
            out_specs=[pl.BlockSpec((MOE_ROWS, fc), lambda it, e, cw, bi, bo, co, fl: (bo[it], 0)),
                       pl.BlockSpec((MOE_ROWS, half), lambda it, e, cw, bi, bo, co, fl: (bo[it], 0))],
            scratch_shapes=[pltpu.VMEM((lookahead + 1, MOE_ROWS // SUBLANES, SUBLANES, half), I32),
                            pltpu.VMEM((d, fc), BF16),
                            pltpu.VMEM((d, fc), BF16), pltpu.SemaphoreType.DMA((lookahead + 1,))]),
        out_shape=[jax.ShapeDtypeStruct((p, fc), BF16), jax.ShapeDtypeStruct((p, half), I32)],
        compiler_params=_cparams(("arbitrary",), 56),
        name="moe_up_gather",
    )(*items, *([tok_blocks] * (lookahead + 1)), hp.reshape(hp.shape[0] // SUBLANES, SUBLANES, half), w1, w3)


def _moe_up(xs, w1, w3, items):
    p, half = xs.shape
    d, f = w1.shape[1], w1.shape[2]
    fc = f // 2
    assert d == 2 * half
    wspec = pl.BlockSpec((1, d, fc), lambda it, e, cw, bi, bo, co, fl: (e[it], 0, 1))
    return pl.pallas_call(
        _moe_up_kernel,
        grid_spec=pltpu.PrefetchScalarGridSpec(
            num_scalar_prefetch=6,
            grid=(items[0].shape[0],),
            in_specs=[pl.BlockSpec((MOE_ROWS, half), lambda it, e, cw, bi, bo, co, fl: (bi[it], 0)), wspec, wspec],
            out_specs=pl.BlockSpec((MOE_ROWS, fc), lambda it, e, cw, bi, bo, co, fl: (bo[it], 0)),
            scratch_shapes=[pltpu.VMEM((d, fc), BF16), pltpu.VMEM((d, fc), BF16)]),
        out_shape=jax.ShapeDtypeStruct((p, fc), BF16),
        compiler_params=_cparams(("arbitrary",), 56),
        name="moe_up",
    )(*items, xs, w1, w3)


def _moe_down(h0, h1, w2, items, d_chunk):
    p, fc = h0.shape
    f = 2 * fc
    d = w2.shape[2]
    hspec = pl.BlockSpec((MOE_ROWS, fc), lambda it, e, cw, bi, bo, co, fl: (bi[it], 0))
    return pl.pallas_call(
        _moe_down_kernel,
        grid_spec=pltpu.PrefetchScalarGridSpec(
            num_scalar_prefetch=6,
            grid=(items[0].shape[0],),
            in_specs=[hspec, hspec,
                      pl.BlockSpec((1, f, d_chunk), lambda it, e, cw, bi, bo, co, fl: (e[it], 0, cw[it]))],
            out_specs=pl.BlockSpec((MOE_ROWS, d_chunk), lambda it, e, cw, bi, bo, co, fl: (bo[it], co[it])),
            scratch_shapes=[pltpu.VMEM((f, d_chunk), BF16)]),
        out_shape=jax.ShapeDtypeStruct((p, d), F32),
        compiler_params=_cparams(("arbitrary",), 56),
        name="moe_down",
    )(*items, h0, h1, w2)


def _hier_moe_ln(h1, hp, logits, w1, w3, w2, ln_g, ln_b, alpha):
    n, d = h1.shape
    e_lanes, gate_lanes, lane_counts = _route(logits)
    a = 2 * n
    e_flat = e_lanes[:, 0:2].reshape(a)
    rank = e_lanes[:, 2:4].reshape(a)
    counts = lane_counts[0, N_GROUPS:N_GROUPS + N_EXPERTS].astype(I32)
    blocks_per_expert = (counts + MOE_ROWS - 1) // MOE_ROWS
    block_start = jnp.cumsum(blocks_per_expert) - blocks_per_expert
    onehot = e_flat[:, None] == jnp.arange(N_EXPERTS, dtype=I32)[None, :]
    pos = jnp.sum(jnp.where(onehot, block_start[None, :], 0), axis=1) * MOE_ROWS + rank
    n_blocks = (a + MOE_ROWS - 1) // MOE_ROWS + N_EXPERTS
    buf_tok = jnp.zeros((n_blocks * MOE_ROWS,), I32).at[pos].set(jnp.arange(a, dtype=I32) // 2)

    d_chunk = min(4096, d)
    up_items = _work_items(blocks_per_expert, block_start, 1, n_blocks)
    blk = jnp.arange(n_blocks, dtype=I32)
    owned = (blk[:, None] >= block_start[None, :]) & (blk[:, None] < (block_start + blocks_per_expert)[None, :])
    left = jnp.clip(counts[None, :] - (blk[:, None] - block_start[None, :]) * MOE_ROWS, 0, MOE_ROWS)
    block_tiles = (jnp.sum(jnp.where(owned, left, 0), axis=1) + SUBLANES - 1) // SUBLANES
    item_tiles = jnp.sum(jnp.where(up_items[2][:, None] == blk[None, :], block_tiles[None, :], 0), axis=1).astype(I32)
    h0, xs = _moe_up_gather(hp, buf_tok, w1, w3, up_items[:4] + (item_tiles,) + up_items[5:])
    h1_mid = _moe_up(xs, w1, w3, up_items)
    y = _moe_down(h0, h1_mid, w2, _work_items(blocks_per_expert, block_start, d // d_chunk, n_blocks), d_chunk)
    return _combine_ln(y, pos.reshape(n, 2), h1, gate_lanes, ln_g, ln_b, alpha)


def kernel(x, w_in, conv_dw_w, conv_dw_b, conv_ln_g, conv_ln_b, w_out, ln1_g, ln1_b, w_router_group, b_router_group, w_router_expert, b_router_expert, w_expert_gate, w_expert_up, w_expert_down, ln2_g, ln2_b):
    b, s, d = x.shape
    depth = w_in.shape[0]
    assert depth == 1
    alpha = float((2.0 * depth) ** 0.25)
    n = b * s
    dc = conv_dw_w.shape[2]
    d_attn = N_HEADS * HEAD_DIM
    d_kv = N_KV_HEADS * HEAD_DIM
    d_qi = IDX_HEADS * IDX_DIM
    n_small = IDX_DIM + IDX_HEADS
    assert w_in.shape[2] == 2 * dc + d_attn + 2 * d_kv + d_qi + n_small

    x2 = x.reshape(n, d)
    w = w_in[0].astype(BF16)
    o_qkvi = 2 * dc
    o_small = o_qkvi + d_attn + 2 * d_kv + d_qi
    w_small = jnp.pad(w[:, o_small:], ((0, 0), (0, LANES - n_small)))
    qkvi_scale = jnp.concatenate([jnp.full((d_attn,), HEAD_DIM ** -0.5 * LOG2E, F32), jnp.ones((2 * d_kv,), F32),
                                  jnp.full((d_qi,), IDX_DIM ** -0.5, F32)])[None, :]

    xb, small = _proj_idx_and_cast(x2, w_small)
    glu = _proj_glu(xb, w, dc)
    qkvi = _proj_scale(xb, w, o_qkvi, o_small - o_qkvi, qkvi_scale, BF16, "proj_qkvi")

    conv_out = _conformer_conv(glu.reshape(b, s, dc), conv_dw_w[0], conv_dw_b, conv_ln_g, conv_ln_b)
    attn_out = _dsa_attention(qkvi.reshape(b, s, -1), small.reshape(b, s, LANES), b, s)

    n_route = N_GROUPS + N_EXPERTS
    w_router = jnp.pad(jnp.concatenate([w_router_group[0], w_router_expert[0]], axis=1),
                       ((0, 0), (0, LANES - n_route))).astype(BF16)
    b_router = jnp.pad(jnp.concatenate([b_router_group[0], b_router_expert[0]]), (0, LANES - n_route))[None, :]
    h1, hp, logits = _out_proj_ln(conv_out.reshape(n, dc), attn_out.reshape(n, d_attn),
                                  w_out[0, :dc].astype(BF16), w_out[0, dc:].astype(BF16),
                                  x2, ln1_g, ln1_b, w_router, b_router, alpha)

    out = _hier_moe_ln(h1, hp, logits, w_expert_gate[0], w_expert_up[0], w_expert_down[0], ln2_g, ln2_b, alpha)
    return out.reshape(b, s, d)
```
